```python
import jax
import jax.numpy as jnp
from jax import lax
import numpy as np

D_MODEL = 1024
BATCH = 8
SEQ = 2048
DEPTH = 2

HEAD_DIM = 64
ROPE_THETA = 500000.0
PARTIAL_ROT_DIM = HEAD_DIM // 4
NORM_EPS = 1e-6
Q_BLOCK = 128

NSA_HEADS = 6
NSA_KV_HEADS = 2
NSA_REP = NSA_HEADS // NSA_KV_HEADS
CMP_LEN = 32
CMP_STRIDE = 16
SLC_LEN = 64
SLC_TOPK = 16
WINDOW = 512
SLC_Q_BLOCK = 32

MLA_HEADS = 6
MLA_Q_LORA = 256
MLA_KV_LORA = 128
MLA_NOPE = 64
MLA_ROPE = 32
MLA_V = 64

MLSTM_HEADS = 4
MLSTM_DIM = 64
MLSTM_CONV = 4
MLSTM_CHUNK = 64

WIDTH_A = NSA_HEADS * HEAD_DIM
WIDTH_B = MLA_HEADS * MLA_V
WIDTH_C = MLSTM_HEADS * MLSTM_DIM
D_MIX = WIDTH_A + WIDTH_B + WIDTH_C
FFN_HIDDEN = -(-8 * D_MODEL // 768) * 256

KV_A = NSA_KV_HEADS * HEAD_DIM
IN_SIZES = [WIDTH_A, KV_A, KV_A, KV_A, KV_A, KV_A, KV_A, 3 * NSA_HEADS,
            MLA_Q_LORA, MLA_KV_LORA, MLA_ROPE,
            WIDTH_C, WIDTH_C, WIDTH_C, MLSTM_HEADS, MLSTM_HEADS]
IN_OFFSETS = [int(v) for v in np.cumsum(IN_SIZES)[:-1]]
D_IN = sum(IN_SIZES)

kernel_name = 'hybrid_nsa_mla_mlstm_block'


def rmsnorm(x, g):
    x32 = x.astype(jnp.float32)
    y = x32 * lax.rsqrt(jnp.mean(x32 * x32, axis=-1, keepdims=True) + NORM_EPS)
    return (y * g.astype(jnp.float32)).astype(x.dtype)


def masked_softmax(s, mask):
    s = jnp.where(mask, s.astype(jnp.float32), -1e30)
    p = jax.nn.softmax(s, axis=-1)
    return jnp.where(mask, p, 0.0)


def rope_tables(positions, rot_dim):
    inv = jnp.power(ROPE_THETA, -jnp.arange(0, rot_dim, 2, dtype=jnp.float32) / rot_dim)
    ang = positions.astype(jnp.float32)[:, None, :, None] * inv
    return jnp.cos(ang), jnp.sin(ang)


def apply_rope(x, cos, sin):
    r2 = cos.shape[-1]
    cos = cos.astype(x.dtype)
    sin = sin.astype(x.dtype)
    x1, x2, rest = x[..., :r2], x[..., r2:2 * r2], x[..., 2 * r2:]
    return jnp.concatenate([x1 * cos - x2 * sin, x1 * sin + x2 * cos, rest], axis=-1)


def to_heads(x, n):
    b, s, _ = x.shape
    return x.reshape(b, s, n, -1).transpose(0, 2, 1, 3)


def from_heads(x):
    b, h, s, d = x.shape
    return x.transpose(0, 2, 1, 3).reshape(b, s, h * d)


def nsa_attention(q, k_cmp_tok, v_cmp_tok, k_slc, v_slc, k_win, v_win, gates,
                  cmp_pos, w_cmp_k, w_cmp_v):
    B, H, S, d = q.shape
    G, R = NSA_KV_HEADS, NSA_REP
    scale = d ** -0.5
    t = jnp.arange(S)
    qg = q.reshape(B, G, R, S, d)

    n_cmp = (S - CMP_LEN) // CMP_STRIDE + 1
    tok = jnp.arange(n_cmp)[:, None] * CMP_STRIDE + jnp.arange(CMP_LEN)[None, :]

    def compress(tokens, w):
        blocks = tokens[:, :, tok] + cmp_pos
        return blocks.reshape(B, G, n_cmp, CMP_LEN * d) @ w

    kc = compress(k_cmp_tok, w_cmp_k)
    vc = compress(v_cmp_tok, w_cmp_v)
    cmp_end = jnp.arange(n_cmp) * CMP_STRIDE + CMP_LEN - 1
    s_cmp = jnp.einsum('bgrqd,bgnd->bgrqn', qg, kc) * scale
    p_cmp = masked_softmax(s_cmp, cmp_end[None, :] <= t[:, None])
    o_cmp = jnp.einsum('bgrqn,bgnd->bgrqd', p_cmp.astype(vc.dtype), vc)

    n_slc = S // SLC_LEN
    ci = jnp.arange(n_cmp)[:, None] * CMP_STRIDE
    sj = jnp.arange(n_slc)[None, :] * SLC_LEN
    overlap = ((ci < sj + SLC_LEN) & (ci + CMP_LEN > sj)).astype(jnp.float32)
    imp = jnp.einsum('bgrqn,nj->bgqj', p_cmp, overlap)
    j = jnp.arange(n_slc)[None, :]
    cur = (t // SLC_LEN)[:, None]
    forced = (j == 0) | (j == cur) | (j == cur - 1)
    future = j * SLC_LEN > t[:, None]
    imp = jnp.where(forced, jnp.inf, imp)
    imp = jnp.where(future, -jnp.inf, imp)
    n_sel = min(SLC_TOPK, n_slc)
    _, idx = lax.top_k(imp, n_sel)

    kb = k_slc.reshape(B, G, n_slc, SLC_LEN, d)
    vb = v_slc.reshape(B, G, n_slc, SLC_LEN, d)
    nqc = S // SLC_Q_BLOCK
    q_ch = qg.reshape(B, G, R, nqc, SLC_Q_BLOCK, d).transpose(3, 0, 1, 2, 4, 5)
    i_ch = idx.reshape(B, G, nqc, SLC_Q_BLOCK, n_sel).transpose(2, 0, 1, 3, 4)
    bi = jnp.arange(B)[:, None, None, None]
    gi = jnp.arange(G)[None, :, None, None]
    n_keys = n_sel * SLC_LEN

    def slc_block(args):
        qc, ic, c_idx = args
        tq = c_idx * SLC_Q_BLOCK + jnp.arange(SLC_Q_BLOCK)
        kg = kb[bi, gi, ic].reshape(B, G, SLC_Q_BLOCK, n_keys, d)
        vg = vb[bi, gi, ic].reshape(B, G, SLC_Q_BLOCK, n_keys, d)
        kpos = (ic[..., None] * SLC_LEN + jnp.arange(SLC_LEN)).reshape(B, G, SLC_Q_BLOCK, n_keys)
        mask = (kpos <= tq[None, None, :, None])[:, :, None]
        s = jnp.einsum('bgrqd,bgqkd->bgrqk', qc, kg) * scale
        p = masked_softmax(s, mask)
        return jnp.einsum('bgrqk,bgqkd->bgrqd', p.astype(vg.dtype), vg)

    o_slc = lax.map(slc_block, (q_ch, i_ch, jnp.arange(nqc)))
    o_slc = o_slc.transpose(1, 2, 3, 0, 4, 5).reshape(B, G, R, S, d)

    kpad = jnp.pad(k_win, ((0, 0), (0, 0), (WINDOW, 0), (0, 0)))
    vpad = jnp.pad(v_win, ((0, 0), (0, 0), (WINDOW, 0), (0, 0)))
    nqb = S // Q_BLOCK
    span = WINDOW + Q_BLOCK
    q_bl = qg.reshape(B, G, R, nqb, Q_BLOCK, d).transpose(3, 0, 1, 2, 4, 5)

    def win_block(args):
        qc, i = args
        start = i * Q_BLOCK
        kw = lax.dynamic_slice_in_dim(kpad, start, span, axis=2)
        vw = lax.dynamic_slice_in_dim(vpad, start, span, axis=2)
        tq = start + jnp.arange(Q_BLOCK)
        kp = start - WINDOW + jnp.arange(span)
        diff = tq[:, None] - kp[None, :]
        mask = (kp[None, :] >= 0) & (diff >= 0) & (diff < WINDOW)
        s = jnp.einsum('bgrqd,bgkd->bgrqk', qc, kw) * scale
        p = masked_softmax(s, mask)
        return jnp.einsum('bgrqk,bgkd->bgrqd', p.astype(vw.dtype), vw)

    o_win = lax.map(win_block, (q_bl, jnp.arange(nqb)))
    o_win = o_win.transpose(1, 2, 3, 0, 4, 5).reshape(B, G, R, S, d)

    g = gates.reshape(B, S, H, 3).transpose(0, 2, 1, 3).reshape(B, G, R, S, 3)
    o = g[..., 0:1] * o_cmp + g[..., 1:2] * o_slc + g[..., 2:3] * o_win
    return from_heads(o.reshape(B, H, S, d))


def causal_attention_blocks(q, k, v, scale):
    B, H, S, dq = q.shape
    nb = S // Q_BLOCK
    qb = q.reshape(B, H, nb, Q_BLOCK, dq).transpose(2, 0, 1, 3, 4)
    kpos = jnp.arange(S)

    def body(args):
        qc, i = args
        tq = i * Q_BLOCK + jnp.arange(Q_BLOCK)
        s = jnp.einsum('bhqd,bhkd->bhqk', qc, k) * scale
        p = masked_softmax(s, kpos[None, :] <= tq[:, None])
        return jnp.einsum('bhqk,bhkd->bhqd', p.astype(v.dtype), v)

    o = lax.map(body, (qb, jnp.arange(nb)))
    return o.transpose(1, 2, 0, 3, 4).reshape(B, H, S, v.shape[-1])


def mla_attention(q_lora, kv_lora, k_rope, cos, sin, g_q_lora, w_uq, g_kv_lora, w_ukv):
    q = to_heads(rmsnorm(q_lora, g_q_lora) @ w_uq, MLA_HEADS)
    q = jnp.concatenate([q[..., :MLA_NOPE], apply_rope(q[..., MLA_NOPE:], cos, sin)], axis=-1)
    kv = to_heads(rmsnorm(kv_lora, g_kv_lora) @ w_ukv, MLA_HEADS)
    k_nope, v = kv[..., :MLA_NOPE], kv[..., MLA_NOPE:]
    k_r = apply_rope(k_rope[:, None], cos, sin)
    k = jnp.concatenate([k_nope, jnp.broadcast_to(k_r, k_nope.shape[:-1] + (MLA_ROPE,))], axis=-1)
    o = causal_attention_blocks(q, k, v, (MLA_NOPE + MLA_ROPE) ** -0.5)
    return from_heads(o)


def mlstm_chunkwise(q, k, v, ig, lf):
    B, NH, S, d = q.shape
    L = MLSTM_CHUNK
    NC = S // L
    f32 = jnp.float32
    qc = q.astype(f32).reshape(B, NH, NC, L, d)
    kc = k.astype(f32).reshape(B, NH, NC, L, d)
    vc = v.astype(f32).reshape(B, NH, NC, L, d)
    ig = ig.reshape(B, NH, NC, L)
    b = jnp.cumsum(lf.reshape(B, NH, NC, L), axis=-1)
    bL = b[..., -1]
    a = bL[..., None] - b + ig
    m_loc = jnp.max(a, axis=-1)
    w_loc = jnp.exp(a - m_loc[..., None])
    C_loc = jnp.einsum('bhcl,bhcld,bhcle->bhcde', w_loc, kc, vc)
    n_loc = jnp.einsum('bhcl,bhcld->bhcd', w_loc, kc)

    def step(carry, xs):
        C, n, m = carry
        Cl, nl, ml, bl = xs
        m_new = jnp.maximum(bl + m, ml)
        e_old = jnp.exp(bl + m - m_new)
        e_loc = jnp.exp(ml - m_new)
        C_new = e_old[..., None, None] * C + e_loc[..., None, None] * Cl
        n_new = e_old[..., None] * n + e_loc[..., None] * nl
        return (C_new, n_new, m_new), (C, n, m)

    init = (jnp.zeros((B, NH, d, d), f32), jnp.zeros((B, NH, d), f32), jnp.zeros((B, NH), f32))
    xs = (jnp.moveaxis(C_loc, 2, 0), jnp.moveaxis(n_loc, 2, 0),
          jnp.moveaxis(m_loc, 2, 0), jnp.moveaxis(bL, 2, 0))
    _, (C_prev, n_prev, m_prev) = lax.scan(step, init, xs)
    C_prev = jnp.moveaxis(C_prev, 0, 2)
    n_prev = jnp.moveaxis(n_prev, 0, 2)
    m_prev = jnp.moveaxis(m_prev, 0, 2)

    causal = jnp.tril(jnp.ones((L, L), dtype=bool))
    D = jnp.where(causal, b[..., :, None] - b[..., None, :] + ig[..., None, :], -jnp.inf)
    g = b + m_prev[..., None]
    m_t = jnp.maximum(g, jnp.max(D, axis=-1))
    Sw = jnp.einsum('bhcld,bhcsd->bhcls', qc, kc) * jnp.exp(D - m_t[..., None])
    inter = jnp.exp(g - m_t)
    num = inter[..., None] * jnp.einsum('bhcld,bhcde->bhcle', qc, C_prev) + jnp.einsum('bhcls,bhcse->bhcle', Sw, vc)
    den = inter * jnp.einsum('bhcld,bhcd->bhcl', qc, n_prev) + jnp.sum(Sw, axis=-1)
    h = num / jnp.maximum(jnp.abs(den), jnp.exp(-m_t))[..., None]
    return h.reshape(B, NH, S, d).astype(q.dtype)


def mlstm_mixer(x_m, v_in, o_in, i_in, f_in, conv_w, conv_b, w_q_m, w_k_m, b_igate, b_fgate, g_mh, skip_m):
    B, S, C = x_m.shape
    xc = lax.conv_general_dilated(x_m, conv_w[:, None, :], (1,), [(MLSTM_CONV - 1, 0)],
                                  dimension_numbers=('NWC', 'WIO', 'NWC'), feature_group_count=C)
    xc = jax.nn.silu(xc + conv_b)
    xh = xc.reshape(B, S, MLSTM_HEADS, MLSTM_DIM)
    q = jnp.einsum('bshd,hde->bhse', xh, w_q_m) * (MLSTM_DIM ** -0.5)
    k = jnp.einsum('bshd,hde->bhse', xh, w_k_m)
    v = to_heads(v_in, MLSTM_HEADS)
    ig = (i_in + b_igate).astype(jnp.float32).transpose(0, 2, 1)
    lf = jax.nn.log_sigmoid((f_in + b_fgate).astype(jnp.float32)).transpose(0, 2, 1)
    h = jax.nn.sigmoid(to_heads(o_in, MLSTM_HEADS)) * mlstm_chunkwise(q, k, v, ig, lf)
    h32 = h.astype(jnp.float32)
    mu = jnp.mean(h32, axis=-1, keepdims=True)
    var = jnp.mean(jnp.square(h32 - mu), axis=-1, keepdims=True)
    hn = from_heads((h32 - mu) * lax.rsqrt(var + NORM_EPS)).astype(x_m.dtype) * g_mh
    return hn + skip_m * xc


def hybrid_mixer(h, cos_p, sin_p, cos_m, sin_m, w_in, cmp_pos, w_cmp_k, w_cmp_v, g_out_a,
                 g_q_lora, w_uq, g_kv_lora, w_ukv, g_out_b, conv_w, conv_b, w_q_m, w_k_m,
                 b_igate, b_fgate, g_mh, skip_m, w_out):
    (a_q, a_ck, a_cv, a_sk, a_sv, a_wk, a_wv, a_g, b_ql, b_kvl, b_kr,
     c_x, c_v, c_o, c_i, c_f) = jnp.split(h @ w_in, IN_OFFSETS, axis=-1)

    def rp(t, n):
        return apply_rope(to_heads(t, n), cos_p, sin_p)

    ya = nsa_attention(rp(a_q, NSA_HEADS), rp(a_ck, NSA_KV_HEADS), to_heads(a_cv, NSA_KV_HEADS),
                       rp(a_sk, NSA_KV_HEADS), to_heads(a_sv, NSA_KV_HEADS),
                       rp(a_wk, NSA_KV_HEADS), to_heads(a_wv, NSA_KV_HEADS),
                       jax.nn.sigmoid(a_g), cmp_pos, w_cmp_k, w_cmp_v)
    yb = mla_attention(b_ql, b_kvl, b_kr, cos_m, sin_m, g_q_lora, w_uq, g_kv_lora, w_ukv)
    yc = mlstm_mixer(c_x, c_v, c_o, c_i, c_f, conv_w, conv_b, w_q_m, w_k_m, b_igate, b_fgate, g_mh, skip_m)
    mix = jnp.concatenate([rmsnorm(ya, g_out_a), rmsnorm(yb, g_out_b), yc], axis=-1)
    return mix @ w_out


def swiglu(h, w_gu, w_down):
    gate, up = jnp.split(h @ w_gu, 2, axis=-1)
    return (jax.nn.silu(gate) * up) @ w_down


def setup_inputs(seed: int = 0) -> dict:
    key = jax.random.key(seed)
    ks = jax.random.split(key, 32)
    f32 = jnp.float32
    L = DEPTH

    def nrm(k, shape, scale):
        return jax.random.normal(k, shape, f32) * scale

    def gain(k, shape):
        return 1.0 + 0.02 * jax.random.normal(k, shape, f32)

    positions = (jnp.arange(SEQ, dtype=jnp.int32)[None, :]
                 + jax.random.randint(ks[2], (BATCH, 1), 0, SEQ, dtype=jnp.int32))
    return {
        'x': nrm(ks[0], (BATCH, SEQ, D_MODEL), 1.0),
        'c': nrm(ks[1], (BATCH, D_MODEL), 1.0),
        'positions': positions,
        'g_norm1': gain(ks[3], (L, D_MODEL)),
        'g_norm2': gain(ks[4], (L, D_MODEL)),
        'w_ada': nrm(ks[5], (L, D_MODEL, 6 * D_MODEL), 0.5 * D_MODEL ** -0.5),
        'b_ada': nrm(ks[6], (L, 6 * D_MODEL), 0.02),
        'w_in': nrm(ks[7], (L, D_MODEL, D_IN), D_MODEL ** -0.5),
        'cmp_pos': nrm(ks[8], (L, CMP_LEN, HEAD_DIM), 0.1),
        'w_cmp_k': nrm(ks[9], (L, CMP_LEN * HEAD_DIM, HEAD_DIM), (CMP_LEN * HEAD_DIM) ** -0.5),
        'w_cmp_v': nrm(ks[10], (L, CMP_LEN * HEAD_DIM, HEAD_DIM), (CMP_LEN * HEAD_DIM) ** -0.5),
        'g_out_a': gain(ks[11], (L, WIDTH_A)),
        'g_q_lora': gain(ks[12], (L, MLA_Q_LORA)),
        'w_uq': nrm(ks[13], (L, MLA_Q_LORA, MLA_HEADS * (MLA_NOPE + MLA_ROPE)), MLA_Q_LORA ** -0.5),
        'g_kv_lora': gain(ks[14], (L, MLA_KV_LORA)),
        'w_ukv': nrm(ks[15], (L, MLA_KV_LORA, MLA_HEADS * (MLA_NOPE + MLA_V)), MLA_KV_LORA ** -0.5),
        'g_out_b': gain(ks[16], (L, WIDTH_B)),
        'conv_w': nrm(ks[17], (L, MLSTM_CONV, WIDTH_C), MLSTM_CONV ** -0.5),
        'conv_b': nrm(ks[18], (L, WIDTH_C), 0.02),
        'w_q_m': nrm(ks[19], (L, MLSTM_HEADS, MLSTM_DIM, MLSTM_DIM), MLSTM_DIM ** -0.5),
        'w_k_m': nrm(ks[20], (L, MLSTM_HEADS, MLSTM_DIM, MLSTM_DIM), MLSTM_DIM ** -0.5),
        'b_igate': nrm(ks[21], (L, MLSTM_HEADS), 0.1),
        'b_fgate': jnp.linspace(3.0, 6.0, MLSTM_HEADS, dtype=f32)[None, :] + nrm(ks[22], (L, MLSTM_HEADS), 0.01),
        'g_mh': gain(ks[23], (L, WIDTH_C)),
        'skip_m': gain(ks[24], (L, WIDTH_C)),
        'w_out': nrm(ks[25], (L, D_MIX, D_MODEL), D_MIX ** -0.5),
        'w_gu': nrm(ks[26], (L, D_MODEL, 2 * FFN_HIDDEN), D_MODEL ** -0.5),
        'w_down': nrm(ks[27], (L, FFN_HIDDEN, D_MODEL), FFN_HIDDEN ** -0.5),
        'g_final': gain(ks[28], (D_MODEL,)),
    }


def reference(x, c, positions, g_norm1, g_norm2, w_ada, b_ada, w_in, cmp_pos, w_cmp_k, w_cmp_v,
              g_out_a, g_q_lora, w_uq, g_kv_lora, w_ukv, g_out_b, conv_w, conv_b, w_q_m, w_k_m,
              b_igate, b_fgate, g_mh, skip_m, w_out, w_gu, w_down, g_final):
    cos_p, sin_p = rope_tables(positions, PARTIAL_ROT_DIM)
    cos_m, sin_m = rope_tables(positions, MLA_ROPE)
    c_act = jax.nn.silu(c)
    for l in range(DEPTH):
        mod = (c_act @ w_ada[l] + b_ada[l])[:, None, :]
        shift1, scale1, gate1, shift2, scale2, gate2 = jnp.split(mod, 6, axis=-1)
        h = rmsnorm(x, g_norm1[l]) * (1.0 + scale1) + shift1
        x = x + gate1 * hybrid_mixer(h, cos_p, sin_p, cos_m, sin_m, w_in[l], cmp_pos[l], w_cmp_k[l],
                                     w_cmp_v[l], g_out_a[l], g_q_lora[l], w_uq[l], g_kv_lora[l],
                                     w_ukv[l], g_out_b[l], conv_w[l], conv_b[l], w_q_m[l], w_k_m[l],
                                     b_igate[l], b_fgate[l], g_mh[l], skip_m[l], w_out[l])
        h = rmsnorm(x, g_norm2[l]) * (1.0 + scale2) + shift2
        x = x + gate2 * swiglu(h, w_gu[l], w_down[l])
    return rmsnorm(x, g_final)
```

```python
import functools

import numpy as np
import jax
import jax.numpy as jnp
from jax import lax
from jax.experimental import pallas as pl
from jax.experimental.pallas import tpu as pltpu

F32 = jnp.float32
BF16 = jnp.bfloat16

D_MODEL = 1024
DEPTH = 2
HEAD_DIM = 64
ROPE_THETA = 500000.0
NSA_ROT_HALF = HEAD_DIM // 8
NORM_EPS = 1e-6

NSA_HEADS = 6
NSA_KV_HEADS = 2
NSA_REP = NSA_HEADS // NSA_KV_HEADS
CMP_LEN = 32
CMP_STRIDE = 16
SLC_LEN = 64
SLC_TOPK = 16
WINDOW = 512

MLA_HEADS = 6
MLA_Q_LORA = 256
MLA_KV_LORA = 128
MLA_NOPE = 64
MLA_ROPE = 32
MLA_V = 64

MLSTM_HEADS = 4
MLSTM_DIM = 64
MLSTM_CONV = 4
MLSTM_CHUNK = 64

WIDTH_A = NSA_HEADS * HEAD_DIM
WIDTH_B = MLA_HEADS * MLA_V
WIDTH_C = MLSTM_HEADS * MLSTM_DIM
FFN_HIDDEN = 2816
N_GATES = 3 * NSA_HEADS

LANES = 128
NEG = -1e30
VMEM_LIMIT = 56 * 1024 * 1024

TM_PROJ = 512
TQ_NSA = 128
TK_SLC = 256
TQ_MLA = 256
TK_MLA = 256
FFN_CHUNK = 256

SEG_Q, SEG_CK, SEG_CV, SEG_SK, SEG_SV, SEG_WK, SEG_WV = 0, 384, 512, 640, 768, 896, 1024
SEG_G, SEG_QL, SEG_KVL, SEG_KR, SEG_CX, SEG_MV, SEG_MO, SEG_IF = 1152, 1280, 1536, 1664, 1792, 2048, 2304, 2560
N_IN = 2688


def _params(*sem):
    return pltpu.CompilerParams(dimension_semantics=sem, vmem_limit_bytes=VMEM_LIMIT)


def _dot(a, b):
    return jnp.dot(a, b, preferred_element_type=F32)


def _dot_nt(a, b):
    return lax.dot_general(a, b, (((1,), (1,)), ((), ())), preferred_element_type=F32)


def _dot_tn(a, b):
    return lax.dot_general(a, b, (((0,), (0,)), ((), ())), preferred_element_type=F32)


def _split3(x):
    hi = x.astype(BF16)
    r1 = x - hi.astype(F32)
    mid = r1.astype(BF16)
    lo = (r1 - mid.astype(F32)).astype(BF16)
    return hi, mid, lo


def _rms(x, g):
    return x * lax.rsqrt(jnp.mean(x * x, axis=-1, keepdims=True) + NORM_EPS) * g


def _sigmoid(x):
    return 1.0 / (1.0 + jnp.exp(-x))


def _silu(x):
    return x * _sigmoid(x)


def _rope(x, cos, sin, half, x1_lane):
    xr = jnp.where(x1_lane, -pltpu.roll(x, LANES - half, 1), pltpu.roll(x, half, 1))
    return x * cos + xr * sin


def _ada_kernel(c_ref, w_ref, b_ref, o_ref):
    c = c_ref[...]
    ca = _silu(c).astype(BF16)
    o_ref[0] = _dot(ca, w_ref[0].astype(BF16)) + b_ref[0]


def _ada(c, w_ada, b_ada):
    L, D, N = w_ada.shape
    B = c.shape[0]
    tn = 1536
    return pl.pallas_call(
        _ada_kernel,
        grid=(L, N // tn),
        in_specs=[pl.BlockSpec((B, D), lambda l, j: (0, 0)),
                  pl.BlockSpec((1, D, tn), lambda l, j: (l, 0, j)),
                  pl.BlockSpec((1, 1, tn), lambda l, j: (l, 0, j))],
        out_specs=pl.BlockSpec((1, B, tn), lambda l, j: (l, 0, j)),
        out_shape=jax.ShapeDtypeStruct((L, B, N), F32),
        compiler_params=_params("parallel", "parallel"),
        name="ada_mod",
    )(c, w_ada, b_ada.reshape(L, 1, N))


def _rope_kernel(pos_ref, inv_ref, cn_ref, sn_ref, cm_ref, sm_ref):
    pos = pos_ref[...]
    ang_n = pos * inv_ref[0:1, :]
    ang_m = pos * inv_ref[1:2, :]
    cn_ref[...] = jnp.cos(ang_n)
    sn_ref[...] = jnp.sin(ang_n)
    cm_ref[...] = jnp.cos(ang_m)
    sm_ref[...] = jnp.sin(ang_m)


def _rope_tables(positions):
    T = positions.size
    inv_n = jnp.power(ROPE_THETA, -jnp.arange(0, 2 * NSA_ROT_HALF, 2, dtype=F32) / (2 * NSA_ROT_HALF))
    inv_m = jnp.power(ROPE_THETA, -jnp.arange(0, MLA_ROPE, 2, dtype=F32) / MLA_ROPE)
    z = lambda n: jnp.zeros((n,), F32)
    head_n = jnp.concatenate([inv_n, inv_n, z(HEAD_DIM - 2 * NSA_ROT_HALF)])
    lane_n = jnp.concatenate([head_n, head_n])
    lane_m = jnp.concatenate([z(MLA_NOPE), inv_m, inv_m, z(LANES - MLA_NOPE - MLA_ROPE)])
    inv = jnp.zeros((8, LANES), F32).at[0].set(lane_n).at[1].set(lane_m)
    posb = jnp.broadcast_to(positions.reshape(T, 1).astype(F32), (T, LANES))
    tm = 2048
    spec = pl.BlockSpec((tm, LANES), lambda i: (i, 0))
    return pl.pallas_call(
        _rope_kernel,
        grid=(T // tm,),
        in_specs=[spec, pl.BlockSpec((8, LANES), lambda i: (0, 0))],
        out_specs=[spec] * 4,
        out_shape=[jax.ShapeDtypeStruct((T, LANES), F32)] * 4,
        compiler_params=_params("parallel"),
        name="rope_tables",
    )(posb, inv)


def _inproj_kernel(x_ref, sh_ref, sc_ref, g1_ref, w_ref, cn_ref, sn_ref, cm_ref, sm_ref,
                   gq_ref, wuq_ref, gkv_ref, wkv_ref,
                   qa_ref, ckv_ref, sk_ref, sv_ref, wk_ref, wv_ref, gate_ref,
                   qm_ref, km_ref, vm_ref, cx_ref, mv_ref, mo_ref, cif_ref):
    x = x_ref[...]
    h = _rms(x, g1_ref[...]) * (1.0 + sc_ref[0]) + sh_ref[0]
    hb = h.astype(BF16)

    def seg(start, width):
        return _dot(hb, w_ref[:, start:start + width])

    lane = lax.broadcasted_iota(jnp.int32, (1, LANES), 1)
    x1_n = (lane % HEAD_DIM) < NSA_ROT_HALF
    x1_m = lane < MLA_NOPE + MLA_ROPE // 2
    cn, sn, cm, sm = cn_ref[...], sn_ref[...], cm_ref[...], sm_ref[...]
    rope_n = lambda t: _rope(t, cn, sn, NSA_ROT_HALF, x1_n)
    rope_m = lambda t: _rope(t, cm, sm, MLA_ROPE // 2, x1_m)

    q = seg(SEG_Q, WIDTH_A)
    for r in range(3):
        t = rope_n(q[:, LANES * r:LANES * (r + 1)]) * (HEAD_DIM ** -0.5)
        qa_ref[:, LANES * r:LANES * (r + 1)] = t.astype(BF16)
    ckv = seg(SEG_CK, 2 * LANES)
    ckv_ref[:, :LANES] = rope_n(ckv[:, :LANES]).astype(BF16)
    ckv_ref[:, LANES:] = ckv[:, LANES:].astype(BF16)
    skv = seg(SEG_SK, 2 * LANES)
    sk_ref[...] = rope_n(skv[:, :LANES]).astype(BF16)
    sv_ref[...] = skv[:, LANES:].astype(BF16)
    wkv = seg(SEG_WK, 2 * LANES)
    wk_ref[...] = rope_n(wkv[:, :LANES]).astype(BF16)
    wv_ref[...] = wkv[:, LANES:].astype(BF16)
    gate_ref[...] = _sigmoid(seg(SEG_G, LANES))

    ql = seg(SEG_QL, MLA_Q_LORA)
    qn = _rms(ql, gq_ref[...]).astype(BF16)
    qm = _dot(qn, wuq_ref[...])
    scale_m = (MLA_NOPE + MLA_ROPE) ** -0.5
    for hh in range(MLA_HEADS):
        t = rope_m(qm[:, LANES * hh:LANES * (hh + 1)]) * scale_m
        qm_ref[:, LANES * hh:LANES * (hh + 1)] = t.astype(BF16)
    kvl_kr = seg(SEG_KVL, 2 * LANES)
    kvn = _rms(kvl_kr[:, :LANES], gkv_ref[...]).astype(BF16)
    kr = rope_m(kvl_kr[:, LANES:])
    kv = _dot(kvn, wkv_ref[...])
    for hh in range(MLA_HEADS):
        km_ref[:, LANES * hh:LANES * (hh + 1)] = (kv[:, LANES * hh:LANES * (hh + 1)] + kr).astype(BF16)
    vm_ref[...] = kv[:, MLA_HEADS * LANES:].astype(BF16)

    cx_ref[...] = seg(SEG_CX, WIDTH_C)
    mv_ref[...] = seg(SEG_MV, WIDTH_C).astype(BF16)
    mo_ref[...] = seg(SEG_MO, WIDTH_C)
    cif_ref[...] = seg(SEG_IF, LANES)


def _inproj(x2, shift, scale, g1, w1, tabs, gq, wuq, gkv, wkv, S):
    T = x2.shape[0]
    tm = TM_PROJ
    per_b = S // tm
    row = lambda w: pl.BlockSpec((tm, w), lambda i: (i, 0))
    full = lambda a: pl.BlockSpec(a.shape, lambda i: (0,) * a.ndim)
    modspec = pl.BlockSpec((1, 1, D_MODEL), lambda i: (i // per_b, 0, 0))
    outs = [(WIDTH_A, BF16), (2 * LANES, BF16), (LANES, BF16), (LANES, BF16), (LANES, BF16), (LANES, BF16),
            (LANES, F32), (MLA_HEADS * LANES, BF16), (MLA_HEADS * LANES, BF16), (WIDTH_B, BF16),
            (WIDTH_C, F32), (WIDTH_C, BF16), (WIDTH_C, F32), (LANES, F32)]
    return pl.pallas_call(
        _inproj_kernel,
        grid=(T // tm,),
        in_specs=[row(D_MODEL), modspec, modspec, full(g1), full(w1)] + [row(LANES)] * 4
                 + [full(gq), full(wuq), full(gkv), full(wkv)],
        out_specs=[row(w) for w, _ in outs],
        out_shape=[jax.ShapeDtypeStruct((T, w), dt) for w, dt in outs],
        compiler_params=_params("parallel"),
        name="in_proj",
    )(x2, shift, scale, g1, w1, *tabs, gq, wuq, gkv, wkv)


def _compress_kernel(xk_ref, xv_ref, wk_ref, wv_ref, pos_ref, wkf_ref, wvf_ref, kc_ref, vc_ref):
    lane = lax.broadcasted_iota(jnp.int32, (LANES, LANES), 1)
    row = lax.broadcasted_iota(jnp.int32, (LANES, LANES), 0)
    pos = pos_ref[...].astype(BF16)

    def one(x_ref, w_ref, wf_ref, o_ref):
        const = _dot(pos, wf_ref[...].astype(BF16))
        const2 = jnp.concatenate([const, const], axis=1)
        res = []
        for g in range(NSA_KV_HEADS):
            r = _dot(x_ref[0, g], w_ref[...])
            nxt = pltpu.roll(pltpu.roll(r, LANES - 1, 0), HEAD_DIM, 1)
            res.append(r + nxt)
        both = jnp.where(lane < HEAD_DIM, res[0], pltpu.roll(res[1], HEAD_DIM, 1)) + const2
        o_ref[0] = jnp.where(row < LANES - 1, both, 0.0).astype(BF16)

    one(xk_ref, wk_ref, wkf_ref, kc_ref)
    one(xv_ref, wv_ref, wvf_ref, vc_ref)


def _compress(ckv, cmp_pos, w_cmp_k, w_cmp_v, B, S):
    ng = S // CMP_STRIDE
    x = ckv.reshape(B, ng, CMP_STRIDE, 2, NSA_KV_HEADS, HEAD_DIM).transpose(3, 0, 4, 1, 2, 5)
    x = x.reshape(2, B, NSA_KV_HEADS, ng, CMP_STRIDE * HEAD_DIM)
    half = CMP_STRIDE * HEAD_DIM

    def halves(w):
        return jnp.concatenate([w[:half], w[half:]], axis=1).astype(BF16)

    xspec = pl.BlockSpec((1, NSA_KV_HEADS, ng, half), lambda b: (b, 0, 0, 0))
    full = lambda a: pl.BlockSpec(a.shape, lambda b: (0,) * a.ndim)
    wk2, wv2 = halves(w_cmp_k), halves(w_cmp_v)
    posf = cmp_pos.reshape(1, CMP_LEN * HEAD_DIM)
    ospec = pl.BlockSpec((1, ng, LANES), lambda b: (b, 0, 0))
    return pl.pallas_call(
        _compress_kernel,
        grid=(B,),
        in_specs=[xspec, xspec, full(wk2), full(wv2), full(posf), full(w_cmp_k), full(w_cmp_v)],
        out_specs=[ospec, ospec],
        out_shape=[jax.ShapeDtypeStruct((B, ng, LANES), BF16)] * 2,
        compiler_params=_params("parallel"),
        name="nsa_compress",
    )(x[0], x[1], wk2, wv2, posf, w_cmp_k, w_cmp_v)


def _nsa_kernel(q_ref, g_ref, kc_ref, vc_ref, sk_ref, sv_ref, wk_ref, wv_ref, o_ref,
                qaug_scr, m_scr, l_scr, acc_scr):
    tq = TQ_NSA
    rows = NSA_HEADS * tq
    t0 = pl.program_id(1) * tq
    n_slc = SLC_TOPK * 2

    lane1 = lax.broadcasted_iota(jnp.int32, (tq, LANES), 1)
    g0_lane = lane1 < HEAD_DIM
    tiles = [q_ref[:, LANES * r:LANES * (r + 1)] for r in range(NSA_REP)]
    zero = jnp.zeros_like(tiles[0])
    q6 = jnp.concatenate([jnp.where(g0_lane, t, zero) for t in tiles]
                         + [jnp.where(g0_lane, zero, t) for t in tiles], axis=0)

    def tq_of(width):
        r = lax.broadcasted_iota(jnp.int32, (rows, width), 0)
        return t0 + (r & (tq - 1))

    s = _dot_nt(q6, kc_ref[0])
    ncol = lax.broadcasted_iota(jnp.int32, (rows, LANES), 1)
    vis = ncol * CMP_STRIDE + (CMP_LEN - 1) <= tq_of(LANES)
    s = jnp.where(vis, s, NEG)
    e = jnp.where(vis, jnp.exp(s - jnp.max(s, axis=-1, keepdims=True)), 0.0)
    den = jnp.sum(e, axis=-1, keepdims=True)
    p = e / jnp.where(den > 0.0, den, 1.0)
    o_cmp = _dot(p.astype(BF16), vc_ref[0])

    jr = lax.broadcasted_iota(jnp.int32, (n_slc, LANES), 0)
    nc = lax.broadcasted_iota(jnp.int32, (n_slc, LANES), 1)
    ovl = ((nc * CMP_STRIDE < jr * SLC_LEN + SLC_LEN) & (nc * CMP_STRIDE + CMP_LEN > jr * SLC_LEN)
           & (nc < LANES - 1))
    ovl = jnp.where(ovl, 1.0, 0.0).astype(BF16)
    jq = lax.broadcasted_iota(jnp.int32, (n_slc, tq), 0)
    tl = t0 + lax.broadcasted_iota(jnp.int32, (n_slc, tq), 1)
    cur = tl // SLC_LEN
    forced = (jq == 0) | (jq == cur) | (jq == cur - 1)
    future = jq * SLC_LEN > tl
    bias_t = []
    for g in range(NSA_KV_HEADS):
        pg = p[(3 * g) * tq:(3 * g + 1) * tq] + p[(3 * g + 1) * tq:(3 * g + 2) * tq] + p[(3 * g + 2) * tq:(3 * g + 3) * tq]
        imp = sum(_dot_nt(ovl, part) for part in _split3(pg))
        imp = jnp.where(forced, jnp.inf, imp)
        imp = jnp.where(future, -jnp.inf, imp)
        rank = jnp.zeros((n_slc, tq), F32)
        for jp in range(n_slc):
            rv = imp[jp:jp + 1, :]
            ahead = jnp.where(rv > imp, 1.0, jnp.where((rv == imp) & (jq > jp), 1.0, 0.0))
            rank = rank + ahead
        bias_t.append(jnp.where(rank < float(SLC_TOPK), 0.0, NEG))
    bias = jnp.concatenate(bias_t + [jnp.zeros((LANES - 2 * n_slc, tq), F32)], axis=0).T
    b0 = jnp.where(lane1 < n_slc, bias, 0.0).astype(BF16)
    b1 = jnp.where((lane1 >= n_slc) & (lane1 < 2 * n_slc), bias, 0.0).astype(BF16)
    qaug_scr[:, :LANES] = q6
    qaug_scr[:, LANES:] = jnp.concatenate([b0] * NSA_REP + [b1] * NSA_REP, axis=0)

    m_scr[...] = jnp.full((rows, 1), NEG, F32)
    l_scr[...] = jnp.zeros((rows, 1), F32)
    acc_scr[...] = jnp.zeros((rows, LANES), F32)
    tq_k = tq_of(TK_SLC)
    kl = lax.broadcasted_iota(jnp.int32, (rows, TK_SLC), 1)

    def slc_step(kt, carry):
        k0 = pl.multiple_of(kt * TK_SLC, TK_SLC)
        sc = _dot_nt(qaug_scr[...], sk_ref[pl.ds(k0, TK_SLC), :])
        sc = jnp.where(k0 + kl <= tq_k, sc, NEG)
        m_old = m_scr[...]
        m_new = jnp.maximum(m_old, jnp.max(sc, axis=-1, keepdims=True))
        alpha = jnp.exp(m_old - m_new)
        pe = jnp.exp(sc - m_new)
        l_scr[...] = alpha * l_scr[...] + jnp.sum(pe, axis=-1, keepdims=True)
        acc_scr[...] = alpha * acc_scr[...] + _dot(pe.astype(BF16), sv_ref[pl.ds(k0, TK_SLC), :])
        m_scr[...] = m_new
        return carry

    lax.fori_loop(0, t0 // TK_SLC + 1, slc_step, 0)
    o_slc = acc_scr[...] / l_scr[...]

    span = WINDOW + tq
    start = pl.multiple_of(jnp.maximum(t0 - WINDOW, 0), tq)
    sw = _dot_nt(q6, wk_ref[pl.ds(start, span), :])
    diff = tq_of(span) - (start + lax.broadcasted_iota(jnp.int32, (rows, span), 1))
    wvis = (diff >= 0) & (diff < WINDOW)
    sw = jnp.where(wvis, sw, NEG)
    ew = jnp.exp(sw - jnp.max(sw, axis=-1, keepdims=True))
    o_win = _dot(ew.astype(BF16), wv_ref[pl.ds(start, span), :]) / jnp.sum(ew, axis=-1, keepdims=True)

    gates = g_ref[...]
    mixed = []
    for hd in range(NSA_HEADS):
        blk = slice(hd * tq, (hd + 1) * tq)
        mixed.append(gates[:, 3 * hd:3 * hd + 1] * o_cmp[blk] + gates[:, 3 * hd + 1:3 * hd + 2] * o_slc[blk]
                     + gates[:, 3 * hd + 2:3 * hd + 3] * o_win[blk])
    for r in range(NSA_REP):
        o_ref[:, LANES * r:LANES * (r + 1)] = jnp.where(g0_lane, mixed[r], mixed[NSA_REP + r])


def _nsa(qa, gates, kc, vc, sk_aug, sv, wk, wv, B, S):
    tq = TQ_NSA
    nq = S // tq
    rows = NSA_HEADS * tq
    qrow = lambda w: pl.BlockSpec((tq, w), lambda b, i: (b * nq + i, 0))
    seq = lambda w: pl.BlockSpec((S, w), lambda b, i: (b, 0))
    cspec = pl.BlockSpec((1, LANES, LANES), lambda b, i: (b, 0, 0))
    return pl.pallas_call(
        _nsa_kernel,
        grid=(B, nq),
        in_specs=[qrow(WIDTH_A), qrow(LANES), cspec, cspec, seq(2 * LANES), seq(LANES), seq(LANES), seq(LANES)],
        out_specs=qrow(WIDTH_A),
        out_shape=jax.ShapeDtypeStruct((B * S, WIDTH_A), F32),
        scratch_shapes=[pltpu.VMEM((rows, 2 * LANES), BF16), pltpu.VMEM((rows, 1), F32),
                        pltpu.VMEM((rows, 1), F32), pltpu.VMEM((rows, LANES), F32)],
        compiler_params=_params("parallel", "arbitrary"),
        name="nsa_attention",
    )(qa, gates, kc, vc, sk_aug, sv, wk, wv)


def _mla_kernel(q_ref, k_ref, v_ref, o_ref, m_scr, l_scr, acc_scr):
    tq, tk = TQ_MLA, TK_MLA
    t0 = pl.program_id(2) * tq
    tq_k = t0 + lax.broadcasted_iota(jnp.int32, (tq, tk), 0)
    kl = lax.broadcasted_iota(jnp.int32, (tq, tk), 1)
    m_scr[...] = jnp.full((2, tq, 1), NEG, F32)
    l_scr[...] = jnp.zeros((2, tq, 1), F32)
    acc_scr[...] = jnp.zeros((2, tq, LANES), F32)

    def step(kt, carry):
        k0 = pl.multiple_of(kt * tk, tk)
        vis = k0 + kl <= tq_k
        v = v_ref[pl.ds(k0, tk), :]
        for hh in range(2):
            sc = _dot_nt(q_ref[:, LANES * hh:LANES * (hh + 1)], k_ref[pl.ds(k0, tk), LANES * hh:LANES * (hh + 1)])
            sc = jnp.where(vis, sc, NEG)
            m_old = m_scr[hh]
            m_new = jnp.maximum(m_old, jnp.max(sc, axis=-1, keepdims=True))
            alpha = jnp.exp(m_old - m_new)
            pe = jnp.exp(sc - m_new)
            l_scr[hh] = alpha * l_scr[hh] + jnp.sum(pe, axis=-1, keepdims=True)
            acc_scr[hh] = alpha * acc_scr[hh] + _dot(pe.astype(BF16), v)
            m_scr[hh] = m_new
        return carry

    lax.fori_loop(0, (t0 + tq - 1) // tk + 1, step, 0)
    lane = lax.broadcasted_iota(jnp.int32, (tq, LANES), 1)
    o_ref[...] = jnp.where(lane < MLA_V, acc_scr[0] / l_scr[0], acc_scr[1] / l_scr[1])


def _mla(qm, km, vm, B, S):
    tq = TQ_MLA
    nq = S // tq
    pairs = MLA_HEADS // 2
    return pl.pallas_call(
        _mla_kernel,
        grid=(B, pairs, nq),
        in_specs=[pl.BlockSpec((tq, 2 * LANES), lambda b, p, i: (b * nq + i, p)),
                  pl.BlockSpec((S, 2 * LANES), lambda b, p, i: (b, p)),
                  pl.BlockSpec((S, LANES), lambda b, p, i: (b, p))],
        out_specs=pl.BlockSpec((tq, LANES), lambda b, p, i: (b * nq + i, p)),
        out_shape=jax.ShapeDtypeStruct((B * S, WIDTH_B), F32),
        scratch_shapes=[pltpu.VMEM((2, tq, 1), F32), pltpu.VMEM((2, tq, 1), F32), pltpu.VMEM((2, tq, LANES), F32)],
        compiler_params=_params("parallel", "parallel", "arbitrary"),
        name="mla_attention",
    )(qm, km, vm)


def _mlstm_kernel(cx_ref, v_ref, o_ref, gc_ref, gr_ref, cw_ref, cb_ref, wq_ref, wk_ref,
                  bc_ref, br_ref, gmh_ref, skip_ref, y_ref, xc_scr, q_scr, k_scr):
    S = cx_ref.shape[0]
    L = MLSTM_CHUNK
    d = MLSTM_DIM
    pairs = MLSTM_HEADS // 2

    x = cx_ref[...]
    rowi = lax.broadcasted_iota(jnp.int32, (S, WIDTH_C), 0)
    conv = x * cw_ref[MLSTM_CONV - 1:MLSTM_CONV, :]
    for back in range(1, MLSTM_CONV):
        shifted = jnp.where(rowi >= back, pltpu.roll(x, back, 0), 0.0)
        conv = conv + shifted * cw_ref[MLSTM_CONV - 1 - back:MLSTM_CONV - back, :]
    xc = _silu(conv + cb_ref[...])
    xc_scr[...] = xc
    xcb = xc.astype(BF16)
    q_scr[...] = _dot(xcb, wq_ref[...]).astype(BF16)
    k_scr[...] = _dot(xcb, wk_ref[...]).astype(BF16)

    ri = lax.broadcasted_iota(jnp.int32, (L, L), 0)
    ci = lax.broadcasted_iota(jnp.int32, (L, L), 1)
    causal = ci <= ri
    tril = jnp.where(causal, 1.0, 0.0).astype(BF16)
    triu = jnp.where(ri <= ci, 1.0, 0.0).astype(BF16)
    lane = lax.broadcasted_iota(jnp.int32, (L, LANES), 1)
    h0_lane = lane < d
    h0_row1 = lax.broadcasted_iota(jnp.int32, (1, LANES), 1) < d
    blk = (lax.broadcasted_iota(jnp.int32, (LANES, LANES), 0) // d) == (lax.broadcasted_iota(jnp.int32, (LANES, LANES), 1) // d)
    rowsel = lax.broadcasted_iota(jnp.int32, (LANES, 1), 0) < d
    ones_row = jnp.ones((1, LANES), F32)

    def log_sig(z):
        return jnp.minimum(z, 0.0) - jnp.log1p(jnp.exp(-jnp.abs(z)))

    def chunk(c, carry):
        cs, ns, ms = carry
        r0 = pl.multiple_of(c * L, L)
        gcol = gc_ref[pl.ds(r0, L), :] + bc_ref[...]
        ig_col = gcol
        lf_col = log_sig(gcol)
        b_col = sum(_dot(tril, part) for part in _split3(lf_col))
        grow = gr_ref[0, c] + br_ref[...]
        lf_row = log_sig(grow)
        b_row = sum(_dot(part, triu) for part in _split3(lf_row))
        new_c, new_n, new_m_parts, outs = [], [], [], []
        for pr in range(pairs):
            qp = q_scr[pl.ds(r0, L), LANES * pr:LANES * (pr + 1)]
            kp = k_scr[pl.ds(r0, L), LANES * pr:LANES * (pr + 1)]
            vp = v_ref[pl.ds(r0, L), LANES * pr:LANES * (pr + 1)]
            kpf = kp.astype(F32)
            qc = _dot(qp, cs[pr].astype(BF16))
            qn_full = qp.astype(F32) * ns[pr]
            num_p, den_p, mt_p, wl_p = [], [], [], []
            e_old_l, e_loc_l, m_new_l = [], [], []
            for hh in range(2):
                hd = 2 * pr + hh
                igc = ig_col[:, hd:hd + 1]
                bc = b_col[:, MLSTM_HEADS + hd:MLSTM_HEADS + hd + 1]
                igr = grow[hd:hd + 1, :]
                brw = b_row[MLSTM_HEADS + hd:MLSTM_HEADS + hd + 1, :]
                b_last = bc[L - 1:L, :]
                m_prev = ms[:, hd:hd + 1]
                a_col = b_last - bc + igc
                m_loc = jnp.max(a_col, axis=0, keepdims=True)
                wl_p.append(jnp.exp(a_col - m_loc))
                m_new = jnp.maximum(b_last + m_prev, m_loc)
                e_old_l.append(jnp.exp(b_last + m_prev - m_new))
                e_loc_l.append(jnp.exp(m_loc - m_new))
                m_new_l.append(m_new)
                dmat = jnp.where(causal, bc - brw + igr, -jnp.inf)
                gl = bc + m_prev
                m_t = jnp.maximum(gl, jnp.max(dmat, axis=-1, keepdims=True))
                hsel = h0_lane if hh == 0 else lane >= d
                qh = jnp.where(hsel, qp, jnp.zeros_like(qp))
                sw = _dot_nt(qh, kp) * jnp.exp(dmat - m_t)
                inter = jnp.exp(gl - m_t)
                qn = jnp.sum(jnp.where(hsel, qn_full, 0.0), axis=-1, keepdims=True)
                num_p.append(inter * qc + _dot(sw.astype(BF16), vp))
                den_p.append(inter * qn + jnp.sum(sw, axis=-1, keepdims=True))
                mt_p.append(m_t)
            num = jnp.where(h0_lane, num_p[0], num_p[1])
            den = jnp.where(h0_lane, den_p[0], den_p[1])
            m_t2 = jnp.where(h0_lane, mt_p[0], mt_p[1])
            outs.append(num / jnp.maximum(jnp.abs(den), jnp.exp(-m_t2)))
            wl2 = jnp.where(h0_lane, wl_p[0], wl_p[1])
            kwf = kpf * wl2
            c_loc = jnp.where(blk, _dot_tn(kwf.astype(BF16), vp), 0.0)
            n_loc = jnp.sum(kwf, axis=0, keepdims=True)
            e_old_row = jnp.where(h0_row1, e_old_l[0], e_old_l[1])
            e_loc_row = jnp.where(h0_row1, e_loc_l[0], e_loc_l[1])
            e_old_colv = jnp.where(rowsel, e_old_l[0], e_old_l[1])
            e_loc_colv = jnp.where(rowsel, e_loc_l[0], e_loc_l[1])
            new_c.append(e_old_colv * cs[pr] + e_loc_colv * c_loc)
            new_n.append(e_old_row * ns[pr] + e_loc_row * n_loc)
            new_m_parts += m_new_l
        hcat = jnp.concatenate(outs, axis=1)
        hg = _sigmoid(o_ref[pl.ds(r0, L), :]) * hcat
        ys = []
        for hd in range(MLSTM_HEADS):
            part = hg[:, LANES * (hd // 2):LANES * (hd // 2 + 1)]
            hs = h0_lane if hd % 2 == 0 else lane >= d
            mu = jnp.sum(jnp.where(hs, part, 0.0), axis=-1, keepdims=True) / d
            cen = jnp.where(hs, part - mu, 0.0)
            var = jnp.sum(cen * cen, axis=-1, keepdims=True) / d
            ys.append(cen * lax.rsqrt(var + NORM_EPS))
        hn = jnp.concatenate([ys[0] + ys[1], ys[2] + ys[3]], axis=1)
        y_ref[pl.ds(r0, L), :] = hn * gmh_ref[...] + skip_ref[...] * xc_scr[pl.ds(r0, L), :]
        lane_m = lax.broadcasted_iota(jnp.int32, (1, LANES), 1)
        m_next = jnp.zeros((1, LANES), F32)
        for hd in range(MLSTM_HEADS):
            m_next = jnp.where(lane_m == hd, new_m_parts[hd] * ones_row, m_next)
        return tuple(new_c), tuple(new_n), m_next

    init = (tuple(jnp.zeros((LANES, LANES), F32) for _ in range(pairs)),
            tuple(jnp.zeros((1, LANES), F32) for _ in range(pairs)),
            jnp.zeros((1, LANES), F32))
    lax.fori_loop(0, S // L, chunk, init)


def _mlstm(cx, mv, mo, cif, conv_w, conv_b, w_q_m, w_k_m, b_igate, b_fgate, g_mh, skip_m, B, S):
    def blockdiag(w, scale):
        out = jnp.zeros((WIDTH_C, WIDTH_C), F32)
        for hd in range(MLSTM_HEADS):
            out = out.at[hd * MLSTM_DIM:(hd + 1) * MLSTM_DIM, hd * MLSTM_DIM:(hd + 1) * MLSTM_DIM].set(w[hd] * scale)
        return out.astype(BF16)

    wq = blockdiag(w_q_m, MLSTM_DIM ** -0.5)
    wk = blockdiag(w_k_m, 1.0)
    bias8 = jnp.concatenate([b_igate, b_fgate])
    bc = jnp.zeros((1, LANES), F32).at[0, :2 * MLSTM_HEADS].set(bias8)
    br = jnp.broadcast_to(bias8[:, None], (2 * MLSTM_HEADS, MLSTM_CHUNK))
    nc = S // MLSTM_CHUNK
    gr = cif[:, :2 * MLSTM_HEADS].reshape(B, nc, MLSTM_CHUNK, 2 * MLSTM_HEADS).transpose(0, 1, 3, 2)
    seq = lambda w: pl.BlockSpec((S, w), lambda b: (b, 0))
    full = lambda a: pl.BlockSpec(a.shape, lambda b: (0,) * a.ndim)
    row = lambda a: a.reshape(1, -1)
    args = [conv_w, row(conv_b), wq, wk, bc, br, row(g_mh), row(skip_m)]
    return pl.pallas_call(
        _mlstm_kernel,
        grid=(B,),
        in_specs=[seq(WIDTH_C), seq(WIDTH_C), seq(WIDTH_C), seq(LANES),
                  pl.BlockSpec((1, nc, 2 * MLSTM_HEADS, MLSTM_CHUNK), lambda b: (b, 0, 0, 0))] + [full(a) for a in args],
        out_specs=seq(WIDTH_C),
        out_shape=jax.ShapeDtypeStruct((B * S, WIDTH_C), F32),
        scratch_shapes=[pltpu.VMEM((S, WIDTH_C), F32), pltpu.VMEM((S, WIDTH_C), BF16), pltpu.VMEM((S, WIDTH_C), BF16)],
        compiler_params=_params("parallel"),
        name="mlstm_mixer",
    )(cx, mv, mo, cif, gr, *args)


def _outproj_kernel(ya_ref, yb_ref, yc_ref, x_ref, gate_ref, ga_ref, gb_ref, w_ref, o_ref):
    a = _rms(ya_ref[...], ga_ref[...]).astype(BF16)
    b = _rms(yb_ref[...], gb_ref[...]).astype(BF16)
    c = yc_ref[...].astype(BF16)
    y = _dot(a, w_ref[0:WIDTH_A, :]) + _dot(b, w_ref[WIDTH_A:WIDTH_A + WIDTH_B, :]) + _dot(c, w_ref[WIDTH_A + WIDTH_B:, :])
    o_ref[...] = x_ref[...] + gate_ref[0] * y


def _outproj(ya, yb, yc, x2, gate, ga, gb, w_out, S):
    T = x2.shape[0]
    tm = TM_PROJ
    per_b = S // tm
    row = lambda w: pl.BlockSpec((tm, w), lambda i: (i, 0))
    full = lambda a: pl.BlockSpec(a.shape, lambda i: (0,) * a.ndim)
    return pl.pallas_call(
        _outproj_kernel,
        grid=(T // tm,),
        in_specs=[row(WIDTH_A), row(WIDTH_B), row(WIDTH_C), row(D_MODEL),
                  pl.BlockSpec((1, 1, D_MODEL), lambda i: (i // per_b, 0, 0)), full(ga), full(gb), full(w_out)],
        out_specs=row(D_MODEL),
        out_shape=jax.ShapeDtypeStruct((T, D_MODEL), F32),
        compiler_params=_params("parallel"),
        name="out_proj",
    )(ya, yb, yc, x2, gate, ga, gb, w_out)


def _ffn_kernel(x_ref, sh_ref, sc_ref, gate_ref, g2_ref, wgu_ref, wd_ref, gf_ref, o_ref, *, final_norm):
    x = x_ref[...]
    h = (_rms(x, g2_ref[...]) * (1.0 + sc_ref[0]) + sh_ref[0]).astype(BF16)
    acc = jnp.zeros(x.shape, F32)
    for j in range(FFN_HIDDEN // FFN_CHUNK):
        gu = _dot(h, wgu_ref[:, 2 * FFN_CHUNK * j:2 * FFN_CHUNK * (j + 1)])
        act = (_silu(gu[:, :FFN_CHUNK]) * gu[:, FFN_CHUNK:]).astype(BF16)
        acc = acc + _dot(act, wd_ref[FFN_CHUNK * j:FFN_CHUNK * (j + 1), :])
    y = x + gate_ref[0] * acc
    if final_norm:
        y = _rms(y, gf_ref[...])
    o_ref[...] = y


def _ffn(x2, shift, scale, gate, g2, wgu, wd, gf, S, final_norm):
    T = x2.shape[0]
    tm = TM_PROJ
    per_b = S // tm
    row = pl.BlockSpec((tm, D_MODEL), lambda i: (i, 0))
    full = lambda a: pl.BlockSpec(a.shape, lambda i: (0,) * a.ndim)
    modspec = pl.BlockSpec((1, 1, D_MODEL), lambda i: (i // per_b, 0, 0))
    return pl.pallas_call(
        functools.partial(_ffn_kernel, final_norm=final_norm),
        grid=(T // tm,),
        in_specs=[row, modspec, modspec, modspec, full(g2), full(wgu), full(wd), full(gf)],
        out_specs=row,
        out_shape=jax.ShapeDtypeStruct((T, D_MODEL), F32),
        compiler_params=_params("parallel"),
        name="ffn_final" if final_norm else "ffn",
    )(x2, shift, scale, gate, g2, wgu, wd, gf)


def _head_tile_perm():
    idx = []
    for r in range(NSA_REP):
        idx += list(range(HEAD_DIM * r, HEAD_DIM * (r + 1)))
        idx += list(range(HEAD_DIM * (NSA_REP + r), HEAD_DIM * (NSA_REP + r + 1)))
    return np.asarray(idx, np.int32)


def _in_cols():
    cols = np.full((N_IN,), -1, np.int64)
    cols[SEG_Q:SEG_Q + WIDTH_A] = _head_tile_perm()
    off = WIDTH_A
    for seg in (SEG_CK, SEG_CV, SEG_SK, SEG_SV, SEG_WK, SEG_WV):
        cols[seg:seg + LANES] = off + np.arange(LANES)
        off += LANES
    cols[SEG_G:SEG_G + N_GATES] = off + np.arange(N_GATES)
    off += N_GATES
    cols[SEG_QL:SEG_QL + MLA_Q_LORA] = off + np.arange(MLA_Q_LORA)
    off += MLA_Q_LORA
    cols[SEG_KVL:SEG_KVL + MLA_KV_LORA] = off + np.arange(MLA_KV_LORA)
    off += MLA_KV_LORA
    cols[SEG_KR + MLA_NOPE:SEG_KR + MLA_NOPE + MLA_ROPE] = off + np.arange(MLA_ROPE)
    off += MLA_ROPE
    for seg in (SEG_CX, SEG_MV, SEG_MO):
        cols[seg:seg + WIDTH_C] = off + np.arange(WIDTH_C)
        off += WIDTH_C
    cols[SEG_IF:SEG_IF + 2 * MLSTM_HEADS] = off + np.arange(2 * MLSTM_HEADS)
    return cols


def _gather_cols(w, cols):
    g = jnp.take(w, jnp.asarray(np.maximum(cols, 0), jnp.int32), axis=1)
    return jnp.where(jnp.asarray(cols >= 0)[None, :], g, 0.0)


def _layer_weights(l, w_in, w_uq, w_ukv, w_out, w_gu, w_down, g_out_a):
    w1 = _gather_cols(w_in[l], _in_cols()).astype(BF16)
    cq = np.full((MLA_HEADS * LANES,), -1, np.int64)
    ck = np.full((MLA_HEADS * LANES,), -1, np.int64)
    cv = np.zeros((WIDTH_B,), np.int64)
    dq = MLA_NOPE + MLA_ROPE
    dkv = MLA_NOPE + MLA_V
    for hd in range(MLA_HEADS):
        cq[LANES * hd:LANES * hd + dq] = dq * hd + np.arange(dq)
        ck[LANES * hd:LANES * hd + MLA_NOPE] = dkv * hd + np.arange(MLA_NOPE)
        cv[MLA_V * hd:MLA_V * (hd + 1)] = dkv * hd + MLA_NOPE + np.arange(MLA_V)
    wuq = _gather_cols(w_uq[l], cq).astype(BF16)
    wkv = jnp.concatenate([_gather_cols(w_ukv[l], ck), _gather_cols(w_ukv[l], cv)], axis=1).astype(BF16)
    perm = _head_tile_perm()
    wo = jnp.concatenate([w_out[l][perm], w_out[l][WIDTH_A:]], axis=0).astype(BF16)
    ga = g_out_a[l][perm].reshape(1, WIDTH_A)
    nch = FFN_HIDDEN // FFN_CHUNK
    wgu = w_gu[l].reshape(D_MODEL, 2, nch, FFN_CHUNK).transpose(0, 2, 1, 3).reshape(D_MODEL, 2 * FFN_HIDDEN).astype(BF16)
    wd = w_down[l].astype(BF16)
    return w1, wuq, wkv, wo, ga, wgu, wd


def kernel(x, c, positions, g_norm1, g_norm2, w_ada, b_ada, w_in, cmp_pos, w_cmp_k, w_cmp_v, g_out_a, g_q_lora, w_uq, g_kv_lora, w_ukv, g_out_b, conv_w, conv_b, w_q_m, w_k_m, b_igate, b_fgate, g_mh, skip_m, w_out, w_gu, w_down, g_final):
    B, S, D = x.shape
    T = B * S
    x2 = x.reshape(T, D)
    tabs = _rope_tables(positions)
    mod = _ada(c, w_ada, b_ada)
    blk_id = jnp.arange(S, dtype=jnp.int32) // SLC_LEN
    onehot = (blk_id[:, None] == jnp.arange(2 * SLC_TOPK, dtype=jnp.int32)[None, :]).astype(BF16)
    code = jnp.concatenate([onehot, onehot, jnp.zeros((S, LANES - 4 * SLC_TOPK), BF16)], axis=1)
    code = jnp.broadcast_to(code[None], (B, S, LANES)).reshape(T, LANES)
    row = lambda v: v.reshape(1, -1)
    for l in range(DEPTH):
        w1, wuq, wkv, wo, ga, wgu, wd = _layer_weights(l, w_in, w_uq, w_ukv, w_out, w_gu, w_down, g_out_a)
        m6 = [mod[l, :, D * i:D * (i + 1)].reshape(B, 1, D) for i in range(6)]
        shift1, scale1, gate1, shift2, scale2, gate2 = m6
        (qa, ckv, sk, sv, wk, wv, gates, qm, km, vm, cx, mv, mo, cif) = _inproj(
            x2, shift1, scale1, row(g_norm1[l]), w1, tabs, row(g_q_lora[l]), wuq, row(g_kv_lora[l]), wkv, S)
        kc, vc = _compress(ckv, cmp_pos[l], w_cmp_k[l], w_cmp_v[l], B, S)
        sk_aug = jnp.concatenate([sk, code], axis=1)
        ya = _nsa(qa, gates, kc, vc, sk_aug, sv, wk, wv, B, S)
        yb = _mla(qm, km, vm, B, S)
        yc = _mlstm(cx, mv, mo, cif, conv_w[l], conv_b[l], w_q_m[l], w_k_m[l], b_igate[l], b_fgate[l],
                    g_mh[l], skip_m[l], B, S)
        x2 = _outproj(ya, yb, yc, x2, gate1, ga, row(g_out_b[l]), wo, S)
        x2 = _ffn(x2, shift2, scale2, gate2, row(g_norm2[l]), wgu, wd, row(g_final), S, final_norm=(l == DEPTH - 1))
    return x2.reshape(B, S, D)
```

```python
import functools

import numpy as np
import jax
import jax.numpy as jnp
from jax import lax
from jax.experimental import pallas as pl
from jax.experimental.pallas import tpu as pltpu

F32 = jnp.float32
BF16 = jnp.bfloat16

D_MODEL = 1024
DEPTH = 2
HEAD_DIM = 64
ROPE_THETA = 500000.0
NSA_ROT_HALF = HEAD_DIM // 8
NORM_EPS = 1e-6

NSA_HEADS = 6
NSA_KV_HEADS = 2
NSA_REP = NSA_HEADS // NSA_KV_HEADS
CMP_LEN = 32
CMP_STRIDE = 16
SLC_LEN = 64
SLC_TOPK = 16
WINDOW = 512

MLA_HEADS = 6
MLA_Q_LORA = 256
MLA_KV_LORA = 128
MLA_NOPE = 64
MLA_ROPE = 32
MLA_V = 64

MLSTM_HEADS = 4
MLSTM_DIM = 64
MLSTM_CONV = 4
MLSTM_CHUNK = 64

WIDTH_A = NSA_HEADS * HEAD_DIM
WIDTH_B = MLA_HEADS * MLA_V
WIDTH_C = MLSTM_HEADS * MLSTM_DIM
FFN_HIDDEN = 2816
N_GATES = 3 * NSA_HEADS
GATE_ROWS = 24

LANES = 128
NEG = -1e30
LOG2E = 1.4426950408889634
VMEM_LIMIT = 56 * 1024 * 1024

TM_PROJ = 512
TQ_NSA = 128
TK_SLC = 256
TQ_MLA = 256
FFN_CHUNK = 256
MXU_LOOKAHEAD = 4

SEG_Q, SEG_CK, SEG_CV, SEG_SK, SEG_SV, SEG_WK, SEG_WV = 0, 384, 512, 640, 768, 896, 1024
SEG_G, SEG_QL, SEG_KVL, SEG_KR, SEG_CX, SEG_MV, SEG_MO, SEG_IF = 1152, 1280, 1536, 1664, 1792, 2048, 2304, 2560
N_IN = 2688


def _params(*sem):
    return pltpu.CompilerParams(dimension_semantics=sem, vmem_limit_bytes=VMEM_LIMIT)


def _dot(a, b):
    return jnp.dot(a, b, preferred_element_type=F32)


def _dot_nt(a, b):
    return lax.dot_general(a, b, (((1,), (1,)), ((), ())), preferred_element_type=F32)


def _dot_tn(a, b):
    return lax.dot_general(a, b, (((0,), (0,)), ((), ())), preferred_element_type=F32)


def _split3(x):
    hi = x.astype(BF16)
    r1 = x - hi.astype(F32)
    mid = r1.astype(BF16)
    lo = (r1 - mid.astype(F32)).astype(BF16)
    return hi, mid, lo


def _rms(x, g):
    return x * lax.rsqrt(jnp.mean(x * x, axis=-1, keepdims=True) + NORM_EPS) * g


def _sigmoid(x):
    return 1.0 / (1.0 + jnp.exp(-x))


def _silu(x):
    return x * _sigmoid(x)


def _rope(x, cos, sin, half, x1_lane):
    xr = jnp.where(x1_lane, -pltpu.roll(x, LANES - half, 1), pltpu.roll(x, half, 1))
    return x * cos + xr * sin


def _ada_kernel(c_ref, w_ref, b_ref, o_ref):
    c = c_ref[...]
    ca = _silu(c).astype(BF16)
    o_ref[0] = _dot(ca, w_ref[0].astype(BF16)) + b_ref[0]


def _ada(c, w_ada, b_ada):
    L, D, N = w_ada.shape
    B = c.shape[0]
    tn = 1536
    return pl.pallas_call(
        _ada_kernel,
        grid=(L, N // tn),
        in_specs=[pl.BlockSpec((B, D), lambda l, j: (0, 0)),
                  pl.BlockSpec((1, D, tn), lambda l, j: (l, 0, j)),
                  pl.BlockSpec((1, 1, tn), lambda l, j: (l, 0, j))],
        out_specs=pl.BlockSpec((1, B, tn), lambda l, j: (l, 0, j)),
        out_shape=jax.ShapeDtypeStruct((L, B, N), F32),
        compiler_params=_params("parallel", "parallel"),
        name="ada_mod",
    )(c, w_ada, b_ada.reshape(L, 1, N))


def _rope_kernel(pos_ref, inv_ref, cn_ref, sn_ref, cm_ref, sm_ref):
    pos = pos_ref[...]
    ang_n = pos * inv_ref[0:1, :]
    ang_m = pos * inv_ref[1:2, :]
    cn_ref[...] = jnp.cos(ang_n)
    sn_ref[...] = jnp.sin(ang_n)
    cm_ref[...] = jnp.cos(ang_m)
    sm_ref[...] = jnp.sin(ang_m)


def _rope_tables(positions):
    T = positions.size
    inv_n = jnp.power(ROPE_THETA, -jnp.arange(0, 2 * NSA_ROT_HALF, 2, dtype=F32) / (2 * NSA_ROT_HALF))
    inv_m = jnp.power(ROPE_THETA, -jnp.arange(0, MLA_ROPE, 2, dtype=F32) / MLA_ROPE)
    z = lambda n: jnp.zeros((n,), F32)
    head_n = jnp.concatenate([inv_n, inv_n, z(HEAD_DIM - 2 * NSA_ROT_HALF)])
    lane_n = jnp.concatenate([head_n, head_n])
    lane_m = jnp.concatenate([z(MLA_NOPE), inv_m, inv_m, z(LANES - MLA_NOPE - MLA_ROPE)])
    inv = jnp.zeros((8, LANES), F32).at[0].set(lane_n).at[1].set(lane_m)
    posb = jnp.broadcast_to(positions.reshape(T, 1).astype(F32), (T, LANES))
    tm = 2048
    spec = pl.BlockSpec((tm, LANES), lambda i: (i, 0))
    return pl.pallas_call(
        _rope_kernel,
        grid=(T // tm,),
        in_specs=[spec, pl.BlockSpec((8, LANES), lambda i: (0, 0))],
        out_specs=[spec] * 4,
        out_shape=[jax.ShapeDtypeStruct((T, LANES), F32)] * 4,
        compiler_params=_params("parallel"),
        name="rope_tables",
    )(posb, inv)


def _inproj_kernel(x_ref, sh_ref, sc_ref, g1_ref, w_ref, cn_ref, sn_ref, cm_ref, sm_ref,
                   gq_ref, wuq_ref, gkv_ref, wkv_ref,
                   qa_ref, ckv_ref, sk_ref, sv_ref, wk_ref, wv_ref, gate_ref,
                   qm_ref, km_ref, vm_ref, cx_ref, mv_ref, mo_ref, cif_ref):
    x = x_ref[...]
    h = _rms(x, g1_ref[...]) * (1.0 + sc_ref[0]) + sh_ref[0]
    hb = h.astype(BF16)

    def seg(start, width):
        return _dot(hb, w_ref[:, start:start + width])

    lane = lax.broadcasted_iota(jnp.int32, (1, LANES), 1)
    x1_n = (lane % HEAD_DIM) < NSA_ROT_HALF
    x1_m = lane < MLA_NOPE + MLA_ROPE // 2
    cn, sn, cm, sm = cn_ref[...], sn_ref[...], cm_ref[...], sm_ref[...]
    rope_n = lambda t: _rope(t, cn, sn, NSA_ROT_HALF, x1_n)
    rope_m = lambda t: _rope(t, cm, sm, MLA_ROPE // 2, x1_m)

    q = seg(SEG_Q, WIDTH_A)
    for r in range(3):
        t = rope_n(q[:, LANES * r:LANES * (r + 1)]) * (HEAD_DIM ** -0.5 * LOG2E)
        qa_ref[:, LANES * r:LANES * (r + 1)] = t.astype(BF16)
    ckv = seg(SEG_CK, 2 * LANES)
    ckv_ref[:, :LANES] = rope_n(ckv[:, :LANES]).astype(BF16)
    ckv_ref[:, LANES:] = ckv[:, LANES:].astype(BF16)
    skv = seg(SEG_SK, 2 * LANES)
    sk_ref[...] = rope_n(skv[:, :LANES]).astype(BF16)
    sv_ref[...] = skv[:, LANES:].astype(BF16)
    wkv = seg(SEG_WK, 2 * LANES)
    wk_ref[...] = rope_n(wkv[:, :LANES]).astype(BF16)
    wv_ref[...] = wkv[:, LANES:].astype(BF16)
    gate_ref[...] = _sigmoid(seg(SEG_G, LANES))

    ql = seg(SEG_QL, MLA_Q_LORA)
    qn = _rms(ql, gq_ref[...]).astype(BF16)
    qm = _dot(qn, wuq_ref[...])
    scale_m = (MLA_NOPE + MLA_ROPE) ** -0.5 * LOG2E
    for hh in range(MLA_HEADS):
        t = rope_m(qm[:, LANES * hh:LANES * (hh + 1)]) * scale_m
        qm_ref[:, LANES * hh:LANES * (hh + 1)] = t.astype(BF16)
    kvl_kr = seg(SEG_KVL, 2 * LANES)
    kvn = _rms(kvl_kr[:, :LANES], gkv_ref[...]).astype(BF16)
    kr = rope_m(kvl_kr[:, LANES:])
    kv = _dot(kvn, wkv_ref[...])
    for hh in range(MLA_HEADS):
        km_ref[:, LANES * hh:LANES * (hh + 1)] = (kv[:, LANES * hh:LANES * (hh + 1)] + kr).astype(BF16)
    vm_ref[...] = kv[:, MLA_HEADS * LANES:].astype(BF16)

    cx_ref[...] = seg(SEG_CX, WIDTH_C)
    mv_ref[...] = seg(SEG_MV, WIDTH_C).astype(BF16)
    mo_ref[...] = seg(SEG_MO, WIDTH_C)
    cif_ref[...] = seg(SEG_IF, LANES)


def _inproj(x2, shift, scale, g1, w1, tabs, gq, wuq, gkv, wkv, S):
    T = x2.shape[0]
    tm = TM_PROJ
    per_b = S // tm
    row = lambda w: pl.BlockSpec((tm, w), lambda i: (i, 0))
    full = lambda a: pl.BlockSpec(a.shape, lambda i: (0,) * a.ndim)
    modspec = pl.BlockSpec((1, 1, D_MODEL), lambda i: (i // per_b, 0, 0))
    outs = [(WIDTH_A, BF16), (2 * LANES, BF16), (LANES, BF16), (LANES, BF16), (LANES, BF16), (LANES, BF16),
            (LANES, F32), (MLA_HEADS * LANES, BF16), (MLA_HEADS * LANES, BF16), (WIDTH_B, BF16),
            (WIDTH_C, F32), (WIDTH_C, BF16), (WIDTH_C, F32), (LANES, F32)]
    return pl.pallas_call(
        _inproj_kernel,
        grid=(T // tm,),
        in_specs=[row(D_MODEL), modspec, modspec, full(g1), full(w1)] + [row(LANES)] * 4
                 + [full(gq), full(wuq), full(gkv), full(wkv)],
        out_specs=[row(w) for w, _ in outs],
        out_shape=[jax.ShapeDtypeStruct((T, w), dt) for w, dt in outs],
        compiler_params=_params("parallel"),
        name="in_proj",
    )(x2, shift, scale, g1, w1, *tabs, gq, wuq, gkv, wkv)


def _compress_kernel(xk_ref, xv_ref, wk_ref, wv_ref, pos_ref, wkf_ref, wvf_ref, kc_ref, vc_ref):
    lane = lax.broadcasted_iota(jnp.int32, (LANES, LANES), 1)
    row = lax.broadcasted_iota(jnp.int32, (LANES, LANES), 0)
    pos = pos_ref[...].astype(BF16)

    def one(x_ref, w_ref, wf_ref, o_ref, transpose):
        const = _dot(pos, wf_ref[...].astype(BF16))
        const2 = jnp.concatenate([const, const], axis=1)
        res = []
        for g in range(NSA_KV_HEADS):
            r = _dot(x_ref[0, g], w_ref[...])
            nxt = pltpu.roll(pltpu.roll(r, LANES - 1, 0), HEAD_DIM, 1)
            res.append(r + nxt)
        both = jnp.where(lane < HEAD_DIM, res[0], pltpu.roll(res[1], HEAD_DIM, 1)) + const2
        both = jnp.where(row < LANES - 1, both, 0.0)
        o_ref[0] = (both.T if transpose else both).astype(BF16)

    one(xk_ref, wk_ref, wkf_ref, kc_ref, False)
    one(xv_ref, wv_ref, wvf_ref, vc_ref, True)


def _compress(ckv, cmp_pos, w_cmp_k, w_cmp_v, B, S):
    ng = S // CMP_STRIDE
    x = ckv.reshape(B, ng, CMP_STRIDE, 2, NSA_KV_HEADS, HEAD_DIM).transpose(3, 0, 4, 1, 2, 5)
    x = x.reshape(2, B, NSA_KV_HEADS, ng, CMP_STRIDE * HEAD_DIM)
    half = CMP_STRIDE * HEAD_DIM

    def halves(w):
        return jnp.concatenate([w[:half], w[half:]], axis=1).astype(BF16)

    xspec = pl.BlockSpec((1, NSA_KV_HEADS, ng, half), lambda b: (b, 0, 0, 0))
    full = lambda a: pl.BlockSpec(a.shape, lambda b: (0,) * a.ndim)
    wk2, wv2 = halves(w_cmp_k), halves(w_cmp_v)
    posf = cmp_pos.reshape(1, CMP_LEN * HEAD_DIM)
    ospec = pl.BlockSpec((1, ng, LANES), lambda b: (b, 0, 0))
    return pl.pallas_call(
        _compress_kernel,
        grid=(B,),
        in_specs=[xspec, xspec, full(wk2), full(wv2), full(posf), full(w_cmp_k), full(w_cmp_v)],
        out_specs=[ospec, ospec],
        out_shape=[jax.ShapeDtypeStruct((B, ng, LANES), BF16)] * 2,
        compiler_params=_params("parallel"),
        name="nsa_compress",
    )(x[0], x[1], wk2, wv2, posf, w_cmp_k, w_cmp_v)


def _nsa_kernel(qT_ref, gT_ref, kc_ref, vcT_ref, sk_ref, svT_ref, wk_ref, wvT_ref, o_ref, qaug_scr, acc_scr, s_scr):
    tq = TQ_NSA
    cols = NSA_HEADS * tq
    pair = 2 * tq
    t0 = pl.program_id(1) * tq
    n_slc = SLC_TOPK * 2

    frow = lax.broadcasted_iota(jnp.int32, (LANES, tq), 0)
    g0_row = frow < HEAD_DIM
    tiles = [qT_ref[LANES * r:LANES * (r + 1), :] for r in range(NSA_REP)]
    zero = jnp.zeros_like(tiles[0])
    q6 = jnp.concatenate([jnp.where(g0_row, t, zero) for t in tiles]
                         + [jnp.where(g0_row, zero, t) for t in tiles], axis=1)
    qaug_scr[0:LANES, :] = q6
    tq_l = t0 + (lax.broadcasted_iota(jnp.int32, (1, cols), 1) & (tq - 1))

    pairs = [slice(pair * pp, pair * (pp + 1)) for pp in range(cols // pair)]
    s = _dot(kc_ref[0], q6)
    span = WINDOW + tq
    start = pl.multiple_of(jnp.maximum(t0 - WINDOW, 0), tq)
    kw = wk_ref[pl.ds(start, span), :]
    win_scores = [_dot(kw, q6[:, sl]) for sl in pairs]

    nrow = lax.broadcasted_iota(jnp.int32, (LANES, cols), 0)
    vis = nrow * CMP_STRIDE + (CMP_LEN - 1) <= tq_l
    s = jnp.where(vis, s, NEG)
    e = jnp.where(vis, jnp.exp2(s - jnp.max(s, axis=0, keepdims=True)), 0.0)
    den = jnp.sum(e, axis=0, keepdims=True)
    p = e / jnp.where(den > 0.0, den, 1.0)
    o_cmp = _dot(vcT_ref[0], p.astype(BF16))

    jr = lax.broadcasted_iota(jnp.int32, (n_slc, LANES), 0)
    nc = lax.broadcasted_iota(jnp.int32, (n_slc, LANES), 1)
    ovl = ((nc * CMP_STRIDE < jr * SLC_LEN + SLC_LEN) & (nc * CMP_STRIDE + CMP_LEN > jr * SLC_LEN)
           & (nc < LANES - 1))
    ovl = jnp.where(ovl, 1.0, 0.0).astype(BF16)
    jq = lax.broadcasted_iota(jnp.int32, (n_slc, tq), 0)
    tl = t0 + lax.broadcasted_iota(jnp.int32, (n_slc, tq), 1)
    cur = tl // SLC_LEN
    forced = (jq == 0) | (jq == cur) | (jq == cur - 1)
    future = jq * SLC_LEN > tl
    bias_t = []
    for g in range(NSA_KV_HEADS):
        pg = p[:, (3 * g) * tq:(3 * g + 1) * tq] + p[:, (3 * g + 1) * tq:(3 * g + 2) * tq] + p[:, (3 * g + 2) * tq:(3 * g + 3) * tq]
        imp = sum(_dot(ovl, part) for part in _split3(pg))
        imp = jnp.where(forced, jnp.inf, imp)
        imp = jnp.where(future, -jnp.inf, imp)
        rank = jnp.zeros((n_slc, tq), F32)
        for jp in range(n_slc):
            rv = imp[jp:jp + 1, :]
            ahead = jnp.where(rv > imp, 1.0, jnp.where((rv == imp) & (jq > jp), 1.0, 0.0))
            rank = rank + ahead
        bias_t.append(jnp.where(rank < float(SLC_TOPK), 0.0, NEG).astype(BF16))
    zb = jnp.zeros((n_slc, NSA_REP * tq), BF16)
    qaug_scr[LANES:LANES + n_slc, :] = jnp.concatenate([bias_t[0]] * NSA_REP + [zb], axis=1)
    qaug_scr[LANES + n_slc:LANES + 2 * n_slc, :] = jnp.concatenate([zb] + [bias_t[1]] * NSA_REP, axis=1)
    qaug_scr[LANES + 2 * n_slc:, :] = jnp.zeros((LANES - 2 * n_slc, cols), BF16)

    vwt = wvT_ref[:, pl.ds(start, span)]
    wrow = start + lax.broadcasted_iota(jnp.int32, (span, pair), 0)
    o_win = []
    for sl, sc in zip(pairs, win_scores):
        diff = tq_l[:, sl] - wrow
        sc = jnp.where((diff >= 0) & (diff < WINDOW), sc, NEG)
        ew = jnp.exp2(sc - jnp.max(sc, axis=0, keepdims=True))
        o_win.append(_dot(vwt, ew.astype(BF16)) / jnp.sum(ew, axis=0, keepdims=True))
    o_win = jnp.concatenate(o_win, axis=1)

    acc_scr[...] = jnp.zeros((LANES, cols), F32)
    krow = lax.broadcasted_iota(jnp.int32, (TK_SLC, pair), 0)

    def slc_scores(k0, sl):
        return _dot(sk_ref[pl.ds(k0, TK_SLC), :], qaug_scr[:, sl])

    for pp, sl in enumerate(pairs):
        s_scr[pp] = slc_scores(0, sl)

    def slc_tile(k0, m, l, masked):
        vt = svT_ref[:, pl.ds(k0, TK_SLC)]
        ms, ls, accs = [], [], []
        ahead = [s_scr[pp] for pp in range(len(pairs))]
        for pp, sl in enumerate(pairs):
            sc = ahead[pp]
            if masked:
                sc = jnp.where(k0 + krow <= tq_l[:, sl], sc, NEG)
            else:
                s_scr[pp] = slc_scores(pl.multiple_of(k0 + TK_SLC, TK_SLC), sl)
            m_new = jnp.maximum(m[:, sl], jnp.max(sc, axis=0, keepdims=True))
            alpha = jnp.exp2(m[:, sl] - m_new)
            pe = jnp.exp2(sc - m_new)
            ls.append(alpha * l[:, sl] + jnp.sum(pe, axis=0, keepdims=True))
            accs.append(alpha * acc_scr[:, sl] + _dot(vt, pe.astype(BF16)))
            ms.append(m_new)
        acc_scr[...] = jnp.concatenate(accs, axis=1)
        return jnp.concatenate(ms, axis=1), jnp.concatenate(ls, axis=1)

    def slc_step(kt, carry):
        return slc_tile(pl.multiple_of(kt * TK_SLC, TK_SLC), carry[0], carry[1], False)

    n_full = t0 // TK_SLC
    m, l = lax.fori_loop(0, n_full, slc_step, (jnp.full((1, cols), NEG, F32), jnp.zeros((1, cols), F32)))
    m, l = slc_tile(pl.multiple_of(n_full * TK_SLC, TK_SLC), m, l, True)
    o_slc = acc_scr[...] / l

    gt = gT_ref[...]
    grow = lambda j: jnp.concatenate([gt[3 * hd + j:3 * hd + j + 1, :] for hd in range(NSA_HEADS)], axis=1)
    mixed = grow(0) * o_cmp + grow(1) * o_slc + grow(2) * o_win
    for r in range(NSA_REP):
        t = jnp.where(g0_row, mixed[:, tq * r:tq * (r + 1)], mixed[:, tq * (NSA_REP + r):tq * (NSA_REP + r + 1)])
        o_ref[:, LANES * r:LANES * (r + 1)] = t.T


def _nsa(qaT, gT, kc, vcT, sk_aug, svT, wk, wvT, B, S):
    tq = TQ_NSA
    nq = S // tq
    cols = NSA_HEADS * tq
    qcol = lambda h: pl.BlockSpec((h, tq), lambda b, i: (0, b * nq + i))
    seq = lambda w: pl.BlockSpec((S, w), lambda b, i: (b, 0))
    seqT = pl.BlockSpec((LANES, S), lambda b, i: (0, b))
    cspec = pl.BlockSpec((1, LANES, LANES), lambda b, i: (b, 0, 0))
    return pl.pallas_call(
        _nsa_kernel,
        grid=(B, nq),
        in_specs=[qcol(WIDTH_A), qcol(gT.shape[0]), cspec, cspec, seq(2 * LANES), seqT, seq(LANES), seqT],
        out_specs=pl.BlockSpec((tq, WIDTH_A), lambda b, i: (b * nq + i, 0)),
        out_shape=jax.ShapeDtypeStruct((B * S, WIDTH_A), F32),
        scratch_shapes=[pltpu.VMEM((2 * LANES, cols), BF16), pltpu.VMEM((LANES, cols), F32),
                        pltpu.VMEM((cols // (2 * tq), TK_SLC, 2 * tq), F32)],
        compiler_params=_params("parallel", "arbitrary"),
        name="nsa_attention",
    )(qaT, gT, kc, vcT, sk_aug, svT, wk, wvT)


def _mla_kernel(qT_ref, k_ref, vT_ref, o_ref, acc_scr, s_scr):
    tq = TQ_MLA
    t0 = pl.program_id(1) * tq
    tq_l = t0 + lax.broadcasted_iota(jnp.int32, (1, tq), 1)
    krow = lax.broadcasted_iota(jnp.int32, (tq, tq), 0)
    acc_scr[...] = jnp.zeros((MLA_HEADS, LANES, tq), F32)

    def scores(k0, hd):
        k = k_ref[pl.ds(k0, tq), LANES * hd:LANES * (hd + 1)]
        return _dot(k, qT_ref[LANES * hd:LANES * (hd + 1), :])

    for hd in range(MXU_LOOKAHEAD):
        s_scr[hd] = scores(0, hd)

    def tile(k0, ms, ls, masked):
        new_m, new_l, accs = [], [], []
        ahead = [s_scr[hd] for hd in range(MXU_LOOKAHEAD)]
        for hd in range(MLA_HEADS):
            sc = ahead.pop(0)
            nxt = hd + MXU_LOOKAHEAD
            if nxt < MLA_HEADS:
                ahead.append(scores(k0, nxt))
            elif not masked:
                s_scr[nxt - MLA_HEADS] = scores(pl.multiple_of(k0 + tq, tq), nxt - MLA_HEADS)
            if masked:
                sc = jnp.where(k0 + krow <= tq_l, sc, NEG)
            m_new = jnp.maximum(ms[hd], jnp.max(sc, axis=0, keepdims=True))
            alpha = jnp.exp2(ms[hd] - m_new)
            pe = jnp.exp2(sc - m_new)
            new_l.append(alpha * ls[hd] + jnp.sum(pe, axis=0, keepdims=True))
            vt = vT_ref[LANES * (hd // 2):LANES * (hd // 2 + 1), pl.ds(k0, tq)]
            accs.append(alpha * acc_scr[hd] + _dot(vt, pe.astype(BF16)))
            new_m.append(m_new)
        acc_scr[...] = jnp.stack(accs)
        return tuple(new_m), tuple(new_l)

    def step(kt, carry):
        return tile(pl.multiple_of(kt * tq, tq), carry[0], carry[1], False)

    n_full = pl.program_id(1)
    init = (tuple(jnp.full((1, tq), NEG, F32) for _ in range(MLA_HEADS)),
            tuple(jnp.zeros((1, tq), F32) for _ in range(MLA_HEADS)))
    ms, ls = lax.fori_loop(0, n_full, step, init)
    ms, ls = tile(pl.multiple_of(n_full * tq, tq), ms, ls, True)
    frow = lax.broadcasted_iota(jnp.int32, (LANES, tq), 0)
    for pr in range(MLA_HEADS // 2):
        t = jnp.where(frow < MLA_V, acc_scr[2 * pr] / ls[2 * pr], acc_scr[2 * pr + 1] / ls[2 * pr + 1])
        o_ref[:, LANES * pr:LANES * (pr + 1)] = t.T


def _mla(qmT, km, vmT, B, S):
    tq = TQ_MLA
    nq = S // tq
    return pl.pallas_call(
        _mla_kernel,
        grid=(B, nq),
        in_specs=[pl.BlockSpec((MLA_HEADS * LANES, tq), lambda b, i: (0, b * nq + i)),
                  pl.BlockSpec((S, MLA_HEADS * LANES), lambda b, i: (b, 0)),
                  pl.BlockSpec((WIDTH_B, S), lambda b, i: (0, b))],
        out_specs=pl.BlockSpec((tq, WIDTH_B), lambda b, i: (b * nq + i, 0)),
        out_shape=jax.ShapeDtypeStruct((B * S, WIDTH_B), F32),
        scratch_shapes=[pltpu.VMEM((MLA_HEADS, LANES, tq), F32), pltpu.VMEM((MXU_LOOKAHEAD, tq, tq), F32)],
        compiler_params=_params("parallel", "arbitrary"),
        name="mla_attention",
    )(qmT, km, vmT)


def _mlstm_kernel(cx_ref, v_ref, o_ref, gc_ref, gr_ref, cw_ref, cb_ref, wq_ref, wk_ref,
                  bc_ref, br_ref, gmh_ref, skip_ref, y_ref, xc_scr, q_scr, k_scr):
    S = cx_ref.shape[0]
    L = MLSTM_CHUNK
    d = MLSTM_DIM
    pairs = MLSTM_HEADS // 2

    x = cx_ref[...]
    rowi = lax.broadcasted_iota(jnp.int32, (S, WIDTH_C), 0)
    conv = x * cw_ref[MLSTM_CONV - 1:MLSTM_CONV, :]
    for back in range(1, MLSTM_CONV):
        shifted = jnp.where(rowi >= back, pltpu.roll(x, back, 0), 0.0)
        conv = conv + shifted * cw_ref[MLSTM_CONV - 1 - back:MLSTM_CONV - back, :]
    xc = _silu(conv + cb_ref[...])
    xc_scr[...] = xc
    xcb = xc.astype(BF16)
    q_scr[...] = _dot(xcb, wq_ref[...]).astype(BF16)
    k_scr[...] = _dot(xcb, wk_ref[...]).astype(BF16)

    ri = lax.broadcasted_iota(jnp.int32, (L, L), 0)
    ci = lax.broadcasted_iota(jnp.int32, (L, L), 1)
    causal = ci <= ri
    tril = jnp.where(causal, 1.0, 0.0).astype(BF16)
    triu = jnp.where(ri <= ci, 1.0, 0.0).astype(BF16)
    lane = lax.broadcasted_iota(jnp.int32, (L, LANES), 1)
    h0_lane = lane < d
    h0_row1 = lax.broadcasted_iota(jnp.int32, (1, LANES), 1) < d
    blk = (lax.broadcasted_iota(jnp.int32, (LANES, LANES), 0) // d) == (lax.broadcasted_iota(jnp.int32, (LANES, LANES), 1) // d)
    rowsel = lax.broadcasted_iota(jnp.int32, (LANES, 1), 0) < d
    ones_row = jnp.ones((1, LANES), F32)

    def log_sig(z):
        return jnp.minimum(z, 0.0) - jnp.log1p(jnp.exp(-jnp.abs(z)))

    def chunk(c, carry):
        cs, ns, ms = carry
        r0 = pl.multiple_of(c * L, L)
        gcol = gc_ref[pl.ds(r0, L), :] + bc_ref[...]
        ig_col = gcol
        lf_col = log_sig(gcol)
        b_col = sum(_dot(tril, part) for part in _split3(lf_col))
        grow = gr_ref[0, c] + br_ref[...]
        lf_row = log_sig(grow)
        b_row = sum(_dot(part, triu) for part in _split3(lf_row))
        new_c, new_n, new_m_parts, outs = [], [], [], []
        for pr in range(pairs):
            qp = q_scr[pl.ds(r0, L), LANES * pr:LANES * (pr + 1)]
            kp = k_scr[pl.ds(r0, L), LANES * pr:LANES * (pr + 1)]
            vp = v_ref[pl.ds(r0, L), LANES * pr:LANES * (pr + 1)]
            kpf = kp.astype(F32)
            qc = _dot(qp, cs[pr].astype(BF16))
            qn_full = qp.astype(F32) * ns[pr]
            num_p, den_p, mt_p, wl_p = [], [], [], []
            e_old_l, e_loc_l, m_new_l = [], [], []
            for hh in range(2):
                hd = 2 * pr + hh
                igc = ig_col[:, hd:hd + 1]
                bc = b_col[:, MLSTM_HEADS + hd:MLSTM_HEADS + hd + 1]
                igr = grow[hd:hd + 1, :]
                brw = b_row[MLSTM_HEADS + hd:MLSTM_HEADS + hd + 1, :]
                b_last = bc[L - 1:L, :]
                m_prev = ms[:, hd:hd + 1]
                a_col = b_last - bc + igc
                m_loc = jnp.max(a_col, axis=0, keepdims=True)
                wl_p.append(jnp.exp(a_col - m_loc))
                m_new = jnp.maximum(b_last + m_prev, m_loc)
                e_old_l.append(jnp.exp(b_last + m_prev - m_new))
                e_loc_l.append(jnp.exp(m_loc - m_new))
                m_new_l.append(m_new)
                dmat = jnp.where(causal, bc - brw + igr, -jnp.inf)
                gl = bc + m_prev
                m_t = jnp.maximum(gl, jnp.max(dmat, axis=-1, keepdims=True))
                hsel = h0_lane if hh == 0 else lane >= d
                qh = jnp.where(hsel, qp, jnp.zeros_like(qp))
                sw = _dot_nt(qh, kp) * jnp.exp(dmat - m_t)
                inter = jnp.exp(gl - m_t)
                qn = jnp.sum(jnp.where(hsel, qn_full, 0.0), axis=-1, keepdims=True)
                num_p.append(inter * qc + _dot(sw.astype(BF16), vp))
                den_p.append(inter * qn + jnp.sum(sw, axis=-1, keepdims=True))
                mt_p.append(m_t)
            num = jnp.where(h0_lane, num_p[0], num_p[1])
            den = jnp.where(h0_lane, den_p[0], den_p[1])
            m_t2 = jnp.where(h0_lane, mt_p[0], mt_p[1])
            outs.append(num / jnp.maximum(jnp.abs(den), jnp.exp(-m_t2)))
            wl2 = jnp.where(h0_lane, wl_p[0], wl_p[1])
            kwf = kpf * wl2
            c_loc = jnp.where(blk, _dot_tn(kwf.astype(BF16), vp), 0.0)
            n_loc = jnp.sum(kwf, axis=0, keepdims=True)
            e_old_row = jnp.where(h0_row1, e_old_l[0], e_old_l[1])
            e_loc_row = jnp.where(h0_row1, e_loc_l[0], e_loc_l[1])
            e_old_colv = jnp.where(rowsel, e_old_l[0], e_old_l[1])
            e_loc_colv = jnp.where(rowsel, e_loc_l[0], e_loc_l[1])
            new_c.append(e_old_colv * cs[pr] + e_loc_colv * c_loc)
            new_n.append(e_old_row * ns[pr] + e_loc_row * n_loc)
            new_m_parts += m_new_l
        hcat = jnp.concatenate(outs, axis=1)
        hg = _sigmoid(o_ref[pl.ds(r0, L), :]) * hcat
        ys = []
        for hd in range(MLSTM_HEADS):
            part = hg[:, LANES * (hd // 2):LANES * (hd // 2 + 1)]
            hs = h0_lane if hd % 2 == 0 else lane >= d
            mu = jnp.sum(jnp.where(hs, part, 0.0), axis=-1, keepdims=True) / d
            cen = jnp.where(hs, part - mu, 0.0)
            var = jnp.sum(cen * cen, axis=-1, keepdims=True) / d
            ys.append(cen * lax.rsqrt(var + NORM_EPS))
        hn = jnp.concatenate([ys[0] + ys[1], ys[2] + ys[3]], axis=1)
        y_ref[pl.ds(r0, L), :] = hn * gmh_ref[...] + skip_ref[...] * xc_scr[pl.ds(r0, L), :]
        lane_m = lax.broadcasted_iota(jnp.int32, (1, LANES), 1)
        m_next = jnp.zeros((1, LANES), F32)
        for hd in range(MLSTM_HEADS):
            m_next = jnp.where(lane_m == hd, new_m_parts[hd] * ones_row, m_next)
        return tuple(new_c), tuple(new_n), m_next

    init = (tuple(jnp.zeros((LANES, LANES), F32) for _ in range(pairs)),
            tuple(jnp.zeros((1, LANES), F32) for _ in range(pairs)),
            jnp.zeros((1, LANES), F32))
    lax.fori_loop(0, S // L, chunk, init)


def _mlstm(cx, mv, mo, cif, conv_w, conv_b, w_q_m, w_k_m, b_igate, b_fgate, g_mh, skip_m, B, S):
    def blockdiag(w, scale):
        out = jnp.zeros((WIDTH_C, WIDTH_C), F32)
        for hd in range(MLSTM_HEADS):
            out = out.at[hd * MLSTM_DIM:(hd + 1) * MLSTM_DIM, hd * MLSTM_DIM:(hd + 1) * MLSTM_DIM].set(w[hd] * scale)
        return out.astype(BF16)

    wq = blockdiag(w_q_m, MLSTM_DIM ** -0.5)
    wk = blockdiag(w_k_m, 1.0)
    bias8 = jnp.concatenate([b_igate, b_fgate])
    bc = jnp.zeros((1, LANES), F32).at[0, :2 * MLSTM_HEADS].set(bias8)
    br = jnp.broadcast_to(bias8[:, None], (2 * MLSTM_HEADS, MLSTM_CHUNK))
    nc = S // MLSTM_CHUNK
    gr = cif[:, :2 * MLSTM_HEADS].reshape(B, nc, MLSTM_CHUNK, 2 * MLSTM_HEADS).transpose(0, 1, 3, 2)
    seq = lambda w: pl.BlockSpec((S, w), lambda b: (b, 0))
    full = lambda a: pl.BlockSpec(a.shape, lambda b: (0,) * a.ndim)
    row = lambda a: a.reshape(1, -1)
    args = [conv_w, row(conv_b), wq, wk, bc, br, row(g_mh), row(skip_m)]
    return pl.pallas_call(
        _mlstm_kernel,
        grid=(B,),
        in_specs=[seq(WIDTH_C), seq(WIDTH_C), seq(WIDTH_C), seq(LANES),
                  pl.BlockSpec((1, nc, 2 * MLSTM_HEADS, MLSTM_CHUNK), lambda b: (b, 0, 0, 0))] + [full(a) for a in args],
        out_specs=seq(WIDTH_C),
        out_shape=jax.ShapeDtypeStruct((B * S, WIDTH_C), F32),
        scratch_shapes=[pltpu.VMEM((S, WIDTH_C), F32), pltpu.VMEM((S, WIDTH_C), BF16), pltpu.VMEM((S, WIDTH_C), BF16)],
        compiler_params=_params("parallel"),
        name="mlstm_mixer",
    )(cx, mv, mo, cif, gr, *args)


def _outproj_kernel(ya_ref, yb_ref, yc_ref, x_ref, gate_ref, ga_ref, gb_ref, w_ref, o_ref):
    a = _rms(ya_ref[...], ga_ref[...]).astype(BF16)
    b = _rms(yb_ref[...], gb_ref[...]).astype(BF16)
    c = yc_ref[...].astype(BF16)
    y = _dot(a, w_ref[0:WIDTH_A, :]) + _dot(b, w_ref[WIDTH_A:WIDTH_A + WIDTH_B, :]) + _dot(c, w_ref[WIDTH_A + WIDTH_B:, :])
    o_ref[...] = x_ref[...] + gate_ref[0] * y


def _outproj(ya, yb, yc, x2, gate, ga, gb, w_out, S):
    T = x2.shape[0]
    tm = TM_PROJ
    per_b = S // tm
    row = lambda w: pl.BlockSpec((tm, w), lambda i: (i, 0))
    full = lambda a: pl.BlockSpec(a.shape, lambda i: (0,) * a.ndim)
    return pl.pallas_call(
        _outproj_kernel,
        grid=(T // tm,),
        in_specs=[row(WIDTH_A), row(WIDTH_B), row(WIDTH_C), row(D_MODEL),
                  pl.BlockSpec((1, 1, D_MODEL), lambda i: (i // per_b, 0, 0)), full(ga), full(gb), full(w_out)],
        out_specs=row(D_MODEL),
        out_shape=jax.ShapeDtypeStruct((T, D_MODEL), F32),
        compiler_params=_params("parallel"),
        name="out_proj",
    )(ya, yb, yc, x2, gate, ga, gb, w_out)


def _ffn_kernel(x_ref, sh_ref, sc_ref, gate_ref, g2_ref, wgu_ref, wd_ref, gf_ref, o_ref, *, final_norm):
    x = x_ref[...]
    h = (_rms(x, g2_ref[...]) * (1.0 + sc_ref[0]) + sh_ref[0]).astype(BF16)
    acc = jnp.zeros(x.shape, F32)
    for j in range(FFN_HIDDEN // FFN_CHUNK):
        gu = _dot(h, wgu_ref[:, 2 * FFN_CHUNK * j:2 * FFN_CHUNK * (j + 1)])
        act = (_silu(gu[:, :FFN_CHUNK]) * gu[:, FFN_CHUNK:]).astype(BF16)
        acc = acc + _dot(act, wd_ref[FFN_CHUNK * j:FFN_CHUNK * (j + 1), :])
    y = x + gate_ref[0] * acc
    if final_norm:
        y = _rms(y, gf_ref[...])
    o_ref[...] = y


def _ffn(x2, shift, scale, gate, g2, wgu, wd, gf, S, final_norm):
    T = x2.shape[0]
    tm = TM_PROJ
    per_b = S // tm
    row = pl.BlockSpec((tm, D_MODEL), lambda i: (i, 0))
    full = lambda a: pl.BlockSpec(a.shape, lambda i: (0,) * a.ndim)
    modspec = pl.BlockSpec((1, 1, D_MODEL), lambda i: (i // per_b, 0, 0))
    return pl.pallas_call(
        functools.partial(_ffn_kernel, final_norm=final_norm),
        grid=(T // tm,),
        in_specs=[row, modspec, modspec, modspec, full(g2), full(wgu), full(wd), full(gf)],
        out_specs=row,
        out_shape=jax.ShapeDtypeStruct((T, D_MODEL), F32),
        compiler_params=_params("parallel"),
        name="ffn_final" if final_norm else "ffn",
    )(x2, shift, scale, gate, g2, wgu, wd, gf)


def _head_tile_perm():
    idx = []
    for r in range(NSA_REP):
        idx += list(range(HEAD_DIM * r, HEAD_DIM * (r + 1)))
        idx += list(range(HEAD_DIM * (NSA_REP + r), HEAD_DIM * (NSA_REP + r + 1)))
    return np.asarray(idx, np.int32)


def _in_cols():
    cols = np.full((N_IN,), -1, np.int64)
    cols[SEG_Q:SEG_Q + WIDTH_A] = _head_tile_perm()
    off = WIDTH_A
    for seg in (SEG_CK, SEG_CV, SEG_SK, SEG_SV, SEG_WK, SEG_WV):
        cols[seg:seg + LANES] = off + np.arange(LANES)
        off += LANES
    cols[SEG_G:SEG_G + N_GATES] = off + np.arange(N_GATES)
    off += N_GATES
    cols[SEG_QL:SEG_QL + MLA_Q_LORA] = off + np.arange(MLA_Q_LORA)
    off += MLA_Q_LORA
    cols[SEG_KVL:SEG_KVL + MLA_KV_LORA] = off + np.arange(MLA_KV_LORA)
    off += MLA_KV_LORA
    cols[SEG_KR + MLA_NOPE:SEG_KR + MLA_NOPE + MLA_ROPE] = off + np.arange(MLA_ROPE)
    off += MLA_ROPE
    for seg in (SEG_CX, SEG_MV, SEG_MO):
        cols[seg:seg + WIDTH_C] = off + np.arange(WIDTH_C)
        off += WIDTH_C
    cols[SEG_IF:SEG_IF + 2 * MLSTM_HEADS] = off + np.arange(2 * MLSTM_HEADS)
    return cols


def _gather_cols(w, cols):
    g = jnp.take(w, jnp.asarray(np.maximum(cols, 0), jnp.int32), axis=1)
    return jnp.where(jnp.asarray(cols >= 0)[None, :], g, 0.0)


def _layer_weights(l, w_in, w_uq, w_ukv, w_out, w_gu, w_down, g_out_a):
    w1 = _gather_cols(w_in[l], _in_cols()).astype(BF16)
    cq = np.full((MLA_HEADS * LANES,), -1, np.int64)
    ck = np.full((MLA_HEADS * LANES,), -1, np.int64)
    cv = np.zeros((WIDTH_B,), np.int64)
    dq = MLA_NOPE + MLA_ROPE
    dkv = MLA_NOPE + MLA_V
    for hd in range(MLA_HEADS):
        cq[LANES * hd:LANES * hd + dq] = dq * hd + np.arange(dq)
        ck[LANES * hd:LANES * hd + MLA_NOPE] = dkv * hd + np.arange(MLA_NOPE)
        cv[MLA_V * hd:MLA_V * (hd + 1)] = dkv * hd + MLA_NOPE + np.arange(MLA_V)
    wuq = _gather_cols(w_uq[l], cq).astype(BF16)
    wkv = jnp.concatenate([_gather_cols(w_ukv[l], ck), _gather_cols(w_ukv[l], cv)], axis=1).astype(BF16)
    perm = _head_tile_perm()
    wo = jnp.concatenate([w_out[l][perm], w_out[l][WIDTH_A:]], axis=0).astype(BF16)
    ga = g_out_a[l][perm].reshape(1, WIDTH_A)
    nch = FFN_HIDDEN // FFN_CHUNK
    wgu = w_gu[l].reshape(D_MODEL, 2, nch, FFN_CHUNK).transpose(0, 2, 1, 3).reshape(D_MODEL, 2 * FFN_HIDDEN).astype(BF16)
    wd = w_down[l].astype(BF16)
    return w1, wuq, wkv, wo, ga, wgu, wd


def kernel(x, c, positions, g_norm1, g_norm2, w_ada, b_ada, w_in, cmp_pos, w_cmp_k, w_cmp_v, g_out_a, g_q_lora, w_uq, g_kv_lora, w_ukv, g_out_b, conv_w, conv_b, w_q_m, w_k_m, b_igate, b_fgate, g_mh, skip_m, w_out, w_gu, w_down, g_final):
    B, S, D = x.shape
    T = B * S
    x2 = x.reshape(T, D)
    tabs = _rope_tables(positions)
    mod = _ada(c, w_ada, b_ada)
    blk_id = jnp.arange(S, dtype=jnp.int32) // SLC_LEN
    onehot = (blk_id[:, None] == jnp.arange(2 * SLC_TOPK, dtype=jnp.int32)[None, :]).astype(BF16)
    code = jnp.concatenate([onehot, onehot, jnp.zeros((S, LANES - 4 * SLC_TOPK), BF16)], axis=1)
    code = jnp.broadcast_to(code[None], (B, S, LANES)).reshape(T, LANES)
    row = lambda v: v.reshape(1, -1)
    for l in range(DEPTH):
        w1, wuq, wkv, wo, ga, wgu, wd = _layer_weights(l, w_in, w_uq, w_ukv, w_out, w_gu, w_down, g_out_a)
        m6 = [mod[l, :, D * i:D * (i + 1)].reshape(B, 1, D) for i in range(6)]
        shift1, scale1, gate1, shift2, scale2, gate2 = m6
        (qa, ckv, sk, sv, wk, wv, gates, qm, km, vm, cx, mv, mo, cif) = _inproj(
            x2, shift1, scale1, row(g_norm1[l]), w1, tabs, row(g_q_lora[l]), wuq, row(g_kv_lora[l]), wkv, S)
        kc, vcT = _compress(ckv, cmp_pos[l], w_cmp_k[l], w_cmp_v[l], B, S)
        sk_aug = jnp.concatenate([sk, code], axis=1)
        ya = _nsa(qa.T, gates[:, :GATE_ROWS].T, kc, vcT, sk_aug, sv.T, wk, wv.T, B, S)
        yb = _mla(qm.T, km, vm.T, B, S)
        yc = _mlstm(cx, mv, mo, cif, conv_w[l], conv_b[l], w_q_m[l], w_k_m[l], b_igate[l], b_fgate[l],
                    g_mh[l], skip_m[l], B, S)
        x2 = _outproj(ya, yb, yc, x2, gate1, ga, row(g_out_b[l]), wo, S)
        x2 = _ffn(x2, shift2, scale2, gate2, row(g_norm2[l]), wgu, wd, row(g_final), S, final_norm=(l == DEPTH - 1))
    return x2.reshape(B, S, D)
```

```python
import functools

import numpy as np
import jax
import jax.numpy as jnp
from jax import lax
from jax.experimental import pallas as pl
from jax.experimental.pallas import tpu as pltpu

F32 = jnp.float32
BF16 = jnp.bfloat16

D_MODEL = 1024
DEPTH = 2
HEAD_DIM = 64
ROPE_THETA = 500000.0
NSA_ROT_HALF = HEAD_DIM // 8
NORM_EPS = 1e-6

NSA_HEADS = 6
NSA_KV_HEADS = 2
NSA_REP = NSA_HEADS // NSA_KV_HEADS
CMP_LEN = 32
CMP_STRIDE = 16
SLC_LEN = 64
SLC_TOPK = 16
WINDOW = 512

MLA_HEADS = 6
MLA_Q_LORA = 256
MLA_KV_LORA = 128
MLA_NOPE = 64
MLA_ROPE = 32
MLA_V = 64

MLSTM_HEADS = 4
MLSTM_DIM = 64
MLSTM_CONV = 4
MLSTM_CHUNK = 64

WIDTH_A = NSA_HEADS * HEAD_DIM
WIDTH_B = MLA_HEADS * MLA_V
WIDTH_C = MLSTM_HEADS * MLSTM_DIM
FFN_HIDDEN = 2816
N_GATES = 3 * NSA_HEADS
GATE_ROWS = 24

LANES = 128
NEG = -1e30
LOG2E = 1.4426950408889634
VMEM_LIMIT = 56 * 1024 * 1024

TM_PROJ = 512
TQ_NSA = 128
TK_SLC = 256
TQ_MLA = 256
FFN_CHUNK = 256
MXU_LOOKAHEAD = 4

SEG_CKV, SEG_SK, SEG_WK, SEG_QL, SEG_KVL, SEG_KR, SEG_CX, SEG_MV, SEG_MO, SEG_IF = (
    0, 256, 384, 512, 768, 896, 1024, 1280, 1536, 1792)
N_STD = 1920
TSEG_Q, TSEG_G, TSEG_SV, TSEG_WV = 0, 384, 416, 544
TSEG_G_ROWS = 32
N_T = 672


def _params(*sem):
    return pltpu.CompilerParams(dimension_semantics=sem, vmem_limit_bytes=VMEM_LIMIT)


def _dot(a, b):
    return jnp.dot(a, b, preferred_element_type=F32)


def _dot_nt(a, b):
    return lax.dot_general(a, b, (((1,), (1,)), ((), ())), preferred_element_type=F32)


def _dot_tn(a, b):
    return lax.dot_general(a, b, (((0,), (0,)), ((), ())), preferred_element_type=F32)


def _split3(x):
    hi = x.astype(BF16)
    r1 = x - hi.astype(F32)
    mid = r1.astype(BF16)
    lo = (r1 - mid.astype(F32)).astype(BF16)
    return hi, mid, lo


def _rms(x, g):
    return x * lax.rsqrt(jnp.mean(x * x, axis=-1, keepdims=True) + NORM_EPS) * g


def _sigmoid(x):
    return 1.0 / (1.0 + jnp.exp(-x))


def _silu(x):
    return x * _sigmoid(x)


def _rope(x, cos, sin, half, x1_lane):
    xr = jnp.where(x1_lane, -pltpu.roll(x, LANES - half, 1), pltpu.roll(x, half, 1))
    return x * cos + xr * sin


def _ada_kernel(c_ref, w_ref, b_ref, o_ref):
    c = c_ref[...]
    ca = _silu(c).astype(BF16)
    o_ref[0] = _dot(ca, w_ref[0].astype(BF16)) + b_ref[0]


def _ada(c, w_ada, b_ada):
    L, D, N = w_ada.shape
    B = c.shape[0]
    tn = 1536
    return pl.pallas_call(
        _ada_kernel,
        grid=(L, N // tn),
        in_specs=[pl.BlockSpec((B, D), lambda l, j: (0, 0)),
                  pl.BlockSpec((1, D, tn), lambda l, j: (l, 0, j)),
                  pl.BlockSpec((1, 1, tn), lambda l, j: (l, 0, j))],
        out_specs=pl.BlockSpec((1, B, tn), lambda l, j: (l, 0, j)),
        out_shape=jax.ShapeDtypeStruct((L, B, N), F32),
        compiler_params=_params("parallel", "parallel"),
        name="ada_mod",
    )(c, w_ada, b_ada.reshape(L, 1, N))


def _rope_kernel(pos_ref, inv_ref, cn_ref, sn_ref, cm_ref, sm_ref):
    pos = pos_ref[...]
    ang_n = pos * inv_ref[0:1, :]
    ang_m = pos * inv_ref[1:2, :]
    cn_ref[...] = jnp.cos(ang_n)
    sn_ref[...] = jnp.sin(ang_n)
    cm_ref[...] = jnp.cos(ang_m)
    sm_ref[...] = jnp.sin(ang_m)


def _rope_tables(positions):
    T = positions.size
    inv_n = jnp.power(ROPE_THETA, -jnp.arange(0, 2 * NSA_ROT_HALF, 2, dtype=F32) / (2 * NSA_ROT_HALF))
    inv_m = jnp.power(ROPE_THETA, -jnp.arange(0, MLA_ROPE, 2, dtype=F32) / MLA_ROPE)
    z = lambda n: jnp.zeros((n,), F32)
    head_n = jnp.concatenate([inv_n, inv_n, z(HEAD_DIM - 2 * NSA_ROT_HALF)])
    lane_n = jnp.concatenate([head_n, head_n])
    lane_m = jnp.concatenate([z(MLA_NOPE), inv_m, inv_m, z(LANES - MLA_NOPE - MLA_ROPE)])
    inv = jnp.zeros((8, LANES), F32).at[0].set(lane_n).at[1].set(lane_m)
    posb = jnp.broadcast_to(positions.reshape(T, 1).astype(F32), (T, LANES))
    tm = 2048
    spec = pl.BlockSpec((tm, LANES), lambda i: (i, 0))
    return pl.pallas_call(
        _rope_kernel,
        grid=(T // tm,),
        in_specs=[spec, pl.BlockSpec((8, LANES), lambda i: (0, 0))],
        out_specs=[spec] * 4,
        out_shape=[jax.ShapeDtypeStruct((T, LANES), F32)] * 4,
        compiler_params=_params("parallel"),
        name="rope_tables",
    )(posb, inv)


def _rope_rows(t, offset, half, cos, sin):
    x1, x2 = t[offset:offset + half], t[offset + half:offset + 2 * half]
    return x1 * cos - x2 * sin, x2 * cos + x1 * sin


def _inproj_kernel(x_ref, sh_ref, sc_ref, g1_ref, ws_ref, wt_ref, cn_ref, sn_ref, cm_ref, sm_ref,
                   cnT_ref, snT_ref, cmT_ref, smT_ref, gq_ref, wuqT_ref, gkv_ref, wkm_ref, wvmT_ref,
                   qaT_ref, gT_ref, ck_ref, cv_ref, ska_ref, wk_ref, svT_ref, wvT_ref,
                   qmT_ref, km_ref, vmT_ref, cx_ref, mv_ref, mo_ref, cif_ref, *, per_b):
    tm = x_ref.shape[0]
    x = x_ref[...]
    h = _rms(x, g1_ref[...]) * (1.0 + sc_ref[0]) + sh_ref[0]
    hb = h.astype(BF16)

    def seg(start, width):
        return _dot(hb, ws_ref[:, start:start + width])

    def seg_t(start, height):
        return _dot_nt(wt_ref[start:start + height, :], hb)

    lane = lax.broadcasted_iota(jnp.int32, (1, LANES), 1)
    x1_n = (lane % HEAD_DIM) < NSA_ROT_HALF
    x1_m = lane < MLA_NOPE + MLA_ROPE // 2
    cn, sn, cm, sm = cn_ref[...], sn_ref[...], cm_ref[...], sm_ref[...]
    rope_n = lambda t: _rope(t, cn, sn, NSA_ROT_HALF, x1_n)
    rope_m = lambda t: _rope(t, cm, sm, MLA_ROPE // 2, x1_m)

    qt = seg_t(TSEG_Q, WIDTH_A)
    cnt, snt = cnT_ref[...], snT_ref[...]
    parts = []
    for hd in range(NSA_HEADS):
        o = HEAD_DIM * hd
        parts += list(_rope_rows(qt, o, NSA_ROT_HALF, cnt, snt)) + [qt[o + 2 * NSA_ROT_HALF:o + HEAD_DIM]]
    qaT_ref[...] = (jnp.concatenate(parts, axis=0) * (HEAD_DIM ** -0.5 * LOG2E)).astype(BF16)
    gT_ref[...] = _sigmoid(seg_t(TSEG_G, TSEG_G_ROWS))[:GATE_ROWS]
    svT_ref[...] = seg_t(TSEG_SV, LANES).astype(BF16)
    wvT_ref[...] = seg_t(TSEG_WV, LANES).astype(BF16)

    ckv = seg(SEG_CKV, 2 * LANES)
    ck_ref[...] = rope_n(ckv[:, :LANES])
    cv_ref[...] = ckv[:, LANES:]
    ska_ref[:, :LANES] = rope_n(seg(SEG_SK, LANES)).astype(BF16)
    srow = (pl.program_id(0) % per_b) * tm + lax.broadcasted_iota(jnp.int32, (tm, LANES), 0)
    lane2 = lax.broadcasted_iota(jnp.int32, (tm, LANES), 1)
    code = (lane2 < 4 * SLC_TOPK) & ((lane2 & (2 * SLC_TOPK - 1)) == srow // SLC_LEN)
    ska_ref[:, LANES:] = jnp.where(code, 1.0, 0.0).astype(BF16)
    wk_ref[...] = rope_n(seg(SEG_WK, LANES)).astype(BF16)

    qn = _rms(seg(SEG_QL, MLA_Q_LORA), gq_ref[...]).astype(BF16)
    qmt = _dot_nt(wuqT_ref[...], qn)
    cmt, smt = cmT_ref[...], smT_ref[...]
    parts = []
    for hd in range(MLA_HEADS):
        o = LANES * hd
        parts += [qmt[o:o + MLA_NOPE]] + list(_rope_rows(qmt, o + MLA_NOPE, MLA_ROPE // 2, cmt, smt))
        parts += [qmt[o + MLA_NOPE + MLA_ROPE:o + LANES]]
    qmT_ref[...] = (jnp.concatenate(parts, axis=0) * ((MLA_NOPE + MLA_ROPE) ** -0.5 * LOG2E)).astype(BF16)
    kvl_kr = seg(SEG_KVL, 2 * LANES)
    kvn = _rms(kvl_kr[:, :LANES], gkv_ref[...]).astype(BF16)
    kr = rope_m(kvl_kr[:, LANES:])
    kk = _dot(kvn, wkm_ref[...])
    for hd in range(MLA_HEADS):
        km_ref[:, LANES * hd:LANES * (hd + 1)] = (kk[:, LANES * hd:LANES * (hd + 1)] + kr).astype(BF16)
    vmT_ref[...] = _dot_nt(wvmT_ref[...], kvn).astype(BF16)

    cx_ref[...] = seg(SEG_CX, WIDTH_C)
    mv_ref[...] = seg(SEG_MV, WIDTH_C).astype(BF16)
    mo_ref[...] = seg(SEG_MO, WIDTH_C)
    cif_ref[...] = seg(SEG_IF, LANES)


def _inproj(x2, shift, scale, g1, w_std, w_t, tabs, tabs_t, gq, wuqT, gkv, wkm, wvmT, S):
    T = x2.shape[0]
    tm = TM_PROJ
    per_b = S // tm
    row = lambda w: pl.BlockSpec((tm, w), lambda i: (i, 0))
    col = lambda h: pl.BlockSpec((h, tm), lambda i: (0, i))
    full = lambda a: pl.BlockSpec(a.shape, lambda i: (0,) * a.ndim)
    modspec = pl.BlockSpec((1, 1, D_MODEL), lambda i: (i // per_b, 0, 0))
    outs = [(WIDTH_A, BF16, True), (GATE_ROWS, F32, True), (LANES, F32, False), (LANES, F32, False), (2 * LANES, BF16, False),
            (LANES, BF16, False), (LANES, BF16, True), (LANES, BF16, True),
            (MLA_HEADS * LANES, BF16, True), (MLA_HEADS * LANES, BF16, False), (WIDTH_B, BF16, True),
            (WIDTH_C, F32, False), (WIDTH_C, BF16, False), (WIDTH_C, F32, False), (LANES, F32, False)]
    return pl.pallas_call(
        functools.partial(_inproj_kernel, per_b=per_b),
        grid=(T // tm,),
        in_specs=[row(D_MODEL), modspec, modspec, full(g1), full(w_std), full(w_t)] + [row(LANES)] * 4
                 + [col(t.shape[0]) for t in tabs_t] + [full(gq), full(wuqT), full(gkv), full(wkm), full(wvmT)],
        out_specs=[col(w) if tr else row(w) for w, _, tr in outs],
        out_shape=[jax.ShapeDtypeStruct((w, T) if tr else (T, w), dt) for w, dt, tr in outs],
        compiler_params=_params("parallel"),
        name="in_proj",
    )(x2, shift, scale, g1, w_std, w_t, *tabs, *tabs_t, gq, wuqT, gkv, wkm, wvmT)


def _compress_kernel(xk_ref, xv_ref, wk_ref, wv_ref, pos_ref, wkf_ref, wvf_ref, kc_ref, vcT_ref):
    ng = xk_ref.shape[0] // CMP_STRIDE
    row = lax.broadcasted_iota(jnp.int32, (ng, LANES), 0)
    pos = pos_ref[...].astype(BF16)
    acc_k = jnp.zeros((ng, 2 * LANES), F32)
    acc_v = jnp.zeros((ng, 2 * LANES), F32)
    for t in range(CMP_STRIDE):
        tok = pl.ds(t, ng, stride=CMP_STRIDE)
        acc_k = acc_k + _dot(xk_ref[tok, :].astype(BF16), wk_ref[t])
        acc_v = acc_v + _dot(xv_ref[tok, :].astype(BF16), wv_ref[t])

    def finish(acc, wf_ref):
        const = _dot(pos, wf_ref[...].astype(BF16))
        both = acc[:, :LANES] + pltpu.roll(acc[:, LANES:], ng - 1, 0) + jnp.concatenate([const, const], axis=1)
        return jnp.where(row < ng - 1, both, 0.0)

    kc_ref[0] = finish(acc_k, wkf_ref).astype(BF16)
    vcT_ref[0] = finish(acc_v, wvf_ref).T.astype(BF16)


def _compress(ck, cv, cmp_pos, w_cmp_k, w_cmp_v, B, S):
    ng = S // CMP_STRIDE

    def per_token(w):
        a = w[:CMP_STRIDE * HEAD_DIM].reshape(CMP_STRIDE, HEAD_DIM, HEAD_DIM)
        b = w[CMP_STRIDE * HEAD_DIM:].reshape(CMP_STRIDE, HEAD_DIM, HEAD_DIM)
        z = jnp.zeros_like(a)
        top = jnp.concatenate([a, z, b, z], axis=2)
        bot = jnp.concatenate([z, a, z, b], axis=2)
        return jnp.concatenate([top, bot], axis=1).astype(BF16)

    full = lambda a: pl.BlockSpec(a.shape, lambda b: (0,) * a.ndim)
    wk3, wv3 = per_token(w_cmp_k), per_token(w_cmp_v)
    posf = cmp_pos.reshape(1, CMP_LEN * HEAD_DIM)
    ospec = pl.BlockSpec((1, ng, LANES), lambda b: (b, 0, 0))
    return pl.pallas_call(
        _compress_kernel,
        grid=(B,),
        in_specs=[pl.BlockSpec((S, LANES), lambda b: (b, 0)), pl.BlockSpec((S, LANES), lambda b: (b, 0)),
                  full(wk3), full(wv3), full(posf), full(w_cmp_k), full(w_cmp_v)],
        out_specs=[ospec, ospec],
        out_shape=[jax.ShapeDtypeStruct((B, ng, LANES), BF16)] * 2,
        compiler_params=_params("parallel"),
        name="nsa_compress",
    )(ck, cv, wk3, wv3, posf, w_cmp_k, w_cmp_v)


def _nsa_kernel(qT_ref, gT_ref, kc_ref, vcT_ref, sk_ref, svT_ref, wk_ref, wvT_ref, o_ref, qaug_scr, acc_scr, s_scr):
    tq = TQ_NSA
    cols = NSA_HEADS * tq
    pair = 2 * tq
    t0 = pl.program_id(1) * tq
    n_slc = SLC_TOPK * 2

    frow = lax.broadcasted_iota(jnp.int32, (LANES, tq), 0)
    g0_row = frow < HEAD_DIM
    tiles = [qT_ref[LANES * r:LANES * (r + 1), :] for r in range(NSA_REP)]
    zero = jnp.zeros_like(tiles[0])
    q6 = jnp.concatenate([jnp.where(g0_row, t, zero) for t in tiles]
                         + [jnp.where(g0_row, zero, t) for t in tiles], axis=1)
    qaug_scr[0:LANES, :] = q6
    tq_l = t0 + (lax.broadcasted_iota(jnp.int32, (1, cols), 1) & (tq - 1))

    pairs = [slice(pair * pp, pair * (pp + 1)) for pp in range(cols // pair)]
    s = _dot(kc_ref[0], q6)
    span = WINDOW + tq
    start = pl.multiple_of(jnp.maximum(t0 - WINDOW, 0), tq)
    kw = wk_ref[pl.ds(start, span), :]
    win_scores = [_dot(kw, q6[:, sl]) for sl in pairs]

    nrow = lax.broadcasted_iota(jnp.int32, (LANES, cols), 0)
    vis = nrow * CMP_STRIDE + (CMP_LEN - 1) <= tq_l
    s = jnp.where(vis, s, NEG)
    e = jnp.where(vis, jnp.exp2(s - jnp.max(s, axis=0, keepdims=True)), 0.0)
    den = jnp.sum(e, axis=0, keepdims=True)
    p = e / jnp.where(den > 0.0, den, 1.0)
    o_cmp = _dot(vcT_ref[0], p.astype(BF16))

    jr = lax.broadcasted_iota(jnp.int32, (n_slc, LANES), 0)
    nc = lax.broadcasted_iota(jnp.int32, (n_slc, LANES), 1)
    ovl = ((nc * CMP_STRIDE < jr * SLC_LEN + SLC_LEN) & (nc * CMP_STRIDE + CMP_LEN > jr * SLC_LEN)
           & (nc < LANES - 1))
    ovl = jnp.where(ovl, 1.0, 0.0).astype(BF16)
    jq = lax.broadcasted_iota(jnp.int32, (n_slc, tq), 0)
    tl = t0 + lax.broadcasted_iota(jnp.int32, (n_slc, tq), 1)
    cur = tl // SLC_LEN
    forced = (jq == 0) | (jq == cur) | (jq == cur - 1)
    future = jq * SLC_LEN > tl
    bias_t = []
    for g in range(NSA_KV_HEADS):
        pg = p[:, (3 * g) * tq:(3 * g + 1) * tq] + p[:, (3 * g + 1) * tq:(3 * g + 2) * tq] + p[:, (3 * g + 2) * tq:(3 * g + 3) * tq]
        imp = sum(_dot(ovl, part) for part in _split3(pg))
        imp = jnp.where(forced, jnp.inf, imp)
        imp = jnp.where(future, -jnp.inf, imp)
        rank = jnp.zeros((n_slc, tq), F32)
        for jp in range(n_slc):
            rv = imp[jp:jp + 1, :]
            ahead = jnp.where(rv > imp, 1.0, jnp.where((rv == imp) & (jq > jp), 1.0, 0.0))
            rank = rank + ahead
        bias_t.append(jnp.where(rank < float(SLC_TOPK), 0.0, NEG).astype(BF16))
    zb = jnp.zeros((n_slc, NSA_REP * tq), BF16)
    qaug_scr[LANES:LANES + n_slc, :] = jnp.concatenate([bias_t[0]] * NSA_REP + [zb], axis=1)
    qaug_scr[LANES + n_slc:LANES + 2 * n_slc, :] = jnp.concatenate([zb] + [bias_t[1]] * NSA_REP, axis=1)
    qaug_scr[LANES + 2 * n_slc:, :] = jnp.zeros((LANES - 2 * n_slc, cols), BF16)

    vwt = wvT_ref[:, pl.ds(start, span)]
    wrow = start + lax.broadcasted_iota(jnp.int32, (span, pair), 0)
    o_win = []
    for sl, sc in zip(pairs, win_scores):
        diff = tq_l[:, sl] - wrow
        sc = jnp.where((diff >= 0) & (diff < WINDOW), sc, NEG)
        ew = jnp.exp2(sc - jnp.max(sc, axis=0, keepdims=True))
        o_win.append(_dot(vwt, ew.astype(BF16)) / jnp.sum(ew, axis=0, keepdims=True))
    o_win = jnp.concatenate(o_win, axis=1)

    acc_scr[...] = jnp.zeros((LANES, cols), F32)
    krow = lax.broadcasted_iota(jnp.int32, (TK_SLC, pair), 0)

    def slc_scores(k0, sl):
        return _dot(sk_ref[pl.ds(k0, TK_SLC), :], qaug_scr[:, sl])

    for pp, sl in enumerate(pairs):
        s_scr[pp] = slc_scores(0, sl)

    def slc_tile(k0, m, l, masked):
        vt = svT_ref[:, pl.ds(k0, TK_SLC)]
        ms, ls, accs = [], [], []
        ahead = [s_scr[pp] for pp in range(len(pairs))]
        for pp, sl in enumerate(pairs):
            sc = ahead[pp]
            if masked:
                sc = jnp.where(k0 + krow <= tq_l[:, sl], sc, NEG)
            else:
                s_scr[pp] = slc_scores(pl.multiple_of(k0 + TK_SLC, TK_SLC), sl)
            m_new = jnp.maximum(m[:, sl], jnp.max(sc, axis=0, keepdims=True))
            alpha = jnp.exp2(m[:, sl] - m_new)
            pe = jnp.exp2(sc - m_new)
            ls.append(alpha * l[:, sl] + jnp.sum(pe, axis=0, keepdims=True))
            accs.append(alpha * acc_scr[:, sl] + _dot(vt, pe.astype(BF16)))
            ms.append(m_new)
        acc_scr[...] = jnp.concatenate(accs, axis=1)
        return jnp.concatenate(ms, axis=1), jnp.concatenate(ls, axis=1)

    def slc_step(kt, carry):
        return slc_tile(pl.multiple_of(kt * TK_SLC, TK_SLC), carry[0], carry[1], False)

    n_full = t0 // TK_SLC
    m, l = lax.fori_loop(0, n_full, slc_step, (jnp.full((1, cols), NEG, F32), jnp.zeros((1, cols), F32)))
    m, l = slc_tile(pl.multiple_of(n_full * TK_SLC, TK_SLC), m, l, True)
    o_slc = acc_scr[...] / l

    gt = gT_ref[...]
    grow = lambda j: jnp.concatenate([gt[3 * hd + j:3 * hd + j + 1, :] for hd in range(NSA_HEADS)], axis=1)
    mixed = grow(0) * o_cmp + grow(1) * o_slc + grow(2) * o_win
    for r in range(NSA_REP):
        t = jnp.where(g0_row, mixed[:, tq * r:tq * (r + 1)], mixed[:, tq * (NSA_REP + r):tq * (NSA_REP + r + 1)])
        o_ref[:, LANES * r:LANES * (r + 1)] = t.T


def _nsa(qaT, gT, kc, vcT, sk_aug, svT, wk, wvT, B, S):
    tq = TQ_NSA
    nq = S // tq
    cols = NSA_HEADS * tq
    qcol = lambda h: pl.BlockSpec((h, tq), lambda b, i: (0, b * nq + i))
    seq = lambda w: pl.BlockSpec((S, w), lambda b, i: (b, 0))
    seqT = pl.BlockSpec((LANES, S), lambda b, i: (0, b))
    cspec = pl.BlockSpec((1, LANES, LANES), lambda b, i: (b, 0, 0))
    return pl.pallas_call(
        _nsa_kernel,
        grid=(B, nq),
        in_specs=[qcol(WIDTH_A), qcol(gT.shape[0]), cspec, cspec, seq(2 * LANES), seqT, seq(LANES), seqT],
        out_specs=pl.BlockSpec((tq, WIDTH_A), lambda b, i: (b * nq + i, 0)),
        out_shape=jax.ShapeDtypeStruct((B * S, WIDTH_A), F32),
        scratch_shapes=[pltpu.VMEM((2 * LANES, cols), BF16), pltpu.VMEM((LANES, cols), F32),
                        pltpu.VMEM((cols // (2 * tq), TK_SLC, 2 * tq), F32)],
        compiler_params=_params("parallel", "arbitrary"),
        name="nsa_attention",
    )(qaT, gT, kc, vcT, sk_aug, svT, wk, wvT)


def _mla_kernel(qT_ref, k_ref, vT_ref, o_ref, acc_scr, s_scr):
    tq = TQ_MLA
    t0 = pl.program_id(1) * tq
    tq_l = t0 + lax.broadcasted_iota(jnp.int32, (1, tq), 1)
    krow = lax.broadcasted_iota(jnp.int32, (tq, tq), 0)
    acc_scr[...] = jnp.zeros((MLA_HEADS, LANES, tq), F32)

    def scores(k0, hd):
        k = k_ref[pl.ds(k0, tq), LANES * hd:LANES * (hd + 1)]
        return _dot(k, qT_ref[LANES * hd:LANES * (hd + 1), :])

    for hd in range(MXU_LOOKAHEAD):
        s_scr[hd] = scores(0, hd)

    def tile(k0, ms, ls, masked):
        new_m, new_l, accs = [], [], []
        ahead = [s_scr[hd] for hd in range(MXU_LOOKAHEAD)]
        for hd in range(MLA_HEADS):
            sc = ahead.pop(0)
            nxt = hd + MXU_LOOKAHEAD
            if nxt < MLA_HEADS:
                ahead.append(scores(k0, nxt))
            elif not masked:
                s_scr[nxt - MLA_HEADS] = scores(pl.multiple_of(k0 + tq, tq), nxt - MLA_HEADS)
            if masked:
                sc = jnp.where(k0 + krow <= tq_l, sc, NEG)
            m_new = jnp.maximum(ms[hd], jnp.max(sc, axis=0, keepdims=True))
            alpha = jnp.exp2(ms[hd] - m_new)
            pe = jnp.exp2(sc - m_new)
            new_l.append(alpha * ls[hd] + jnp.sum(pe, axis=0, keepdims=True))
            vt = vT_ref[LANES * (hd // 2):LANES * (hd // 2 + 1), pl.ds(k0, tq)]
            accs.append(alpha * acc_scr[hd] + _dot(vt, pe.astype(BF16)))
            new_m.append(m_new)
        acc_scr[...] = jnp.stack(accs)
        return tuple(new_m), tuple(new_l)

    def step(kt, carry):
        return tile(pl.multiple_of(kt * tq, tq), carry[0], carry[1], False)

    n_full = pl.program_id(1)
    init = (tuple(jnp.full((1, tq), NEG, F32) for _ in range(MLA_HEADS)),
            tuple(jnp.zeros((1, tq), F32) for _ in range(MLA_HEADS)))
    ms, ls = lax.fori_loop(0, n_full, step, init)
    ms, ls = tile(pl.multiple_of(n_full * tq, tq), ms, ls, True)
    frow = lax.broadcasted_iota(jnp.int32, (LANES, tq), 0)
    for pr in range(MLA_HEADS // 2):
        t = jnp.where(frow < MLA_V, acc_scr[2 * pr] / ls[2 * pr], acc_scr[2 * pr + 1] / ls[2 * pr + 1])
        o_ref[:, LANES * pr:LANES * (pr + 1)] = t.T


def _mla(qmT, km, vmT, B, S):
    tq = TQ_MLA
    nq = S // tq
    return pl.pallas_call(
        _mla_kernel,
        grid=(B, nq),
        in_specs=[pl.BlockSpec((MLA_HEADS * LANES, tq), lambda b, i: (0, b * nq + i)),
                  pl.BlockSpec((S, MLA_HEADS * LANES), lambda b, i: (b, 0)),
                  pl.BlockSpec((WIDTH_B, S), lambda b, i: (0, b))],
        out_specs=pl.BlockSpec((tq, WIDTH_B), lambda b, i: (b * nq + i, 0)),
        out_shape=jax.ShapeDtypeStruct((B * S, WIDTH_B), F32),
        scratch_shapes=[pltpu.VMEM((MLA_HEADS, LANES, tq), F32), pltpu.VMEM((MXU_LOOKAHEAD, tq, tq), F32)],
        compiler_params=_params("parallel", "arbitrary"),
        name="mla_attention",
    )(qmT, km, vmT)


def _mlstm_kernel(cx_ref, v_ref, o_ref, gc_ref, gr_ref, cw_ref, cb_ref, wq_ref, wk_ref,
                  bc_ref, br_ref, gmh_ref, skip_ref, y_ref, xc_scr, q_scr, k_scr):
    S = cx_ref.shape[0]
    L = MLSTM_CHUNK
    d = MLSTM_DIM
    pairs = MLSTM_HEADS // 2

    x = cx_ref[...]
    rowi = lax.broadcasted_iota(jnp.int32, (S, WIDTH_C), 0)
    conv = x * cw_ref[MLSTM_CONV - 1:MLSTM_CONV, :]
    for back in range(1, MLSTM_CONV):
        shifted = jnp.where(rowi >= back, pltpu.roll(x, back, 0), 0.0)
        conv = conv + shifted * cw_ref[MLSTM_CONV - 1 - back:MLSTM_CONV - back, :]
    xc = _silu(conv + cb_ref[...])
    xc_scr[...] = xc
    xcb = xc.astype(BF16)
    q_scr[...] = _dot(xcb, wq_ref[...]).astype(BF16)
    k_scr[...] = _dot(xcb, wk_ref[...]).astype(BF16)

    ri = lax.broadcasted_iota(jnp.int32, (L, L), 0)
    ci = lax.broadcasted_iota(jnp.int32, (L, L), 1)
    causal = ci <= ri
    tril = jnp.where(causal, 1.0, 0.0).astype(BF16)
    triu = jnp.where(ri <= ci, 1.0, 0.0).astype(BF16)
    lane = lax.broadcasted_iota(jnp.int32, (L, LANES), 1)
    h0_lane = lane < d
    h0_row1 = lax.broadcasted_iota(jnp.int32, (1, LANES), 1) < d
    blk = (lax.broadcasted_iota(jnp.int32, (LANES, LANES), 0) // d) == (lax.broadcasted_iota(jnp.int32, (LANES, LANES), 1) // d)
    rowsel = lax.broadcasted_iota(jnp.int32, (LANES, 1), 0) < d
    ones_row = jnp.ones((1, LANES), F32)

    def log_sig(z):
        return jnp.minimum(z, 0.0) - jnp.log1p(jnp.exp(-jnp.abs(z)))

    def chunk(c, carry):
        cs, ns, ms = carry
        r0 = pl.multiple_of(c * L, L)
        gcol = gc_ref[pl.ds(r0, L), :] + bc_ref[...]
        ig_col = gcol
        lf_col = log_sig(gcol)
        b_col = sum(_dot(tril, part) for part in _split3(lf_col))
        grow = gr_ref[0, c] + br_ref[...]
        lf_row = log_sig(grow)
        b_row = sum(_dot(part, triu) for part in _split3(lf_row))
        new_c, new_n, new_m_parts, outs = [], [], [], []
        for pr in range(pairs):
            qp = q_scr[pl.ds(r0, L), LANES * pr:LANES * (pr + 1)]
            kp = k_scr[pl.ds(r0, L), LANES * pr:LANES * (pr + 1)]
            vp = v_ref[pl.ds(r0, L), LANES * pr:LANES * (pr + 1)]
            kpf = kp.astype(F32)
            qc = _dot(qp, cs[pr].astype(BF16))
            qn_full = qp.astype(F32) * ns[pr]
            num_p, den_p, mt_p, wl_p = [], [], [], []
            e_old_l, e_loc_l, m_new_l = [], [], []
            for hh in range(2):
                hd = 2 * pr + hh
                igc = ig_col[:, hd:hd + 1]
                bc = b_col[:, MLSTM_HEADS + hd:MLSTM_HEADS + hd + 1]
                igr = grow[hd:hd + 1, :]
                brw = b_row[MLSTM_HEADS + hd:MLSTM_HEADS + hd + 1, :]
                b_last = bc[L - 1:L, :]
                m_prev = ms[:, hd:hd + 1]
                a_col = b_last - bc + igc
                m_loc = jnp.max(a_col, axis=0, keepdims=True)
                wl_p.append(jnp.exp(a_col - m_loc))
                m_new = jnp.maximum(b_last + m_prev, m_loc)
                e_old_l.append(jnp.exp(b_last + m_prev - m_new))
                e_loc_l.append(jnp.exp(m_loc - m_new))
                m_new_l.append(m_new)
                dmat = jnp.where(causal, bc - brw + igr, -jnp.inf)
                gl = bc + m_prev
                m_t = jnp.maximum(gl, jnp.max(dmat, axis=-1, keepdims=True))
                hsel = h0_lane if hh == 0 else lane >= d
                qh = jnp.where(hsel, qp, jnp.zeros_like(qp))
                sw = _dot_nt(qh, kp) * jnp.exp(dmat - m_t)
                inter = jnp.exp(gl - m_t)
                qn = jnp.sum(jnp.where(hsel, qn_full, 0.0), axis=-1, keepdims=True)
                num_p.append(inter * qc + _dot(sw.astype(BF16), vp))
                den_p.append(inter * qn + jnp.sum(sw, axis=-1, keepdims=True))
                mt_p.append(m_t)
            num = jnp.where(h0_lane, num_p[0], num_p[1])
            den = jnp.where(h0_lane, den_p[0], den_p[1])
            m_t2 = jnp.where(h0_lane, mt_p[0], mt_p[1])
            outs.append(num / jnp.maximum(jnp.abs(den), jnp.exp(-m_t2)))
            wl2 = jnp.where(h0_lane, wl_p[0], wl_p[1])
            kwf = kpf * wl2
            c_loc = jnp.where(blk, _dot_tn(kwf.astype(BF16), vp), 0.0)
            n_loc = jnp.sum(kwf, axis=0, keepdims=True)
            e_old_row = jnp.where(h0_row1, e_old_l[0], e_old_l[1])
            e_loc_row = jnp.where(h0_row1, e_loc_l[0], e_loc_l[1])
            e_old_colv = jnp.where(rowsel, e_old_l[0], e_old_l[1])
            e_loc_colv = jnp.where(rowsel, e_loc_l[0], e_loc_l[1])
            new_c.append(e_old_colv * cs[pr] + e_loc_colv * c_loc)
            new_n.append(e_old_row * ns[pr] + e_loc_row * n_loc)
            new_m_parts += m_new_l
        hcat = jnp.concatenate(outs, axis=1)
        hg = _sigmoid(o_ref[pl.ds(r0, L), :]) * hcat
        ys = []
        for hd in range(MLSTM_HEADS):
            part = hg[:, LANES * (hd // 2):LANES * (hd // 2 + 1)]
            hs = h0_lane if hd % 2 == 0 else lane >= d
            mu = jnp.sum(jnp.where(hs, part, 0.0), axis=-1, keepdims=True) / d
            cen = jnp.where(hs, part - mu, 0.0)
            var = jnp.sum(cen * cen, axis=-1, keepdims=True) / d
            ys.append(cen * lax.rsqrt(var + NORM_EPS))
        hn = jnp.concatenate([ys[0] + ys[1], ys[2] + ys[3]], axis=1)
        y_ref[pl.ds(r0, L), :] = hn * gmh_ref[...] + skip_ref[...] * xc_scr[pl.ds(r0, L), :]
        lane_m = lax.broadcasted_iota(jnp.int32, (1, LANES), 1)
        m_next = jnp.zeros((1, LANES), F32)
        for hd in range(MLSTM_HEADS):
            m_next = jnp.where(lane_m == hd, new_m_parts[hd] * ones_row, m_next)
        return tuple(new_c), tuple(new_n), m_next

    init = (tuple(jnp.zeros((LANES, LANES), F32) for _ in range(pairs)),
            tuple(jnp.zeros((1, LANES), F32) for _ in range(pairs)),
            jnp.zeros((1, LANES), F32))
    lax.fori_loop(0, S // L, chunk, init)


def _mlstm(cx, mv, mo, cif, conv_w, conv_b, w_q_m, w_k_m, b_igate, b_fgate, g_mh, skip_m, B, S):
    def blockdiag(w, scale):
        out = jnp.zeros((WIDTH_C, WIDTH_C), F32)
        for hd in range(MLSTM_HEADS):
            out = out.at[hd * MLSTM_DIM:(hd + 1) * MLSTM_DIM, hd * MLSTM_DIM:(hd + 1) * MLSTM_DIM].set(w[hd] * scale)
        return out.astype(BF16)

    wq = blockdiag(w_q_m, MLSTM_DIM ** -0.5)
    wk = blockdiag(w_k_m, 1.0)
    bias8 = jnp.concatenate([b_igate, b_fgate])
    bc = jnp.zeros((1, LANES), F32).at[0, :2 * MLSTM_HEADS].set(bias8)
    br = jnp.broadcast_to(bias8[:, None], (2 * MLSTM_HEADS, MLSTM_CHUNK))
    nc = S // MLSTM_CHUNK
    gr = cif[:, :2 * MLSTM_HEADS].reshape(B, nc, MLSTM_CHUNK, 2 * MLSTM_HEADS).transpose(0, 1, 3, 2)
    seq = lambda w: pl.BlockSpec((S, w), lambda b: (b, 0))
    full = lambda a: pl.BlockSpec(a.shape, lambda b: (0,) * a.ndim)
    row = lambda a: a.reshape(1, -1)
    args = [conv_w, row(conv_b), wq, wk, bc, br, row(g_mh), row(skip_m)]
    return pl.pallas_call(
        _mlstm_kernel,
        grid=(B,),
        in_specs=[seq(WIDTH_C), seq(WIDTH_C), seq(WIDTH_C), seq(LANES),
                  pl.BlockSpec((1, nc, 2 * MLSTM_HEADS, MLSTM_CHUNK), lambda b: (b, 0, 0, 0))] + [full(a) for a in args],
        out_specs=seq(WIDTH_C),
        out_shape=jax.ShapeDtypeStruct((B * S, WIDTH_C), F32),
        scratch_shapes=[pltpu.VMEM((S, WIDTH_C), F32), pltpu.VMEM((S, WIDTH_C), BF16), pltpu.VMEM((S, WIDTH_C), BF16)],
        compiler_params=_params("parallel"),
        name="mlstm_mixer",
    )(cx, mv, mo, cif, gr, *args)


def _outproj_kernel(ya_ref, yb_ref, yc_ref, x_ref, gate_ref, ga_ref, gb_ref, w_ref, o_ref):
    a = _rms(ya_ref[...], ga_ref[...]).astype(BF16)
    b = _rms(yb_ref[...], gb_ref[...]).astype(BF16)
    c = yc_ref[...].astype(BF16)
    y = _dot(a, w_ref[0:WIDTH_A, :]) + _dot(b, w_ref[WIDTH_A:WIDTH_A + WIDTH_B, :]) + _dot(c, w_ref[WIDTH_A + WIDTH_B:, :])
    o_ref[...] = x_ref[...] + gate_ref[0] * y


def _outproj(ya, yb, yc, x2, gate, ga, gb, w_out, S):
    T = x2.shape[0]
    tm = TM_PROJ
    per_b = S // tm
    row = lambda w: pl.BlockSpec((tm, w), lambda i: (i, 0))
    full = lambda a: pl.BlockSpec(a.shape, lambda i: (0,) * a.ndim)
    return pl.pallas_call(
        _outproj_kernel,
        grid=(T // tm,),
        in_specs=[row(WIDTH_A), row(WIDTH_B), row(WIDTH_C), row(D_MODEL),
                  pl.BlockSpec((1, 1, D_MODEL), lambda i: (i // per_b, 0, 0)), full(ga), full(gb), full(w_out)],
        out_specs=row(D_MODEL),
        out_shape=jax.ShapeDtypeStruct((T, D_MODEL), F32),
        compiler_params=_params("parallel"),
        name="out_proj",
    )(ya, yb, yc, x2, gate, ga, gb, w_out)


def _ffn_kernel(x_ref, sh_ref, sc_ref, gate_ref, g2_ref, wgu_ref, wd_ref, gf_ref, o_ref, *, final_norm):
    x = x_ref[...]
    h = (_rms(x, g2_ref[...]) * (1.0 + sc_ref[0]) + sh_ref[0]).astype(BF16)
    acc = jnp.zeros(x.shape, F32)
    for j in range(FFN_HIDDEN // FFN_CHUNK):
        gate = _dot(h, wgu_ref[:, FFN_CHUNK * j:FFN_CHUNK * (j + 1)])
        up = _dot(h, wgu_ref[:, FFN_HIDDEN + FFN_CHUNK * j:FFN_HIDDEN + FFN_CHUNK * (j + 1)])
        act = (_silu(gate) * up).astype(BF16)
        acc = acc + _dot(act, wd_ref[FFN_CHUNK * j:FFN_CHUNK * (j + 1), :])
    y = x + gate_ref[0] * acc
    if final_norm:
        y = _rms(y, gf_ref[...])
    o_ref[...] = y


def _ffn(x2, shift, scale, gate, g2, wgu, wd, gf, S, final_norm):
    T = x2.shape[0]
    tm = TM_PROJ
    per_b = S // tm
    row = pl.BlockSpec((tm, D_MODEL), lambda i: (i, 0))
    full = lambda a: pl.BlockSpec(a.shape, lambda i: (0,) * a.ndim)
    modspec = pl.BlockSpec((1, 1, D_MODEL), lambda i: (i // per_b, 0, 0))
    return pl.pallas_call(
        functools.partial(_ffn_kernel, final_norm=final_norm),
        grid=(T // tm,),
        in_specs=[row, modspec, modspec, modspec, full(g2), full(wgu), full(wd), full(gf)],
        out_specs=row,
        out_shape=jax.ShapeDtypeStruct((T, D_MODEL), F32),
        compiler_params=_params("parallel"),
        name="ffn_final" if final_norm else "ffn",
    )(x2, shift, scale, gate, g2, wgu, wd, gf)


def _head_tile_perm():
    idx = []
    for r in range(NSA_REP):
        idx += list(range(HEAD_DIM * r, HEAD_DIM * (r + 1)))
        idx += list(range(HEAD_DIM * (NSA_REP + r), HEAD_DIM * (NSA_REP + r + 1)))
    return np.asarray(idx, np.int32)


def _in_cols():
    std = np.full((N_STD,), -1, np.int64)
    tr = np.full((N_T,), -1, np.int64)
    off = 0
    tr[TSEG_Q:TSEG_Q + WIDTH_A] = _head_tile_perm()
    off += WIDTH_A
    std[SEG_CKV:SEG_CKV + 2 * LANES] = off + np.arange(2 * LANES)
    off += 2 * LANES
    std[SEG_SK:SEG_SK + LANES] = off + np.arange(LANES)
    off += LANES
    tr[TSEG_SV:TSEG_SV + LANES] = off + np.arange(LANES)
    off += LANES
    std[SEG_WK:SEG_WK + LANES] = off + np.arange(LANES)
    off += LANES
    tr[TSEG_WV:TSEG_WV + LANES] = off + np.arange(LANES)
    off += LANES
    tr[TSEG_G:TSEG_G + N_GATES] = off + np.arange(N_GATES)
    off += N_GATES
    std[SEG_QL:SEG_QL + MLA_Q_LORA] = off + np.arange(MLA_Q_LORA)
    off += MLA_Q_LORA
    std[SEG_KVL:SEG_KVL + MLA_KV_LORA] = off + np.arange(MLA_KV_LORA)
    off += MLA_KV_LORA
    std[SEG_KR + MLA_NOPE:SEG_KR + MLA_NOPE + MLA_ROPE] = off + np.arange(MLA_ROPE)
    off += MLA_ROPE
    for seg in (SEG_CX, SEG_MV, SEG_MO):
        std[seg:seg + WIDTH_C] = off + np.arange(WIDTH_C)
        off += WIDTH_C
    std[SEG_IF:SEG_IF + 2 * MLSTM_HEADS] = off + np.arange(2 * MLSTM_HEADS)
    return std, tr


def _gather_cols(w, cols):
    g = jnp.take(w, jnp.asarray(np.maximum(cols, 0), jnp.int32), axis=1)
    return jnp.where(jnp.asarray(cols >= 0)[None, :], g, 0.0)


def _layer_weights(l, w_in, w_uq, w_ukv, w_out, w_gu, w_down, g_out_a):
    std, tr = _in_cols()
    w_std = _gather_cols(w_in[l], std).astype(BF16)
    w_t = _gather_cols(w_in[l], tr).T.astype(BF16)
    cq = np.full((MLA_HEADS * LANES,), -1, np.int64)
    ck = np.full((MLA_HEADS * LANES,), -1, np.int64)
    cv = np.zeros((WIDTH_B,), np.int64)
    dq = MLA_NOPE + MLA_ROPE
    dkv = MLA_NOPE + MLA_V
    for hd in range(MLA_HEADS):
        cq[LANES * hd:LANES * hd + dq] = dq * hd + np.arange(dq)
        ck[LANES * hd:LANES * hd + MLA_NOPE] = dkv * hd + np.arange(MLA_NOPE)
        cv[MLA_V * hd:MLA_V * (hd + 1)] = dkv * hd + MLA_NOPE + np.arange(MLA_V)
    wuqT = _gather_cols(w_uq[l], cq).T.astype(BF16)
    wkm = _gather_cols(w_ukv[l], ck).astype(BF16)
    wvmT = _gather_cols(w_ukv[l], cv).T.astype(BF16)
    perm = _head_tile_perm()
    wo = jnp.concatenate([w_out[l][perm], w_out[l][WIDTH_A:]], axis=0).astype(BF16)
    ga = g_out_a[l][perm].reshape(1, WIDTH_A)
    return w_std, w_t, wuqT, wkm, wvmT, wo, ga, w_gu[l].astype(BF16), w_down[l].astype(BF16)


def kernel(x, c, positions, g_norm1, g_norm2, w_ada, b_ada, w_in, cmp_pos, w_cmp_k, w_cmp_v, g_out_a, g_q_lora, w_uq, g_kv_lora, w_ukv, g_out_b, conv_w, conv_b, w_q_m, w_k_m, b_igate, b_fgate, g_mh, skip_m, w_out, w_gu, w_down, g_final):
    B, S, D = x.shape
    T = B * S
    x2 = x.reshape(T, D)
    tabs = _rope_tables(positions)
    cn, sn, cm, sm = tabs
    tabs_t = (cn[:, :NSA_ROT_HALF].T, sn[:, :NSA_ROT_HALF].T,
              cm[:, MLA_NOPE:MLA_NOPE + MLA_ROPE // 2].T, sm[:, MLA_NOPE:MLA_NOPE + MLA_ROPE // 2].T)
    mod = _ada(c, w_ada, b_ada)
    row = lambda v: v.reshape(1, -1)
    for l in range(DEPTH):
        w_std, w_t, wuqT, wkm, wvmT, wo, ga, wgu, wd = _layer_weights(l, w_in, w_uq, w_ukv, w_out, w_gu, w_down, g_out_a)
        m6 = [mod[l, :, D * i:D * (i + 1)].reshape(B, 1, D) for i in range(6)]
        shift1, scale1, gate1, shift2, scale2, gate2 = m6
        (qaT, gT, ck, cv, sk_aug, wk, svT, wvT, qmT, km, vmT, cx, mv, mo, cif) = _inproj(
            x2, shift1, scale1, row(g_norm1[l]), w_std, w_t, tabs, tabs_t, row(g_q_lora[l]), wuqT,
            row(g_kv_lora[l]), wkm, wvmT, S)
        kc, vcT = _compress(ck, cv, cmp_pos[l], w_cmp_k[l], w_cmp_v[l], B, S)
        ya = _nsa(qaT, gT, kc, vcT, sk_aug, svT, wk, wvT, B, S)
        yb = _mla(qmT, km, vmT, B, S)
        yc = _mlstm(cx, mv, mo, cif, conv_w[l], conv_b[l], w_q_m[l], w_k_m[l], b_igate[l], b_fgate[l],
                    g_mh[l], skip_m[l], B, S)
        x2 = _outproj(ya, yb, yc, x2, gate1, ga, row(g_out_b[l]), wo, S)
        x2 = _ffn(x2, shift2, scale2, gate2, row(g_norm2[l]), wgu, wd, row(g_final), S, final_norm=(l == DEPTH - 1))
    return x2.reshape(B, S, D)
```

```python
import functools

import numpy as np
import jax
import jax.numpy as jnp
from jax import lax
from jax.experimental import pallas as pl
from jax.experimental.pallas import tpu as pltpu

F32 = jnp.float32
BF16 = jnp.bfloat16

D_MODEL = 1024
DEPTH = 2
HEAD_DIM = 64
ROPE_THETA = 500000.0
NSA_ROT_HALF = HEAD_DIM // 8
NORM_EPS = 1e-6

NSA_HEADS = 6
NSA_KV_HEADS = 2
NSA_REP = NSA_HEADS // NSA_KV_HEADS
CMP_LEN = 32
CMP_STRIDE = 16
SLC_LEN = 64
SLC_TOPK = 16
WINDOW = 512

MLA_HEADS = 6
MLA_Q_LORA = 256
MLA_KV_LORA = 128
MLA_NOPE = 64
MLA_ROPE = 32
MLA_V = 64

MLSTM_HEADS = 4
MLSTM_DIM = 64
MLSTM_CONV = 4
MLSTM_CHUNK = 64

WIDTH_A = NSA_HEADS * HEAD_DIM
WIDTH_B = MLA_HEADS * MLA_V
WIDTH_C = MLSTM_HEADS * MLSTM_DIM
FFN_HIDDEN = 2816
N_GATES = 3 * NSA_HEADS
GATE_ROWS = 24

LANES = 128
NEG = -1e30
LOG2E = 1.4426950408889634
VMEM_LIMIT = 56 * 1024 * 1024

TM_PROJ = 512
TQ_NSA = 128
TK_SLC = 256
TQ_MLA = 256
FFN_CHUNK = 256
MLSTM_TILE = 128
MLSTM_GROUP = 2
MXU_LOOKAHEAD = 4

SEG_CKV, SEG_SK, SEG_WK, SEG_QL, SEG_KVL, SEG_KR, SEG_CX, SEG_MV, SEG_MO, SEG_IF = (
    0, 256, 384, 512, 768, 896, 1024, 1280, 1536, 1792)
N_STD = 2048
TSEG_Q, TSEG_G, TSEG_SV, TSEG_WV = 0, 384, 416, 544
TSEG_G_ROWS = 32
N_T = 672


def _params(*sem):
    return pltpu.CompilerParams(dimension_semantics=sem, vmem_limit_bytes=VMEM_LIMIT)


def _dot(a, b):
    return jnp.dot(a, b, preferred_element_type=F32)


def _dot_nt(a, b):
    return lax.dot_general(a, b, (((1,), (1,)), ((), ())), preferred_element_type=F32)


def _dot_tn(a, b):
    return lax.dot_general(a, b, (((0,), (0,)), ((), ())), preferred_element_type=F32)


def _split3(x):
    hi = x.astype(BF16)
    r1 = x - hi.astype(F32)
    mid = r1.astype(BF16)
    lo = (r1 - mid.astype(F32)).astype(BF16)
    return hi, mid, lo


def _rms(x, g):
    return x * lax.rsqrt(jnp.mean(x * x, axis=-1, keepdims=True) + NORM_EPS) * g


def _sigmoid(x):
    return 1.0 / (1.0 + jnp.exp(-x))


def _silu(x):
    return x * _sigmoid(x)


def _rope(x, cos, sin, half, x1_lane):
    xr = jnp.where(x1_lane, -pltpu.roll(x, LANES - half, 1), pltpu.roll(x, half, 1))
    return x * cos + xr * sin


def _ada_kernel(c_ref, w_ref, b_ref, o_ref):
    c = c_ref[...]
    ca = _silu(c).astype(BF16)
    o_ref[0] = _dot(ca, w_ref[0].astype(BF16)) + b_ref[0]


def _ada(c, w_ada, b_ada):
    L, D, N = w_ada.shape
    B = c.shape[0]
    tn = 1536
    return pl.pallas_call(
        _ada_kernel,
        grid=(L, N // tn),
        in_specs=[pl.BlockSpec((B, D), lambda l, j: (0, 0)),
                  pl.BlockSpec((1, D, tn), lambda l, j: (l, 0, j)),
                  pl.BlockSpec((1, 1, tn), lambda l, j: (l, 0, j))],
        out_specs=pl.BlockSpec((1, B, tn), lambda l, j: (l, 0, j)),
        out_shape=jax.ShapeDtypeStruct((L, B, N), F32),
        compiler_params=_params("parallel", "parallel"),
        name="ada_mod",
    )(c, w_ada, b_ada.reshape(L, 1, N))


def _rope_kernel(pos_ref, inv_ref, cn_ref, sn_ref, cm_ref, sm_ref):
    pos = pos_ref[...]
    ang_n = pos * inv_ref[0:1, :]
    ang_m = pos * inv_ref[1:2, :]
    cn_ref[...] = jnp.cos(ang_n)
    sn_ref[...] = jnp.sin(ang_n)
    cm_ref[...] = jnp.cos(ang_m)
    sm_ref[...] = jnp.sin(ang_m)


def _rope_tables(positions):
    T = positions.size
    inv_n = jnp.power(ROPE_THETA, -jnp.arange(0, 2 * NSA_ROT_HALF, 2, dtype=F32) / (2 * NSA_ROT_HALF))
    inv_m = jnp.power(ROPE_THETA, -jnp.arange(0, MLA_ROPE, 2, dtype=F32) / MLA_ROPE)
    z = lambda n: jnp.zeros((n,), F32)
    head_n = jnp.concatenate([inv_n, inv_n, z(HEAD_DIM - 2 * NSA_ROT_HALF)])
    lane_n = jnp.concatenate([head_n, head_n])
    lane_m = jnp.concatenate([z(MLA_NOPE), inv_m, inv_m, z(LANES - MLA_NOPE - MLA_ROPE)])
    inv = jnp.zeros((8, LANES), F32).at[0].set(lane_n).at[1].set(lane_m)
    posb = jnp.broadcast_to(positions.reshape(T, 1).astype(F32), (T, LANES))
    tm = 2048
    spec = pl.BlockSpec((tm, LANES), lambda i: (i, 0))
    return pl.pallas_call(
        _rope_kernel,
        grid=(T // tm,),
        in_specs=[spec, pl.BlockSpec((8, LANES), lambda i: (0, 0))],
        out_specs=[spec] * 4,
        out_shape=[jax.ShapeDtypeStruct((T, LANES), F32)] * 4,
        compiler_params=_params("parallel"),
        name="rope_tables",
    )(posb, inv)


def _rope_rows(t, offset, half, cos, sin):
    x1, x2 = t[offset:offset + half], t[offset + half:offset + 2 * half]
    return x1 * cos - x2 * sin, x2 * cos + x1 * sin


def _inproj_kernel(x_ref, sh_ref, sc_ref, g1_ref, ws_ref, wt_ref, cn_ref, sn_ref, cm_ref, sm_ref,
                   cnT_ref, snT_ref, cmT_ref, smT_ref, gq_ref, wuqT_ref, gkv_ref, wkm_ref, wvmT_ref,
                   qaT_ref, gT_ref, ck_ref, cv_ref, ska_ref, wk_ref, svT_ref, wvT_ref,
                   qmT_ref, km_ref, vmT_ref, cx_ref, mv_ref, mo_ref, cif_ref, *, per_b):
    tm = x_ref.shape[0]
    x = x_ref[...]
    h = _rms(x, g1_ref[...]) * (1.0 + sc_ref[0]) + sh_ref[0]
    hb = h.astype(BF16)

    def seg(start, width):
        return _dot(hb, ws_ref[:, start:start + width])

    def seg_t(start, height):
        return _dot_nt(wt_ref[start:start + height, :], hb)

    lane = lax.broadcasted_iota(jnp.int32, (1, LANES), 1)
    x1_n = (lane % HEAD_DIM) < NSA_ROT_HALF
    x1_m = lane < MLA_NOPE + MLA_ROPE // 2
    cn, sn, cm, sm = cn_ref[...], sn_ref[...], cm_ref[...], sm_ref[...]
    rope_n = lambda t: _rope(t, cn, sn, NSA_ROT_HALF, x1_n)
    rope_m = lambda t: _rope(t, cm, sm, MLA_ROPE // 2, x1_m)

    qt = seg_t(TSEG_Q, WIDTH_A)
    cnt, snt = cnT_ref[...], snT_ref[...]
    parts = []
    for hd in range(NSA_HEADS):
        o = HEAD_DIM * hd
        parts += list(_rope_rows(qt, o, NSA_ROT_HALF, cnt, snt)) + [qt[o + 2 * NSA_ROT_HALF:o + HEAD_DIM]]
    qaT_ref[...] = (jnp.concatenate(parts, axis=0) * (HEAD_DIM ** -0.5 * LOG2E)).astype(BF16)
    gT_ref[...] = _sigmoid(seg_t(TSEG_G, TSEG_G_ROWS))[:GATE_ROWS]
    svT_ref[...] = seg_t(TSEG_SV, LANES).astype(BF16)
    wvT_ref[...] = seg_t(TSEG_WV, LANES).astype(BF16)

    ckv = seg(SEG_CKV, 2 * LANES)
    ck_ref[...] = rope_n(ckv[:, :LANES])
    cv_ref[...] = ckv[:, LANES:]
    ska_ref[:, :LANES] = rope_n(seg(SEG_SK, LANES)).astype(BF16)
    srow = (pl.program_id(0) % per_b) * tm + lax.broadcasted_iota(jnp.int32, (tm, LANES), 0)
    lane2 = lax.broadcasted_iota(jnp.int32, (tm, LANES), 1)
    code = (lane2 < 4 * SLC_TOPK) & ((lane2 & (2 * SLC_TOPK - 1)) == srow // SLC_LEN)
    ska_ref[:, LANES:] = jnp.where(code, 1.0, 0.0).astype(BF16)
    wk_ref[...] = rope_n(seg(SEG_WK, LANES)).astype(BF16)

    qn = _rms(seg(SEG_QL, MLA_Q_LORA), gq_ref[...]).astype(BF16)
    qmt = _dot_nt(wuqT_ref[...], qn)
    cmt, smt = cmT_ref[...], smT_ref[...]
    parts = []
    for hd in range(MLA_HEADS):
        o = LANES * hd
        parts += [qmt[o:o + MLA_NOPE]] + list(_rope_rows(qmt, o + MLA_NOPE, MLA_ROPE // 2, cmt, smt))
        parts += [qmt[o + MLA_NOPE + MLA_ROPE:o + LANES]]
    qmT_ref[...] = (jnp.concatenate(parts, axis=0) * ((MLA_NOPE + MLA_ROPE) ** -0.5 * LOG2E)).astype(BF16)
    kvl_kr = seg(SEG_KVL, 2 * LANES)
    kvn = _rms(kvl_kr[:, :LANES], gkv_ref[...]).astype(BF16)
    kr = rope_m(kvl_kr[:, LANES:])
    kk = _dot(kvn, wkm_ref[...])
    for hd in range(MLA_HEADS):
        km_ref[:, LANES * hd:LANES * (hd + 1)] = (kk[:, LANES * hd:LANES * (hd + 1)] + kr).astype(BF16)
    vmT_ref[...] = _dot_nt(wvmT_ref[...], kvn).astype(BF16)

    cx_ref[...] = seg(SEG_CX, WIDTH_C)
    mv_ref[...] = seg(SEG_MV, WIDTH_C).astype(BF16)
    mo_ref[...] = seg(SEG_MO, WIDTH_C)
    cif_ref[...] = seg(SEG_IF, 2 * LANES)


def _inproj(x2, shift, scale, g1, w_std, w_t, tabs, tabs_t, gq, wuqT, gkv, wkm, wvmT, S):
    T = x2.shape[0]
    tm = TM_PROJ
    per_b = S // tm
    row = lambda w: pl.BlockSpec((tm, w), lambda i: (i, 0))
    col = lambda h: pl.BlockSpec((h, tm), lambda i: (0, i))
    full = lambda a: pl.BlockSpec(a.shape, lambda i: (0,) * a.ndim)
    modspec = pl.BlockSpec((1, 1, D_MODEL), lambda i: (i // per_b, 0, 0))
    outs = [(WIDTH_A, BF16, True), (GATE_ROWS, F32, True), (LANES, F32, False), (LANES, F32, False), (2 * LANES, BF16, False),
            (LANES, BF16, False), (LANES, BF16, True), (LANES, BF16, True),
            (MLA_HEADS * LANES, BF16, True), (MLA_HEADS * LANES, BF16, False), (WIDTH_B, BF16, True),
            (WIDTH_C, F32, False), (WIDTH_C, BF16, False), (WIDTH_C, F32, False), (2 * LANES, F32, False)]
    return pl.pallas_call(
        functools.partial(_inproj_kernel, per_b=per_b),
        grid=(T // tm,),
        in_specs=[row(D_MODEL), modspec, modspec, full(g1), full(w_std), full(w_t)] + [row(LANES)] * 4
                 + [col(t.shape[0]) for t in tabs_t] + [full(gq), full(wuqT), full(gkv), full(wkm), full(wvmT)],
        out_specs=[col(w) if tr else row(w) for w, _, tr in outs],
        out_shape=[jax.ShapeDtypeStruct((w, T) if tr else (T, w), dt) for w, dt, tr in outs],
        compiler_params=_params("parallel"),
        name="in_proj",
    )(x2, shift, scale, g1, w_std, w_t, *tabs, *tabs_t, gq, wuqT, gkv, wkm, wvmT)


def _compress_kernel(xk_ref, xv_ref, wk_ref, wv_ref, pos_ref, wkf_ref, wvf_ref, kc_ref, vcT_ref):
    ng = xk_ref.shape[0] // CMP_STRIDE
    row = lax.broadcasted_iota(jnp.int32, (ng, LANES), 0)
    pos = pos_ref[...].astype(BF16)
    acc_k = jnp.zeros((ng, 2 * LANES), F32)
    acc_v = jnp.zeros((ng, 2 * LANES), F32)
    for t in range(CMP_STRIDE):
        tok = pl.ds(t, ng, stride=CMP_STRIDE)
        acc_k = acc_k + _dot(xk_ref[tok, :].astype(BF16), wk_ref[t])
        acc_v = acc_v + _dot(xv_ref[tok, :].astype(BF16), wv_ref[t])

    def finish(acc, wf_ref):
        const = _dot(pos, wf_ref[...].astype(BF16))
        both = acc[:, :LANES] + pltpu.roll(acc[:, LANES:], ng - 1, 0) + jnp.concatenate([const, const], axis=1)
        return jnp.where(row < ng - 1, both, 0.0)

    kc_ref[0] = finish(acc_k, wkf_ref).astype(BF16)
    vcT_ref[0] = finish(acc_v, wvf_ref).T.astype(BF16)


def _compress(ck, cv, cmp_pos, w_cmp_k, w_cmp_v, B, S):
    ng = S // CMP_STRIDE

    def per_token(w):
        a = w[:CMP_STRIDE * HEAD_DIM].reshape(CMP_STRIDE, HEAD_DIM, HEAD_DIM)
        b = w[CMP_STRIDE * HEAD_DIM:].reshape(CMP_STRIDE, HEAD_DIM, HEAD_DIM)
        z = jnp.zeros_like(a)
        top = jnp.concatenate([a, z, b, z], axis=2)
        bot = jnp.concatenate([z, a, z, b], axis=2)
        return jnp.concatenate([top, bot], axis=1).astype(BF16)

    full = lambda a: pl.BlockSpec(a.shape, lambda b: (0,) * a.ndim)
    wk3, wv3 = per_token(w_cmp_k), per_token(w_cmp_v)
    posf = cmp_pos.reshape(1, CMP_LEN * HEAD_DIM)
    ospec = pl.BlockSpec((1, ng, LANES), lambda b: (b, 0, 0))
    return pl.pallas_call(
        _compress_kernel,
        grid=(B,),
        in_specs=[pl.BlockSpec((S, LANES), lambda b: (b, 0)), pl.BlockSpec((S, LANES), lambda b: (b, 0)),
                  full(wk3), full(wv3), full(posf), full(w_cmp_k), full(w_cmp_v)],
        out_specs=[ospec, ospec],
        out_shape=[jax.ShapeDtypeStruct((B, ng, LANES), BF16)] * 2,
        compiler_params=_params("parallel"),
        name="nsa_compress",
    )(ck, cv, wk3, wv3, posf, w_cmp_k, w_cmp_v)


def _nsa_kernel(qT_ref, gT_ref, kc_ref, vcT_ref, sk_ref, svT_ref, wk_ref, wvT_ref, o_ref, qaug_scr, acc_scr, s_scr):
    tq = TQ_NSA
    cols = NSA_HEADS * tq
    pair = 2 * tq
    t0 = pl.program_id(1) * tq
    n_slc = SLC_TOPK * 2

    frow = lax.broadcasted_iota(jnp.int32, (LANES, tq), 0)
    g0_row = frow < HEAD_DIM
    tiles = [qT_ref[LANES * r:LANES * (r + 1), :] for r in range(NSA_REP)]
    zero = jnp.zeros_like(tiles[0])
    q6 = jnp.concatenate([jnp.where(g0_row, t, zero) for t in tiles]
                         + [jnp.where(g0_row, zero, t) for t in tiles], axis=1)
    qaug_scr[0:LANES, :] = q6
    tq_l = t0 + (lax.broadcasted_iota(jnp.int32, (1, cols), 1) & (tq - 1))

    pairs = [slice(pair * pp, pair * (pp + 1)) for pp in range(cols // pair)]
    s = _dot(kc_ref[0], q6)
    span = WINDOW + tq
    start = pl.multiple_of(jnp.maximum(t0 - WINDOW, 0), tq)
    kw = wk_ref[pl.ds(start, span), :]
    win_scores = [_dot(kw, q6[:, sl]) for sl in pairs]

    nrow = lax.broadcasted_iota(jnp.int32, (LANES, cols), 0)
    vis = nrow * CMP_STRIDE + (CMP_LEN - 1) <= tq_l
    s = jnp.where(vis, s, NEG)
    e = jnp.where(vis, jnp.exp2(s - jnp.max(s, axis=0, keepdims=True)), 0.0)
    den = jnp.sum(e, axis=0, keepdims=True)
    p = e / jnp.where(den > 0.0, den, 1.0)
    o_cmp = _dot(vcT_ref[0], p.astype(BF16))

    jr = lax.broadcasted_iota(jnp.int32, (n_slc, LANES), 0)
    nc = lax.broadcasted_iota(jnp.int32, (n_slc, LANES), 1)
    ovl = ((nc * CMP_STRIDE < jr * SLC_LEN + SLC_LEN) & (nc * CMP_STRIDE + CMP_LEN > jr * SLC_LEN)
           & (nc < LANES - 1))
    ovl = jnp.where(ovl, 1.0, 0.0).astype(BF16)
    jq = lax.broadcasted_iota(jnp.int32, (n_slc, tq), 0)
    tl = t0 + lax.broadcasted_iota(jnp.int32, (n_slc, tq), 1)
    cur = tl // SLC_LEN
    forced = (jq == 0) | (jq == cur) | (jq == cur - 1)
    future = jq * SLC_LEN > tl
    bias_t = []
    for g in range(NSA_KV_HEADS):
        pg = p[:, (3 * g) * tq:(3 * g + 1) * tq] + p[:, (3 * g + 1) * tq:(3 * g + 2) * tq] + p[:, (3 * g + 2) * tq:(3 * g + 3) * tq]
        imp = sum(_dot(ovl, part) for part in _split3(pg))
        imp = jnp.where(forced, jnp.inf, imp)
        imp = jnp.where(future, -jnp.inf, imp)
        rank = jnp.zeros((n_slc, tq), F32)
        for jp in range(n_slc):
            rv = imp[jp:jp + 1, :]
            ahead = jnp.where(rv > imp, 1.0, jnp.where((rv == imp) & (jq > jp), 1.0, 0.0))
            rank = rank + ahead
        bias_t.append(jnp.where(rank < float(SLC_TOPK), 0.0, NEG).astype(BF16))
    zb = jnp.zeros((n_slc, NSA_REP * tq), BF16)
    qaug_scr[LANES:LANES + n_slc, :] = jnp.concatenate([bias_t[0]] * NSA_REP + [zb], axis=1)
    qaug_scr[LANES + n_slc:LANES + 2 * n_slc, :] = jnp.concatenate([zb] + [bias_t[1]] * NSA_REP, axis=1)
    qaug_scr[LANES + 2 * n_slc:, :] = jnp.zeros((LANES - 2 * n_slc, cols), BF16)

    vwt = wvT_ref[:, pl.ds(start, span)]
    wrow = start + lax.broadcasted_iota(jnp.int32, (span, pair), 0)
    o_win = []
    for sl, sc in zip(pairs, win_scores):
        diff = tq_l[:, sl] - wrow
        sc = jnp.where((diff >= 0) & (diff < WINDOW), sc, NEG)
        ew = jnp.exp2(sc - jnp.max(sc, axis=0, keepdims=True))
        o_win.append(_dot(vwt, ew.astype(BF16)) / jnp.sum(ew, axis=0, keepdims=True))
    o_win = jnp.concatenate(o_win, axis=1)

    acc_scr[...] = jnp.zeros((LANES, cols), F32)
    krow = lax.broadcasted_iota(jnp.int32, (TK_SLC, pair), 0)

    def slc_scores(k0, sl):
        return _dot(sk_ref[pl.ds(k0, TK_SLC), :], qaug_scr[:, sl])

    for pp, sl in enumerate(pairs):
        s_scr[pp] = slc_scores(0, sl)

    def slc_tile(k0, m, l, masked):
        vt = svT_ref[:, pl.ds(k0, TK_SLC)]
        ms, ls, accs = [], [], []
        ahead = [s_scr[pp] for pp in range(len(pairs))]
        for pp, sl in enumerate(pairs):
            sc = ahead[pp]
            if masked:
                sc = jnp.where(k0 + krow <= tq_l[:, sl], sc, NEG)
            else:
                s_scr[pp] = slc_scores(pl.multiple_of(k0 + TK_SLC, TK_SLC), sl)
            m_new = jnp.maximum(m[:, sl], jnp.max(sc, axis=0, keepdims=True))
            alpha = jnp.exp2(m[:, sl] - m_new)
            pe = jnp.exp2(sc - m_new)
            ls.append(alpha * l[:, sl] + jnp.sum(pe, axis=0, keepdims=True))
            accs.append(alpha * acc_scr[:, sl] + _dot(vt, pe.astype(BF16)))
            ms.append(m_new)
        acc_scr[...] = jnp.concatenate(accs, axis=1)
        return jnp.concatenate(ms, axis=1), jnp.concatenate(ls, axis=1)

    def slc_step(kt, carry):
        return slc_tile(pl.multiple_of(kt * TK_SLC, TK_SLC), carry[0], carry[1], False)

    n_full = t0 // TK_SLC
    m, l = lax.fori_loop(0, n_full, slc_step, (jnp.full((1, cols), NEG, F32), jnp.zeros((1, cols), F32)))
    m, l = slc_tile(pl.multiple_of(n_full * TK_SLC, TK_SLC), m, l, True)
    o_slc = acc_scr[...] / l

    gt = gT_ref[...]
    grow = lambda j: jnp.concatenate([gt[3 * hd + j:3 * hd + j + 1, :] for hd in range(NSA_HEADS)], axis=1)
    mixed = grow(0) * o_cmp + grow(1) * o_slc + grow(2) * o_win
    for r in range(NSA_REP):
        t = jnp.where(g0_row, mixed[:, tq * r:tq * (r + 1)], mixed[:, tq * (NSA_REP + r):tq * (NSA_REP + r + 1)])
        o_ref[:, LANES * r:LANES * (r + 1)] = t.T


def _nsa(qaT, gT, kc, vcT, sk_aug, svT, wk, wvT, B, S):
    tq = TQ_NSA
    nq = S // tq
    cols = NSA_HEADS * tq
    qcol = lambda h: pl.BlockSpec((h, tq), lambda b, i: (0, b * nq + i))
    seq = lambda w: pl.BlockSpec((S, w), lambda b, i: (b, 0))
    seqT = pl.BlockSpec((LANES, S), lambda b, i: (0, b))
    cspec = pl.BlockSpec((1, LANES, LANES), lambda b, i: (b, 0, 0))
    return pl.pallas_call(
        _nsa_kernel,
        grid=(B, nq),
        in_specs=[qcol(WIDTH_A), qcol(gT.shape[0]), cspec, cspec, seq(2 * LANES), seqT, seq(LANES), seqT],
        out_specs=pl.BlockSpec((tq, WIDTH_A), lambda b, i: (b * nq + i, 0)),
        out_shape=jax.ShapeDtypeStruct((B * S, WIDTH_A), F32),
        scratch_shapes=[pltpu.VMEM((2 * LANES, cols), BF16), pltpu.VMEM((LANES, cols), F32),
                        pltpu.VMEM((cols // (2 * tq), TK_SLC, 2 * tq), F32)],
        compiler_params=_params("parallel", "arbitrary"),
        name="nsa_attention",
    )(qaT, gT, kc, vcT, sk_aug, svT, wk, wvT)


def _mla_kernel(qT_ref, k_ref, vT_ref, o_ref, acc_scr, s_scr):
    tq = TQ_MLA
    t0 = pl.program_id(1) * tq
    tq_l = t0 + lax.broadcasted_iota(jnp.int32, (1, tq), 1)
    krow = lax.broadcasted_iota(jnp.int32, (tq, tq), 0)
    acc_scr[...] = jnp.zeros((MLA_HEADS, LANES, tq), F32)

    def scores(k0, hd):
        k = k_ref[pl.ds(k0, tq), LANES * hd:LANES * (hd + 1)]
        return _dot(k, qT_ref[LANES * hd:LANES * (hd + 1), :])

    for hd in range(MXU_LOOKAHEAD):
        s_scr[hd] = scores(0, hd)

    def tile(k0, ms, ls, masked):
        new_m, new_l, accs = [], [], []
        ahead = [s_scr[hd] for hd in range(MXU_LOOKAHEAD)]
        for hd in range(MLA_HEADS):
            sc = ahead.pop(0)
            nxt = hd + MXU_LOOKAHEAD
            if nxt < MLA_HEADS:
                ahead.append(scores(k0, nxt))
            elif not masked:
                s_scr[nxt - MLA_HEADS] = scores(pl.multiple_of(k0 + tq, tq), nxt - MLA_HEADS)
            if masked:
                sc = jnp.where(k0 + krow <= tq_l, sc, NEG)
            m_new = jnp.maximum(ms[hd], jnp.max(sc, axis=0, keepdims=True))
            alpha = jnp.exp2(ms[hd] - m_new)
            pe = jnp.exp2(sc - m_new)
            new_l.append(alpha * ls[hd] + jnp.sum(pe, axis=0, keepdims=True))
            vt = vT_ref[LANES * (hd // 2):LANES * (hd // 2 + 1), pl.ds(k0, tq)]
            accs.append(alpha * acc_scr[hd] + _dot(vt, pe.astype(BF16)))
            new_m.append(m_new)
        acc_scr[...] = jnp.stack(accs)
        return tuple(new_m), tuple(new_l)

    def step(kt, carry):
        return tile(pl.multiple_of(kt * tq, tq), carry[0], carry[1], False)

    n_full = pl.program_id(1)
    init = (tuple(jnp.full((1, tq), NEG, F32) for _ in range(MLA_HEADS)),
            tuple(jnp.zeros((1, tq), F32) for _ in range(MLA_HEADS)))
    ms, ls = lax.fori_loop(0, n_full, step, init)
    ms, ls = tile(pl.multiple_of(n_full * tq, tq), ms, ls, True)
    frow = lax.broadcasted_iota(jnp.int32, (LANES, tq), 0)
    for pr in range(MLA_HEADS // 2):
        t = jnp.where(frow < MLA_V, acc_scr[2 * pr] / ls[2 * pr], acc_scr[2 * pr + 1] / ls[2 * pr + 1])
        o_ref[:, LANES * pr:LANES * (pr + 1)] = t.T


def _mla(qmT, km, vmT, B, S):
    tq = TQ_MLA
    nq = S // tq
    return pl.pallas_call(
        _mla_kernel,
        grid=(B, nq),
        in_specs=[pl.BlockSpec((MLA_HEADS * LANES, tq), lambda b, i: (0, b * nq + i)),
                  pl.BlockSpec((S, MLA_HEADS * LANES), lambda b, i: (b, 0)),
                  pl.BlockSpec((WIDTH_B, S), lambda b, i: (0, b))],
        out_specs=pl.BlockSpec((tq, WIDTH_B), lambda b, i: (b * nq + i, 0)),
        out_shape=jax.ShapeDtypeStruct((B * S, WIDTH_B), F32),
        scratch_shapes=[pltpu.VMEM((MLA_HEADS, LANES, tq), F32), pltpu.VMEM((MXU_LOOKAHEAD, tq, tq), F32)],
        compiler_params=_params("parallel", "arbitrary"),
        name="mla_attention",
    )(qmT, km, vmT)


def _mlstm_kernel_old(cx_ref, v_ref, o_ref, gc_ref, gr_ref, cw_ref, cb_ref, wq_ref, wk_ref,
                      bc_ref, br_ref, gmh_ref, skip_ref, y_ref, xc_scr, q_scr, k_scr):
    S = cx_ref.shape[0]
    L = MLSTM_CHUNK
    d = MLSTM_DIM
    pairs = MLSTM_HEADS // 2

    x = cx_ref[...]
    rowi = lax.broadcasted_iota(jnp.int32, (S, WIDTH_C), 0)
    conv = x * cw_ref[MLSTM_CONV - 1:MLSTM_CONV, :]
    for back in range(1, MLSTM_CONV):
        shifted = jnp.where(rowi >= back, pltpu.roll(x, back, 0), 0.0)
        conv = conv + shifted * cw_ref[MLSTM_CONV - 1 - back:MLSTM_CONV - back, :]
    xc = _silu(conv + cb_ref[...])
    xc_scr[...] = xc
    xcb = xc.astype(BF16)
    q_scr[...] = _dot(xcb, wq_ref[...]).astype(BF16)
    k_scr[...] = _dot(xcb, wk_ref[...]).astype(BF16)

    ri = lax.broadcasted_iota(jnp.int32, (L, L), 0)
    ci = lax.broadcasted_iota(jnp.int32, (L, L), 1)
    causal = ci <= ri
    tril = jnp.where(causal, 1.0, 0.0).astype(BF16)
    triu = jnp.where(ri <= ci, 1.0, 0.0).astype(BF16)
    lane = lax.broadcasted_iota(jnp.int32, (L, LANES), 1)
    h0_lane = lane < d
    h0_row1 = lax.broadcasted_iota(jnp.int32, (1, LANES), 1) < d
    blk = (lax.broadcasted_iota(jnp.int32, (LANES, LANES), 0) // d) == (lax.broadcasted_iota(jnp.int32, (LANES, LANES), 1) // d)
    rowsel = lax.broadcasted_iota(jnp.int32, (LANES, 1), 0) < d
    ones_row = jnp.ones((1, LANES), F32)

    def log_sig(z):
        return jnp.minimum(z, 0.0) - jnp.log1p(jnp.exp(-jnp.abs(z)))

    def chunk(c, carry):
        cs, ns, ms = carry
        r0 = pl.multiple_of(c * L, L)
        gcol = gc_ref[pl.ds(r0, L), :] + bc_ref[...]
        ig_col = gcol
        lf_col = log_sig(gcol)
        b_col = sum(_dot(tril, part) for part in _split3(lf_col))
        grow = gr_ref[0, c] + br_ref[...]
        lf_row = log_sig(grow)
        b_row = sum(_dot(part, triu) for part in _split3(lf_row))
        new_c, new_n, new_m_parts, outs = [], [], [], []
        for pr in range(pairs):
            qp = q_scr[pl.ds(r0, L), LANES * pr:LANES * (pr + 1)]
            kp = k_scr[pl.ds(r0, L), LANES * pr:LANES * (pr + 1)]
            vp = v_ref[pl.ds(r0, L), LANES * pr:LANES * (pr + 1)]
            kpf = kp.astype(F32)
            qc = _dot(qp, cs[pr].astype(BF16))
            qn_full = qp.astype(F32) * ns[pr]
            num_p, den_p, mt_p, wl_p = [], [], [], []
            e_old_l, e_loc_l, m_new_l = [], [], []
            for hh in range(2):
                hd = 2 * pr + hh
                igc = ig_col[:, hd:hd + 1]
                bc = b_col[:, MLSTM_HEADS + hd:MLSTM_HEADS + hd + 1]
                igr = grow[hd:hd + 1, :]
                brw = b_row[MLSTM_HEADS + hd:MLSTM_HEADS + hd + 1, :]
                b_last = bc[L - 1:L, :]
                m_prev = ms[:, hd:hd + 1]
                a_col = b_last - bc + igc
                m_loc = jnp.max(a_col, axis=0, keepdims=True)
                wl_p.append(jnp.exp(a_col - m_loc))
                m_new = jnp.maximum(b_last + m_prev, m_loc)
                e_old_l.append(jnp.exp(b_last + m_prev - m_new))
                e_loc_l.append(jnp.exp(m_loc - m_new))
                m_new_l.append(m_new)
                dmat = jnp.where(causal, bc - brw + igr, -jnp.inf)
                gl = bc + m_prev
                m_t = jnp.maximum(gl, jnp.max(dmat, axis=-1, keepdims=True))
                hsel = h0_lane if hh == 0 else lane >= d
                qh = jnp.where(hsel, qp, jnp.zeros_like(qp))
                sw = _dot_nt(qh, kp) * jnp.exp(dmat - m_t)
                inter = jnp.exp(gl - m_t)
                qn = jnp.sum(jnp.where(hsel, qn_full, 0.0), axis=-1, keepdims=True)
                num_p.append(inter * qc + _dot(sw.astype(BF16), vp))
                den_p.append(inter * qn + jnp.sum(sw, axis=-1, keepdims=True))
                mt_p.append(m_t)
            num = jnp.where(h0_lane, num_p[0], num_p[1])
            den = jnp.where(h0_lane, den_p[0], den_p[1])
            m_t2 = jnp.where(h0_lane, mt_p[0], mt_p[1])
            outs.append(num / jnp.maximum(jnp.abs(den), jnp.exp(-m_t2)))
            wl2 = jnp.where(h0_lane, wl_p[0], wl_p[1])
            kwf = kpf * wl2
            c_loc = jnp.where(blk, _dot_tn(kwf.astype(BF16), vp), 0.0)
            n_loc = jnp.sum(kwf, axis=0, keepdims=True)
            e_old_row = jnp.where(h0_row1, e_old_l[0], e_old_l[1])
            e_loc_row = jnp.where(h0_row1, e_loc_l[0], e_loc_l[1])
            e_old_colv = jnp.where(rowsel, e_old_l[0], e_old_l[1])
            e_loc_colv = jnp.where(rowsel, e_loc_l[0], e_loc_l[1])
            new_c.append(e_old_colv * cs[pr] + e_loc_colv * c_loc)
            new_n.append(e_old_row * ns[pr] + e_loc_row * n_loc)
            new_m_parts += m_new_l
        hcat = jnp.concatenate(outs, axis=1)
        hg = _sigmoid(o_ref[pl.ds(r0, L), :]) * hcat
        ys = []
        for hd in range(MLSTM_HEADS):
            part = hg[:, LANES * (hd // 2):LANES * (hd // 2 + 1)]
            hs = h0_lane if hd % 2 == 0 else lane >= d
            mu = jnp.sum(jnp.where(hs, part, 0.0), axis=-1, keepdims=True) / d
            cen = jnp.where(hs, part - mu, 0.0)
            var = jnp.sum(cen * cen, axis=-1, keepdims=True) / d
            ys.append(cen * lax.rsqrt(var + NORM_EPS))
        hn = jnp.concatenate([ys[0] + ys[1], ys[2] + ys[3]], axis=1)
        y_ref[pl.ds(r0, L), :] = hn * gmh_ref[...] + skip_ref[...] * xc_scr[pl.ds(r0, L), :]
        lane_m = lax.broadcasted_iota(jnp.int32, (1, LANES), 1)
        m_next = jnp.zeros((1, LANES), F32)
        for hd in range(MLSTM_HEADS):
            m_next = jnp.where(lane_m == hd, new_m_parts[hd] * ones_row, m_next)
        return tuple(new_c), tuple(new_n), m_next

    init = (tuple(jnp.zeros((LANES, LANES), F32) for _ in range(pairs)),
            tuple(jnp.zeros((1, LANES), F32) for _ in range(pairs)),
            jnp.zeros((1, LANES), F32))
    lax.fori_loop(0, S // L, chunk, init)


def _mlstm_old(cx, mv, mo, cif, conv_w, conv_b, w_q_m, w_k_m, b_igate, b_fgate, g_mh, skip_m, B, S):
    def blockdiag(w, scale):
        out = jnp.zeros((WIDTH_C, WIDTH_C), F32)
        for hd in range(MLSTM_HEADS):
            out = out.at[hd * MLSTM_DIM:(hd + 1) * MLSTM_DIM, hd * MLSTM_DIM:(hd + 1) * MLSTM_DIM].set(w[hd] * scale)
        return out.astype(BF16)

    wq = blockdiag(w_q_m, MLSTM_DIM ** -0.5)
    wk = blockdiag(w_k_m, 1.0)
    bias8 = jnp.concatenate([b_igate, b_fgate])
    bc = jnp.zeros((1, LANES), F32).at[0, :2 * MLSTM_HEADS].set(bias8)
    br = jnp.broadcast_to(bias8[:, None], (2 * MLSTM_HEADS, MLSTM_CHUNK))
    nc = S // MLSTM_CHUNK
    gr = cif[:, :2 * MLSTM_HEADS].reshape(B, nc, MLSTM_CHUNK, 2 * MLSTM_HEADS).transpose(0, 1, 3, 2)
    seq = lambda w: pl.BlockSpec((S, w), lambda b: (b, 0))
    full = lambda a: pl.BlockSpec(a.shape, lambda b: (0,) * a.ndim)
    row = lambda a: a.reshape(1, -1)
    args = [conv_w, row(conv_b), wq, wk, bc, br, row(g_mh), row(skip_m)]
    return pl.pallas_call(
        _mlstm_kernel,
        grid=(B,),
        in_specs=[seq(WIDTH_C), seq(WIDTH_C), seq(WIDTH_C), seq(LANES),
                  pl.BlockSpec((1, nc, 2 * MLSTM_HEADS, MLSTM_CHUNK), lambda b: (b, 0, 0, 0))] + [full(a) for a in args],
        out_specs=seq(WIDTH_C),
        out_shape=jax.ShapeDtypeStruct((B * S, WIDTH_C), F32),
        scratch_shapes=[pltpu.VMEM((S, WIDTH_C), F32), pltpu.VMEM((S, WIDTH_C), BF16), pltpu.VMEM((S, WIDTH_C), BF16)],
        compiler_params=_params("parallel"),
        name="mlstm_mixer",
    )(cx, mv, mo, cif, gr, *args)


def _log_sigmoid(z):
    return jnp.minimum(z, 0.0) - jnp.log1p(jnp.exp(-jnp.abs(z)))


def _mlstm_kernel(cx_ref, v_ref, o_ref, gc_ref, gr_ref, cw_ref, cb_ref, wq_ref, wk_ref, bc_ref, br_ref,
                  gmh_ref, skip_ref, y_ref,
                  xc_scr, q_scr, k_scr, b_scr, cmu_scr, w_scr, inter_scr, floor_scr, mfull_scr, ut_scr,
                  eo_scr, el_scr, cprev_scr):
    S = cx_ref.shape[0]
    L = MLSTM_TILE
    NC = S // L
    d = MLSTM_DIM
    pairs = MLSTM_HEADS // 2
    group = MLSTM_GROUP

    x = cx_ref[...]
    rowi = lax.broadcasted_iota(jnp.int32, (S, WIDTH_C), 0)
    conv = x * cw_ref[MLSTM_CONV - 1:MLSTM_CONV, :]
    for back in range(1, MLSTM_CONV):
        shifted = jnp.where(rowi >= back, pltpu.roll(x, back, 0), 0.0)
        conv = conv + shifted * cw_ref[MLSTM_CONV - 1 - back:MLSTM_CONV - back, :]
    xc = _silu(conv + cb_ref[...])
    xc_scr[...] = xc
    xcb = xc.astype(BF16)
    q_scr[...] = _dot(xcb, wq_ref[...]).astype(BF16)
    k_scr[...] = _dot(xcb, wk_ref[...]).astype(BF16)

    in_chunk = lax.broadcasted_iota(jnp.int32, (S, LANES), 0) & (L - 1)

    def scan_rows(val, op, fill):
        sh = 1
        while sh < L:
            val = op(val, jnp.where(in_chunk >= sh, pltpu.roll(val, sh, 0), fill))
            sh *= 2
        return val

    ig = gc_ref[:, :LANES] + bc_ref[0:1, :]
    lf = _log_sigmoid(gc_ref[:, LANES:] + bc_ref[1:2, :])
    b = scan_rows(lf, jnp.add, 0.0)
    u = ig - b
    cmu = scan_rows(u, jnp.maximum, -jnp.inf)
    b_scr[...] = b
    cmu_scr[...] = cmu
    b_last = b_scr[pl.ds(L - 1, NC, stride=L), :]
    u_max = cmu_scr[pl.ds(L - 1, NC, stride=L), :]

    m = jnp.zeros((1, LANES), F32)
    m_rows = []
    for c in range(NC):
        m_rows.append(m)
        m = b_last[c:c + 1] + jnp.maximum(m, u_max[c:c + 1])
    m_prev = jnp.concatenate(m_rows, axis=0)
    m_top = jnp.maximum(m_prev, u_max)
    e_old = jnp.exp(m_prev - m_top)
    e_loc = jnp.exp(u_max - m_top)
    m_tok = jnp.maximum(cmu.reshape(NC, L, LANES), m_prev.reshape(NC, 1, LANES))
    inter = jnp.exp(m_prev.reshape(NC, 1, LANES) - m_tok).reshape(S, LANES)
    w_loc = jnp.exp(u - jnp.broadcast_to(u_max.reshape(NC, 1, LANES), (NC, L, LANES)).reshape(S, LANES))
    m_tok = m_tok.reshape(S, LANES)
    floor = jnp.exp(-(b + m_tok))

    hrow = lax.broadcasted_iota(jnp.int32, (LANES, 2 * LANES), 0)
    hcol = lax.broadcasted_iota(jnp.int32, (LANES, 2 * LANES), 1)
    to_heads = jnp.where(hcol // d == hrow, 1.0, 0.0).astype(BF16)
    frow = lax.broadcasted_iota(jnp.int32, (LANES, 4 * LANES), 0)
    fcol = lax.broadcasted_iota(jnp.int32, (LANES, 4 * LANES), 1)
    to_full = jnp.where(fcol // LANES == frow, 1.0, 0.0).astype(BF16)

    def spread(val, onehot):
        return sum(_dot(part, onehot) for part in _split3(val))

    w_scr[...] = spread(w_loc, to_heads)
    inter_scr[...] = spread(inter, to_heads)
    floor_scr[...] = spread(floor, to_heads)
    mfull_scr[...] = spread(m_tok, to_full)
    eo_full, el_full = spread(e_old, to_full), spread(e_loc, to_full)
    for c in range(NC):
        eo_scr[c] = eo_full[c:c + 1]
        el_scr[c] = el_full[c:c + 1]

    lane_in_chunk = lax.broadcasted_iota(jnp.int32, (2 * MLSTM_HEADS, S), 1) & (L - 1)
    gt = gr_ref[...] + br_ref[...]
    bt = _log_sigmoid(gt)
    sh = 1
    while sh < L:
        bt = bt + jnp.where(lane_in_chunk >= sh, pltpu.roll(bt, sh, 1), 0.0)
        sh *= 2
    ut_scr[0:MLSTM_HEADS, :] = gt[0:MLSTM_HEADS] - bt[MLSTM_HEADS:]

    arow = lax.broadcasted_iota(jnp.int32, (LANES, 2 * LANES), 0)
    acol = lax.broadcasted_iota(jnp.int32, (LANES, 2 * LANES), 1)
    blk2 = (arow // d) == ((acol & (LANES - 1)) // d)
    ones_v = jnp.ones((L, LANES), BF16)

    def head_rows(ref, c, pr):
        top = jnp.broadcast_to(ref[c, :, LANES * (2 * pr):LANES * (2 * pr + 1)], (d, LANES))
        bot = jnp.broadcast_to(ref[c, :, LANES * (2 * pr + 1):LANES * (2 * pr + 2)], (d, LANES))
        half = jnp.concatenate([top, bot], axis=0)
        return jnp.concatenate([half, half], axis=1)

    def state_group(g, carry):
        local = []
        for cc in range(group):
            r0 = pl.multiple_of((g * group + cc) * L, L)
            for pr in range(pairs):
                ps = slice(LANES * pr, LANES * (pr + 1))
                kw = (k_scr[pl.ds(r0, L), ps].astype(F32) * w_scr[pl.ds(r0, L), ps]).astype(BF16)
                vo = jnp.concatenate([v_ref[pl.ds(r0, L), ps], ones_v], axis=1)
                local.append(jnp.where(blk2, _dot_tn(kw, vo), 0.0))
        state = list(carry)
        for cc in range(group):
            c = g * group + cc
            for pr in range(pairs):
                cprev_scr[c, pr] = state[pr].astype(BF16)
                state[pr] = head_rows(eo_scr, c, pr) * state[pr] + head_rows(el_scr, c, pr) * local[cc * pairs + pr]
        return tuple(state)

    lax.fori_loop(0, NC // group, state_group, tuple(jnp.zeros((LANES, 2 * LANES), F32) for _ in range(pairs)))

    li = lax.broadcasted_iota(jnp.int32, (L, L), 0)
    si = lax.broadcasted_iota(jnp.int32, (L, L), 1)
    causal = si <= li
    lane = lax.broadcasted_iota(jnp.int32, (L, LANES), 1)
    h0_lane = lane < d
    avg = jnp.where((li // d) == (si // d), 1.0 / d, 0.0).astype(BF16)

    def dot2(val, rhs):
        hi = val.astype(BF16)
        return _dot(hi, rhs) + _dot((val - hi.astype(F32)).astype(BF16), rhs)

    def out_group(g, carry):
        units = [(cc, pr) for cc in range(group) for pr in range(pairs)]
        chunk_of = {u_: g * group + u_[0] for u_ in units}
        rows = {u_: pl.multiple_of(chunk_of[u_] * L, L) for u_ in units}
        qc, sc, pv = {}, {}, {}
        for u_ in units:
            c, pr = chunk_of[u_], u_[1]
            ps = slice(LANES * pr, LANES * (pr + 1))
            qp = q_scr[pl.ds(rows[u_], L), ps]
            kp = k_scr[pl.ds(rows[u_], L), ps]
            qc[u_] = _dot(qp, cprev_scr[c, pr])
            sc[u_] = [_dot_nt(jnp.where(h0_lane if hh == 0 else lane >= d, qp, jnp.zeros_like(qp)), kp)
                      for hh in range(2)]
        for u_ in units:
            pr = u_[1]
            ps = slice(LANES * pr, LANES * (pr + 1))
            vo = jnp.concatenate([v_ref[pl.ds(rows[u_], L), ps], ones_v], axis=1)
            pv[u_] = []
            for hh in range(2):
                hd = 2 * pr + hh
                u_row = ut_scr[hd:hd + 1, pl.ds(rows[u_], L)]
                decay = jnp.where(causal, jnp.exp(u_row - mfull_scr[pl.ds(rows[u_], L), LANES * hd:LANES * (hd + 1)]), 0.0)
                pv[u_].append(_dot((sc[u_][hh] * decay).astype(BF16), vo))
        hg, cen = {}, {}
        for u_ in units:
            pr = u_[1]
            ps = slice(LANES * pr, LANES * (pr + 1))
            it = inter_scr[pl.ds(rows[u_], L), ps]
            num = it * qc[u_][:, :LANES] + jnp.where(h0_lane, pv[u_][0][:, :LANES], pv[u_][1][:, :LANES])
            den = it * qc[u_][:, LANES:] + jnp.where(h0_lane, pv[u_][0][:, LANES:], pv[u_][1][:, LANES:])
            hcell = num / jnp.maximum(jnp.abs(den), floor_scr[pl.ds(rows[u_], L), ps])
            hg[u_] = _sigmoid(o_ref[pl.ds(rows[u_], L), ps]) * hcell
        for u_ in units:
            cen[u_] = hg[u_] - dot2(hg[u_], avg)
        for u_ in units:
            pr = u_[1]
            ps = slice(LANES * pr, LANES * (pr + 1))
            var = dot2(cen[u_] * cen[u_], avg)
            y_ref[pl.ds(rows[u_], L), ps] = (cen[u_] * lax.rsqrt(var + NORM_EPS) * gmh_ref[:, ps]
                                             + skip_ref[:, ps] * xc_scr[pl.ds(rows[u_], L), ps])
        return carry

    lax.fori_loop(0, NC // group, out_group, 0)


def _mlstm(cx, mv, mo, cif, conv_w, conv_b, w_q_m, w_k_m, b_igate, b_fgate, g_mh, skip_m, B, S):
    def blockdiag(w, scale):
        out = jnp.zeros((WIDTH_C, WIDTH_C), F32)
        for hd in range(MLSTM_HEADS):
            out = out.at[hd * MLSTM_DIM:(hd + 1) * MLSTM_DIM, hd * MLSTM_DIM:(hd + 1) * MLSTM_DIM].set(w[hd] * scale)
        return out.astype(BF16)

    nc = S // MLSTM_TILE
    wq = blockdiag(w_q_m, MLSTM_DIM ** -0.5)
    wk = blockdiag(w_k_m, 1.0)
    bc = jnp.zeros((2, LANES), F32).at[0, :MLSTM_HEADS].set(b_igate).at[1, :MLSTM_HEADS].set(b_fgate)
    bias8 = jnp.concatenate([b_igate, b_fgate])
    br = jnp.broadcast_to(bias8[:, None], (2 * MLSTM_HEADS, S))
    gr = jnp.concatenate([cif[:, :MLSTM_HEADS], cif[:, LANES:LANES + MLSTM_HEADS]], axis=1).T
    seq = lambda w: pl.BlockSpec((S, w), lambda b: (b, 0))
    full = lambda a: pl.BlockSpec(a.shape, lambda b: (0,) * a.ndim)
    row = lambda a: a.reshape(1, -1)
    args = [conv_w, row(conv_b), wq, wk, bc, br, row(g_mh), row(skip_m)]
    tok = lambda w, dt: pltpu.VMEM((S, w), dt)
    return pl.pallas_call(
        _mlstm_kernel,
        grid=(B,),
        in_specs=[seq(WIDTH_C), seq(WIDTH_C), seq(WIDTH_C), seq(2 * LANES),
                  pl.BlockSpec((2 * MLSTM_HEADS, S), lambda b: (0, b))] + [full(a) for a in args],
        out_specs=seq(WIDTH_C),
        out_shape=jax.ShapeDtypeStruct((B * S, WIDTH_C), F32),
        scratch_shapes=[tok(WIDTH_C, F32), tok(WIDTH_C, BF16), tok(WIDTH_C, BF16), tok(LANES, F32), tok(LANES, F32),
                        tok(WIDTH_C, F32), tok(WIDTH_C, F32), tok(WIDTH_C, F32), tok(4 * LANES, F32),
                        pltpu.VMEM((2 * MLSTM_HEADS, S), F32), pltpu.VMEM((nc, 1, 4 * LANES), F32),
                        pltpu.VMEM((nc, 1, 4 * LANES), F32), pltpu.VMEM((nc, MLSTM_HEADS // 2, LANES, 2 * LANES), BF16)],
        compiler_params=_params("parallel"),
        name="mlstm_mixer",
    )(cx, mv, mo, cif, gr, *args)


def _outproj_kernel(ya_ref, yb_ref, yc_ref, x_ref, gate_ref, ga_ref, gb_ref, w_ref, o_ref):
    a = _rms(ya_ref[...], ga_ref[...]).astype(BF16)
    b = _rms(yb_ref[...], gb_ref[...]).astype(BF16)
    c = yc_ref[...].astype(BF16)
    y = _dot(a, w_ref[0:WIDTH_A, :]) + _dot(b, w_ref[WIDTH_A:WIDTH_A + WIDTH_B, :]) + _dot(c, w_ref[WIDTH_A + WIDTH_B:, :])
    o_ref[...] = x_ref[...] + gate_ref[0] * y


def _outproj(ya, yb, yc, x2, gate, ga, gb, w_out, S):
    T = x2.shape[0]
    tm = TM_PROJ
    per_b = S // tm
    row = lambda w: pl.BlockSpec((tm, w), lambda i: (i, 0))
    full = lambda a: pl.BlockSpec(a.shape, lambda i: (0,) * a.ndim)
    return pl.pallas_call(
        _outproj_kernel,
        grid=(T // tm,),
        in_specs=[row(WIDTH_A), row(WIDTH_B), row(WIDTH_C), row(D_MODEL),
                  pl.BlockSpec((1, 1, D_MODEL), lambda i: (i // per_b, 0, 0)), full(ga), full(gb), full(w_out)],
        out_specs=row(D_MODEL),
        out_shape=jax.ShapeDtypeStruct((T, D_MODEL), F32),
        compiler_params=_params("parallel"),
        name="out_proj",
    )(ya, yb, yc, x2, gate, ga, gb, w_out)


def _ffn_kernel(x_ref, sh_ref, sc_ref, gate_ref, g2_ref, wgu_ref, wd_ref, gf_ref, o_ref, *, final_norm):
    x = x_ref[...]
    h = (_rms(x, g2_ref[...]) * (1.0 + sc_ref[0]) + sh_ref[0]).astype(BF16)
    acc = jnp.zeros(x.shape, F32)
    for j in range(FFN_HIDDEN // FFN_CHUNK):
        gate = _dot(h, wgu_ref[:, FFN_CHUNK * j:FFN_CHUNK * (j + 1)])
        up = _dot(h, wgu_ref[:, FFN_HIDDEN + FFN_CHUNK * j:FFN_HIDDEN + FFN_CHUNK * (j + 1)])
        act = (_silu(gate) * up).astype(BF16)
        acc = acc + _dot(act, wd_ref[FFN_CHUNK * j:FFN_CHUNK * (j + 1), :])
    y = x + gate_ref[0] * acc
    if final_norm:
        y = _rms(y, gf_ref[...])
    o_ref[...] = y


def _ffn(x2, shift, scale, gate, g2, wgu, wd, gf, S, final_norm):
    T = x2.shape[0]
    tm = TM_PROJ
    per_b = S // tm
    row = pl.BlockSpec((tm, D_MODEL), lambda i: (i, 0))
    full = lambda a: pl.BlockSpec(a.shape, lambda i: (0,) * a.ndim)
    modspec = pl.BlockSpec((1, 1, D_MODEL), lambda i: (i // per_b, 0, 0))
    return pl.pallas_call(
        functools.partial(_ffn_kernel, final_norm=final_norm),
        grid=(T // tm,),
        in_specs=[row, modspec, modspec, modspec, full(g2), full(wgu), full(wd), full(gf)],
        out_specs=row,
        out_shape=jax.ShapeDtypeStruct((T, D_MODEL), F32),
        compiler_params=_params("parallel"),
        name="ffn_final" if final_norm else "ffn",
    )(x2, shift, scale, gate, g2, wgu, wd, gf)


def _head_tile_perm():
    idx = []
    for r in range(NSA_REP):
        idx += list(range(HEAD_DIM * r, HEAD_DIM * (r + 1)))
        idx += list(range(HEAD_DIM * (NSA_REP + r), HEAD_DIM * (NSA_REP + r + 1)))
    return np.asarray(idx, np.int32)


def _in_cols():
    std = np.full((N_STD,), -1, np.int64)
    tr = np.full((N_T,), -1, np.int64)
    off = 0
    tr[TSEG_Q:TSEG_Q + WIDTH_A] = _head_tile_perm()
    off += WIDTH_A
    std[SEG_CKV:SEG_CKV + 2 * LANES] = off + np.arange(2 * LANES)
    off += 2 * LANES
    std[SEG_SK:SEG_SK + LANES] = off + np.arange(LANES)
    off += LANES
    tr[TSEG_SV:TSEG_SV + LANES] = off + np.arange(LANES)
    off += LANES
    std[SEG_WK:SEG_WK + LANES] = off + np.arange(LANES)
    off += LANES
    tr[TSEG_WV:TSEG_WV + LANES] = off + np.arange(LANES)
    off += LANES
    tr[TSEG_G:TSEG_G + N_GATES] = off + np.arange(N_GATES)
    off += N_GATES
    std[SEG_QL:SEG_QL + MLA_Q_LORA] = off + np.arange(MLA_Q_LORA)
    off += MLA_Q_LORA
    std[SEG_KVL:SEG_KVL + MLA_KV_LORA] = off + np.arange(MLA_KV_LORA)
    off += MLA_KV_LORA
    std[SEG_KR + MLA_NOPE:SEG_KR + MLA_NOPE + MLA_ROPE] = off + np.arange(MLA_ROPE)
    off += MLA_ROPE
    for seg in (SEG_CX, SEG_MV, SEG_MO):
        std[seg:seg + WIDTH_C] = off + np.arange(WIDTH_C)
        off += WIDTH_C
    std[SEG_IF:SEG_IF + MLSTM_HEADS] = off + np.arange(MLSTM_HEADS)
    std[SEG_IF + LANES:SEG_IF + LANES + MLSTM_HEADS] = off + MLSTM_HEADS + np.arange(MLSTM_HEADS)
    return std, tr


def _gather_cols(w, cols):
    g = jnp.take(w, jnp.asarray(np.maximum(cols, 0), jnp.int32), axis=1)
    return jnp.where(jnp.asarray(cols >= 0)[None, :], g, 0.0)


def _layer_weights(l, w_in, w_uq, w_ukv, w_out, w_gu, w_down, g_out_a):
    std, tr = _in_cols()
    w_std = _gather_cols(w_in[l], std).astype(BF16)
    w_t = _gather_cols(w_in[l], tr).T.astype(BF16)
    cq = np.full((MLA_HEADS * LANES,), -1, np.int64)
    ck = np.full((MLA_HEADS * LANES,), -1, np.int64)
    cv = np.zeros((WIDTH_B,), np.int64)
    dq = MLA_NOPE + MLA_ROPE
    dkv = MLA_NOPE + MLA_V
    for hd in range(MLA_HEADS):
        cq[LANES * hd:LANES * hd + dq] = dq * hd + np.arange(dq)
        ck[LANES * hd:LANES * hd + MLA_NOPE] = dkv * hd + np.arange(MLA_NOPE)
        cv[MLA_V * hd:MLA_V * (hd + 1)] = dkv * hd + MLA_NOPE + np.arange(MLA_V)
    wuqT = _gather_cols(w_uq[l], cq).T.astype(BF16)
    wkm = _gather_cols(w_ukv[l], ck).astype(BF16)
    wvmT = _gather_cols(w_ukv[l], cv).T.astype(BF16)
    perm = _head_tile_perm()
    wo = jnp.concatenate([w_out[l][perm], w_out[l][WIDTH_A:]], axis=0).astype(BF16)
    ga = g_out_a[l][perm].reshape(1, WIDTH_A)
    return w_std, w_t, wuqT, wkm, wvmT, wo, ga, w_gu[l].astype(BF16), w_down[l].astype(BF16)


def kernel(x, c, positions, g_norm1, g_norm2, w_ada, b_ada, w_in, cmp_pos, w_cmp_k, w_cmp_v, g_out_a, g_q_lora, w_uq, g_kv_lora, w_ukv, g_out_b, conv_w, conv_b, w_q_m, w_k_m, b_igate, b_fgate, g_mh, skip_m, w_out, w_gu, w_down, g_final):
    B, S, D = x.shape
    T = B * S
    x2 = x.reshape(T, D)
    tabs = _rope_tables(positions)
    cn, sn, cm, sm = tabs
    tabs_t = (cn[:, :NSA_ROT_HALF].T, sn[:, :NSA_ROT_HALF].T,
              cm[:, MLA_NOPE:MLA_NOPE + MLA_ROPE // 2].T, sm[:, MLA_NOPE:MLA_NOPE + MLA_ROPE // 2].T)
    mod = _ada(c, w_ada, b_ada)
    row = lambda v: v.reshape(1, -1)
    for l in range(DEPTH):
        w_std, w_t, wuqT, wkm, wvmT, wo, ga, wgu, wd = _layer_weights(l, w_in, w_uq, w_ukv, w_out, w_gu, w_down, g_out_a)
        m6 = [mod[l, :, D * i:D * (i + 1)].reshape(B, 1, D) for i in range(6)]
        shift1, scale1, gate1, shift2, scale2, gate2 = m6
        (qaT, gT, ck, cv, sk_aug, wk, svT, wvT, qmT, km, vmT, cx, mv, mo, cif) = _inproj(
            x2, shift1, scale1, row(g_norm1[l]), w_std, w_t, tabs, tabs_t, row(g_q_lora[l]), wuqT,
            row(g_kv_lora[l]), wkm, wvmT, S)
        kc, vcT = _compress(ck, cv, cmp_pos[l], w_cmp_k[l], w_cmp_v[l], B, S)
        ya = _nsa(qaT, gT, kc, vcT, sk_aug, svT, wk, wvT, B, S)
        yb = _mla(qmT, km, vmT, B, S)
        yc = _mlstm(cx, mv, mo, cif, conv_w[l], conv_b[l], w_q_m[l], w_k_m[l], b_igate[l], b_fgate[l],
                    g_mh[l], skip_m[l], B, S)
        x2 = _outproj(ya, yb, yc, x2, gate1, ga, row(g_out_b[l]), wo, S)
        x2 = _ffn(x2, shift2, scale2, gate2, row(g_norm2[l]), wgu, wd, row(g_final), S, final_norm=(l == DEPTH - 1))
    return x2.reshape(B, S, D)
```

```python
import functools

import numpy as np
import jax
import jax.numpy as jnp
from jax import lax
from jax.experimental import pallas as pl
from jax.experimental.pallas import tpu as pltpu

F32 = jnp.float32
BF16 = jnp.bfloat16

D_MODEL = 1024
DEPTH = 2
HEAD_DIM = 64
ROPE_THETA = 500000.0
NSA_ROT_HALF = HEAD_DIM // 8
NORM_EPS = 1e-6

NSA_HEADS = 6
NSA_KV_HEADS = 2
NSA_REP = NSA_HEADS // NSA_KV_HEADS
CMP_LEN = 32
CMP_STRIDE = 16
SLC_LEN = 64
SLC_TOPK = 16
WINDOW = 512

MLA_HEADS = 6
MLA_Q_LORA = 256
MLA_KV_LORA = 128
MLA_NOPE = 64
MLA_ROPE = 32
MLA_V = 64

MLSTM_HEADS = 4
MLSTM_DIM = 64
MLSTM_CONV = 4

WIDTH_A = NSA_HEADS * HEAD_DIM
WIDTH_B = MLA_HEADS * MLA_V
WIDTH_C = MLSTM_HEADS * MLSTM_DIM
FFN_HIDDEN = 2816
N_GATES = 3 * NSA_HEADS
GATE_ROWS = 24

LANES = 128
NEG = -1e30
LOG2E = 1.4426950408889634
VMEM_LIMIT = 56 * 1024 * 1024

TM_PROJ = 512
TQ_NSA = 256
TK_SLC = 256
CHAIN_LANES = 256
TQ_MLA = 256
FFN_CHUNK = 256
MLSTM_TILE = 128
MLSTM_GROUP = 8
MXU_LOOKAHEAD = 6

SEG_CKV, SEG_SK, SEG_WK, SEG_QL, SEG_KVL, SEG_KR, SEG_CX, SEG_MV, SEG_MO, SEG_IF = (
    0, 256, 384, 512, 768, 896, 1024, 1280, 1536, 1792)
N_STD = 2048
TSEG_Q, TSEG_G, TSEG_SV, TSEG_WV = 0, 384, 416, 544
TSEG_G_ROWS = 32
N_T = 672


def _params(*sem):
    return pltpu.CompilerParams(dimension_semantics=sem, vmem_limit_bytes=VMEM_LIMIT)


def _dot(a, b):
    return jnp.dot(a, b, preferred_element_type=F32)


def _dot_nt(a, b):
    return lax.dot_general(a, b, (((1,), (1,)), ((), ())), preferred_element_type=F32)


def _dot_tn(a, b):
    return lax.dot_general(a, b, (((0,), (0,)), ((), ())), preferred_element_type=F32)


def _split3(x):
    hi = x.astype(BF16)
    r1 = x - hi.astype(F32)
    mid = r1.astype(BF16)
    lo = (r1 - mid.astype(F32)).astype(BF16)
    return hi, mid, lo


def _rms(x, g):
    return x * lax.rsqrt(jnp.mean(x * x, axis=-1, keepdims=True) + NORM_EPS) * g


def _sigmoid(x):
    return 1.0 / (1.0 + jnp.exp(-x))


def _silu(x):
    return x * _sigmoid(x)


def _rope(x, cos, sin, half, x1_lane):
    xr = jnp.where(x1_lane, -pltpu.roll(x, LANES - half, 1), pltpu.roll(x, half, 1))
    return x * cos + xr * sin


def _ada_kernel(c_ref, w_ref, b_ref, o_ref):
    c = c_ref[...]
    ca = _silu(c).astype(BF16)
    o_ref[0] = _dot(ca, w_ref[0].astype(BF16)) + b_ref[0]


def _ada(c, w_ada, b_ada):
    L, D, N = w_ada.shape
    B = c.shape[0]
    tn = 1536
    return pl.pallas_call(
        _ada_kernel,
        grid=(L, N // tn),
        in_specs=[pl.BlockSpec((B, D), lambda l, j: (0, 0)),
                  pl.BlockSpec((1, D, tn), lambda l, j: (l, 0, j)),
                  pl.BlockSpec((1, 1, tn), lambda l, j: (l, 0, j))],
        out_specs=pl.BlockSpec((1, B, tn), lambda l, j: (l, 0, j)),
        out_shape=jax.ShapeDtypeStruct((L, B, N), F32),
        compiler_params=_params("parallel", "parallel"),
        name="ada_mod",
    )(c, w_ada, b_ada.reshape(L, 1, N))


def _rope_kernel(pos_ref, inv_ref, cn_ref, sn_ref, cm_ref, sm_ref):
    pos = pos_ref[...]
    ang_n = pos * inv_ref[0:1, :]
    ang_m = pos * inv_ref[1:2, :]
    cn_ref[...] = jnp.cos(ang_n)
    sn_ref[...] = jnp.sin(ang_n)
    cm_ref[...] = jnp.cos(ang_m)
    sm_ref[...] = jnp.sin(ang_m)


def _rope_tables(positions):
    T = positions.size
    inv_n = jnp.power(ROPE_THETA, -jnp.arange(0, 2 * NSA_ROT_HALF, 2, dtype=F32) / (2 * NSA_ROT_HALF))
    inv_m = jnp.power(ROPE_THETA, -jnp.arange(0, MLA_ROPE, 2, dtype=F32) / MLA_ROPE)
    z = lambda n: jnp.zeros((n,), F32)
    head_n = jnp.concatenate([inv_n, inv_n, z(HEAD_DIM - 2 * NSA_ROT_HALF)])
    lane_n = jnp.concatenate([head_n, head_n])
    lane_m = jnp.concatenate([z(MLA_NOPE), inv_m, inv_m, z(LANES - MLA_NOPE - MLA_ROPE)])
    inv = jnp.zeros((8, LANES), F32).at[0].set(lane_n).at[1].set(lane_m)
    posb = jnp.broadcast_to(positions.reshape(T, 1).astype(F32), (T, LANES))
    tm = 2048
    spec = pl.BlockSpec((tm, LANES), lambda i: (i, 0))
    return pl.pallas_call(
        _rope_kernel,
        grid=(T // tm,),
        in_specs=[spec, pl.BlockSpec((8, LANES), lambda i: (0, 0))],
        out_specs=[spec] * 4,
        out_shape=[jax.ShapeDtypeStruct((T, LANES), F32)] * 4,
        compiler_params=_params("parallel"),
        name="rope_tables",
    )(posb, inv)


def _rope_rows(t, offset, half, cos, sin):
    x1, x2 = t[offset:offset + half], t[offset + half:offset + 2 * half]
    return x1 * cos - x2 * sin, x2 * cos + x1 * sin


def _inproj_kernel(x_ref, sh_ref, sc_ref, g1_ref, ws_ref, wt_ref, cn_ref, sn_ref, cm_ref, sm_ref,
                   cnT_ref, snT_ref, cmT_ref, smT_ref, gq_ref, wuqT_ref, gkv_ref, wkm_ref, wvmT_ref,
                   qaT_ref, gT_ref, ck_ref, cv_ref, ska_ref, wk_ref, svT_ref, wvT_ref,
                   qmT_ref, km_ref, vmT_ref, cx_ref, mv_ref, mo_ref, cif_ref, *, per_b):
    tm = x_ref.shape[0]
    x = x_ref[...]
    h = _rms(x, g1_ref[...]) * (1.0 + sc_ref[0]) + sh_ref[0]
    hb = h.astype(BF16)

    def seg(start, width):
        return _dot(hb, ws_ref[:, start:start + width])

    def seg_t(start, height):
        return _dot_nt(wt_ref[start:start + height, :], hb)

    lane = lax.broadcasted_iota(jnp.int32, (1, LANES), 1)
    x1_n = (lane % HEAD_DIM) < NSA_ROT_HALF
    x1_m = lane < MLA_NOPE + MLA_ROPE // 2
    cn, sn, cm, sm = cn_ref[...], sn_ref[...], cm_ref[...], sm_ref[...]
    rope_n = lambda t: _rope(t, cn, sn, NSA_ROT_HALF, x1_n)
    rope_m = lambda t: _rope(t, cm, sm, MLA_ROPE // 2, x1_m)

    qn = _rms(seg(SEG_QL, MLA_Q_LORA), gq_ref[...]).astype(BF16)
    kvl_kr = seg(SEG_KVL, 2 * LANES)
    kvn = _rms(kvl_kr[:, :LANES], gkv_ref[...]).astype(BF16)
    kr = rope_m(kvl_kr[:, LANES:])

    qt = seg_t(TSEG_Q, WIDTH_A)
    cnt, snt = cnT_ref[...], snT_ref[...]
    parts = []
    for hd in range(NSA_HEADS):
        o = HEAD_DIM * hd
        parts += list(_rope_rows(qt, o, NSA_ROT_HALF, cnt, snt)) + [qt[o + 2 * NSA_ROT_HALF:o + HEAD_DIM]]
    qaT_ref[...] = (jnp.concatenate(parts, axis=0) * (HEAD_DIM ** -0.5 * LOG2E)).astype(BF16)
    gT_ref[...] = _sigmoid(seg_t(TSEG_G, TSEG_G_ROWS))[:GATE_ROWS]
    svT_ref[...] = seg_t(TSEG_SV, LANES).astype(BF16)
    wvT_ref[...] = seg_t(TSEG_WV, LANES).astype(BF16)

    ckv = seg(SEG_CKV, 2 * LANES)
    ck_ref[...] = rope_n(ckv[:, :LANES])
    cv_ref[...] = ckv[:, LANES:]
    ska_ref[:, :LANES] = rope_n(seg(SEG_SK, LANES)).astype(BF16)
    srow = (pl.program_id(0) % per_b) * tm + lax.broadcasted_iota(jnp.int32, (tm, LANES), 0)
    lane2 = lax.broadcasted_iota(jnp.int32, (tm, LANES), 1)
    code = (lane2 < 4 * SLC_TOPK) & ((lane2 & (2 * SLC_TOPK - 1)) == srow // SLC_LEN)
    ska_ref[:, LANES:] = jnp.where(code, 1.0, 0.0).astype(BF16)
    wk_ref[...] = rope_n(seg(SEG_WK, LANES)).astype(BF16)

    cx_ref[...] = seg(SEG_CX, WIDTH_C)
    mv_ref[...] = seg(SEG_MV, WIDTH_C).astype(BF16)
    mo_ref[...] = seg(SEG_MO, WIDTH_C)
    cif_ref[...] = seg(SEG_IF, 2 * LANES)

    qmt = _dot_nt(wuqT_ref[...], qn)
    cmt, smt = cmT_ref[...], smT_ref[...]
    parts = []
    for hd in range(MLA_HEADS):
        o = LANES * hd
        parts += [qmt[o:o + MLA_NOPE]] + list(_rope_rows(qmt, o + MLA_NOPE, MLA_ROPE // 2, cmt, smt))
        parts += [qmt[o + MLA_NOPE + MLA_ROPE:o + LANES]]
    qmT_ref[...] = (jnp.concatenate(parts, axis=0) * ((MLA_NOPE + MLA_ROPE) ** -0.5 * LOG2E)).astype(BF16)
    kk = _dot(kvn, wkm_ref[...])
    for hd in range(MLA_HEADS):
        km_ref[:, LANES * hd:LANES * (hd + 1)] = (kk[:, LANES * hd:LANES * (hd + 1)] + kr).astype(BF16)
    vmT_ref[...] = _dot_nt(wvmT_ref[...], kvn).astype(BF16)


def _inproj(x2, shift, scale, g1, w_std, w_t, tabs, tabs_t, gq, wuqT, gkv, wkm, wvmT, S):
    T = x2.shape[0]
    tm = TM_PROJ
    per_b = S // tm
    row = lambda w: pl.BlockSpec((tm, w), lambda i: (i, 0))
    col = lambda h: pl.BlockSpec((h, tm), lambda i: (0, i))
    full = lambda a: pl.BlockSpec(a.shape, lambda i: (0,) * a.ndim)
    modspec = pl.BlockSpec((1, 1, D_MODEL), lambda i: (i // per_b, 0, 0))
    outs = [(WIDTH_A, BF16, True), (GATE_ROWS, F32, True), (LANES, F32, False), (LANES, F32, False), (2 * LANES, BF16, False),
            (LANES, BF16, False), (LANES, BF16, True), (LANES, BF16, True),
            (MLA_HEADS * LANES, BF16, True), (MLA_HEADS * LANES, BF16, False), (WIDTH_B, BF16, True),
            (WIDTH_C, F32, False), (WIDTH_C, BF16, False), (WIDTH_C, F32, False), (2 * LANES, F32, False)]
    return pl.pallas_call(
        functools.partial(_inproj_kernel, per_b=per_b),
        grid=(T // tm,),
        in_specs=[row(D_MODEL), modspec, modspec, full(g1), full(w_std), full(w_t)] + [row(LANES)] * 4
                 + [col(t.shape[0]) for t in tabs_t] + [full(gq), full(wuqT), full(gkv), full(wkm), full(wvmT)],
        out_specs=[col(w) if tr else row(w) for w, _, tr in outs],
        out_shape=[jax.ShapeDtypeStruct((w, T) if tr else (T, w), dt) for w, dt, tr in outs],
        compiler_params=_params("parallel"),
        name="in_proj",
    )(x2, shift, scale, g1, w_std, w_t, *tabs, *tabs_t, gq, wuqT, gkv, wkm, wvmT)


def _compress_kernel(xk_ref, xv_ref, wk_ref, wv_ref, pos_ref, wkf_ref, wvf_ref, kc_ref, vcT_ref):
    ng = xk_ref.shape[0] // CMP_STRIDE
    row = lax.broadcasted_iota(jnp.int32, (ng, LANES), 0)
    pos = pos_ref[...].astype(BF16)
    acc_k = jnp.zeros((ng, 2 * LANES), F32)
    acc_v = jnp.zeros((ng, 2 * LANES), F32)
    for t in range(CMP_STRIDE):
        tok = pl.ds(t, ng, stride=CMP_STRIDE)
        acc_k = acc_k + _dot(xk_ref[tok, :].astype(BF16), wk_ref[t])
        acc_v = acc_v + _dot(xv_ref[tok, :].astype(BF16), wv_ref[t])

    def finish(acc, wf_ref):
        const = _dot(pos, wf_ref[...].astype(BF16))
        both = acc[:, :LANES] + pltpu.roll(acc[:, LANES:], ng - 1, 0) + jnp.concatenate([const, const], axis=1)
        return jnp.where(row < ng - 1, both, 0.0)

    kc_ref[0] = finish(acc_k, wkf_ref).astype(BF16)
    vcT_ref[0] = finish(acc_v, wvf_ref).T.astype(BF16)


def _compress(ck, cv, cmp_pos, w_cmp_k, w_cmp_v, B, S):
    ng = S // CMP_STRIDE

    def per_token(w):
        a = w[:CMP_STRIDE * HEAD_DIM].reshape(CMP_STRIDE, HEAD_DIM, HEAD_DIM)
        b = w[CMP_STRIDE * HEAD_DIM:].reshape(CMP_STRIDE, HEAD_DIM, HEAD_DIM)
        z = jnp.zeros_like(a)
        top = jnp.concatenate([a, z, b, z], axis=2)
        bot = jnp.concatenate([z, a, z, b], axis=2)
        return jnp.concatenate([top, bot], axis=1).astype(BF16)

    full = lambda a: pl.BlockSpec(a.shape, lambda b: (0,) * a.ndim)
    wk3, wv3 = per_token(w_cmp_k), per_token(w_cmp_v)
    posf = cmp_pos.reshape(1, CMP_LEN * HEAD_DIM)
    ospec = pl.BlockSpec((1, ng, LANES), lambda b: (b, 0, 0))
    return pl.pallas_call(
        _compress_kernel,
        grid=(B,),
        in_specs=[pl.BlockSpec((S, LANES), lambda b: (b, 0)), pl.BlockSpec((S, LANES), lambda b: (b, 0)),
                  full(wk3), full(wv3), full(posf), full(w_cmp_k), full(w_cmp_v)],
        out_specs=[ospec, ospec],
        out_shape=[jax.ShapeDtypeStruct((B, ng, LANES), BF16)] * 2,
        compiler_params=_params("parallel"),
        name="nsa_compress",
    )(ck, cv, wk3, wv3, posf, w_cmp_k, w_cmp_v)


def _nsa_kernel(qT_ref, gT_ref, kc_ref, vcT_ref, sk_ref, svT_ref, wk_ref, wvT_ref, o_ref, qaug_scr, acc_scr, s_scr):
    tq = TQ_NSA
    cols = NSA_HEADS * tq
    pair = CHAIN_LANES
    t0 = pl.program_id(1) * tq
    n_slc = SLC_TOPK * 2

    frow = lax.broadcasted_iota(jnp.int32, (LANES, tq), 0)
    g0_row = frow < HEAD_DIM
    tiles = [qT_ref[LANES * r:LANES * (r + 1), :] for r in range(NSA_REP)]
    zero = jnp.zeros_like(tiles[0])
    q6 = jnp.concatenate([jnp.where(g0_row, t, zero) for t in tiles]
                         + [jnp.where(g0_row, zero, t) for t in tiles], axis=1)
    qaug_scr[0:LANES, :] = q6
    tq_l = t0 + (lax.broadcasted_iota(jnp.int32, (1, cols), 1) & (tq - 1))

    pairs = [slice(pair * pp, pair * (pp + 1)) for pp in range(cols // pair)]
    s = _dot(kc_ref[0], q6)
    span = WINDOW + tq
    start = pl.multiple_of(jnp.maximum(t0 - WINDOW, 0), tq)
    kw = wk_ref[pl.ds(start, span), :]
    win_scores = [_dot(kw, q6[:, sl]) for sl in pairs]

    tq_1 = tq_l[:, :tq]
    nrow = lax.broadcasted_iota(jnp.int32, (LANES, tq), 0)
    cmp_bias = jnp.where(nrow * CMP_STRIDE + (CMP_LEN - 1) <= tq_1, 0.0, NEG)
    s = s + jnp.concatenate([cmp_bias] * NSA_HEADS, axis=1)
    e = jnp.exp2(s - jnp.max(s, axis=0, keepdims=True))
    seen = jnp.where(tq_l >= CMP_LEN - 1, 1.0, 0.0)
    p = e * (seen / jnp.sum(e, axis=0, keepdims=True))
    o_cmp = _dot(vcT_ref[0], p.astype(BF16))

    jr = lax.broadcasted_iota(jnp.int32, (n_slc, LANES), 0)
    nc = lax.broadcasted_iota(jnp.int32, (n_slc, LANES), 1)
    ovl = ((nc * CMP_STRIDE < jr * SLC_LEN + SLC_LEN) & (nc * CMP_STRIDE + CMP_LEN > jr * SLC_LEN)
           & (nc < LANES - 1))
    ovl = jnp.where(ovl, 1.0, 0.0).astype(BF16)
    jq = lax.broadcasted_iota(jnp.int32, (n_slc, tq), 0)
    tl = t0 + lax.broadcasted_iota(jnp.int32, (n_slc, tq), 1)
    cur = tl // SLC_LEN
    forced = (jq == 0) | (jq == cur) | (jq == cur - 1)
    future = jq * SLC_LEN > tl
    bias_t = []
    for g in range(NSA_KV_HEADS):
        pg = p[:, (3 * g) * tq:(3 * g + 1) * tq] + p[:, (3 * g + 1) * tq:(3 * g + 2) * tq] + p[:, (3 * g + 2) * tq:(3 * g + 3) * tq]
        imp = sum(_dot(ovl, part) for part in _split3(pg))
        imp = jnp.where(forced, jnp.inf, imp)
        imp = jnp.where(future, -jnp.inf, imp)
        rank = jnp.zeros((n_slc, tq), F32)
        for jp in range(n_slc):
            rv = imp[jp:jp + 1, :]
            ahead = jnp.where(rv > imp, 1.0, jnp.where((rv == imp) & (jq > jp), 1.0, 0.0))
            rank = rank + ahead
        bias_t.append(jnp.where(rank < float(SLC_TOPK), 0.0, NEG).astype(BF16))
    zb = jnp.zeros((n_slc, NSA_REP * tq), BF16)
    qaug_scr[LANES:LANES + n_slc, :] = jnp.concatenate([bias_t[0]] * NSA_REP + [zb], axis=1)
    qaug_scr[LANES + n_slc:LANES + 2 * n_slc, :] = jnp.concatenate([zb] + [bias_t[1]] * NSA_REP, axis=1)
    qaug_scr[LANES + 2 * n_slc:, :] = jnp.zeros((LANES - 2 * n_slc, cols), BF16)

    vwt = wvT_ref[:, pl.ds(start, span)]
    wrow = start + lax.broadcasted_iota(jnp.int32, (span, pair), 0)
    o_win = []
    for sl, sc in zip(pairs, win_scores):
        in_window = lax.bitcast_convert_type(tq_l[:, sl] - wrow, jnp.uint32) < jnp.uint32(WINDOW)
        sc = jnp.where(in_window, sc, NEG)
        ew = jnp.exp2(sc - jnp.max(sc, axis=0, keepdims=True))
        o_win.append(_dot(vwt, ew.astype(BF16)) / jnp.sum(ew, axis=0, keepdims=True))
    o_win = jnp.concatenate(o_win, axis=1)

    acc_scr[...] = jnp.zeros((LANES, cols), F32)
    krow = lax.broadcasted_iota(jnp.int32, (TK_SLC, pair), 0)

    def slc_scores(k0, sl):
        return _dot(sk_ref[pl.ds(k0, TK_SLC), :], qaug_scr[:, sl])

    nch = len(pairs)
    la = s_scr.shape[0]
    for pp in range(la):
        s_scr[pp] = slc_scores(0, pairs[pp])

    def slc_tile(k0, m, l, masked):
        vt = svT_ref[:, pl.ds(k0, TK_SLC)]
        ms, ls, accs = [], [], []
        ahead = [s_scr[pp] for pp in range(la)]
        for pp, sl in enumerate(pairs):
            sc = ahead.pop(0)
            nxt = pp + la
            if nxt < nch:
                ahead.append(slc_scores(k0, pairs[nxt]))
            elif not masked:
                s_scr[nxt - nch] = slc_scores(pl.multiple_of(k0 + TK_SLC, TK_SLC), pairs[nxt - nch])
            if masked:
                sc = jnp.where(k0 + krow <= tq_l[:, sl], sc, NEG)
            m_new = jnp.maximum(m[:, sl], jnp.max(sc, axis=0, keepdims=True))
            alpha = jnp.exp2(m[:, sl] - m_new)
            pe = jnp.exp2(sc - m_new)
            ls.append(alpha * l[:, sl] + jnp.sum(pe, axis=0, keepdims=True))
            accs.append(alpha * acc_scr[:, sl] + _dot(vt, pe.astype(BF16)))
            ms.append(m_new)
        acc_scr[...] = jnp.concatenate(accs, axis=1)
        return jnp.concatenate(ms, axis=1), jnp.concatenate(ls, axis=1)

    def slc_step(kt, carry):
        return slc_tile(pl.multiple_of(kt * TK_SLC, TK_SLC), carry[0], carry[1], False)

    n_full = t0 // TK_SLC
    m, l = lax.fori_loop(0, n_full, slc_step, (jnp.full((1, cols), NEG, F32), jnp.zeros((1, cols), F32)))
    m, l = slc_tile(pl.multiple_of(n_full * TK_SLC, TK_SLC), m, l, True)
    o_slc = acc_scr[...] / l

    gt = gT_ref[...]
    grow = lambda j: jnp.concatenate([gt[3 * hd + j:3 * hd + j + 1, :] for hd in range(NSA_HEADS)], axis=1)
    mixed = grow(0) * o_cmp + grow(1) * o_slc + grow(2) * o_win
    for r in range(NSA_REP):
        t = jnp.where(g0_row, mixed[:, tq * r:tq * (r + 1)], mixed[:, tq * (NSA_REP + r):tq * (NSA_REP + r + 1)])
        o_ref[:, LANES * r:LANES * (r + 1)] = t.T


def _nsa(qaT, gT, kc, vcT, sk_aug, svT, wk, wvT, B, S):
    tq = TQ_NSA
    nq = S // tq
    cols = NSA_HEADS * tq
    qcol = lambda h: pl.BlockSpec((h, tq), lambda b, i: (0, b * nq + i))
    seq = lambda w: pl.BlockSpec((S, w), lambda b, i: (b, 0))
    seqT = pl.BlockSpec((LANES, S), lambda b, i: (0, b))
    cspec = pl.BlockSpec((1, LANES, LANES), lambda b, i: (b, 0, 0))
    return pl.pallas_call(
        _nsa_kernel,
        grid=(B, nq),
        in_specs=[qcol(WIDTH_A), qcol(gT.shape[0]), cspec, cspec, seq(2 * LANES), seqT, seq(LANES), seqT],
        out_specs=pl.BlockSpec((tq, WIDTH_A), lambda b, i: (b * nq + i, 0)),
        out_shape=jax.ShapeDtypeStruct((B * S, WIDTH_A), F32),
        scratch_shapes=[pltpu.VMEM((2 * LANES, cols), BF16), pltpu.VMEM((LANES, cols), F32),
                        pltpu.VMEM((cols // CHAIN_LANES, TK_SLC, CHAIN_LANES), F32)],
        compiler_params=_params("parallel", "arbitrary"),
        name="nsa_attention",
    )(qaT, gT, kc, vcT, sk_aug, svT, wk, wvT)


def _mla_kernel(qT_ref, k_ref, vT_ref, o_ref, acc_scr, s_scr):
    tq = TQ_MLA
    t0 = pl.program_id(1) * tq
    tq_l = t0 + lax.broadcasted_iota(jnp.int32, (1, tq), 1)
    krow = lax.broadcasted_iota(jnp.int32, (tq, tq), 0)
    acc_scr[...] = jnp.zeros((MLA_HEADS, LANES, tq), F32)

    def scores(k0, hd):
        k = k_ref[pl.ds(k0, tq), LANES * hd:LANES * (hd + 1)]
        return _dot(k, qT_ref[LANES * hd:LANES * (hd + 1), :])

    for hd in range(MXU_LOOKAHEAD):
        s_scr[hd] = scores(0, hd)

    def tile(k0, ms, ls, masked):
        new_m, new_l, accs = [], [], []
        ahead = [s_scr[hd] for hd in range(MXU_LOOKAHEAD)]
        for hd in range(MLA_HEADS):
            sc = ahead.pop(0)
            nxt = hd + MXU_LOOKAHEAD
            if nxt < MLA_HEADS:
                ahead.append(scores(k0, nxt))
            elif not masked:
                s_scr[nxt - MLA_HEADS] = scores(pl.multiple_of(k0 + tq, tq), nxt - MLA_HEADS)
            if masked:
                sc = jnp.where(k0 + krow <= tq_l, sc, NEG)
            m_new = jnp.maximum(ms[hd], jnp.max(sc, axis=0, keepdims=True))
            alpha = jnp.exp2(ms[hd] - m_new)
            pe = jnp.exp2(sc - m_new)
            new_l.append(alpha * ls[hd] + jnp.sum(pe, axis=0, keepdims=True))
            vt = vT_ref[LANES * (hd // 2):LANES * (hd // 2 + 1), pl.ds(k0, tq)]
            accs.append(alpha * acc_scr[hd] + _dot(vt, pe.astype(BF16)))
            new_m.append(m_new)
        acc_scr[...] = jnp.stack(accs)
        return tuple(new_m), tuple(new_l)

    def step(kt, carry):
        return tile(pl.multiple_of(kt * tq, tq), carry[0], carry[1], False)

    n_full = pl.program_id(1)
    init = (tuple(jnp.full((1, tq), NEG, F32) for _ in range(MLA_HEADS)),
            tuple(jnp.zeros((1, tq), F32) for _ in range(MLA_HEADS)))
    ms, ls = lax.fori_loop(0, n_full, step, init)
    ms, ls = tile(pl.multiple_of(n_full * tq, tq), ms, ls, True)
    frow = lax.broadcasted_iota(jnp.int32, (LANES, tq), 0)
    for pr in range(MLA_HEADS // 2):
        t = jnp.where(frow < MLA_V, acc_scr[2 * pr] / ls[2 * pr], acc_scr[2 * pr + 1] / ls[2 * pr + 1])
        o_ref[:, LANES * pr:LANES * (pr + 1)] = t.T


def _mla(qmT, km, vmT, B, S):
    tq = TQ_MLA
    nq = S // tq
    return pl.pallas_call(
        _mla_kernel,
        grid=(B, nq),
        in_specs=[pl.BlockSpec((MLA_HEADS * LANES, tq), lambda b, i: (0, b * nq + i)),
                  pl.BlockSpec((S, MLA_HEADS * LANES), lambda b, i: (b, 0)),
                  pl.BlockSpec((WIDTH_B, S), lambda b, i: (0, b))],
        out_specs=pl.BlockSpec((tq, WIDTH_B), lambda b, i: (b * nq + i, 0)),
        out_shape=jax.ShapeDtypeStruct((B * S, WIDTH_B), F32),
        scratch_shapes=[pltpu.VMEM((MLA_HEADS, LANES, tq), F32), pltpu.VMEM((MXU_LOOKAHEAD, tq, tq), F32)],
        compiler_params=_params("parallel", "arbitrary"),
        name="mla_attention",
    )(qmT, km, vmT)


def _log_sigmoid(z):
    return jnp.minimum(z, 0.0) - jnp.log1p(jnp.exp(-jnp.abs(z)))


def _mlstm_kernel(cx_ref, v_ref, o_ref, gc_ref, gr_ref, cw_ref, cb_ref, wq_ref, wk_ref, bc_ref, br_ref,
                  gmh_ref, skip_ref, y_ref,
                  xc_scr, q_scr, k_scr, b_scr, cmu_scr, w_scr, inter_scr, floor_scr, mfull_scr, ut_scr,
                  eo_scr, el_scr, cprev_scr):
    S = cx_ref.shape[0]
    L = MLSTM_TILE
    NC = S // L
    d = MLSTM_DIM
    pairs = MLSTM_HEADS // 2
    group = MLSTM_GROUP

    x = cx_ref[...]
    rowi = lax.broadcasted_iota(jnp.int32, (S, WIDTH_C), 0)
    conv = x * cw_ref[MLSTM_CONV - 1:MLSTM_CONV, :]
    for back in range(1, MLSTM_CONV):
        shifted = jnp.where(rowi >= back, pltpu.roll(x, back, 0), 0.0)
        conv = conv + shifted * cw_ref[MLSTM_CONV - 1 - back:MLSTM_CONV - back, :]
    xc = _silu(conv + cb_ref[...])
    xc_scr[...] = xc
    xcb = xc.astype(BF16)
    q_scr[...] = _dot(xcb, wq_ref[...]).astype(BF16)
    k_scr[...] = _dot(xcb, wk_ref[...]).astype(BF16)

    in_chunk = lax.broadcasted_iota(jnp.int32, (S, LANES), 0) & (L - 1)

    def scan_rows(val, op, fill):
        sh = 1
        while sh < L:
            val = op(val, jnp.where(in_chunk >= sh, pltpu.roll(val, sh, 0), fill))
            sh *= 2
        return val

    ig = gc_ref[:, :LANES] + bc_ref[0:1, :]
    lf = _log_sigmoid(gc_ref[:, LANES:] + bc_ref[1:2, :])
    b = scan_rows(lf, jnp.add, 0.0)
    u = ig - b
    cmu = scan_rows(u, jnp.maximum, -jnp.inf)
    b_scr[...] = b
    cmu_scr[...] = cmu
    b_last = b_scr[pl.ds(L - 1, NC, stride=L), :]
    u_max = cmu_scr[pl.ds(L - 1, NC, stride=L), :]

    m = jnp.zeros((1, LANES), F32)
    m_rows = []
    for c in range(NC):
        m_rows.append(m)
        m = b_last[c:c + 1] + jnp.maximum(m, u_max[c:c + 1])
    m_prev = jnp.concatenate(m_rows, axis=0)
    m_top = jnp.maximum(m_prev, u_max)
    e_old = jnp.exp(m_prev - m_top)
    e_loc = jnp.exp(u_max - m_top)
    m_tok = jnp.maximum(cmu.reshape(NC, L, LANES), m_prev.reshape(NC, 1, LANES))
    inter = jnp.exp(m_prev.reshape(NC, 1, LANES) - m_tok).reshape(S, LANES)
    w_loc = jnp.exp(u - jnp.broadcast_to(u_max.reshape(NC, 1, LANES), (NC, L, LANES)).reshape(S, LANES))
    m_tok = m_tok.reshape(S, LANES)
    floor = jnp.exp(-(b + m_tok))

    hrow = lax.broadcasted_iota(jnp.int32, (LANES, 2 * LANES), 0)
    hcol = lax.broadcasted_iota(jnp.int32, (LANES, 2 * LANES), 1)
    to_heads = jnp.where(hcol // d == hrow, 1.0, 0.0).astype(BF16)
    frow = lax.broadcasted_iota(jnp.int32, (LANES, 4 * LANES), 0)
    fcol = lax.broadcasted_iota(jnp.int32, (LANES, 4 * LANES), 1)
    to_full = jnp.where(fcol // LANES == frow, 1.0, 0.0).astype(BF16)

    def spread(val, onehot):
        return sum(_dot(part, onehot) for part in _split3(val))

    w_scr[...] = spread(w_loc, to_heads)
    inter_scr[...] = spread(inter, to_heads)
    floor_scr[...] = spread(floor, to_heads)
    mfull_scr[...] = spread(m_tok, to_full)
    eo_full, el_full = spread(e_old, to_full), spread(e_loc, to_full)
    for c in range(NC):
        eo_scr[c] = eo_full[c:c + 1]
        el_scr[c] = el_full[c:c + 1]

    lane_in_chunk = lax.broadcasted_iota(jnp.int32, (2 * MLSTM_HEADS, S), 1) & (L - 1)
    gt = gr_ref[...] + br_ref[...]
    bt = _log_sigmoid(gt)
    sh = 1
    while sh < L:
        bt = bt + jnp.where(lane_in_chunk >= sh, pltpu.roll(bt, sh, 1), 0.0)
        sh *= 2
    ut_scr[0:MLSTM_HEADS, :] = gt[0:MLSTM_HEADS] - bt[MLSTM_HEADS:]

    arow = lax.broadcasted_iota(jnp.int32, (LANES, 2 * LANES), 0)
    acol = lax.broadcasted_iota(jnp.int32, (LANES, 2 * LANES), 1)
    blk2 = (arow // d) == ((acol & (LANES - 1)) // d)
    ones_v = jnp.ones((L, LANES), BF16)

    def head_rows(ref, c, pr):
        top = jnp.broadcast_to(ref[c, :, LANES * (2 * pr):LANES * (2 * pr + 1)], (d, LANES))
        bot = jnp.broadcast_to(ref[c, :, LANES * (2 * pr + 1):LANES * (2 * pr + 2)], (d, LANES))
        half = jnp.concatenate([top, bot], axis=0)
        return jnp.concatenate([half, half], axis=1)

    def state_group(g, carry):
        local = []
        for cc in range(group):
            r0 = pl.multiple_of((g * group + cc) * L, L)
            for pr in range(pairs):
                ps = slice(LANES * pr, LANES * (pr + 1))
                kw = (k_scr[pl.ds(r0, L), ps].astype(F32) * w_scr[pl.ds(r0, L), ps]).astype(BF16)
                vo = jnp.concatenate([v_ref[pl.ds(r0, L), ps], ones_v], axis=1)
                local.append(jnp.where(blk2, _dot_tn(kw, vo), 0.0))
        state = list(carry)
        for cc in range(group):
            c = g * group + cc
            for pr in range(pairs):
                cprev_scr[c, pr] = state[pr].astype(BF16)
                state[pr] = head_rows(eo_scr, c, pr) * state[pr] + head_rows(el_scr, c, pr) * local[cc * pairs + pr]
        return tuple(state)

    lax.fori_loop(0, NC // group, state_group, tuple(jnp.zeros((LANES, 2 * LANES), F32) for _ in range(pairs)))

    li = lax.broadcasted_iota(jnp.int32, (L, L), 0)
    si = lax.broadcasted_iota(jnp.int32, (L, L), 1)
    causal = si <= li
    lane = lax.broadcasted_iota(jnp.int32, (L, LANES), 1)
    h0_lane = lane < d
    avg = jnp.where((li // d) == (si // d), 1.0 / d, 0.0).astype(BF16)

    def dot2(val, rhs):
        hi = val.astype(BF16)
        return _dot(hi, rhs) + _dot((val - hi.astype(F32)).astype(BF16), rhs)

    def out_group(g, carry):
        units = [(cc, pr) for cc in range(group) for pr in range(pairs)]
        chunk_of = {u_: g * group + u_[0] for u_ in units}
        rows = {u_: pl.multiple_of(chunk_of[u_] * L, L) for u_ in units}
        qc, sc, pv = {}, {}, {}
        for u_ in units:
            c, pr = chunk_of[u_], u_[1]
            ps = slice(LANES * pr, LANES * (pr + 1))
            qp = q_scr[pl.ds(rows[u_], L), ps]
            kp = k_scr[pl.ds(rows[u_], L), ps]
            qc[u_] = _dot(qp, cprev_scr[c, pr])
            sc[u_] = [_dot_nt(jnp.where(h0_lane if hh == 0 else lane >= d, qp, jnp.zeros_like(qp)), kp)
                      for hh in range(2)]
        for u_ in units:
            pr = u_[1]
            ps = slice(LANES * pr, LANES * (pr + 1))
            vo = jnp.concatenate([v_ref[pl.ds(rows[u_], L), ps], ones_v], axis=1)
            pv[u_] = []
            for hh in range(2):
                hd = 2 * pr + hh
                u_row = ut_scr[hd:hd + 1, pl.ds(rows[u_], L)]
                decay = jnp.where(causal, jnp.exp(u_row - mfull_scr[pl.ds(rows[u_], L), LANES * hd:LANES * (hd + 1)]), 0.0)
                pv[u_].append(_dot((sc[u_][hh] * decay).astype(BF16), vo))
        hg, cen = {}, {}
        for u_ in units:
            pr = u_[1]
            ps = slice(LANES * pr, LANES * (pr + 1))
            it = inter_scr[pl.ds(rows[u_], L), ps]
            num = it * qc[u_][:, :LANES] + jnp.where(h0_lane, pv[u_][0][:, :LANES], pv[u_][1][:, :LANES])
            den = it * qc[u_][:, LANES:] + jnp.where(h0_lane, pv[u_][0][:, LANES:], pv[u_][1][:, LANES:])
            hcell = num / jnp.maximum(jnp.abs(den), floor_scr[pl.ds(rows[u_], L), ps])
            hg[u_] = _sigmoid(o_ref[pl.ds(rows[u_], L), ps]) * hcell
        for u_ in units:
            cen[u_] = hg[u_] - dot2(hg[u_], avg)
        for u_ in units:
            pr = u_[1]
            ps = slice(LANES * pr, LANES * (pr + 1))
            var = dot2(cen[u_] * cen[u_], avg)
            y_ref[pl.ds(rows[u_], L), ps] = (cen[u_] * lax.rsqrt(var + NORM_EPS) * gmh_ref[:, ps]
                                             + skip_ref[:, ps] * xc_scr[pl.ds(rows[u_], L), ps])
        return carry

    lax.fori_loop(0, NC // group, out_group, 0)


def _mlstm(cx, mv, mo, cif, conv_w, conv_b, w_q_m, w_k_m, b_igate, b_fgate, g_mh, skip_m, B, S):
    def blockdiag(w, scale):
        out = jnp.zeros((WIDTH_C, WIDTH_C), F32)
        for hd in range(MLSTM_HEADS):
            out = out.at[hd * MLSTM_DIM:(hd + 1) * MLSTM_DIM, hd * MLSTM_DIM:(hd + 1) * MLSTM_DIM].set(w[hd] * scale)
        return out.astype(BF16)

    nc = S // MLSTM_TILE
    wq = blockdiag(w_q_m, MLSTM_DIM ** -0.5)
    wk = blockdiag(w_k_m, 1.0)
    bc = jnp.zeros((2, LANES), F32).at[0, :MLSTM_HEADS].set(b_igate).at[1, :MLSTM_HEADS].set(b_fgate)
    bias8 = jnp.concatenate([b_igate, b_fgate])
    br = jnp.broadcast_to(bias8[:, None], (2 * MLSTM_HEADS, S))
    gr = jnp.concatenate([cif[:, :MLSTM_HEADS], cif[:, LANES:LANES + MLSTM_HEADS]], axis=1).T
    seq = lambda w: pl.BlockSpec((S, w), lambda b: (b, 0))
    full = lambda a: pl.BlockSpec(a.shape, lambda b: (0,) * a.ndim)
    row = lambda a: a.reshape(1, -1)
    args = [conv_w, row(conv_b), wq, wk, bc, br, row(g_mh), row(skip_m)]
    tok = lambda w, dt: pltpu.VMEM((S, w), dt)
    return pl.pallas_call(
        _mlstm_kernel,
        grid=(B,),
        in_specs=[seq(WIDTH_C), seq(WIDTH_C), seq(WIDTH_C), seq(2 * LANES),
                  pl.BlockSpec((2 * MLSTM_HEADS, S), lambda b: (0, b))] + [full(a) for a in args],
        out_specs=seq(WIDTH_C),
        out_shape=jax.ShapeDtypeStruct((B * S, WIDTH_C), F32),
        scratch_shapes=[tok(WIDTH_C, F32), tok(WIDTH_C, BF16), tok(WIDTH_C, BF16), tok(LANES, F32), tok(LANES, F32),
                        tok(WIDTH_C, F32), tok(WIDTH_C, F32), tok(WIDTH_C, F32), tok(4 * LANES, F32),
                        pltpu.VMEM((2 * MLSTM_HEADS, S), F32), pltpu.VMEM((nc, 1, 4 * LANES), F32),
                        pltpu.VMEM((nc, 1, 4 * LANES), F32), pltpu.VMEM((nc, MLSTM_HEADS // 2, LANES, 2 * LANES), BF16)],
        compiler_params=_params("parallel"),
        name="mlstm_mixer",
    )(cx, mv, mo, cif, gr, *args)


def _outproj_kernel(ya_ref, yb_ref, yc_ref, x_ref, gate_ref, ga_ref, gb_ref, w_ref, o_ref):
    a = _rms(ya_ref[...], ga_ref[...]).astype(BF16)
    b = _rms(yb_ref[...], gb_ref[...]).astype(BF16)
    c = yc_ref[...].astype(BF16)
    y = _dot(a, w_ref[0:WIDTH_A, :]) + _dot(b, w_ref[WIDTH_A:WIDTH_A + WIDTH_B, :]) + _dot(c, w_ref[WIDTH_A + WIDTH_B:, :])
    o_ref[...] = x_ref[...] + gate_ref[0] * y


def _outproj(ya, yb, yc, x2, gate, ga, gb, w_out, S):
    T = x2.shape[0]
    tm = TM_PROJ
    per_b = S // tm
    row = lambda w: pl.BlockSpec((tm, w), lambda i: (i, 0))
    full = lambda a: pl.BlockSpec(a.shape, lambda i: (0,) * a.ndim)
    return pl.pallas_call(
        _outproj_kernel,
        grid=(T // tm,),
        in_specs=[row(WIDTH_A), row(WIDTH_B), row(WIDTH_C), row(D_MODEL),
                  pl.BlockSpec((1, 1, D_MODEL), lambda i: (i // per_b, 0, 0)), full(ga), full(gb), full(w_out)],
        out_specs=row(D_MODEL),
        out_shape=jax.ShapeDtypeStruct((T, D_MODEL), F32),
        compiler_params=_params("parallel"),
        name="out_proj",
    )(ya, yb, yc, x2, gate, ga, gb, w_out)


def _ffn_kernel(x_ref, sh_ref, sc_ref, gate_ref, g2_ref, wgu_ref, wd_ref, gf_ref, o_ref, *, final_norm):
    x = x_ref[...]
    h = (_rms(x, g2_ref[...]) * (1.0 + sc_ref[0]) + sh_ref[0]).astype(BF16)
    acc = jnp.zeros(x.shape, F32)
    for j in range(FFN_HIDDEN // FFN_CHUNK):
        gate = _dot(h, wgu_ref[:, FFN_CHUNK * j:FFN_CHUNK * (j + 1)])
        up = _dot(h, wgu_ref[:, FFN_HIDDEN + FFN_CHUNK * j:FFN_HIDDEN + FFN_CHUNK * (j + 1)])
        act = (_silu(gate) * up).astype(BF16)
        acc = acc + _dot(act, wd_ref[FFN_CHUNK * j:FFN_CHUNK * (j + 1), :])
    y = x + gate_ref[0] * acc
    if final_norm:
        y = _rms(y, gf_ref[...])
    o_ref[...] = y


def _ffn(x2, shift, scale, gate, g2, wgu, wd, gf, S, final_norm):
    T = x2.shape[0]
    tm = TM_PROJ
    per_b = S // tm
    row = pl.BlockSpec((tm, D_MODEL), lambda i: (i, 0))
    full = lambda a: pl.BlockSpec(a.shape, lambda i: (0,) * a.ndim)
    modspec = pl.BlockSpec((1, 1, D_MODEL), lambda i: (i // per_b, 0, 0))
    return pl.pallas_call(
        functools.partial(_ffn_kernel, final_norm=final_norm),
        grid=(T // tm,),
        in_specs=[row, modspec, modspec, modspec, full(g2), full(wgu), full(wd), full(gf)],
        out_specs=row,
        out_shape=jax.ShapeDtypeStruct((T, D_MODEL), F32),
        compiler_params=_params("parallel"),
        name="ffn_final" if final_norm else "ffn",
    )(x2, shift, scale, gate, g2, wgu, wd, gf)


def _head_tile_perm():
    idx = []
    for r in range(NSA_REP):
        idx += list(range(HEAD_DIM * r, HEAD_DIM * (r + 1)))
        idx += list(range(HEAD_DIM * (NSA_REP + r), HEAD_DIM * (NSA_REP + r + 1)))
    return np.asarray(idx, np.int32)


def _in_cols():
    std = np.full((N_STD,), -1, np.int64)
    tr = np.full((N_T,), -1, np.int64)
    off = 0
    tr[TSEG_Q:TSEG_Q + WIDTH_A] = _head_tile_perm()
    off += WIDTH_A
    std[SEG_CKV:SEG_CKV + 2 * LANES] = off + np.arange(2 * LANES)
    off += 2 * LANES
    std[SEG_SK:SEG_SK + LANES] = off + np.arange(LANES)
    off += LANES
    tr[TSEG_SV:TSEG_SV + LANES] = off + np.arange(LANES)
    off += LANES
    std[SEG_WK:SEG_WK + LANES] = off + np.arange(LANES)
    off += LANES
    tr[TSEG_WV:TSEG_WV + LANES] = off + np.arange(LANES)
    off += LANES
    tr[TSEG_G:TSEG_G + N_GATES] = off + np.arange(N_GATES)
    off += N_GATES
    std[SEG_QL:SEG_QL + MLA_Q_LORA] = off + np.arange(MLA_Q_LORA)
    off += MLA_Q_LORA
    std[SEG_KVL:SEG_KVL + MLA_KV_LORA] = off + np.arange(MLA_KV_LORA)
    off += MLA_KV_LORA
    std[SEG_KR + MLA_NOPE:SEG_KR + MLA_NOPE + MLA_ROPE] = off + np.arange(MLA_ROPE)
    off += MLA_ROPE
    for seg in (SEG_CX, SEG_MV, SEG_MO):
        std[seg:seg + WIDTH_C] = off + np.arange(WIDTH_C)
        off += WIDTH_C
    std[SEG_IF:SEG_IF + MLSTM_HEADS] = off + np.arange(MLSTM_HEADS)
    std[SEG_IF + LANES:SEG_IF + LANES + MLSTM_HEADS] = off + MLSTM_HEADS + np.arange(MLSTM_HEADS)
    return std, tr


def _gather_cols(w, cols):
    g = jnp.take(w, jnp.asarray(np.maximum(cols, 0), jnp.int32), axis=1)
    return jnp.where(jnp.asarray(cols >= 0)[None, :], g, 0.0)


def _layer_weights(l, w_in, w_uq, w_ukv, w_out, w_gu, w_down, g_out_a):
    std, tr = _in_cols()
    w_std = _gather_cols(w_in[l], std).astype(BF16)
    w_t = _gather_cols(w_in[l], tr).T.astype(BF16)
    cq = np.full((MLA_HEADS * LANES,), -1, np.int64)
    ck = np.full((MLA_HEADS * LANES,), -1, np.int64)
    cv = np.zeros((WIDTH_B,), np.int64)
    dq = MLA_NOPE + MLA_ROPE
    dkv = MLA_NOPE + MLA_V
    for hd in range(MLA_HEADS):
        cq[LANES * hd:LANES * hd + dq] = dq * hd + np.arange(dq)
        ck[LANES * hd:LANES * hd + MLA_NOPE] = dkv * hd + np.arange(MLA_NOPE)
        cv[MLA_V * hd:MLA_V * (hd + 1)] = dkv * hd + MLA_NOPE + np.arange(MLA_V)
    wuqT = _gather_cols(w_uq[l], cq).T.astype(BF16)
    wkm = _gather_cols(w_ukv[l], ck).astype(BF16)
    wvmT = _gather_cols(w_ukv[l], cv).T.astype(BF16)
    perm = _head_tile_perm()
    wo = jnp.concatenate([w_out[l][perm], w_out[l][WIDTH_A:]], axis=0).astype(BF16)
    ga = g_out_a[l][perm].reshape(1, WIDTH_A)
    return w_std, w_t, wuqT, wkm, wvmT, wo, ga, w_gu[l].astype(BF16), w_down[l].astype(BF16)


def kernel(x, c, positions, g_norm1, g_norm2, w_ada, b_ada, w_in, cmp_pos, w_cmp_k, w_cmp_v, g_out_a, g_q_lora, w_uq, g_kv_lora, w_ukv, g_out_b, conv_w, conv_b, w_q_m, w_k_m, b_igate, b_fgate, g_mh, skip_m, w_out, w_gu, w_down, g_final):
    B, S, D = x.shape
    T = B * S
    x2 = x.reshape(T, D)
    tabs = _rope_tables(positions)
    cn, sn, cm, sm = tabs
    tabs_t = (cn[:, :NSA_ROT_HALF].T, sn[:, :NSA_ROT_HALF].T,
              cm[:, MLA_NOPE:MLA_NOPE + MLA_ROPE // 2].T, sm[:, MLA_NOPE:MLA_NOPE + MLA_ROPE // 2].T)
    mod = _ada(c, w_ada, b_ada)
    row = lambda v: v.reshape(1, -1)
    for l in range(DEPTH):
        w_std, w_t, wuqT, wkm, wvmT, wo, ga, wgu, wd = _layer_weights(l, w_in, w_uq, w_ukv, w_out, w_gu, w_down, g_out_a)
        m6 = [mod[l, :, D * i:D * (i + 1)].reshape(B, 1, D) for i in range(6)]
        shift1, scale1, gate1, shift2, scale2, gate2 = m6
        (qaT, gT, ck, cv, sk_aug, wk, svT, wvT, qmT, km, vmT, cx, mv, mo, cif) = _inproj(
            x2, shift1, scale1, row(g_norm1[l]), w_std, w_t, tabs, tabs_t, row(g_q_lora[l]), wuqT,
            row(g_kv_lora[l]), wkm, wvmT, S)
        kc, vcT = _compress(ck, cv, cmp_pos[l], w_cmp_k[l], w_cmp_v[l], B, S)
        ya = _nsa(qaT, gT, kc, vcT, sk_aug, svT, wk, wvT, B, S)
        yb = _mla(qmT, km, vmT, B, S)
        yc = _mlstm(cx, mv, mo, cif, conv_w[l], conv_b[l], w_q_m[l], w_k_m[l], b_igate[l], b_fgate[l],
                    g_mh[l], skip_m[l], B, S)
        x2 = _outproj(ya, yb, yc, x2, gate1, ga, row(g_out_b[l]), wo, S)
        x2 = _ffn(x2, shift2, scale2, gate2, row(g_norm2[l]), wgu, wd, row(g_final), S, final_norm=(l == DEPTH - 1))
    return x2.reshape(B, S, D)
```

```python
import functools

import numpy as np
import jax
import jax.numpy as jnp
from jax import lax
from jax.experimental import pallas as pl
from jax.experimental.pallas import tpu as pltpu

F32 = jnp.float32
BF16 = jnp.bfloat16

D_MODEL = 1024
DEPTH = 2
HEAD_DIM = 64
ROPE_THETA = 500000.0
NSA_ROT_HALF = HEAD_DIM // 8
NORM_EPS = 1e-6

NSA_HEADS = 6
NSA_KV_HEADS = 2
NSA_REP = NSA_HEADS // NSA_KV_HEADS
CMP_LEN = 32
CMP_STRIDE = 16
SLC_LEN = 64
SLC_TOPK = 16
WINDOW = 512

MLA_HEADS = 6
MLA_Q_LORA = 256
MLA_KV_LORA = 128
MLA_NOPE = 64
MLA_ROPE = 32
MLA_V = 64

MLSTM_HEADS = 4
MLSTM_DIM = 64
MLSTM_CONV = 4

WIDTH_A = NSA_HEADS * HEAD_DIM
WIDTH_B = MLA_HEADS * MLA_V
WIDTH_C = MLSTM_HEADS * MLSTM_DIM
FFN_HIDDEN = 2816
N_GATES = 3 * NSA_HEADS
GATE_ROWS = 24

LANES = 128
NEG = -1e30
LOG2E = 1.4426950408889634
VMEM_LIMIT = 56 * 1024 * 1024

TM_PROJ = 512
TQ_NSA = 256
TK_SLC = 256
CHAIN_LANES = 256
TQ_MLA = 256
FFN_CHUNK = 256
MLSTM_TILE = 128
MLSTM_GROUP = 8
MXU_LOOKAHEAD = 4

SEG_CKV, SEG_SK, SEG_WK, SEG_QL, SEG_KVL, SEG_KR, SEG_CX, SEG_MV, SEG_MO = (
    0, 256, 384, 512, 768, 896, 1024, 1280, 1536)
N_STD = 1792
TSEG_Q, TSEG_G, TSEG_SV, TSEG_WV, TSEG_IF = 0, 384, 416, 544, 672
TSEG_G_ROWS = 32
TSEG_IF_ROWS = 16
N_T = 688


def _params(*sem):
    return pltpu.CompilerParams(dimension_semantics=sem, vmem_limit_bytes=VMEM_LIMIT)


def _dot(a, b):
    return jnp.dot(a, b, preferred_element_type=F32)


def _dot_nt(a, b):
    return lax.dot_general(a, b, (((1,), (1,)), ((), ())), preferred_element_type=F32)


def _dot_tn(a, b):
    return lax.dot_general(a, b, (((0,), (0,)), ((), ())), preferred_element_type=F32)


def _split3(x):
    hi = x.astype(BF16)
    r1 = x - hi.astype(F32)
    mid = r1.astype(BF16)
    lo = (r1 - mid.astype(F32)).astype(BF16)
    return hi, mid, lo


def _rms(x, g):
    return x * lax.rsqrt(jnp.mean(x * x, axis=-1, keepdims=True) + NORM_EPS) * g


def _sigmoid(x):
    return 1.0 / (1.0 + jnp.exp(-x))


def _silu(x):
    return x * _sigmoid(x)


def _rope(x, cos, sin, half, x1_lane):
    xr = jnp.where(x1_lane, -pltpu.roll(x, LANES - half, 1), pltpu.roll(x, half, 1))
    return x * cos + xr * sin


def _ada_kernel(c_ref, w_ref, b_ref, o_ref):
    c = c_ref[...]
    ca = _silu(c).astype(BF16)
    o_ref[0] = _dot(ca, w_ref[0].astype(BF16)) + b_ref[0]


def _ada(c, w_ada, b_ada):
    L, D, N = w_ada.shape
    B = c.shape[0]
    tn = 1536
    return pl.pallas_call(
        _ada_kernel,
        grid=(L, N // tn),
        in_specs=[pl.BlockSpec((B, D), lambda l, j: (0, 0)),
                  pl.BlockSpec((1, D, tn), lambda l, j: (l, 0, j)),
                  pl.BlockSpec((1, 1, tn), lambda l, j: (l, 0, j))],
        out_specs=pl.BlockSpec((1, B, tn), lambda l, j: (l, 0, j)),
        out_shape=jax.ShapeDtypeStruct((L, B, N), F32),
        compiler_params=_params("parallel", "parallel"),
        name="ada_mod",
    )(c, w_ada, b_ada.reshape(L, 1, N))


ROPE_FREQS = NSA_ROT_HALF + MLA_ROPE // 2


def _rope_kernel(pos_ref, inv_ref, spread_ref, one_ref, cn_ref, sn_ref, cm_ref, sm_ref):
    ang = pos_ref[...] * inv_ref[...]
    lane = lax.broadcasted_iota(jnp.int32, (1, LANES), 1)

    def spread(val):
        hi, mid, lo = _split3(val)
        packed = jnp.where(lane < ROPE_FREQS, hi, jnp.where(lane < 2 * ROPE_FREQS, mid, lo))
        return _dot(packed, spread_ref[...])

    c = spread(jnp.cos(ang)) + one_ref[...]
    s = spread(jnp.sin(ang))
    cn_ref[...] = c[:, :LANES]
    cm_ref[...] = c[:, LANES:]
    sn_ref[...] = s[:, :LANES]
    sm_ref[...] = s[:, LANES:]


def _rope_tables(positions):
    T = positions.size
    inv_n = jnp.power(ROPE_THETA, -jnp.arange(0, 2 * NSA_ROT_HALF, 2, dtype=F32) / (2 * NSA_ROT_HALF))
    inv_m = jnp.power(ROPE_THETA, -jnp.arange(0, MLA_ROPE, 2, dtype=F32) / MLA_ROPE)
    copies = 3
    inv = jnp.concatenate([inv_n, inv_m] * copies + [jnp.zeros((LANES - copies * ROPE_FREQS,), F32)]).reshape(1, LANES)
    onehot = np.zeros((LANES, 2 * LANES), np.float32)
    rotated = np.zeros((1, 2 * LANES), np.float32)
    for k in range(copies):
        for f in range(NSA_ROT_HALF):
            for lane in (f, f + NSA_ROT_HALF, HEAD_DIM + f, HEAD_DIM + f + NSA_ROT_HALF):
                onehot[ROPE_FREQS * k + f, lane] = 1.0
                rotated[0, lane] = 1.0
        for f in range(MLA_ROPE // 2):
            for lane in (MLA_NOPE + f, MLA_NOPE + MLA_ROPE // 2 + f):
                onehot[ROPE_FREQS * k + NSA_ROT_HALF + f, LANES + lane] = 1.0
                rotated[0, LANES + lane] = 1.0
    posb = jnp.broadcast_to(positions.reshape(T, 1).astype(F32), (T, LANES))
    tm = 2048
    spec = pl.BlockSpec((tm, LANES), lambda i: (i, 0))
    full = lambda shape: pl.BlockSpec(shape, lambda i: (0, 0))
    return pl.pallas_call(
        _rope_kernel,
        grid=(T // tm,),
        in_specs=[spec, full((1, LANES)), full((LANES, 2 * LANES)), full((1, 2 * LANES))],
        out_specs=[spec] * 4,
        out_shape=[jax.ShapeDtypeStruct((T, LANES), F32)] * 4,
        compiler_params=_params("parallel"),
        name="rope_tables",
    )(posb, inv, jnp.asarray(onehot, BF16), jnp.asarray(1.0 - rotated))


def _rope_rows(t, offset, half, cos, sin):
    x1, x2 = t[offset:offset + half], t[offset + half:offset + 2 * half]
    return x1 * cos - x2 * sin, x2 * cos + x1 * sin


def _inproj_kernel(x_ref, sh_ref, sc_ref, g1_ref, ws_ref, wt_ref, cn_ref, sn_ref, cm_ref, sm_ref,
                   cnT_ref, snT_ref, cmT_ref, smT_ref, gq_ref, wuqT_ref, gkv_ref, wkm_ref, wvmT_ref,
                   qaT_ref, gT_ref, ck_ref, cv_ref, ska_ref, wk_ref, svT_ref, wvT_ref,
                   qmT_ref, km_ref, vmT_ref, cx_ref, mv_ref, mo_ref, gif_ref, *, per_b):
    tm = x_ref.shape[0]
    x = x_ref[...]
    h = _rms(x, g1_ref[...]) * (1.0 + sc_ref[0]) + sh_ref[0]
    hb = h.astype(BF16)

    def seg(start, width):
        return _dot(hb, ws_ref[:, start:start + width])

    def seg_t(start, height):
        return _dot_nt(wt_ref[start:start + height, :], hb)

    lane = lax.broadcasted_iota(jnp.int32, (1, LANES), 1)
    x1_n = (lane % HEAD_DIM) < NSA_ROT_HALF
    x1_m = lane < MLA_NOPE + MLA_ROPE // 2
    cn, sn, cm, sm = cn_ref[...], sn_ref[...], cm_ref[...], sm_ref[...]
    rope_n = lambda t: _rope(t, cn, sn, NSA_ROT_HALF, x1_n)
    rope_m = lambda t: _rope(t, cm, sm, MLA_ROPE // 2, x1_m)

    qn = _rms(seg(SEG_QL, MLA_Q_LORA), gq_ref[...]).astype(BF16)
    kvl_kr = seg(SEG_KVL, 2 * LANES)
    kvn = _rms(kvl_kr[:, :LANES], gkv_ref[...]).astype(BF16)
    kr = rope_m(kvl_kr[:, LANES:])

    qt = seg_t(TSEG_Q, WIDTH_A)
    cnt, snt = cnT_ref[...], snT_ref[...]
    parts = []
    for hd in range(NSA_HEADS):
        o = HEAD_DIM * hd
        parts += list(_rope_rows(qt, o, NSA_ROT_HALF, cnt, snt)) + [qt[o + 2 * NSA_ROT_HALF:o + HEAD_DIM]]
    qaT_ref[...] = (jnp.concatenate(parts, axis=0) * (HEAD_DIM ** -0.5 * LOG2E)).astype(BF16)
    gT_ref[...] = _sigmoid(seg_t(TSEG_G, TSEG_G_ROWS))[:GATE_ROWS]
    svT_ref[...] = seg_t(TSEG_SV, LANES).astype(BF16)
    wvT_ref[...] = seg_t(TSEG_WV, LANES).astype(BF16)

    ckv = seg(SEG_CKV, 2 * LANES)
    ck_ref[...] = rope_n(ckv[:, :LANES])
    cv_ref[...] = ckv[:, LANES:]
    ska_ref[:, :LANES] = rope_n(seg(SEG_SK, LANES)).astype(BF16)
    srow = (pl.program_id(0) % per_b) * tm + lax.broadcasted_iota(jnp.int32, (tm, LANES), 0)
    lane2 = lax.broadcasted_iota(jnp.int32, (tm, LANES), 1)
    code = (lane2 < 4 * SLC_TOPK) & ((lane2 & (2 * SLC_TOPK - 1)) == srow // SLC_LEN)
    ska_ref[:, LANES:] = jnp.where(code, 1.0, 0.0).astype(BF16)
    wk_ref[...] = rope_n(seg(SEG_WK, LANES)).astype(BF16)

    cx_ref[...] = seg(SEG_CX, WIDTH_C)
    mv_ref[...] = seg(SEG_MV, WIDTH_C).astype(BF16)
    mo_ref[...] = seg(SEG_MO, WIDTH_C)
    gif_ref[...] = seg_t(TSEG_IF, TSEG_IF_ROWS)[:2 * MLSTM_HEADS]

    qmt = _dot_nt(wuqT_ref[...], qn)
    cmt, smt = cmT_ref[...], smT_ref[...]
    parts = []
    for hd in range(MLA_HEADS):
        o = LANES * hd
        parts += [qmt[o:o + MLA_NOPE]] + list(_rope_rows(qmt, o + MLA_NOPE, MLA_ROPE // 2, cmt, smt))
        parts += [qmt[o + MLA_NOPE + MLA_ROPE:o + LANES]]
    qmT_ref[...] = (jnp.concatenate(parts, axis=0) * ((MLA_NOPE + MLA_ROPE) ** -0.5 * LOG2E)).astype(BF16)
    kk = _dot(kvn, wkm_ref[...])
    for hd in range(MLA_HEADS):
        km_ref[:, LANES * hd:LANES * (hd + 1)] = (kk[:, LANES * hd:LANES * (hd + 1)] + kr).astype(BF16)
    vmT_ref[...] = _dot_nt(wvmT_ref[...], kvn).astype(BF16)


def _inproj(x2, shift, scale, g1, w_std, w_t, tabs, tabs_t, gq, wuqT, gkv, wkm, wvmT, S):
    T = x2.shape[0]
    tm = TM_PROJ
    per_b = S // tm
    row = lambda w: pl.BlockSpec((tm, w), lambda i: (i, 0))
    col = lambda h: pl.BlockSpec((h, tm), lambda i: (0, i))
    full = lambda a: pl.BlockSpec(a.shape, lambda i: (0,) * a.ndim)
    modspec = pl.BlockSpec((1, 1, D_MODEL), lambda i: (i // per_b, 0, 0))
    outs = [(WIDTH_A, BF16, True), (GATE_ROWS, F32, True), (LANES, F32, False), (LANES, F32, False), (2 * LANES, BF16, False),
            (LANES, BF16, False), (LANES, BF16, True), (LANES, BF16, True),
            (MLA_HEADS * LANES, BF16, True), (MLA_HEADS * LANES, BF16, False), (WIDTH_B, BF16, True),
            (WIDTH_C, F32, False), (WIDTH_C, BF16, False), (WIDTH_C, F32, False), (2 * MLSTM_HEADS, F32, True)]
    return pl.pallas_call(
        functools.partial(_inproj_kernel, per_b=per_b),
        grid=(T // tm,),
        in_specs=[row(D_MODEL), modspec, modspec, full(g1), full(w_std), full(w_t)] + [row(LANES)] * 4
                 + [col(t.shape[0]) for t in tabs_t] + [full(gq), full(wuqT), full(gkv), full(wkm), full(wvmT)],
        out_specs=[col(w) if tr else row(w) for w, _, tr in outs],
        out_shape=[jax.ShapeDtypeStruct((w, T) if tr else (T, w), dt) for w, dt, tr in outs],
        compiler_params=_params("parallel"),
        name="in_proj",
    )(x2, shift, scale, g1, w_std, w_t, *tabs, *tabs_t, gq, wuqT, gkv, wkm, wvmT)


def _compress_kernel(xk_ref, xv_ref, wk_ref, wv_ref, pos_ref, wkf_ref, wvf_ref, kc_ref, vcT_ref):
    ng = xk_ref.shape[0] // CMP_STRIDE
    row = lax.broadcasted_iota(jnp.int32, (ng, LANES), 0)
    pos = pos_ref[...].astype(BF16)
    acc_k = jnp.zeros((ng, 2 * LANES), F32)
    acc_v = jnp.zeros((ng, 2 * LANES), F32)
    for t in range(CMP_STRIDE):
        tok = pl.ds(t, ng, stride=CMP_STRIDE)
        acc_k = acc_k + _dot(xk_ref[tok, :].astype(BF16), wk_ref[t])
        acc_v = acc_v + _dot(xv_ref[tok, :].astype(BF16), wv_ref[t])

    def finish(acc, wf_ref):
        const = _dot(pos, wf_ref[...].astype(BF16))
        both = acc[:, :LANES] + pltpu.roll(acc[:, LANES:], ng - 1, 0) + jnp.concatenate([const, const], axis=1)
        return jnp.where(row < ng - 1, both, 0.0)

    kc_ref[0] = finish(acc_k, wkf_ref).astype(BF16)
    vcT_ref[0] = finish(acc_v, wvf_ref).T.astype(BF16)


def _compress(ck, cv, cmp_pos, w_cmp_k, w_cmp_v, B, S):
    ng = S // CMP_STRIDE

    def per_token(w):
        a = w[:CMP_STRIDE * HEAD_DIM].reshape(CMP_STRIDE, HEAD_DIM, HEAD_DIM)
        b = w[CMP_STRIDE * HEAD_DIM:].reshape(CMP_STRIDE, HEAD_DIM, HEAD_DIM)
        z = jnp.zeros_like(a)
        top = jnp.concatenate([a, z, b, z], axis=2)
        bot = jnp.concatenate([z, a, z, b], axis=2)
        return jnp.concatenate([top, bot], axis=1).astype(BF16)

    full = lambda a: pl.BlockSpec(a.shape, lambda b: (0,) * a.ndim)
    wk3, wv3 = per_token(w_cmp_k), per_token(w_cmp_v)
    posf = cmp_pos.reshape(1, CMP_LEN * HEAD_DIM)
    ospec = pl.BlockSpec((1, ng, LANES), lambda b: (b, 0, 0))
    return pl.pallas_call(
        _compress_kernel,
        grid=(B,),
        in_specs=[pl.BlockSpec((S, LANES), lambda b: (b, 0)), pl.BlockSpec((S, LANES), lambda b: (b, 0)),
                  full(wk3), full(wv3), full(posf), full(w_cmp_k), full(w_cmp_v)],
        out_specs=[ospec, ospec],
        out_shape=[jax.ShapeDtypeStruct((B, ng, LANES), BF16)] * 2,
        compiler_params=_params("parallel"),
        name="nsa_compress",
    )(ck, cv, wk3, wv3, posf, w_cmp_k, w_cmp_v)


def _nsa_kernel(qT_ref, gT_ref, kc_ref, vcT_ref, sk_ref, svT_ref, wk_ref, wvT_ref, o_ref, qaug_scr, acc_scr, s_scr):
    tq = TQ_NSA
    cols = NSA_HEADS * tq
    pair = CHAIN_LANES
    t0 = pl.program_id(1) * tq
    n_slc = SLC_TOPK * 2

    frow = lax.broadcasted_iota(jnp.int32, (LANES, tq), 0)
    g0_row = frow < HEAD_DIM
    tiles = [qT_ref[LANES * r:LANES * (r + 1), :] for r in range(NSA_REP)]
    zero = jnp.zeros_like(tiles[0])
    q6 = jnp.concatenate([jnp.where(g0_row, t, zero) for t in tiles]
                         + [jnp.where(g0_row, zero, t) for t in tiles], axis=1)
    qaug_scr[0:LANES, :] = q6
    tq_l = t0 + (lax.broadcasted_iota(jnp.int32, (1, cols), 1) & (tq - 1))

    pairs = [slice(pair * pp, pair * (pp + 1)) for pp in range(cols // pair)]
    s = _dot(kc_ref[0], q6)
    span = WINDOW + tq
    start = pl.multiple_of(jnp.maximum(t0 - WINDOW, 0), tq)
    kw = wk_ref[pl.ds(start, span), :]
    win_scores = [_dot(kw, q6[:, sl]) for sl in pairs]

    tq_1 = tq_l[:, :tq]
    nrow = lax.broadcasted_iota(jnp.int32, (LANES, tq), 0)
    cmp_bias = jnp.where(nrow * CMP_STRIDE + (CMP_LEN - 1) <= tq_1, 0.0, NEG)
    s = s + jnp.concatenate([cmp_bias] * NSA_HEADS, axis=1)
    e = jnp.exp2(s - jnp.max(s, axis=0, keepdims=True))
    seen = jnp.where(tq_l >= CMP_LEN - 1, 1.0, 0.0)
    p = e * (seen / jnp.sum(e, axis=0, keepdims=True))
    o_cmp = _dot(vcT_ref[0], p.astype(BF16))

    jr = lax.broadcasted_iota(jnp.int32, (n_slc, LANES), 0)
    nc = lax.broadcasted_iota(jnp.int32, (n_slc, LANES), 1)
    ovl = ((nc * CMP_STRIDE < jr * SLC_LEN + SLC_LEN) & (nc * CMP_STRIDE + CMP_LEN > jr * SLC_LEN)
           & (nc < LANES - 1))
    ovl = jnp.where(ovl, 1.0, 0.0).astype(BF16)
    jq = lax.broadcasted_iota(jnp.int32, (n_slc, tq), 0)
    tl = t0 + lax.broadcasted_iota(jnp.int32, (n_slc, tq), 1)
    cur = tl // SLC_LEN
    forced = (jq == 0) | (jq == cur) | (jq == cur - 1)
    future = jq * SLC_LEN > tl
    bias_t = []
    for g in range(NSA_KV_HEADS):
        pg = p[:, (3 * g) * tq:(3 * g + 1) * tq] + p[:, (3 * g + 1) * tq:(3 * g + 2) * tq] + p[:, (3 * g + 2) * tq:(3 * g + 3) * tq]
        imp = sum(_dot(ovl, part) for part in _split3(pg))
        imp = jnp.where(forced, jnp.inf, imp)
        imp = jnp.where(future, -jnp.inf, imp)
        rank = jnp.zeros((n_slc, tq), F32)
        for jp in range(n_slc):
            rv = imp[jp:jp + 1, :]
            ahead = jnp.where(rv > imp, 1.0, jnp.where((rv == imp) & (jq > jp), 1.0, 0.0))
            rank = rank + ahead
        bias_t.append(jnp.where(rank < float(SLC_TOPK), 0.0, NEG).astype(BF16))
    zb = jnp.zeros((n_slc, NSA_REP * tq), BF16)
    qaug_scr[LANES:LANES + n_slc, :] = jnp.concatenate([bias_t[0]] * NSA_REP + [zb], axis=1)
    qaug_scr[LANES + n_slc:LANES + 2 * n_slc, :] = jnp.concatenate([zb] + [bias_t[1]] * NSA_REP, axis=1)
    qaug_scr[LANES + 2 * n_slc:, :] = jnp.zeros((LANES - 2 * n_slc, cols), BF16)

    vwt = wvT_ref[:, pl.ds(start, span)]
    wrow = start + lax.broadcasted_iota(jnp.int32, (span, pair), 0)
    o_win = []
    for sl, sc in zip(pairs, win_scores):
        in_window = lax.bitcast_convert_type(tq_l[:, sl] - wrow, jnp.uint32) < jnp.uint32(WINDOW)
        sc = jnp.where(in_window, sc, NEG)
        ew = jnp.exp2(sc - jnp.max(sc, axis=0, keepdims=True))
        o_win.append(_dot(vwt, ew.astype(BF16)) / jnp.sum(ew, axis=0, keepdims=True))
    o_win = jnp.concatenate(o_win, axis=1)

    acc_scr[...] = jnp.zeros((LANES, cols), F32)
    krow = lax.broadcasted_iota(jnp.int32, (TK_SLC, pair), 0)

    def slc_scores(k0, sl):
        return _dot(sk_ref[pl.ds(k0, TK_SLC), :], qaug_scr[:, sl])

    nch = len(pairs)
    la = s_scr.shape[0]
    for pp in range(la):
        s_scr[pp] = slc_scores(0, pairs[pp])

    def slc_tile(k0, m, l, masked):
        vt = svT_ref[:, pl.ds(k0, TK_SLC)]
        ms, ls, accs = [], [], []
        ahead = [s_scr[pp] for pp in range(la)]
        for pp, sl in enumerate(pairs):
            sc = ahead.pop(0)
            nxt = pp + la
            if nxt < nch:
                ahead.append(slc_scores(k0, pairs[nxt]))
            elif not masked:
                s_scr[nxt - nch] = slc_scores(pl.multiple_of(k0 + TK_SLC, TK_SLC), pairs[nxt - nch])
            if masked:
                sc = jnp.where(k0 + krow <= tq_l[:, sl], sc, NEG)
            m_new = jnp.maximum(m[:, sl], jnp.max(sc, axis=0, keepdims=True))
            alpha = jnp.exp2(m[:, sl] - m_new)
            pe = jnp.exp2(sc - m_new)
            ls.append(alpha * l[:, sl] + jnp.sum(pe, axis=0, keepdims=True))
            accs.append(alpha * acc_scr[:, sl] + _dot(vt, pe.astype(BF16)))
            ms.append(m_new)
        acc_scr[...] = jnp.concatenate(accs, axis=1)
        return jnp.concatenate(ms, axis=1), jnp.concatenate(ls, axis=1)

    def slc_step(kt, carry):
        return slc_tile(pl.multiple_of(kt * TK_SLC, TK_SLC), carry[0], carry[1], False)

    n_full = t0 // TK_SLC
    m, l = lax.fori_loop(0, n_full, slc_step, (jnp.full((1, cols), NEG, F32), jnp.zeros((1, cols), F32)))
    m, l = slc_tile(pl.multiple_of(n_full * TK_SLC, TK_SLC), m, l, True)
    o_slc = acc_scr[...] / l

    gt = gT_ref[...]
    grow = lambda j: jnp.concatenate([gt[3 * hd + j:3 * hd + j + 1, :] for hd in range(NSA_HEADS)], axis=1)
    mixed = grow(0) * o_cmp + grow(1) * o_slc + grow(2) * o_win
    for r in range(NSA_REP):
        t = jnp.where(g0_row, mixed[:, tq * r:tq * (r + 1)], mixed[:, tq * (NSA_REP + r):tq * (NSA_REP + r + 1)])
        o_ref[:, LANES * r:LANES * (r + 1)] = t.T


def _nsa(qaT, gT, kc, vcT, sk_aug, svT, wk, wvT, B, S):
    tq = TQ_NSA
    nq = S // tq
    cols = NSA_HEADS * tq
    qcol = lambda h: pl.BlockSpec((h, tq), lambda b, i: (0, b * nq + i))
    seq = lambda w: pl.BlockSpec((S, w), lambda b, i: (b, 0))
    seqT = pl.BlockSpec((LANES, S), lambda b, i: (0, b))
    cspec = pl.BlockSpec((1, LANES, LANES), lambda b, i: (b, 0, 0))
    return pl.pallas_call(
        _nsa_kernel,
        grid=(B, nq),
        in_specs=[qcol(WIDTH_A), qcol(gT.shape[0]), cspec, cspec, seq(2 * LANES), seqT, seq(LANES), seqT],
        out_specs=pl.BlockSpec((tq, WIDTH_A), lambda b, i: (b * nq + i, 0)),
        out_shape=jax.ShapeDtypeStruct((B * S, WIDTH_A), F32),
        scratch_shapes=[pltpu.VMEM((2 * LANES, cols), BF16), pltpu.VMEM((LANES, cols), F32),
                        pltpu.VMEM((cols // CHAIN_LANES, TK_SLC, CHAIN_LANES), F32)],
        compiler_params=_params("parallel", "arbitrary"),
        name="nsa_attention",
    )(qaT, gT, kc, vcT, sk_aug, svT, wk, wvT)


def _mla_kernel(qT_ref, k_ref, vT_ref, o_ref, acc_scr, s_scr):
    tq = TQ_MLA
    t0 = pl.program_id(1) * tq
    tq_l = t0 + lax.broadcasted_iota(jnp.int32, (1, tq), 1)
    krow = lax.broadcasted_iota(jnp.int32, (tq, tq), 0)
    acc_scr[...] = jnp.zeros((MLA_HEADS, LANES, tq), F32)

    def scores(k0, hd):
        k = k_ref[pl.ds(k0, tq), LANES * hd:LANES * (hd + 1)]
        return _dot(k, qT_ref[LANES * hd:LANES * (hd + 1), :])

    for hd in range(MXU_LOOKAHEAD):
        s_scr[hd] = scores(0, hd)

    def tile(k0, ms, ls, masked):
        new_m, new_l, accs = [], [], []
        ahead = [s_scr[hd] for hd in range(MXU_LOOKAHEAD)]
        for hd in range(MLA_HEADS):
            sc = ahead.pop(0)
            nxt = hd + MXU_LOOKAHEAD
            if nxt < MLA_HEADS:
                ahead.append(scores(k0, nxt))
            elif not masked:
                s_scr[nxt - MLA_HEADS] = scores(pl.multiple_of(k0 + tq, tq), nxt - MLA_HEADS)
            if masked:
                sc = jnp.where(k0 + krow <= tq_l, sc, NEG)
            m_new = jnp.maximum(ms[hd], jnp.max(sc, axis=0, keepdims=True))
            alpha = jnp.exp2(ms[hd] - m_new)
            pe = jnp.exp2(sc - m_new)
            new_l.append(alpha * ls[hd] + jnp.sum(pe, axis=0, keepdims=True))
            vt = vT_ref[LANES * (hd // 2):LANES * (hd // 2 + 1), pl.ds(k0, tq)]
            accs.append(alpha * acc_scr[hd] + _dot(vt, pe.astype(BF16)))
            new_m.append(m_new)
        acc_scr[...] = jnp.stack(accs)
        return tuple(new_m), tuple(new_l)

    def step(kt, carry):
        return tile(pl.multiple_of(kt * tq, tq), carry[0], carry[1], False)

    n_full = pl.program_id(1)
    init = (tuple(jnp.full((1, tq), NEG, F32) for _ in range(MLA_HEADS)),
            tuple(jnp.zeros((1, tq), F32) for _ in range(MLA_HEADS)))
    ms, ls = lax.fori_loop(0, n_full, step, init)
    ms, ls = tile(pl.multiple_of(n_full * tq, tq), ms, ls, True)
    frow = lax.broadcasted_iota(jnp.int32, (LANES, tq), 0)
    for pr in range(MLA_HEADS // 2):
        t = jnp.where(frow < MLA_V, acc_scr[2 * pr] / ls[2 * pr], acc_scr[2 * pr + 1] / ls[2 * pr + 1])
        o_ref[:, LANES * pr:LANES * (pr + 1)] = t.T


def _mla(qmT, km, vmT, B, S):
    tq = TQ_MLA
    nq = S // tq
    return pl.pallas_call(
        _mla_kernel,
        grid=(B, nq),
        in_specs=[pl.BlockSpec((MLA_HEADS * LANES, tq), lambda b, i: (0, b * nq + i)),
                  pl.BlockSpec((S, MLA_HEADS * LANES), lambda b, i: (b, 0)),
                  pl.BlockSpec((WIDTH_B, S), lambda b, i: (0, b))],
        out_specs=pl.BlockSpec((tq, WIDTH_B), lambda b, i: (b * nq + i, 0)),
        out_shape=jax.ShapeDtypeStruct((B * S, WIDTH_B), F32),
        scratch_shapes=[pltpu.VMEM((MLA_HEADS, LANES, tq), F32), pltpu.VMEM((MXU_LOOKAHEAD, tq, tq), F32)],
        compiler_params=_params("parallel", "arbitrary"),
        name="mla_attention",
    )(qmT, km, vmT)


def _log_sigmoid(z):
    return jnp.minimum(z, 0.0) - jnp.log1p(jnp.exp(-jnp.abs(z)))


def _mlstm_kernel(cx_ref, v_ref, o_ref, g_ref, cw_ref, cb_ref, wq_ref, wk_ref, br_ref, gmh_ref, skip_ref, y_ref,
                  xc_scr, q_scr, k_scr, w_scr, inter_scr, floor_scr, mfull_scr, ut_scr, eo_scr, el_scr, cprev_scr):
    S = cx_ref.shape[0]
    L = MLSTM_TILE
    NC = S // L
    d = MLSTM_DIM
    pairs = MLSTM_HEADS // 2
    group = MLSTM_GROUP

    x = cx_ref[...]
    rowi = lax.broadcasted_iota(jnp.int32, (S, WIDTH_C), 0)
    conv = x * cw_ref[MLSTM_CONV - 1:MLSTM_CONV, :]
    for back in range(1, MLSTM_CONV):
        shifted = jnp.where(rowi >= back, pltpu.roll(x, back, 0), 0.0)
        conv = conv + shifted * cw_ref[MLSTM_CONV - 1 - back:MLSTM_CONV - back, :]
    xc = _silu(conv + cb_ref[...])
    xc_scr[...] = xc
    xcb = xc.astype(BF16)
    q_scr[...] = _dot(xcb, wq_ref[...]).astype(BF16)
    k_scr[...] = _dot(xcb, wk_ref[...]).astype(BF16)

    nh = MLSTM_HEADS
    lane_in_chunk = lax.broadcasted_iota(jnp.int32, (nh, S), 1) & (L - 1)

    def scan_lanes(val, op, fill):
        sh = 1
        while sh < L:
            val = op(val, jnp.where(lane_in_chunk >= sh, pltpu.roll(val, sh, 1), fill))
            sh *= 2
        return val

    gt = g_ref[...] + br_ref[...]
    ig = gt[0:nh]
    b = scan_lanes(_log_sigmoid(gt[nh:]), jnp.add, 0.0)
    u = ig - b
    cmu = scan_lanes(u, jnp.maximum, -jnp.inf)
    ut_scr[0:nh, :] = u

    m = jnp.zeros((nh, 1), F32)
    w_loc, inter, floor, m_tok = [], [], [], []
    for c in range(NC):
        blk = slice(L * c, L * (c + 1))
        b_last, u_max = b[:, L * (c + 1) - 1:L * (c + 1)], cmu[:, L * (c + 1) - 1:L * (c + 1)]
        m_top = jnp.maximum(m, u_max)
        eo_scr[c] = jnp.broadcast_to(jnp.exp(m - m_top), (nh, LANES))
        el_scr[c] = jnp.broadcast_to(jnp.exp(u_max - m_top), (nh, LANES))
        mt = jnp.maximum(cmu[:, blk], m)
        m_tok.append(mt)
        inter.append(jnp.exp(m - mt))
        w_loc.append(jnp.exp(u[:, blk] - u_max))
        floor.append(jnp.exp(-(b[:, blk] + mt)))
        m = b_last + m_top

    hrow = lax.broadcasted_iota(jnp.int32, (4 * nh, 2 * LANES), 0)
    hcol = lax.broadcasted_iota(jnp.int32, (4 * nh, 2 * LANES), 1)
    to_heads = jnp.where((hrow < 3 * nh) & (hcol // d == hrow % nh), 1.0, 0.0).astype(BF16)
    frow = lax.broadcasted_iota(jnp.int32, (4 * nh, 4 * LANES), 0)
    fcol = lax.broadcasted_iota(jnp.int32, (4 * nh, 4 * LANES), 1)
    to_full = jnp.where((frow < 3 * nh) & (fcol // LANES == frow % nh), 1.0, 0.0).astype(BF16)

    def spread(chunks, onehot):
        val = jnp.concatenate(chunks, axis=1)
        hi, mid, lo = _split3(val)
        pieces = jnp.concatenate([hi.astype(F32), mid.astype(F32), lo.astype(F32), jnp.zeros_like(val)], axis=0)
        return _dot_tn(pieces.astype(BF16), onehot)

    w_scr[...] = spread(w_loc, to_heads)
    inter_scr[...] = spread(inter, to_heads)
    floor_scr[...] = spread(floor, to_heads)
    mfull_scr[...] = spread(m_tok, to_full)

    arow = lax.broadcasted_iota(jnp.int32, (LANES, 2 * LANES), 0)
    acol = lax.broadcasted_iota(jnp.int32, (LANES, 2 * LANES), 1)
    blk2 = (arow // d) == ((acol & (LANES - 1)) // d)
    ones_v = jnp.ones((L, LANES), BF16)

    def head_rows(ref, c, pr):
        top = jnp.broadcast_to(ref[c, 2 * pr:2 * pr + 1, :], (d, LANES))
        bot = jnp.broadcast_to(ref[c, 2 * pr + 1:2 * pr + 2, :], (d, LANES))
        half = jnp.concatenate([top, bot], axis=0)
        return jnp.concatenate([half, half], axis=1)

    def state_group(g, carry):
        local = []
        for cc in range(group):
            r0 = pl.multiple_of((g * group + cc) * L, L)
            for pr in range(pairs):
                ps = slice(LANES * pr, LANES * (pr + 1))
                kw = (k_scr[pl.ds(r0, L), ps].astype(F32) * w_scr[pl.ds(r0, L), ps]).astype(BF16)
                vo = jnp.concatenate([v_ref[pl.ds(r0, L), ps], ones_v], axis=1)
                local.append(jnp.where(blk2, _dot_tn(kw, vo), 0.0))
        state = list(carry)
        for cc in range(group):
            c = g * group + cc
            for pr in range(pairs):
                cprev_scr[c, pr] = state[pr].astype(BF16)
                state[pr] = head_rows(eo_scr, c, pr) * state[pr] + head_rows(el_scr, c, pr) * local[cc * pairs + pr]
        return tuple(state)

    lax.fori_loop(0, NC // group, state_group, tuple(jnp.zeros((LANES, 2 * LANES), F32) for _ in range(pairs)))

    li = lax.broadcasted_iota(jnp.int32, (L, L), 0)
    si = lax.broadcasted_iota(jnp.int32, (L, L), 1)
    causal = si <= li
    lane = lax.broadcasted_iota(jnp.int32, (L, LANES), 1)
    h0_lane = lane < d
    avg = jnp.where((li // d) == (si // d), 1.0 / d, 0.0).astype(BF16)

    def dot2(val, rhs):
        hi = val.astype(BF16)
        return _dot(hi, rhs) + _dot((val - hi.astype(F32)).astype(BF16), rhs)

    def out_group(g, carry):
        units = [(cc, pr) for cc in range(group) for pr in range(pairs)]
        chunk_of = {u_: g * group + u_[0] for u_ in units}
        rows = {u_: pl.multiple_of(chunk_of[u_] * L, L) for u_ in units}
        qc, sc, pv = {}, {}, {}
        for u_ in units:
            c, pr = chunk_of[u_], u_[1]
            ps = slice(LANES * pr, LANES * (pr + 1))
            qp = q_scr[pl.ds(rows[u_], L), ps]
            kp = k_scr[pl.ds(rows[u_], L), ps]
            qc[u_] = _dot(qp, cprev_scr[c, pr])
            sc[u_] = [_dot_nt(jnp.where(h0_lane if hh == 0 else lane >= d, qp, jnp.zeros_like(qp)), kp)
                      for hh in range(2)]
        for u_ in units:
            pr = u_[1]
            ps = slice(LANES * pr, LANES * (pr + 1))
            vo = jnp.concatenate([v_ref[pl.ds(rows[u_], L), ps], ones_v], axis=1)
            pv[u_] = []
            for hh in range(2):
                hd = 2 * pr + hh
                u_row = ut_scr[hd:hd + 1, pl.ds(rows[u_], L)]
                decay = jnp.where(causal, jnp.exp(u_row - mfull_scr[pl.ds(rows[u_], L), LANES * hd:LANES * (hd + 1)]), 0.0)
                pv[u_].append(_dot((sc[u_][hh] * decay).astype(BF16), vo))
        hg, cen = {}, {}
        for u_ in units:
            pr = u_[1]
            ps = slice(LANES * pr, LANES * (pr + 1))
            it = inter_scr[pl.ds(rows[u_], L), ps]
            num = it * qc[u_][:, :LANES] + jnp.where(h0_lane, pv[u_][0][:, :LANES], pv[u_][1][:, :LANES])
            den = it * qc[u_][:, LANES:] + jnp.where(h0_lane, pv[u_][0][:, LANES:], pv[u_][1][:, LANES:])
            hcell = num / jnp.maximum(jnp.abs(den), floor_scr[pl.ds(rows[u_], L), ps])
            hg[u_] = _sigmoid(o_ref[pl.ds(rows[u_], L), ps]) * hcell
        for u_ in units:
            cen[u_] = hg[u_] - dot2(hg[u_], avg)
        for u_ in units:
            pr = u_[1]
            ps = slice(LANES * pr, LANES * (pr + 1))
            var = dot2(cen[u_] * cen[u_], avg)
            y_ref[pl.ds(rows[u_], L), ps] = (cen[u_] * lax.rsqrt(var + NORM_EPS) * gmh_ref[:, ps]
                                             + skip_ref[:, ps] * xc_scr[pl.ds(rows[u_], L), ps])
        return carry

    lax.fori_loop(0, NC // group, out_group, 0)


def _mlstm(cx, mv, mo, gif_t, conv_w, conv_b, w_q_m, w_k_m, b_igate, b_fgate, g_mh, skip_m, B, S):
    def blockdiag(w, scale):
        out = jnp.zeros((WIDTH_C, WIDTH_C), F32)
        for hd in range(MLSTM_HEADS):
            out = out.at[hd * MLSTM_DIM:(hd + 1) * MLSTM_DIM, hd * MLSTM_DIM:(hd + 1) * MLSTM_DIM].set(w[hd] * scale)
        return out.astype(BF16)

    nc = S // MLSTM_TILE
    wq = blockdiag(w_q_m, MLSTM_DIM ** -0.5)
    wk = blockdiag(w_k_m, 1.0)
    bias8 = jnp.concatenate([b_igate, b_fgate])
    br = jnp.broadcast_to(bias8[:, None], (2 * MLSTM_HEADS, S))
    seq = lambda w: pl.BlockSpec((S, w), lambda b: (b, 0))
    full = lambda a: pl.BlockSpec(a.shape, lambda b: (0,) * a.ndim)
    row = lambda a: a.reshape(1, -1)
    args = [conv_w, row(conv_b), wq, wk, br, row(g_mh), row(skip_m)]
    tok = lambda w, dt: pltpu.VMEM((S, w), dt)
    per_chunk = pltpu.VMEM((nc, MLSTM_HEADS, LANES), F32)
    return pl.pallas_call(
        _mlstm_kernel,
        grid=(B,),
        in_specs=[seq(WIDTH_C), seq(WIDTH_C), seq(WIDTH_C),
                  pl.BlockSpec((2 * MLSTM_HEADS, S), lambda b: (0, b))] + [full(a) for a in args],
        out_specs=seq(WIDTH_C),
        out_shape=jax.ShapeDtypeStruct((B * S, WIDTH_C), F32),
        scratch_shapes=[tok(WIDTH_C, F32), tok(WIDTH_C, BF16), tok(WIDTH_C, BF16),
                        tok(WIDTH_C, F32), tok(WIDTH_C, F32), tok(WIDTH_C, F32), tok(4 * LANES, F32),
                        pltpu.VMEM((2 * MLSTM_HEADS, S), F32), per_chunk, per_chunk,
                        pltpu.VMEM((nc, MLSTM_HEADS // 2, LANES, 2 * LANES), BF16)],
        compiler_params=_params("parallel"),
        name="mlstm_mixer",
    )(cx, mv, mo, gif_t, *args)


def _mix_ffn_kernel(ya_ref, yb_ref, yc_ref, x_ref, gate1_ref, ga_ref, gb_ref, wo_ref,
                    sh_ref, sc_ref, gate_ref, g2_ref, wgu_ref, wd_ref, gf_ref, o_ref, *, final_norm):
    a = _rms(ya_ref[...], ga_ref[...]).astype(BF16)
    b = _rms(yb_ref[...], gb_ref[...]).astype(BF16)
    c = yc_ref[...].astype(BF16)
    mixed = (_dot(a, wo_ref[0:WIDTH_A, :]) + _dot(b, wo_ref[WIDTH_A:WIDTH_A + WIDTH_B, :])
             + _dot(c, wo_ref[WIDTH_A + WIDTH_B:, :]))
    x = x_ref[...] + gate1_ref[0] * mixed
    h = (_rms(x, g2_ref[...]) * (1.0 + sc_ref[0]) + sh_ref[0]).astype(BF16)
    acc = jnp.zeros(x.shape, F32)
    for j in range(FFN_HIDDEN // FFN_CHUNK):
        gate = _dot(h, wgu_ref[:, FFN_CHUNK * j:FFN_CHUNK * (j + 1)])
        up = _dot(h, wgu_ref[:, FFN_HIDDEN + FFN_CHUNK * j:FFN_HIDDEN + FFN_CHUNK * (j + 1)])
        act = (_silu(gate) * up).astype(BF16)
        acc = acc + _dot(act, wd_ref[FFN_CHUNK * j:FFN_CHUNK * (j + 1), :])
    y = x + gate_ref[0] * acc
    if final_norm:
        y = _rms(y, gf_ref[...])
    o_ref[...] = y


def _mix_ffn(ya, yb, yc, x2, gate1, ga, gb, w_out, shift, scale, gate, g2, wgu, wd, gf, S, final_norm):
    T = x2.shape[0]
    tm = TM_PROJ
    per_b = S // tm
    row = lambda w: pl.BlockSpec((tm, w), lambda i: (i, 0))
    full = lambda a: pl.BlockSpec(a.shape, lambda i: (0,) * a.ndim)
    modspec = pl.BlockSpec((1, 1, D_MODEL), lambda i: (i // per_b, 0, 0))
    return pl.pallas_call(
        functools.partial(_mix_ffn_kernel, final_norm=final_norm),
        grid=(T // tm,),
        in_specs=[row(WIDTH_A), row(WIDTH_B), row(WIDTH_C), row(D_MODEL), modspec, full(ga), full(gb), full(w_out),
                  modspec, modspec, modspec, full(g2), full(wgu), full(wd), full(gf)],
        out_specs=row(D_MODEL),
        out_shape=jax.ShapeDtypeStruct((T, D_MODEL), F32),
        compiler_params=_params("parallel"),
        name="mix_ffn_final" if final_norm else "mix_ffn",
    )(ya, yb, yc, x2, gate1, ga, gb, w_out, shift, scale, gate, g2, wgu, wd, gf)


def _head_tile_perm():
    idx = []
    for r in range(NSA_REP):
        idx += list(range(HEAD_DIM * r, HEAD_DIM * (r + 1)))
        idx += list(range(HEAD_DIM * (NSA_REP + r), HEAD_DIM * (NSA_REP + r + 1)))
    return np.asarray(idx, np.int32)


def _in_cols():
    std = np.full((N_STD,), -1, np.int64)
    tr = np.full((N_T,), -1, np.int64)
    off = 0
    tr[TSEG_Q:TSEG_Q + WIDTH_A] = _head_tile_perm()
    off += WIDTH_A
    std[SEG_CKV:SEG_CKV + 2 * LANES] = off + np.arange(2 * LANES)
    off += 2 * LANES
    std[SEG_SK:SEG_SK + LANES] = off + np.arange(LANES)
    off += LANES
    tr[TSEG_SV:TSEG_SV + LANES] = off + np.arange(LANES)
    off += LANES
    std[SEG_WK:SEG_WK + LANES] = off + np.arange(LANES)
    off += LANES
    tr[TSEG_WV:TSEG_WV + LANES] = off + np.arange(LANES)
    off += LANES
    tr[TSEG_G:TSEG_G + N_GATES] = off + np.arange(N_GATES)
    off += N_GATES
    std[SEG_QL:SEG_QL + MLA_Q_LORA] = off + np.arange(MLA_Q_LORA)
    off += MLA_Q_LORA
    std[SEG_KVL:SEG_KVL + MLA_KV_LORA] = off + np.arange(MLA_KV_LORA)
    off += MLA_KV_LORA
    std[SEG_KR + MLA_NOPE:SEG_KR + MLA_NOPE + MLA_ROPE] = off + np.arange(MLA_ROPE)
    off += MLA_ROPE
    for seg in (SEG_CX, SEG_MV, SEG_MO):
        std[seg:seg + WIDTH_C] = off + np.arange(WIDTH_C)
        off += WIDTH_C
    tr[TSEG_IF:TSEG_IF + 2 * MLSTM_HEADS] = off + np.arange(2 * MLSTM_HEADS)
    return std, tr


def _gather_cols(w, cols):
    pieces, start = [], 0
    for i in range(1, len(cols) + 1):
        run_ends = i == len(cols) or (cols[i] != cols[i - 1] + 1 if cols[i - 1] >= 0 else cols[i] >= 0)
        if run_ends:
            first = int(cols[start])
            pieces.append(w[:, first:first + i - start] if first >= 0 else jnp.zeros((w.shape[0], i - start), w.dtype))
            start = i
    return jnp.concatenate(pieces, axis=1)


def _layer_weights(l, w_in, w_uq, w_ukv, w_out, w_gu, w_down, g_out_a):
    std, tr = _in_cols()
    w_std = _gather_cols(w_in[l], std).astype(BF16)
    w_t = _gather_cols(w_in[l], tr).T.astype(BF16)
    cq = np.full((MLA_HEADS * LANES,), -1, np.int64)
    ck = np.full((MLA_HEADS * LANES,), -1, np.int64)
    cv = np.zeros((WIDTH_B,), np.int64)
    dq = MLA_NOPE + MLA_ROPE
    dkv = MLA_NOPE + MLA_V
    for hd in range(MLA_HEADS):
        cq[LANES * hd:LANES * hd + dq] = dq * hd + np.arange(dq)
        ck[LANES * hd:LANES * hd + MLA_NOPE] = dkv * hd + np.arange(MLA_NOPE)
        cv[MLA_V * hd:MLA_V * (hd + 1)] = dkv * hd + MLA_NOPE + np.arange(MLA_V)
    wuqT = _gather_cols(w_uq[l], cq).T.astype(BF16)
    wkm = _gather_cols(w_ukv[l], ck).astype(BF16)
    wvmT = _gather_cols(w_ukv[l], cv).T.astype(BF16)
    perm = _head_tile_perm()
    head_rows = [w_out[l][int(o):int(o) + HEAD_DIM] for o in perm[::HEAD_DIM]]
    wo = jnp.concatenate(head_rows + [w_out[l][WIDTH_A:]], axis=0).astype(BF16)
    ga = _gather_cols(g_out_a[l].reshape(1, WIDTH_A), perm)
    return w_std, w_t, wuqT, wkm, wvmT, wo, ga, w_gu[l].astype(BF16), w_down[l].astype(BF16)


def kernel(x, c, positions, g_norm1, g_norm2, w_ada, b_ada, w_in, cmp_pos, w_cmp_k, w_cmp_v, g_out_a, g_q_lora, w_uq, g_kv_lora, w_ukv, g_out_b, conv_w, conv_b, w_q_m, w_k_m, b_igate, b_fgate, g_mh, skip_m, w_out, w_gu, w_down, g_final):
    B, S, D = x.shape
    T = B * S
    x2 = x.reshape(T, D)
    tabs = _rope_tables(positions)
    cn, sn, cm, sm = tabs
    tabs_t = (cn[:, :NSA_ROT_HALF].T, sn[:, :NSA_ROT_HALF].T,
              cm[:, MLA_NOPE:MLA_NOPE + MLA_ROPE // 2].T, sm[:, MLA_NOPE:MLA_NOPE + MLA_ROPE // 2].T)
    mod = _ada(c, w_ada, b_ada)
    row = lambda v: v.reshape(1, -1)
    for l in range(DEPTH):
        w_std, w_t, wuqT, wkm, wvmT, wo, ga, wgu, wd = _layer_weights(l, w_in, w_uq, w_ukv, w_out, w_gu, w_down, g_out_a)
        m6 = [mod[l, :, D * i:D * (i + 1)].reshape(B, 1, D) for i in range(6)]
        shift1, scale1, gate1, shift2, scale2, gate2 = m6
        (qaT, gT, ck, cv, sk_aug, wk, svT, wvT, qmT, km, vmT, cx, mv, mo, gif_t) = _inproj(
            x2, shift1, scale1, row(g_norm1[l]), w_std, w_t, tabs, tabs_t, row(g_q_lora[l]), wuqT,
            row(g_kv_lora[l]), wkm, wvmT, S)
        kc, vcT = _compress(ck, cv, cmp_pos[l], w_cmp_k[l], w_cmp_v[l], B, S)
        ya = _nsa(qaT, gT, kc, vcT, sk_aug, svT, wk, wvT, B, S)
        yb = _mla(qmT, km, vmT, B, S)
        yc = _mlstm(cx, mv, mo, gif_t, conv_w[l], conv_b[l], w_q_m[l], w_k_m[l], b_igate[l], b_fgate[l],
                    g_mh[l], skip_m[l], B, S)
        x2 = _mix_ffn(ya, yb, yc, x2, gate1, ga, row(g_out_b[l]), wo, shift2, scale2, gate2, row(g_norm2[l]),
                      wgu, wd, row(g_final), S, final_norm=(l == DEPTH - 1))
    return x2.reshape(B, S, D)
```

```python
import functools

import numpy as np
import jax
import jax.numpy as jnp
from jax import lax
from jax.experimental import pallas as pl
from jax.experimental.pallas import tpu as pltpu

F32 = jnp.float32
BF16 = jnp.bfloat16

D_MODEL = 1024
DEPTH = 2
HEAD_DIM = 64
ROPE_THETA = 500000.0
NSA_ROT_HALF = HEAD_DIM // 8
NORM_EPS = 1e-6

NSA_HEADS = 6
NSA_KV_HEADS = 2
NSA_REP = NSA_HEADS // NSA_KV_HEADS
CMP_LEN = 32
CMP_STRIDE = 16
SLC_LEN = 64
SLC_TOPK = 16
WINDOW = 512

MLA_HEADS = 6
MLA_Q_LORA = 256
MLA_KV_LORA = 128
MLA_NOPE = 64
MLA_ROPE = 32
MLA_V = 64

MLSTM_HEADS = 4
MLSTM_DIM = 64
MLSTM_CONV = 4

WIDTH_A = NSA_HEADS * HEAD_DIM
WIDTH_B = MLA_HEADS * MLA_V
WIDTH_C = MLSTM_HEADS * MLSTM_DIM
FFN_HIDDEN = 2816
N_GATES = 3 * NSA_HEADS
GATE_ROWS = 24

LANES = 128
NEG = -1e30
LOG2E = 1.4426950408889634
VMEM_LIMIT = 56 * 1024 * 1024

TM_PROJ = 512
TQ_NSA = 256
TK_SLC = 256
CHAIN_LANES = 256
TQ_MLA = 256
FFN_CHUNK = 256
MLSTM_TILE = 128
MLSTM_GROUP = 8
MXU_LOOKAHEAD = 4

SEG_CKV, SEG_SK, SEG_WK, SEG_QL, SEG_KVL, SEG_KR, SEG_CX, SEG_MV, SEG_MO = (
    0, 256, 384, 512, 768, 896, 1024, 1280, 1536)
N_STD = 1792
TSEG_Q, TSEG_G, TSEG_SV, TSEG_WV, TSEG_IF = 0, 384, 416, 544, 672
TSEG_G_ROWS = 32
TSEG_IF_ROWS = 16
N_T = 688


def _params(*sem):
    return pltpu.CompilerParams(dimension_semantics=sem, vmem_limit_bytes=VMEM_LIMIT)


def _dot(a, b):
    return jnp.dot(a, b, preferred_element_type=F32)


def _dot_nt(a, b):
    return lax.dot_general(a, b, (((1,), (1,)), ((), ())), preferred_element_type=F32)


def _dot_tn(a, b):
    return lax.dot_general(a, b, (((0,), (0,)), ((), ())), preferred_element_type=F32)


def _split3(x):
    hi = x.astype(BF16)
    r1 = x - hi.astype(F32)
    mid = r1.astype(BF16)
    lo = (r1 - mid.astype(F32)).astype(BF16)
    return hi, mid, lo


def _rms(x, g):
    return x * lax.rsqrt(jnp.mean(x * x, axis=-1, keepdims=True) + NORM_EPS) * g


def _sigmoid(x):
    return 1.0 / (1.0 + jnp.exp(-x))


def _silu(x):
    return x * _sigmoid(x)


def _rope(x, cos, sin, half, x1_lane):
    xr = jnp.where(x1_lane, -pltpu.roll(x, LANES - half, 1), pltpu.roll(x, half, 1))
    return x * cos + xr * sin


def _ada_kernel(c_ref, w_ref, b_ref, o_ref):
    c = c_ref[...]
    ca = _silu(c).astype(BF16)
    o_ref[0] = _dot(ca, w_ref[0].astype(BF16)) + b_ref[0]


def _ada(c, w_ada, b_ada):
    L, D, N = w_ada.shape
    B = c.shape[0]
    tn = 1536
    return pl.pallas_call(
        _ada_kernel,
        grid=(L, N // tn),
        in_specs=[pl.BlockSpec((B, D), lambda l, j: (0, 0)),
                  pl.BlockSpec((1, D, tn), lambda l, j: (l, 0, j)),
                  pl.BlockSpec((1, 1, tn), lambda l, j: (l, 0, j))],
        out_specs=pl.BlockSpec((1, B, tn), lambda l, j: (l, 0, j)),
        out_shape=jax.ShapeDtypeStruct((L, B, N), F32),
        compiler_params=_params("parallel", "parallel"),
        name="ada_mod",
    )(c, w_ada, b_ada.reshape(L, 1, N))


ROPE_FREQS = NSA_ROT_HALF + MLA_ROPE // 2


def _rope_kernel(pos_ref, inv_ref, spread_ref, one_ref, cn_ref, sn_ref, cm_ref, sm_ref):
    ang = pos_ref[...] * inv_ref[...]
    lane = lax.broadcasted_iota(jnp.int32, (1, LANES), 1)

    def spread(val):
        hi, mid, lo = _split3(val)
        packed = jnp.where(lane < ROPE_FREQS, hi, jnp.where(lane < 2 * ROPE_FREQS, mid, lo))
        return _dot(packed, spread_ref[...])

    c = spread(jnp.cos(ang)) + one_ref[...]
    s = spread(jnp.sin(ang))
    cn_ref[...] = c[:, :LANES]
    cm_ref[...] = c[:, LANES:]
    sn_ref[...] = s[:, :LANES]
    sm_ref[...] = s[:, LANES:]


def _rope_tables(positions):
    T = positions.size
    inv_n = jnp.power(ROPE_THETA, -jnp.arange(0, 2 * NSA_ROT_HALF, 2, dtype=F32) / (2 * NSA_ROT_HALF))
    inv_m = jnp.power(ROPE_THETA, -jnp.arange(0, MLA_ROPE, 2, dtype=F32) / MLA_ROPE)
    copies = 3
    inv = jnp.concatenate([inv_n, inv_m] * copies + [jnp.zeros((LANES - copies * ROPE_FREQS,), F32)]).reshape(1, LANES)
    onehot = np.zeros((LANES, 2 * LANES), np.float32)
    rotated = np.zeros((1, 2 * LANES), np.float32)
    for k in range(copies):
        for f in range(NSA_ROT_HALF):
            for lane in (f, f + NSA_ROT_HALF, HEAD_DIM + f, HEAD_DIM + f + NSA_ROT_HALF):
                onehot[ROPE_FREQS * k + f, lane] = 1.0
                rotated[0, lane] = 1.0
        for f in range(MLA_ROPE // 2):
            for lane in (MLA_NOPE + f, MLA_NOPE + MLA_ROPE // 2 + f):
                onehot[ROPE_FREQS * k + NSA_ROT_HALF + f, LANES + lane] = 1.0
                rotated[0, LANES + lane] = 1.0
    posb = jnp.broadcast_to(positions.reshape(T, 1).astype(F32), (T, LANES))
    tm = 2048
    spec = pl.BlockSpec((tm, LANES), lambda i: (i, 0))
    full = lambda shape: pl.BlockSpec(shape, lambda i: (0, 0))
    return pl.pallas_call(
        _rope_kernel,
        grid=(T // tm,),
        in_specs=[spec, full((1, LANES)), full((LANES, 2 * LANES)), full((1, 2 * LANES))],
        out_specs=[spec] * 4,
        out_shape=[jax.ShapeDtypeStruct((T, LANES), F32)] * 4,
        compiler_params=_params("parallel"),
        name="rope_tables",
    )(posb, inv, jnp.asarray(onehot, BF16), jnp.asarray(1.0 - rotated))


def _rope_rows(t, offset, half, cos, sin):
    x1, x2 = t[offset:offset + half], t[offset + half:offset + 2 * half]
    return x1 * cos - x2 * sin, x2 * cos + x1 * sin


def _inproj_kernel(x_ref, sh_ref, sc_ref, g1_ref, ws_ref, wt_ref, cn_ref, sn_ref, cm_ref, sm_ref,
                   cnT_ref, snT_ref, cmT_ref, smT_ref, gq_ref, wuqT_ref, gkv_ref, wkm_ref, wvmT_ref,
                   qaT_ref, gT_ref, ck_ref, cv_ref, ska_ref, wk_ref, svT_ref, wvT_ref,
                   qmT_ref, km_ref, vmT_ref, cx_ref, mv_ref, mo_ref, gif_ref, *, per_b):
    tm = x_ref.shape[0]
    x = x_ref[...]
    h = _rms(x, g1_ref[...]) * (1.0 + sc_ref[0]) + sh_ref[0]
    hb = h.astype(BF16)

    def seg(start, width):
        return _dot(hb, ws_ref[:, start:start + width])

    def seg_t(start, height):
        return _dot_nt(wt_ref[start:start + height, :], hb)

    lane = lax.broadcasted_iota(jnp.int32, (1, LANES), 1)
    x1_n = (lane % HEAD_DIM) < NSA_ROT_HALF
    x1_m = lane < MLA_NOPE + MLA_ROPE // 2
    cn, sn, cm, sm = cn_ref[...], sn_ref[...], cm_ref[...], sm_ref[...]
    rope_n = lambda t: _rope(t, cn, sn, NSA_ROT_HALF, x1_n)
    rope_m = lambda t: _rope(t, cm, sm, MLA_ROPE // 2, x1_m)

    qn = _rms(seg(SEG_QL, MLA_Q_LORA), gq_ref[...]).astype(BF16)
    kvl_kr = seg(SEG_KVL, 2 * LANES)
    kvn = _rms(kvl_kr[:, :LANES], gkv_ref[...]).astype(BF16)
    kr = rope_m(kvl_kr[:, LANES:])

    qt = seg_t(TSEG_Q, WIDTH_A)
    cnt, snt = cnT_ref[...], snT_ref[...]
    parts = []
    for hd in range(NSA_HEADS):
        o = HEAD_DIM * hd
        parts += list(_rope_rows(qt, o, NSA_ROT_HALF, cnt, snt)) + [qt[o + 2 * NSA_ROT_HALF:o + HEAD_DIM]]
    qaT_ref[...] = (jnp.concatenate(parts, axis=0) * (HEAD_DIM ** -0.5 * LOG2E)).astype(BF16)
    gT_ref[...] = _sigmoid(seg_t(TSEG_G, TSEG_G_ROWS))[:GATE_ROWS]
    svT_ref[...] = seg_t(TSEG_SV, LANES).astype(BF16)
    wvT_ref[...] = seg_t(TSEG_WV, LANES).astype(BF16)

    ckv = seg(SEG_CKV, 2 * LANES)
    ck_ref[...] = rope_n(ckv[:, :LANES])
    cv_ref[...] = ckv[:, LANES:]
    ska_ref[:, :LANES] = rope_n(seg(SEG_SK, LANES)).astype(BF16)
    srow = (pl.program_id(0) % per_b) * tm + lax.broadcasted_iota(jnp.int32, (tm, LANES), 0)
    lane2 = lax.broadcasted_iota(jnp.int32, (tm, LANES), 1)
    code = (lane2 < 4 * SLC_TOPK) & ((lane2 & (2 * SLC_TOPK - 1)) == srow // SLC_LEN)
    ska_ref[:, LANES:] = jnp.where(code, 1.0, 0.0).astype(BF16)
    wk_ref[...] = rope_n(seg(SEG_WK, LANES)).astype(BF16)

    cx_ref[...] = seg(SEG_CX, WIDTH_C)
    mv_ref[...] = seg(SEG_MV, WIDTH_C).astype(BF16)
    mo_ref[...] = seg(SEG_MO, WIDTH_C)
    gif_ref[...] = seg_t(TSEG_IF, TSEG_IF_ROWS)[:2 * MLSTM_HEADS]

    qmt = _dot_nt(wuqT_ref[...], qn)
    cmt, smt = cmT_ref[...], smT_ref[...]
    parts = []
    for hd in range(MLA_HEADS):
        o = LANES * hd
        parts += [qmt[o:o + MLA_NOPE]] + list(_rope_rows(qmt, o + MLA_NOPE, MLA_ROPE // 2, cmt, smt))
        parts += [qmt[o + MLA_NOPE + MLA_ROPE:o + LANES]]
    qmT_ref[...] = (jnp.concatenate(parts, axis=0) * ((MLA_NOPE + MLA_ROPE) ** -0.5 * LOG2E)).astype(BF16)
    kk = _dot(kvn, wkm_ref[...])
    for hd in range(MLA_HEADS):
        km_ref[:, LANES * hd:LANES * (hd + 1)] = (kk[:, LANES * hd:LANES * (hd + 1)] + kr).astype(BF16)
    vmT_ref[...] = _dot_nt(wvmT_ref[...], kvn).astype(BF16)


def _inproj(x2, shift, scale, g1, w_std, w_t, tabs, tabs_t, gq, wuqT, gkv, wkm, wvmT, S):
    T = x2.shape[0]
    tm = TM_PROJ
    per_b = S // tm
    row = lambda w: pl.BlockSpec((tm, w), lambda i: (i, 0))
    col = lambda h: pl.BlockSpec((h, tm), lambda i: (0, i))
    full = lambda a: pl.BlockSpec(a.shape, lambda i: (0,) * a.ndim)
    modspec = pl.BlockSpec((1, 1, D_MODEL), lambda i: (i // per_b, 0, 0))
    outs = [(WIDTH_A, BF16, True), (GATE_ROWS, F32, True), (LANES, F32, False), (LANES, F32, False), (2 * LANES, BF16, False),
            (LANES, BF16, False), (LANES, BF16, True), (LANES, BF16, True),
            (MLA_HEADS * LANES, BF16, True), (MLA_HEADS * LANES, BF16, False), (WIDTH_B, BF16, True),
            (WIDTH_C, F32, False), (WIDTH_C, BF16, False), (WIDTH_C, F32, False), (2 * MLSTM_HEADS, F32, True)]
    return pl.pallas_call(
        functools.partial(_inproj_kernel, per_b=per_b),
        grid=(T // tm,),
        in_specs=[row(D_MODEL), modspec, modspec, full(g1), full(w_std), full(w_t)] + [row(LANES)] * 4
                 + [col(t.shape[0]) for t in tabs_t] + [full(gq), full(wuqT), full(gkv), full(wkm), full(wvmT)],
        out_specs=[col(w) if tr else row(w) for w, _, tr in outs],
        out_shape=[jax.ShapeDtypeStruct((w, T) if tr else (T, w), dt) for w, dt, tr in outs],
        compiler_params=_params("parallel"),
        name="in_proj",
    )(x2, shift, scale, g1, w_std, w_t, *tabs, *tabs_t, gq, wuqT, gkv, wkm, wvmT)


def _compress_kernel(xk_ref, xv_ref, wk_ref, wv_ref, pos_ref, wkf_ref, wvf_ref, kc_ref, vcT_ref):
    ng = xk_ref.shape[0] // CMP_STRIDE
    row = lax.broadcasted_iota(jnp.int32, (ng, LANES), 0)
    pos = pos_ref[...].astype(BF16)
    acc_k = jnp.zeros((ng, 2 * LANES), F32)
    acc_v = jnp.zeros((ng, 2 * LANES), F32)
    for t in range(CMP_STRIDE):
        tok = pl.ds(t, ng, stride=CMP_STRIDE)
        acc_k = acc_k + _dot(xk_ref[tok, :].astype(BF16), wk_ref[t])
        acc_v = acc_v + _dot(xv_ref[tok, :].astype(BF16), wv_ref[t])

    def finish(acc, wf_ref):
        const = _dot(pos, wf_ref[...].astype(BF16))
        both = acc[:, :LANES] + pltpu.roll(acc[:, LANES:], ng - 1, 0) + jnp.concatenate([const, const], axis=1)
        return jnp.where(row < ng - 1, both, 0.0)

    kc_ref[0] = finish(acc_k, wkf_ref).astype(BF16)
    vcT_ref[0] = finish(acc_v, wvf_ref).T.astype(BF16)


def _compress(ck, cv, cmp_pos, w_cmp_k, w_cmp_v, B, S):
    ng = S // CMP_STRIDE

    def per_token(w):
        a = w[:CMP_STRIDE * HEAD_DIM].reshape(CMP_STRIDE, HEAD_DIM, HEAD_DIM)
        b = w[CMP_STRIDE * HEAD_DIM:].reshape(CMP_STRIDE, HEAD_DIM, HEAD_DIM)
        z = jnp.zeros_like(a)
        top = jnp.concatenate([a, z, b, z], axis=2)
        bot = jnp.concatenate([z, a, z, b], axis=2)
        return jnp.concatenate([top, bot], axis=1).astype(BF16)

    full = lambda a: pl.BlockSpec(a.shape, lambda b: (0,) * a.ndim)
    wk3, wv3 = per_token(w_cmp_k), per_token(w_cmp_v)
    posf = cmp_pos.reshape(1, CMP_LEN * HEAD_DIM)
    ospec = pl.BlockSpec((1, ng, LANES), lambda b: (b, 0, 0))
    return pl.pallas_call(
        _compress_kernel,
        grid=(B,),
        in_specs=[pl.BlockSpec((S, LANES), lambda b: (b, 0)), pl.BlockSpec((S, LANES), lambda b: (b, 0)),
                  full(wk3), full(wv3), full(posf), full(w_cmp_k), full(w_cmp_v)],
        out_specs=[ospec, ospec],
        out_shape=[jax.ShapeDtypeStruct((B, ng, LANES), BF16)] * 2,
        compiler_params=_params("parallel"),
        name="nsa_compress",
    )(ck, cv, wk3, wv3, posf, w_cmp_k, w_cmp_v)


def _nsa_kernel(qT_ref, gT_ref, kc_ref, vcT_ref, sk_ref, svT_ref, wk_ref, wvT_ref, o_ref, qaug_scr, acc_scr, s_scr):
    tq = TQ_NSA
    cols = NSA_HEADS * tq
    pair = CHAIN_LANES
    t0 = pl.program_id(1) * tq
    n_slc = SLC_TOPK * 2

    frow = lax.broadcasted_iota(jnp.int32, (LANES, tq), 0)
    g0_row = frow < HEAD_DIM
    tiles = [qT_ref[LANES * r:LANES * (r + 1), :] for r in range(NSA_REP)]
    zero = jnp.zeros_like(tiles[0])
    q6 = jnp.concatenate([jnp.where(g0_row, t, zero) for t in tiles]
                         + [jnp.where(g0_row, zero, t) for t in tiles], axis=1)
    qaug_scr[0:LANES, :] = q6
    tq_l = t0 + (lax.broadcasted_iota(jnp.int32, (1, cols), 1) & (tq - 1))

    pairs = [slice(pair * pp, pair * (pp + 1)) for pp in range(cols // pair)]
    vrows = lambda pp: slice(HEAD_DIM * (pp // NSA_REP), HEAD_DIM * (pp // NSA_REP + 1))
    s = _dot(kc_ref[0], q6)
    span = WINDOW + tq
    start = pl.multiple_of(jnp.maximum(t0 - WINDOW, 0), tq)
    kw = wk_ref[pl.ds(start, span), :]
    win_scores = [_dot(kw, q6[:, sl]) for sl in pairs]

    tq_1 = tq_l[:, :tq]
    nrow = lax.broadcasted_iota(jnp.int32, (LANES, tq), 0)
    cmp_bias = jnp.where(nrow * CMP_STRIDE + (CMP_LEN - 1) <= tq_1, 0.0, NEG)
    s = s + jnp.concatenate([cmp_bias] * NSA_HEADS, axis=1)
    e = jnp.exp2(s - jnp.max(s, axis=0, keepdims=True))
    seen = jnp.where(tq_l >= CMP_LEN - 1, 1.0, 0.0)
    p = e * (seen / jnp.sum(e, axis=0, keepdims=True))
    p_b = p.astype(BF16)
    o_cmp = [_dot(vcT_ref[0, vrows(pp), :], p_b[:, sl]) for pp, sl in enumerate(pairs)]

    jr = lax.broadcasted_iota(jnp.int32, (n_slc, LANES), 0)
    nc = lax.broadcasted_iota(jnp.int32, (n_slc, LANES), 1)
    ovl = ((nc * CMP_STRIDE < jr * SLC_LEN + SLC_LEN) & (nc * CMP_STRIDE + CMP_LEN > jr * SLC_LEN)
           & (nc < LANES - 1))
    ovl = jnp.where(ovl, 1.0, 0.0).astype(BF16)
    jq = lax.broadcasted_iota(jnp.int32, (n_slc, tq), 0)
    tl = t0 + lax.broadcasted_iota(jnp.int32, (n_slc, tq), 1)
    cur = tl // SLC_LEN
    forced = (jq == 0) | (jq == cur) | (jq == cur - 1)
    future = jq * SLC_LEN > tl
    bias_t = []
    for g in range(NSA_KV_HEADS):
        pg = p[:, (3 * g) * tq:(3 * g + 1) * tq] + p[:, (3 * g + 1) * tq:(3 * g + 2) * tq] + p[:, (3 * g + 2) * tq:(3 * g + 3) * tq]
        imp = sum(_dot(ovl, part) for part in _split3(pg))
        imp = jnp.where(forced, jnp.inf, imp)
        imp = jnp.where(future, -jnp.inf, imp)
        rank = jnp.zeros((n_slc, tq), F32)
        for jp in range(n_slc):
            rv = imp[jp:jp + 1, :]
            ahead = jnp.where(rv > imp, 1.0, jnp.where((rv == imp) & (jq > jp), 1.0, 0.0))
            rank = rank + ahead
        bias_t.append(jnp.where(rank < float(SLC_TOPK), 0.0, NEG).astype(BF16))
    zb = jnp.zeros((n_slc, NSA_REP * tq), BF16)
    qaug_scr[LANES:LANES + n_slc, :] = jnp.concatenate([bias_t[0]] * NSA_REP + [zb], axis=1)
    qaug_scr[LANES + n_slc:LANES + 2 * n_slc, :] = jnp.concatenate([zb] + [bias_t[1]] * NSA_REP, axis=1)
    qaug_scr[LANES + 2 * n_slc:, :] = jnp.zeros((LANES - 2 * n_slc, cols), BF16)

    vwt = wvT_ref[:, pl.ds(start, span)]
    wrow = start + lax.broadcasted_iota(jnp.int32, (span, pair), 0)
    o_win = []
    for pp, (sl, sc) in enumerate(zip(pairs, win_scores)):
        in_window = lax.bitcast_convert_type(tq_l[:, sl] - wrow, jnp.uint32) < jnp.uint32(WINDOW)
        sc = jnp.where(in_window, sc, NEG)
        ew = jnp.exp2(sc - jnp.max(sc, axis=0, keepdims=True))
        o_win.append(_dot(vwt[vrows(pp)], ew.astype(BF16)) / jnp.sum(ew, axis=0, keepdims=True))

    acc_scr[...] = jnp.zeros((HEAD_DIM, cols), F32)
    krow = lax.broadcasted_iota(jnp.int32, (TK_SLC, pair), 0)

    def slc_scores(k0, sl):
        return _dot(sk_ref[pl.ds(k0, TK_SLC), :], qaug_scr[:, sl])

    nch = len(pairs)
    la = s_scr.shape[0]
    for pp in range(la):
        s_scr[pp] = slc_scores(0, pairs[pp])

    def slc_tile(k0, m, l, masked):
        vt = svT_ref[:, pl.ds(k0, TK_SLC)]
        ms, ls, accs = [], [], []
        ahead = [s_scr[pp] for pp in range(la)]
        for pp, sl in enumerate(pairs):
            sc = ahead.pop(0)
            nxt = pp + la
            if nxt < nch:
                ahead.append(slc_scores(k0, pairs[nxt]))
            elif not masked:
                s_scr[nxt - nch] = slc_scores(pl.multiple_of(k0 + TK_SLC, TK_SLC), pairs[nxt - nch])
            if masked:
                sc = jnp.where(k0 + krow <= tq_l[:, sl], sc, NEG)
            m_new = jnp.maximum(m[:, sl], jnp.max(sc, axis=0, keepdims=True))
            alpha = jnp.exp2(m[:, sl] - m_new)
            pe = jnp.exp2(sc - m_new)
            ls.append(alpha * l[:, sl] + jnp.sum(pe, axis=0, keepdims=True))
            accs.append(alpha * acc_scr[:, sl] + _dot(vt[vrows(pp)], pe.astype(BF16)))
            ms.append(m_new)
        acc_scr[...] = jnp.concatenate(accs, axis=1)
        return jnp.concatenate(ms, axis=1), jnp.concatenate(ls, axis=1)

    def slc_step(kt, carry):
        return slc_tile(pl.multiple_of(kt * TK_SLC, TK_SLC), carry[0], carry[1], False)

    n_full = t0 // TK_SLC
    m, l = lax.fori_loop(0, n_full, slc_step, (jnp.full((1, cols), NEG, F32), jnp.zeros((1, cols), F32)))
    m, l = slc_tile(pl.multiple_of(n_full * TK_SLC, TK_SLC), m, l, True)
    o_slc = acc_scr[...] / l

    gt = gT_ref[...]
    mixed = []
    for hd, sl in enumerate(pairs):
        gate = lambda j: gt[3 * hd + j:3 * hd + j + 1, :]
        mixed.append(gate(0) * o_cmp[hd] + gate(1) * o_slc[:, sl] + gate(2) * o_win[hd])
    for r in range(NSA_REP):
        o_ref[:, LANES * r:LANES * (r + 1)] = jnp.concatenate([mixed[r], mixed[NSA_REP + r]], axis=0).T


def _nsa(qaT, gT, kc, vcT, sk_aug, svT, wk, wvT, B, S):
    tq = TQ_NSA
    assert CHAIN_LANES == tq, "the kernel treats one softmax chain as one head"
    nq = S // tq
    cols = NSA_HEADS * tq
    qcol =lambda h: pl.BlockSpec((h, tq), lambda b, i: (0, b * nq + i))
    seq = lambda w: pl.BlockSpec((S, w), lambda b, i: (b, 0))
    seqT = pl.BlockSpec((LANES, S), lambda b, i: (0, b))
    cspec = pl.BlockSpec((1, LANES, LANES), lambda b, i: (b, 0, 0))
    return pl.pallas_call(
        _nsa_kernel,
        grid=(B, nq),
        in_specs=[qcol(WIDTH_A), qcol(gT.shape[0]), cspec, cspec, seq(2 * LANES), seqT, seq(LANES), seqT],
        out_specs=pl.BlockSpec((tq, WIDTH_A), lambda b, i: (b * nq + i, 0)),
        out_shape=jax.ShapeDtypeStruct((B * S, WIDTH_A), F32),
        scratch_shapes=[pltpu.VMEM((2 * LANES, cols), BF16), pltpu.VMEM((HEAD_DIM, cols), F32),
                        pltpu.VMEM((cols // CHAIN_LANES, TK_SLC, CHAIN_LANES), F32)],
        compiler_params=_params("parallel", "arbitrary"),
        name="nsa_attention",
    )(qaT, gT, kc, vcT, sk_aug, svT, wk, wvT)


def _mla_kernel(qT_ref, k_ref, vT_ref, o_ref, acc_scr, s_scr):
    tq = TQ_MLA
    t0 = pl.program_id(1) * tq
    tq_l = t0 + lax.broadcasted_iota(jnp.int32, (1, tq), 1)
    krow = lax.broadcasted_iota(jnp.int32, (tq, tq), 0)
    acc_scr[...] = jnp.zeros((MLA_HEADS, MLA_V, tq), F32)

    def scores(k0, hd):
        k = k_ref[pl.ds(k0, tq), LANES * hd:LANES * (hd + 1)]
        return _dot(k, qT_ref[LANES * hd:LANES * (hd + 1), :])

    for hd in range(MXU_LOOKAHEAD):
        s_scr[hd] = scores(0, hd)

    def tile(k0, ms, ls, masked):
        new_m, new_l, accs = [], [], []
        ahead = [s_scr[hd] for hd in range(MXU_LOOKAHEAD)]
        for hd in range(MLA_HEADS):
            sc = ahead.pop(0)
            nxt = hd + MXU_LOOKAHEAD
            if nxt < MLA_HEADS:
                ahead.append(scores(k0, nxt))
            elif not masked:
                s_scr[nxt - MLA_HEADS] = scores(pl.multiple_of(k0 + tq, tq), nxt - MLA_HEADS)
            if masked:
                sc = jnp.where(k0 + krow <= tq_l, sc, NEG)
            m_new = jnp.maximum(ms[hd], jnp.max(sc, axis=0, keepdims=True))
            alpha = jnp.exp2(ms[hd] - m_new)
            pe = jnp.exp2(sc - m_new)
            new_l.append(alpha * ls[hd] + jnp.sum(pe, axis=0, keepdims=True))
            vt = vT_ref[MLA_V * hd:MLA_V * (hd + 1), pl.ds(k0, tq)]
            accs.append(alpha * acc_scr[hd] + _dot(vt, pe.astype(BF16)))
            new_m.append(m_new)
        acc_scr[...] = jnp.stack(accs)
        return tuple(new_m), tuple(new_l)

    def step(kt, carry):
        return tile(pl.multiple_of(kt * tq, tq), carry[0], carry[1], False)

    n_full = pl.program_id(1)
    init = (tuple(jnp.full((1, tq), NEG, F32) for _ in range(MLA_HEADS)),
            tuple(jnp.zeros((1, tq), F32) for _ in range(MLA_HEADS)))
    ms, ls = lax.fori_loop(0, n_full, step, init)
    ms, ls = tile(pl.multiple_of(n_full * tq, tq), ms, ls, True)
    for pr in range(MLA_HEADS // 2):
        t = jnp.concatenate([acc_scr[2 * pr] / ls[2 * pr], acc_scr[2 * pr + 1] / ls[2 * pr + 1]], axis=0)
        o_ref[:, LANES * pr:LANES * (pr + 1)] = t.T


def _mla(qmT, km, vmT, B, S):
    tq = TQ_MLA
    nq = S // tq
    return pl.pallas_call(
        _mla_kernel,
        grid=(B, nq),
        in_specs=[pl.BlockSpec((MLA_HEADS * LANES, tq), lambda b, i: (0, b * nq + i)),
                  pl.BlockSpec((S, MLA_HEADS * LANES), lambda b, i: (b, 0)),
                  pl.BlockSpec((WIDTH_B, S), lambda b, i: (0, b))],
        out_specs=pl.BlockSpec((tq, WIDTH_B), lambda b, i: (b * nq + i, 0)),
        out_shape=jax.ShapeDtypeStruct((B * S, WIDTH_B), F32),
        scratch_shapes=[pltpu.VMEM((MLA_HEADS, MLA_V, tq), F32), pltpu.VMEM((MXU_LOOKAHEAD, tq, tq), F32)],
        compiler_params=_params("parallel", "arbitrary"),
        name="mla_attention",
    )(qmT, km, vmT)


def _log_sigmoid(z):
    return jnp.minimum(z, 0.0) - jnp.log1p(jnp.exp(-jnp.abs(z)))


def _mlstm_kernel(cx_ref, v_ref, o_ref, g_ref, cw_ref, cb_ref, wq_ref, wk_ref, br_ref, gmh_ref, skip_ref, y_ref,
                  xc_scr, q_scr, k_scr, w_scr, inter_scr, floor_scr, mfull_scr, ut_scr, eo_scr, el_scr, cprev_scr):
    S = cx_ref.shape[0]
    L = MLSTM_TILE
    NC = S // L
    d = MLSTM_DIM
    pairs = MLSTM_HEADS // 2
    group = MLSTM_GROUP

    x = cx_ref[...]
    rowi = lax.broadcasted_iota(jnp.int32, (S, WIDTH_C), 0)
    conv = x * cw_ref[MLSTM_CONV - 1:MLSTM_CONV, :]
    for back in range(1, MLSTM_CONV):
        shifted = jnp.where(rowi >= back, pltpu.roll(x, back, 0), 0.0)
        conv = conv + shifted * cw_ref[MLSTM_CONV - 1 - back:MLSTM_CONV - back, :]
    xc = _silu(conv + cb_ref[...])
    xc_scr[...] = xc
    xcb = xc.astype(BF16)
    q_scr[...] = _dot(xcb, wq_ref[...]).astype(BF16)
    k_scr[...] = _dot(xcb, wk_ref[...]).astype(BF16)

    nh = MLSTM_HEADS
    lane_in_chunk = lax.broadcasted_iota(jnp.int32, (nh, S), 1) & (L - 1)

    def scan_lanes(val, op, fill):
        sh = 1
        while sh < L:
            val = op(val, jnp.where(lane_in_chunk >= sh, pltpu.roll(val, sh, 1), fill))
            sh *= 2
        return val

    gt = g_ref[...] + br_ref[...]
    ig = gt[0:nh]
    b = scan_lanes(_log_sigmoid(gt[nh:]), jnp.add, 0.0)
    u = ig - b
    cmu = scan_lanes(u, jnp.maximum, -jnp.inf)
    ut_scr[0:nh, :] = u

    m = jnp.zeros((nh, 1), F32)
    w_loc, inter, floor, m_tok = [], [], [], []
    for c in range(NC):
        blk = slice(L * c, L * (c + 1))
        b_last, u_max = b[:, L * (c + 1) - 1:L * (c + 1)], cmu[:, L * (c + 1) - 1:L * (c + 1)]
        m_top = jnp.maximum(m, u_max)
        eo_scr[c] = jnp.broadcast_to(jnp.exp(m - m_top), (nh, LANES))
        el_scr[c] = jnp.broadcast_to(jnp.exp(u_max - m_top), (nh, LANES))
        mt = jnp.maximum(cmu[:, blk], m)
        m_tok.append(mt)
        inter.append(jnp.exp(m - mt))
        w_loc.append(jnp.exp(u[:, blk] - u_max))
        floor.append(jnp.exp(-(b[:, blk] + mt)))
        m = b_last + m_top

    hrow = lax.broadcasted_iota(jnp.int32, (4 * nh, 2 * LANES), 0)
    hcol = lax.broadcasted_iota(jnp.int32, (4 * nh, 2 * LANES), 1)
    to_heads = jnp.where((hrow < 3 * nh) & (hcol // d == hrow % nh), 1.0, 0.0).astype(BF16)
    frow = lax.broadcasted_iota(jnp.int32, (4 * nh, 4 * LANES), 0)
    fcol = lax.broadcasted_iota(jnp.int32, (4 * nh, 4 * LANES), 1)
    to_full = jnp.where((frow < 3 * nh) & (fcol // LANES == frow % nh), 1.0, 0.0).astype(BF16)

    def spread(chunks, onehot):
        val = jnp.concatenate(chunks, axis=1)
        hi, mid, lo = _split3(val)
        pieces = jnp.concatenate([hi.astype(F32), mid.astype(F32), lo.astype(F32), jnp.zeros_like(val)], axis=0)
        return _dot_tn(pieces.astype(BF16), onehot)

    w_scr[...] = spread(w_loc, to_heads)
    inter_scr[...] = spread(inter, to_heads)
    floor_scr[...] = spread(floor, to_heads)
    mfull_scr[...] = spread(m_tok, to_full)

    arow = lax.broadcasted_iota(jnp.int32, (LANES, 2 * LANES), 0)
    acol = lax.broadcasted_iota(jnp.int32, (LANES, 2 * LANES), 1)
    blk2 = (arow // d) == ((acol & (LANES - 1)) // d)
    ones_v = jnp.ones((L, LANES), BF16)

    def head_rows(ref, c, pr):
        top = jnp.broadcast_to(ref[c, 2 * pr:2 * pr + 1, :], (d, LANES))
        bot = jnp.broadcast_to(ref[c, 2 * pr + 1:2 * pr + 2, :], (d, LANES))
        half = jnp.concatenate([top, bot], axis=0)
        return jnp.concatenate([half, half], axis=1)

    def state_group(g, carry):
        local = []
        for cc in range(group):
            r0 = pl.multiple_of((g * group + cc) * L, L)
            for pr in range(pairs):
                ps = slice(LANES * pr, LANES * (pr + 1))
                kw = (k_scr[pl.ds(r0, L), ps].astype(F32) * w_scr[pl.ds(r0, L), ps]).astype(BF16)
                vo = jnp.concatenate([v_ref[pl.ds(r0, L), ps], ones_v], axis=1)
                local.append(jnp.where(blk2, _dot_tn(kw, vo), 0.0))
        state = list(carry)
        for cc in range(group):
            c = g * group + cc
            for pr in range(pairs):
                cprev_scr[c, pr] = state[pr].astype(BF16)
                state[pr] = head_rows(eo_scr, c, pr) * state[pr] + head_rows(el_scr, c, pr) * local[cc * pairs + pr]
        return tuple(state)

    lax.fori_loop(0, NC // group, state_group, tuple(jnp.zeros((LANES, 2 * LANES), F32) for _ in range(pairs)))

    li = lax.broadcasted_iota(jnp.int32, (L, L), 0)
    si = lax.broadcasted_iota(jnp.int32, (L, L), 1)
    causal = si <= li
    lane = lax.broadcasted_iota(jnp.int32, (L, LANES), 1)
    h0_lane = lane < d
    avg = jnp.where((li // d) == (si // d), 1.0 / d, 0.0).astype(BF16)

    def dot2(val, rhs):
        hi = val.astype(BF16)
        return _dot(hi, rhs) + _dot((val - hi.astype(F32)).astype(BF16), rhs)

    def out_group(g, carry):
        units = [(cc, pr) for cc in range(group) for pr in range(pairs)]
        chunk_of = {u_: g * group + u_[0] for u_ in units}
        rows = {u_: pl.multiple_of(chunk_of[u_] * L, L) for u_ in units}
        qc, sc, pv = {}, {}, {}
        for u_ in units:
            c, pr = chunk_of[u_], u_[1]
            ps = slice(LANES * pr, LANES * (pr + 1))
            qp = q_scr[pl.ds(rows[u_], L), ps]
            kp = k_scr[pl.ds(rows[u_], L), ps]
            qc[u_] = _dot(qp, cprev_scr[c, pr])
            sc[u_] = [_dot_nt(jnp.where(h0_lane if hh == 0 else lane >= d, qp, jnp.zeros_like(qp)), kp)
                      for hh in range(2)]
        for u_ in units:
            pr = u_[1]
            ps = slice(LANES * pr, LANES * (pr + 1))
            vo = jnp.concatenate([v_ref[pl.ds(rows[u_], L), ps], ones_v], axis=1)
            pv[u_] = []
            for hh in range(2):
                hd = 2 * pr + hh
                u_row = ut_scr[hd:hd + 1, pl.ds(rows[u_], L)]
                decay = jnp.where(causal, jnp.exp(u_row - mfull_scr[pl.ds(rows[u_], L), LANES * hd:LANES * (hd + 1)]), 0.0)
                pv[u_].append(_dot((sc[u_][hh] * decay).astype(BF16), vo))
        hg, cen = {}, {}
        for u_ in units:
            pr = u_[1]
            ps = slice(LANES * pr, LANES * (pr + 1))
            it = inter_scr[pl.ds(rows[u_], L), ps]
            num = it * qc[u_][:, :LANES] + jnp.where(h0_lane, pv[u_][0][:, :LANES], pv[u_][1][:, :LANES])
            den = it * qc[u_][:, LANES:] + jnp.where(h0_lane, pv[u_][0][:, LANES:], pv[u_][1][:, LANES:])
            hcell = num / jnp.maximum(jnp.abs(den), floor_scr[pl.ds(rows[u_], L), ps])
            hg[u_] = _sigmoid(o_ref[pl.ds(rows[u_], L), ps]) * hcell
        for u_ in units:
            cen[u_] = hg[u_] - dot2(hg[u_], avg)
        for u_ in units:
            pr = u_[1]
            ps = slice(LANES * pr, LANES * (pr + 1))
            var = dot2(cen[u_] * cen[u_], avg)
            y_ref[pl.ds(rows[u_], L), ps] = (cen[u_] * lax.rsqrt(var + NORM_EPS) * gmh_ref[:, ps]
                                             + skip_ref[:, ps] * xc_scr[pl.ds(rows[u_], L), ps])
        return carry

    lax.fori_loop(0, NC // group, out_group, 0)


def _mlstm(cx, mv, mo, gif_t, conv_w, conv_b, w_q_m, w_k_m, b_igate, b_fgate, g_mh, skip_m, B, S):
    def blockdiag(w, scale):
        out = jnp.zeros((WIDTH_C, WIDTH_C), F32)
        for hd in range(MLSTM_HEADS):
            out = out.at[hd * MLSTM_DIM:(hd + 1) * MLSTM_DIM, hd * MLSTM_DIM:(hd + 1) * MLSTM_DIM].set(w[hd] * scale)
        return out.astype(BF16)

    nc = S // MLSTM_TILE
    wq = blockdiag(w_q_m, MLSTM_DIM ** -0.5)
    wk = blockdiag(w_k_m, 1.0)
    bias8 = jnp.concatenate([b_igate, b_fgate])
    br = jnp.broadcast_to(bias8[:, None], (2 * MLSTM_HEADS, S))
    seq = lambda w: pl.BlockSpec((S, w), lambda b: (b, 0))
    full = lambda a: pl.BlockSpec(a.shape, lambda b: (0,) * a.ndim)
    row = lambda a: a.reshape(1, -1)
    args = [conv_w, row(conv_b), wq, wk, br, row(g_mh), row(skip_m)]
    tok = lambda w, dt: pltpu.VMEM((S, w), dt)
    per_chunk = pltpu.VMEM((nc, MLSTM_HEADS, LANES), F32)
    return pl.pallas_call(
        _mlstm_kernel,
        grid=(B,),
        in_specs=[seq(WIDTH_C), seq(WIDTH_C), seq(WIDTH_C),
                  pl.BlockSpec((2 * MLSTM_HEADS, S), lambda b: (0, b))] + [full(a) for a in args],
        out_specs=seq(WIDTH_C),
        out_shape=jax.ShapeDtypeStruct((B * S, WIDTH_C), F32),
        scratch_shapes=[tok(WIDTH_C, F32), tok(WIDTH_C, BF16), tok(WIDTH_C, BF16),
                        tok(WIDTH_C, F32), tok(WIDTH_C, F32), tok(WIDTH_C, F32), tok(4 * LANES, F32),
                        pltpu.VMEM((2 * MLSTM_HEADS, S), F32), per_chunk, per_chunk,
                        pltpu.VMEM((nc, MLSTM_HEADS // 2, LANES, 2 * LANES), BF16)],
        compiler_params=_params("parallel"),
        name="mlstm_mixer",
    )(cx, mv, mo, gif_t, *args)


def _mix_ffn_kernel(ya_ref, yb_ref, yc_ref, x_ref, gate1_ref, ga_ref, gb_ref, wo_ref,
                    sh_ref, sc_ref, gate_ref, g2_ref, wgu_ref, wd32_ref, gf_ref, o_ref, wd_ref, *, final_norm):
    @pl.when(pl.program_id(0) == 0)
    def _():
        wd_ref[...] = wd32_ref[...].astype(BF16)

    a = _rms(ya_ref[...], ga_ref[...]).astype(BF16)
    b = _rms(yb_ref[...], gb_ref[...]).astype(BF16)
    c = yc_ref[...].astype(BF16)
    mixed = (_dot(a, wo_ref[0:WIDTH_A, :]) + _dot(b, wo_ref[WIDTH_A:WIDTH_A + WIDTH_B, :])
             + _dot(c, wo_ref[WIDTH_A + WIDTH_B:, :]))
    x = x_ref[...] + gate1_ref[0] * mixed
    h = (_rms(x, g2_ref[...]) * (1.0 + sc_ref[0]) + sh_ref[0]).astype(BF16)
    acc = jnp.zeros(x.shape, F32)
    for j in range(FFN_HIDDEN // FFN_CHUNK):
        gate = _dot(h, wgu_ref[:, FFN_CHUNK * j:FFN_CHUNK * (j + 1)])
        up = _dot(h, wgu_ref[:, FFN_HIDDEN + FFN_CHUNK * j:FFN_HIDDEN + FFN_CHUNK * (j + 1)])
        act = (_silu(gate) * up).astype(BF16)
        acc = acc + _dot(act, wd_ref[FFN_CHUNK * j:FFN_CHUNK * (j + 1), :])
    y = x + gate_ref[0] * acc
    if final_norm:
        y = _rms(y, gf_ref[...])
    o_ref[...] = y


def _mix_ffn(ya, yb, yc, x2, gate1, ga, gb, w_out, shift, scale, gate, g2, wgu, wd, gf, S, final_norm):
    T = x2.shape[0]
    tm = TM_PROJ
    per_b = S // tm
    row = lambda w: pl.BlockSpec((tm, w), lambda i: (i, 0))
    full = lambda a: pl.BlockSpec(a.shape, lambda i: (0,) * a.ndim)
    modspec = pl.BlockSpec((1, 1, D_MODEL), lambda i: (i // per_b, 0, 0))
    return pl.pallas_call(
        functools.partial(_mix_ffn_kernel, final_norm=final_norm),
        grid=(T // tm,),
        in_specs=[row(WIDTH_A), row(WIDTH_B), row(WIDTH_C), row(D_MODEL), modspec, full(ga), full(gb), full(w_out),
                  modspec, modspec, modspec, full(g2), full(wgu), full(wd), full(gf)],
        out_specs=row(D_MODEL),
        out_shape=jax.ShapeDtypeStruct((T, D_MODEL), F32),
        scratch_shapes=[pltpu.VMEM(wd.shape, BF16)],
        compiler_params=_params("arbitrary"),
        name="mix_ffn_final" if final_norm else "mix_ffn",
    )(ya, yb, yc, x2, gate1, ga, gb, w_out, shift, scale, gate, g2, wgu, wd, gf)


def _head_tile_perm():
    idx = []
    for r in range(NSA_REP):
        idx += list(range(HEAD_DIM * r, HEAD_DIM * (r + 1)))
        idx += list(range(HEAD_DIM * (NSA_REP + r), HEAD_DIM * (NSA_REP + r + 1)))
    return np.asarray(idx, np.int32)


def _in_cols():
    std = np.full((N_STD,), -1, np.int64)
    tr = np.full((N_T,), -1, np.int64)
    off = 0
    tr[TSEG_Q:TSEG_Q + WIDTH_A] = _head_tile_perm()
    off += WIDTH_A
    std[SEG_CKV:SEG_CKV + 2 * LANES] = off + np.arange(2 * LANES)
    off += 2 * LANES
    std[SEG_SK:SEG_SK + LANES] = off + np.arange(LANES)
    off += LANES
    tr[TSEG_SV:TSEG_SV + LANES] = off + np.arange(LANES)
    off += LANES
    std[SEG_WK:SEG_WK + LANES] = off + np.arange(LANES)
    off += LANES
    tr[TSEG_WV:TSEG_WV + LANES] = off + np.arange(LANES)
    off += LANES
    tr[TSEG_G:TSEG_G + N_GATES] = off + np.arange(N_GATES)
    off += N_GATES
    std[SEG_QL:SEG_QL + MLA_Q_LORA] = off + np.arange(MLA_Q_LORA)
    off += MLA_Q_LORA
    std[SEG_KVL:SEG_KVL + MLA_KV_LORA] = off + np.arange(MLA_KV_LORA)
    off += MLA_KV_LORA
    std[SEG_KR + MLA_NOPE:SEG_KR + MLA_NOPE + MLA_ROPE] = off + np.arange(MLA_ROPE)
    off += MLA_ROPE
    for seg in (SEG_CX, SEG_MV, SEG_MO):
        std[seg:seg + WIDTH_C] = off + np.arange(WIDTH_C)
        off += WIDTH_C
    tr[TSEG_IF:TSEG_IF + 2 * MLSTM_HEADS] = off + np.arange(2 * MLSTM_HEADS)
    return std, tr


def _gather_cols(w, cols):
    pieces, start = [], 0
    for i in range(1, len(cols) + 1):
        run_ends = i == len(cols) or (cols[i] != cols[i - 1] + 1 if cols[i - 1] >= 0 else cols[i] >= 0)
        if run_ends:
            first = int(cols[start])
            pieces.append(w[:, first:first + i - start] if first >= 0 else jnp.zeros((w.shape[0], i - start), w.dtype))
            start = i
    return jnp.concatenate(pieces, axis=1)


def _layer_weights(l, w_in, w_uq, w_ukv, w_out, w_gu, w_down, g_out_a):
    std, tr = _in_cols()
    w_std = _gather_cols(w_in[l], std).astype(BF16)
    w_t = _gather_cols(w_in[l], tr).T.astype(BF16)
    cq = np.full((MLA_HEADS * LANES,), -1, np.int64)
    ck = np.full((MLA_HEADS * LANES,), -1, np.int64)
    cv = np.zeros((WIDTH_B,), np.int64)
    dq = MLA_NOPE + MLA_ROPE
    dkv = MLA_NOPE + MLA_V
    for hd in range(MLA_HEADS):
        cq[LANES * hd:LANES * hd + dq] = dq * hd + np.arange(dq)
        ck[LANES * hd:LANES * hd + MLA_NOPE] = dkv * hd + np.arange(MLA_NOPE)
        cv[MLA_V * hd:MLA_V * (hd + 1)] = dkv * hd + MLA_NOPE + np.arange(MLA_V)
    wuqT = _gather_cols(w_uq[l], cq).T.astype(BF16)
    wkm = _gather_cols(w_ukv[l], ck).astype(BF16)
    wvmT = _gather_cols(w_ukv[l], cv).T.astype(BF16)
    perm = _head_tile_perm()
    head_rows = [w_out[l][int(o):int(o) + HEAD_DIM] for o in perm[::HEAD_DIM]]
    wo = jnp.concatenate(head_rows + [w_out[l][WIDTH_A:]], axis=0).astype(BF16)
    ga = _gather_cols(g_out_a[l].reshape(1, WIDTH_A), perm)
    return w_std, w_t, wuqT, wkm, wvmT, wo, ga, w_gu[l].astype(BF16), w_down[l]


def kernel(x, c, positions, g_norm1, g_norm2, w_ada, b_ada, w_in, cmp_pos, w_cmp_k, w_cmp_v, g_out_a, g_q_lora, w_uq, g_kv_lora, w_ukv, g_out_b, conv_w, conv_b, w_q_m, w_k_m, b_igate, b_fgate, g_mh, skip_m, w_out, w_gu, w_down, g_final):
    B, S, D = x.shape
    T = B * S
    x2 = x.reshape(T, D)
    tabs = _rope_tables(positions)
    cn, sn, cm, sm = tabs
    tabs_t = (cn[:, :NSA_ROT_HALF].T, sn[:, :NSA_ROT_HALF].T,
              cm[:, MLA_NOPE:MLA_NOPE + MLA_ROPE // 2].T, sm[:, MLA_NOPE:MLA_NOPE + MLA_ROPE // 2].T)
    mod = _ada(c, w_ada, b_ada)
    row = lambda v: v.reshape(1, -1)
    for l in range(DEPTH):
        w_std, w_t, wuqT, wkm, wvmT, wo, ga, wgu, wd = _layer_weights(l, w_in, w_uq, w_ukv, w_out, w_gu, w_down, g_out_a)
        m6 = [mod[l, :, D * i:D * (i + 1)].reshape(B, 1, D) for i in range(6)]
        shift1, scale1, gate1, shift2, scale2, gate2 = m6
        (qaT, gT, ck, cv, sk_aug, wk, svT, wvT, qmT, km, vmT, cx, mv, mo, gif_t) = _inproj(
            x2, shift1, scale1, row(g_norm1[l]), w_std, w_t, tabs, tabs_t, row(g_q_lora[l]), wuqT,
            row(g_kv_lora[l]), wkm, wvmT, S)
        kc, vcT = _compress(ck, cv, cmp_pos[l], w_cmp_k[l], w_cmp_v[l], B, S)
        ya = _nsa(qaT, gT, kc, vcT, sk_aug, svT, wk, wvT, B, S)
        yb = _mla(qmT, km, vmT, B, S)
        yc = _mlstm(cx, mv, mo, gif_t, conv_w[l], conv_b[l], w_q_m[l], w_k_m[l], b_igate[l], b_fgate[l],
                    g_mh[l], skip_m[l], B, S)
        x2 = _mix_ffn(ya, yb, yc, x2, gate1, ga, row(g_out_b[l]), wo, shift2, scale2, gate2, row(g_norm2[l]),
                      wgu, wd, row(g_final), S, final_norm=(l == DEPTH - 1))
    return x2.reshape(B, S, D)
```

```python
import functools

import numpy as np
import jax
import jax.numpy as jnp
from jax import lax
from jax.experimental import pallas as pl
from jax.experimental.pallas import tpu as pltpu

F32 = jnp.float32
BF16 = jnp.bfloat16

D_MODEL = 1024
DEPTH = 2
HEAD_DIM = 64
ROPE_THETA = 500000.0
NSA_ROT_HALF = HEAD_DIM // 8
NORM_EPS = 1e-6

NSA_HEADS = 6
NSA_KV_HEADS = 2
NSA_REP = NSA_HEADS // NSA_KV_HEADS
CMP_LEN = 32
CMP_STRIDE = 16
SLC_LEN = 64
SLC_TOPK = 16
WINDOW = 512

MLA_HEADS = 6
MLA_Q_LORA = 256
MLA_KV_LORA = 128
MLA_NOPE = 64
MLA_ROPE = 32
MLA_V = 64

MLSTM_HEADS = 4
MLSTM_DIM = 64
MLSTM_CONV = 4

WIDTH_A = NSA_HEADS * HEAD_DIM
WIDTH_B = MLA_HEADS * MLA_V
WIDTH_C = MLSTM_HEADS * MLSTM_DIM
FFN_HIDDEN = 2816
N_GATES = 3 * NSA_HEADS
GATE_ROWS = 24

LANES = 128
NEG = -1e30
LOG2E = 1.4426950408889634
VMEM_LIMIT = 56 * 1024 * 1024

TM_PROJ = 512
TQ_NSA = 256
TK_SLC = 256
CHAIN_LANES = 256
TQ_MLA = 256
FFN_CHUNK = 256
MLSTM_TILE = 128
MLSTM_GROUP = 8
MXU_LOOKAHEAD = 4

SEG_CKV, SEG_SK, SEG_WK, SEG_QL, SEG_KVL, SEG_KR, SEG_CX, SEG_MV, SEG_MO = (
    0, 256, 384, 512, 768, 896, 1024, 1280, 1536)
N_STD = 1792
TSEG_Q, TSEG_G, TSEG_SV, TSEG_WV, TSEG_IF = 0, 384, 416, 544, 672
TSEG_G_ROWS = 32
TSEG_IF_ROWS = 16
N_T = 688


def _params(*sem):
    return pltpu.CompilerParams(dimension_semantics=sem, vmem_limit_bytes=VMEM_LIMIT)


def _dot(a, b):
    return jnp.dot(a, b, preferred_element_type=F32)


def _dot_nt(a, b):
    return lax.dot_general(a, b, (((1,), (1,)), ((), ())), preferred_element_type=F32)


def _dot_tn(a, b):
    return lax.dot_general(a, b, (((0,), (0,)), ((), ())), preferred_element_type=F32)


def _split3(x):
    hi = x.astype(BF16)
    r1 = x - hi.astype(F32)
    mid = r1.astype(BF16)
    lo = (r1 - mid.astype(F32)).astype(BF16)
    return hi, mid, lo


def _rms(x, g):
    return x * lax.rsqrt(jnp.mean(x * x, axis=-1, keepdims=True) + NORM_EPS) * g


def _sigmoid(x):
    return 1.0 / (1.0 + jnp.exp(-x))


def _silu(x):
    return x * _sigmoid(x)


def _rope(x, cos, sin, half, x1_lane):
    xr = jnp.where(x1_lane, -pltpu.roll(x, LANES - half, 1), pltpu.roll(x, half, 1))
    return x * cos + xr * sin


def _ada_kernel(c_ref, w_ref, b_ref, o_ref):
    c = c_ref[...]
    ca = _silu(c).astype(BF16)
    o_ref[0] = _dot(ca, w_ref[0].astype(BF16)) + b_ref[0]


def _ada(c, w_ada, b_ada):
    L, D, N = w_ada.shape
    B = c.shape[0]
    tn = 1536
    return pl.pallas_call(
        _ada_kernel,
        grid=(L, N // tn),
        in_specs=[pl.BlockSpec((B, D), lambda l, j: (0, 0)),
                  pl.BlockSpec((1, D, tn), lambda l, j: (l, 0, j)),
                  pl.BlockSpec((1, 1, tn), lambda l, j: (l, 0, j))],
        out_specs=pl.BlockSpec((1, B, tn), lambda l, j: (l, 0, j)),
        out_shape=jax.ShapeDtypeStruct((L, B, N), F32),
        compiler_params=_params("parallel", "parallel"),
        name="ada_mod",
    )(c, w_ada, b_ada.reshape(L, 1, N))


ROPE_FREQS = NSA_ROT_HALF + MLA_ROPE // 2


def _rope_kernel(pos_ref, inv_ref, spread_ref, one_ref, posr_ref, invc_ref,
                 cn_ref, sn_ref, cm_ref, sm_ref, cnt_ref, snt_ref, cmt_ref, smt_ref):
    ang = pos_ref[...] * inv_ref[...]
    lane = lax.broadcasted_iota(jnp.int32, (1, LANES), 1)

    def spread(val):
        hi, mid, lo = _split3(val)
        packed = jnp.where(lane < ROPE_FREQS, hi, jnp.where(lane < 2 * ROPE_FREQS, mid, lo))
        return _dot(packed, spread_ref[...])

    c = spread(jnp.cos(ang)) + one_ref[...]
    s = spread(jnp.sin(ang))
    cn_ref[...] = c[:, :LANES]
    cm_ref[...] = c[:, LANES:]
    sn_ref[...] = s[:, :LANES]
    sm_ref[...] = s[:, LANES:]
    tm = posr_ref.shape[1]
    ang_t = jnp.concatenate([invc_ref[...]] * (tm // LANES), axis=1) * jnp.broadcast_to(posr_ref[...], (4 * NSA_ROT_HALF, tm))
    c_t, s_t = jnp.cos(ang_t), jnp.sin(ang_t)
    cnt_ref[...] = c_t[:NSA_ROT_HALF]
    snt_ref[...] = s_t[:NSA_ROT_HALF]
    cmt_ref[...] = c_t[NSA_ROT_HALF:ROPE_FREQS]
    smt_ref[...] = s_t[NSA_ROT_HALF:ROPE_FREQS]


def _rope_tables(positions):
    T = positions.size
    inv_n = jnp.power(ROPE_THETA, -jnp.arange(0, 2 * NSA_ROT_HALF, 2, dtype=F32) / (2 * NSA_ROT_HALF))
    inv_m = jnp.power(ROPE_THETA, -jnp.arange(0, MLA_ROPE, 2, dtype=F32) / MLA_ROPE)
    copies = 3
    inv = jnp.concatenate([inv_n, inv_m] * copies + [jnp.zeros((LANES - copies * ROPE_FREQS,), F32)]).reshape(1, LANES)
    onehot = np.zeros((LANES, 2 * LANES), np.float32)
    rotated = np.zeros((1, 2 * LANES), np.float32)
    for k in range(copies):
        for f in range(NSA_ROT_HALF):
            for lane in (f, f + NSA_ROT_HALF, HEAD_DIM + f, HEAD_DIM + f + NSA_ROT_HALF):
                onehot[ROPE_FREQS * k + f, lane] = 1.0
                rotated[0, lane] = 1.0
        for f in range(MLA_ROPE // 2):
            for lane in (MLA_NOPE + f, MLA_NOPE + MLA_ROPE // 2 + f):
                onehot[ROPE_FREQS * k + NSA_ROT_HALF + f, LANES + lane] = 1.0
                rotated[0, LANES + lane] = 1.0
    posf = positions.reshape(T).astype(F32)
    posb = jnp.broadcast_to(posf[:, None], (T, LANES))
    rows_t = 4 * NSA_ROT_HALF
    inv_col = jnp.concatenate([inv_n, inv_m, jnp.zeros((rows_t - ROPE_FREQS,), F32)])
    inv_col = jnp.broadcast_to(inv_col[:, None], (rows_t, LANES))
    tm = 2048
    spec = pl.BlockSpec((tm, LANES), lambda i: (i, 0))
    col = lambda h: pl.BlockSpec((h, tm), lambda i: (0, i))
    full = lambda shape: pl.BlockSpec(shape, lambda i: (0, 0))
    sds = jax.ShapeDtypeStruct
    outs = pl.pallas_call(
        _rope_kernel,
        grid=(T // tm,),
        in_specs=[spec, full((1, LANES)), full((LANES, 2 * LANES)), full((1, 2 * LANES)), col(1), full((rows_t, LANES))],
        out_specs=[spec] * 4 + [col(NSA_ROT_HALF)] * 2 + [col(MLA_ROPE // 2)] * 2,
        out_shape=[sds((T, LANES), F32)] * 4 + [sds((NSA_ROT_HALF, T), F32)] * 2 + [sds((MLA_ROPE // 2, T), F32)] * 2,
        compiler_params=_params("parallel"),
        name="rope_tables",
    )(posb, inv, jnp.asarray(onehot, BF16), jnp.asarray(1.0 - rotated), posf.reshape(1, T), inv_col)
    return outs[:4], outs[4:]


def _rope_rows(t, offset, half, cos, sin):
    x1, x2 = t[offset:offset + half], t[offset + half:offset + 2 * half]
    return x1 * cos - x2 * sin, x2 * cos + x1 * sin


def _inproj_kernel(x_ref, sh_ref, sc_ref, g1_ref, ws_ref, wt_ref, cn_ref, sn_ref, cm_ref, sm_ref,
                   cnT_ref, snT_ref, cmT_ref, smT_ref, gq_ref, wuqT_ref, gkv_ref, wkm_ref, wvmT_ref,
                   qaT_ref, gT_ref, ck_ref, cv_ref, ska_ref, wk_ref, svT_ref, wvT_ref,
                   qmT_ref, km_ref, vmT_ref, cx_ref, mv_ref, mo_ref, gif_ref, *, per_b):
    tm = x_ref.shape[0]
    x = x_ref[...]
    h = _rms(x, g1_ref[...]) * (1.0 + sc_ref[0]) + sh_ref[0]
    hb = h.astype(BF16)

    def seg(start, width):
        return _dot(hb, ws_ref[:, start:start + width])

    def seg_t(start, height):
        return _dot_nt(wt_ref[start:start + height, :], hb)

    lane = lax.broadcasted_iota(jnp.int32, (1, LANES), 1)
    x1_n = (lane % HEAD_DIM) < NSA_ROT_HALF
    x1_m = lane < MLA_NOPE + MLA_ROPE // 2
    cn, sn, cm, sm = cn_ref[...], sn_ref[...], cm_ref[...], sm_ref[...]
    rope_n = lambda t: _rope(t, cn, sn, NSA_ROT_HALF, x1_n)
    rope_m = lambda t: _rope(t, cm, sm, MLA_ROPE // 2, x1_m)

    qn = _rms(seg(SEG_QL, MLA_Q_LORA), gq_ref[...]).astype(BF16)
    kvl_kr = seg(SEG_KVL, 2 * LANES)
    kvn = _rms(kvl_kr[:, :LANES], gkv_ref[...]).astype(BF16)
    kr = rope_m(kvl_kr[:, LANES:])

    qt = seg_t(TSEG_Q, WIDTH_A)
    cnt, snt = cnT_ref[...], snT_ref[...]
    parts = []
    for hd in range(NSA_HEADS):
        o = HEAD_DIM * hd
        parts += list(_rope_rows(qt, o, NSA_ROT_HALF, cnt, snt)) + [qt[o + 2 * NSA_ROT_HALF:o + HEAD_DIM]]
    qaT_ref[...] = (jnp.concatenate(parts, axis=0) * (HEAD_DIM ** -0.5 * LOG2E)).astype(BF16)
    gT_ref[...] = _sigmoid(seg_t(TSEG_G, TSEG_G_ROWS))[:GATE_ROWS]
    svT_ref[...] = seg_t(TSEG_SV, LANES).astype(BF16)
    wvT_ref[...] = seg_t(TSEG_WV, LANES).astype(BF16)

    ckv = seg(SEG_CKV, 2 * LANES)
    ck_ref[...] = rope_n(ckv[:, :LANES])
    cv_ref[...] = ckv[:, LANES:]
    ska_ref[:, :LANES] = rope_n(seg(SEG_SK, LANES)).astype(BF16)
    srow = (pl.program_id(0) % per_b) * tm + lax.broadcasted_iota(jnp.int32, (tm, LANES), 0)
    lane2 = lax.broadcasted_iota(jnp.int32, (tm, LANES), 1)
    code = (lane2 < 4 * SLC_TOPK) & ((lane2 & (2 * SLC_TOPK - 1)) == srow // SLC_LEN)
    ska_ref[:, LANES:] = jnp.where(code, 1.0, 0.0).astype(BF16)
    wk_ref[...] = rope_n(seg(SEG_WK, LANES)).astype(BF16)

    cx_ref[...] = seg(SEG_CX, WIDTH_C)
    mv_ref[...] = seg(SEG_MV, WIDTH_C).astype(BF16)
    mo_ref[...] = seg(SEG_MO, WIDTH_C)
    gif_ref[...] = seg_t(TSEG_IF, TSEG_IF_ROWS)[:2 * MLSTM_HEADS]

    qmt = _dot_nt(wuqT_ref[...], qn)
    cmt, smt = cmT_ref[...], smT_ref[...]
    parts = []
    for hd in range(MLA_HEADS):
        o = LANES * hd
        parts += [qmt[o:o + MLA_NOPE]] + list(_rope_rows(qmt, o + MLA_NOPE, MLA_ROPE // 2, cmt, smt))
        parts += [qmt[o + MLA_NOPE + MLA_ROPE:o + LANES]]
    qmT_ref[...] = (jnp.concatenate(parts, axis=0) * ((MLA_NOPE + MLA_ROPE) ** -0.5 * LOG2E)).astype(BF16)
    kk = _dot(kvn, wkm_ref[...])
    for hd in range(MLA_HEADS):
        km_ref[:, LANES * hd:LANES * (hd + 1)] = (kk[:, LANES * hd:LANES * (hd + 1)] + kr).astype(BF16)
    vmT_ref[...] = _dot_nt(wvmT_ref[...], kvn).astype(BF16)


N_MOD = 6


def _mod_spec(per_b, first, k):
    return pl.BlockSpec((1, 1, D_MODEL), lambda i: (first + (i // per_b) * N_MOD + k, 0, 0))


def _inproj(x2, mod3, mod_first, g1, w_std, w_t, tabs, tabs_t, gq, wuqT, gkv, wkm, wvmT, S):
    T = x2.shape[0]
    tm = TM_PROJ
    per_b = S // tm
    row = lambda w: pl.BlockSpec((tm, w), lambda i: (i, 0))
    col = lambda h: pl.BlockSpec((h, tm), lambda i: (0, i))
    full = lambda a: pl.BlockSpec(a.shape, lambda i: (0,) * a.ndim)
    outs = [(WIDTH_A, BF16, True), (GATE_ROWS, F32, True), (LANES, F32, False), (LANES, F32, False), (2 * LANES, BF16, False),
            (LANES, BF16, False), (LANES, BF16, True), (LANES, BF16, True),
            (MLA_HEADS * LANES, BF16, True), (MLA_HEADS * LANES, BF16, False), (WIDTH_B, BF16, True),
            (WIDTH_C, F32, False), (WIDTH_C, BF16, False), (WIDTH_C, F32, False), (2 * MLSTM_HEADS, F32, True)]
    return pl.pallas_call(
        functools.partial(_inproj_kernel, per_b=per_b),
        grid=(T // tm,),
        in_specs=[row(D_MODEL), _mod_spec(per_b, mod_first, 0), _mod_spec(per_b, mod_first, 1),
                  full(g1), full(w_std), full(w_t)] + [row(LANES)] * 4
                 + [col(t.shape[0]) for t in tabs_t] + [full(gq), full(wuqT), full(gkv), full(wkm), full(wvmT)],
        out_specs=[col(w) if tr else row(w) for w, _, tr in outs],
        out_shape=[jax.ShapeDtypeStruct((w, T) if tr else (T, w), dt) for w, dt, tr in outs],
        compiler_params=_params("parallel"),
        name="in_proj",
    )(x2, mod3, mod3, g1, w_std, w_t, *tabs, *tabs_t, gq, wuqT, gkv, wkm, wvmT)


def _compress_kernel(xk_ref, xv_ref, wk_ref, wv_ref, pos_ref, wkf_ref, wvf_ref, kc_ref, vcT_ref):
    ng = xk_ref.shape[0] // CMP_STRIDE
    row = lax.broadcasted_iota(jnp.int32, (ng, LANES), 0)
    pos = pos_ref[...].astype(BF16)
    acc_k = jnp.zeros((ng, 2 * LANES), F32)
    acc_v = jnp.zeros((ng, 2 * LANES), F32)
    for t in range(CMP_STRIDE):
        tok = pl.ds(t, ng, stride=CMP_STRIDE)
        acc_k = acc_k + _dot(xk_ref[tok, :].astype(BF16), wk_ref[t])
        acc_v = acc_v + _dot(xv_ref[tok, :].astype(BF16), wv_ref[t])

    def finish(acc, wf_ref):
        const = _dot(pos, wf_ref[...].astype(BF16))
        both = acc[:, :LANES] + pltpu.roll(acc[:, LANES:], ng - 1, 0) + jnp.concatenate([const, const], axis=1)
        return jnp.where(row < ng - 1, both, 0.0)

    kc_ref[0] = finish(acc_k, wkf_ref).astype(BF16)
    vcT_ref[0] = finish(acc_v, wvf_ref).T.astype(BF16)


def _compress(ck, cv, cmp_pos, w_cmp_k, w_cmp_v, B, S):
    ng = S // CMP_STRIDE

    def per_token(w):
        a = w[:CMP_STRIDE * HEAD_DIM].reshape(CMP_STRIDE, HEAD_DIM, HEAD_DIM)
        b = w[CMP_STRIDE * HEAD_DIM:].reshape(CMP_STRIDE, HEAD_DIM, HEAD_DIM)
        z = jnp.zeros_like(a)
        top = jnp.concatenate([a, z, b, z], axis=2)
        bot = jnp.concatenate([z, a, z, b], axis=2)
        return jnp.concatenate([top, bot], axis=1).astype(BF16)

    full = lambda a: pl.BlockSpec(a.shape, lambda b: (0,) * a.ndim)
    wk3, wv3 = per_token(w_cmp_k), per_token(w_cmp_v)
    posf = cmp_pos.reshape(1, CMP_LEN * HEAD_DIM)
    ospec = pl.BlockSpec((1, ng, LANES), lambda b: (b, 0, 0))
    return pl.pallas_call(
        _compress_kernel,
        grid=(B,),
        in_specs=[pl.BlockSpec((S, LANES), lambda b: (b, 0)), pl.BlockSpec((S, LANES), lambda b: (b, 0)),
                  full(wk3), full(wv3), full(posf), full(w_cmp_k), full(w_cmp_v)],
        out_specs=[ospec, ospec],
        out_shape=[jax.ShapeDtypeStruct((B, ng, LANES), BF16)] * 2,
        compiler_params=_params("parallel"),
        name="nsa_compress",
    )(ck, cv, wk3, wv3, posf, w_cmp_k, w_cmp_v)


def _nsa_kernel(qT_ref, gT_ref, kc_ref, vcT_ref, sk_ref, svT_ref, wk_ref, wvT_ref, o_ref, qaug_scr, acc_scr, s_scr):
    tq = TQ_NSA
    cols = NSA_HEADS * tq
    pair = CHAIN_LANES
    t0 = pl.program_id(1) * tq
    n_slc = SLC_TOPK * 2

    frow = lax.broadcasted_iota(jnp.int32, (LANES, tq), 0)
    g0_row = frow < HEAD_DIM
    tiles = [qT_ref[LANES * r:LANES * (r + 1), :] for r in range(NSA_REP)]
    zero = jnp.zeros_like(tiles[0])
    q6 = jnp.concatenate([jnp.where(g0_row, t, zero) for t in tiles]
                         + [jnp.where(g0_row, zero, t) for t in tiles], axis=1)
    qaug_scr[0:LANES, :] = q6
    tq_l = t0 + (lax.broadcasted_iota(jnp.int32, (1, cols), 1) & (tq - 1))

    pairs = [slice(pair * pp, pair * (pp + 1)) for pp in range(cols // pair)]
    vrows = lambda pp: slice(HEAD_DIM * (pp // NSA_REP), HEAD_DIM * (pp // NSA_REP + 1))
    s = _dot(kc_ref[0], q6)
    span = WINDOW + tq
    start = pl.multiple_of(jnp.maximum(t0 - WINDOW, 0), tq)
    kw = wk_ref[pl.ds(start, span), :]
    win_scores = [_dot(kw, q6[:, sl]) for sl in pairs]

    tq_1 = tq_l[:, :tq]
    nrow = lax.broadcasted_iota(jnp.int32, (LANES, tq), 0)
    cmp_bias = jnp.where(nrow * CMP_STRIDE + (CMP_LEN - 1) <= tq_1, 0.0, NEG)
    s = s + jnp.concatenate([cmp_bias] * NSA_HEADS, axis=1)
    e = jnp.exp2(s - jnp.max(s, axis=0, keepdims=True))
    seen = jnp.where(tq_l >= CMP_LEN - 1, 1.0, 0.0)
    p = e * (seen / jnp.sum(e, axis=0, keepdims=True))
    p_b = p.astype(BF16)
    o_cmp = [_dot(vcT_ref[0, vrows(pp), :], p_b[:, sl]) for pp, sl in enumerate(pairs)]

    jr = lax.broadcasted_iota(jnp.int32, (n_slc, LANES), 0)
    nc = lax.broadcasted_iota(jnp.int32, (n_slc, LANES), 1)
    ovl = ((nc * CMP_STRIDE < jr * SLC_LEN + SLC_LEN) & (nc * CMP_STRIDE + CMP_LEN > jr * SLC_LEN)
           & (nc < LANES - 1))
    ovl = jnp.where(ovl, 1.0, 0.0).astype(BF16)
    jq = lax.broadcasted_iota(jnp.int32, (n_slc, tq), 0)
    tl = t0 + lax.broadcasted_iota(jnp.int32, (n_slc, tq), 1)
    cur = tl // SLC_LEN
    forced = (jq == 0) | (jq == cur) | (jq == cur - 1)
    future = jq * SLC_LEN > tl
    bias_t = []
    for g in range(NSA_KV_HEADS):
        pg = p[:, (3 * g) * tq:(3 * g + 1) * tq] + p[:, (3 * g + 1) * tq:(3 * g + 2) * tq] + p[:, (3 * g + 2) * tq:(3 * g + 3) * tq]
        imp = sum(_dot(ovl, part) for part in _split3(pg))
        imp = jnp.where(forced, jnp.inf, imp)
        imp = jnp.where(future, -jnp.inf, imp)
        rows8 = [imp[8 * r:8 * (r + 1)] for r in range(n_slc // 8)]
        sub = lax.broadcasted_iota(jnp.int32, (8, tq), 0)
        ranks = [jnp.zeros((8, tq), F32) for _ in rows8]
        for jp in range(n_slc):
            rv = imp[jp:jp + 1, :]
            for r, blk in enumerate(rows8):
                if 8 * r > jp:
                    ahead = rv >= blk
                elif 8 * r + 7 < jp:
                    ahead = rv > blk
                else:
                    ahead = (rv > blk) | ((rv == blk) & (sub > jp - 8 * r))
                ranks[r] = ranks[r] + jnp.where(ahead, 1.0, 0.0)
        rank = jnp.concatenate(ranks, axis=0)
        bias_t.append(jnp.where(rank < float(SLC_TOPK), 0.0, NEG).astype(BF16))
    zb = jnp.zeros((n_slc, NSA_REP * tq), BF16)
    qaug_scr[LANES:LANES + n_slc, :] = jnp.concatenate([bias_t[0]] * NSA_REP + [zb], axis=1)
    qaug_scr[LANES + n_slc:LANES + 2 * n_slc, :] = jnp.concatenate([zb] + [bias_t[1]] * NSA_REP, axis=1)
    qaug_scr[LANES + 2 * n_slc:, :] = jnp.zeros((LANES - 2 * n_slc, cols), BF16)

    vwt = wvT_ref[:, pl.ds(start, span)]
    wrow = start + lax.broadcasted_iota(jnp.int32, (span, pair), 0)
    o_win = []
    for pp, (sl, sc) in enumerate(zip(pairs, win_scores)):
        in_window = lax.bitcast_convert_type(tq_l[:, sl] - wrow, jnp.uint32) < jnp.uint32(WINDOW)
        sc = jnp.where(in_window, sc, NEG)
        ew = jnp.exp2(sc - jnp.max(sc, axis=0, keepdims=True))
        o_win.append(_dot(vwt[vrows(pp)], ew.astype(BF16)) / jnp.sum(ew, axis=0, keepdims=True))

    acc_scr[...] = jnp.zeros((HEAD_DIM, cols), F32)
    krow = lax.broadcasted_iota(jnp.int32, (TK_SLC, pair), 0)

    def slc_scores(k0, sl):
        return _dot(sk_ref[pl.ds(k0, TK_SLC), :], qaug_scr[:, sl])

    nch = len(pairs)
    la = s_scr.shape[0]
    for pp in range(la):
        s_scr[pp] = slc_scores(0, pairs[pp])

    def slc_tile(k0, m, l, masked):
        vt = svT_ref[:, pl.ds(k0, TK_SLC)]
        ms, ls, accs = [], [], []
        ahead = [s_scr[pp] for pp in range(la)]
        for pp, sl in enumerate(pairs):
            sc = ahead.pop(0)
            nxt = pp + la
            if nxt < nch:
                ahead.append(slc_scores(k0, pairs[nxt]))
            elif not masked:
                s_scr[nxt - nch] = slc_scores(pl.multiple_of(k0 + TK_SLC, TK_SLC), pairs[nxt - nch])
            if masked:
                sc = jnp.where(k0 + krow <= tq_l[:, sl], sc, NEG)
            m_new = jnp.maximum(m[:, sl], jnp.max(sc, axis=0, keepdims=True))
            alpha = jnp.exp2(m[:, sl] - m_new)
            pe = jnp.exp2(sc - m_new)
            ls.append(alpha * l[:, sl] + jnp.sum(pe, axis=0, keepdims=True))
            accs.append(alpha * acc_scr[:, sl] + _dot(vt[vrows(pp)], pe.astype(BF16)))
            ms.append(m_new)
        acc_scr[...] = jnp.concatenate(accs, axis=1)
        return jnp.concatenate(ms, axis=1), jnp.concatenate(ls, axis=1)

    def slc_step(kt, carry):
        return slc_tile(pl.multiple_of(kt * TK_SLC, TK_SLC), carry[0], carry[1], False)

    n_full = t0 // TK_SLC
    m, l = lax.fori_loop(0, n_full, slc_step, (jnp.full((1, cols), NEG, F32), jnp.zeros((1, cols), F32)))
    m, l = slc_tile(pl.multiple_of(n_full * TK_SLC, TK_SLC), m, l, True)
    o_slc = acc_scr[...] / l

    gt = gT_ref[...]
    mixed = []
    for hd, sl in enumerate(pairs):
        gate = lambda j: gt[3 * hd + j:3 * hd + j + 1, :]
        mixed.append(gate(0) * o_cmp[hd] + gate(1) * o_slc[:, sl] + gate(2) * o_win[hd])
    for r in range(NSA_REP):
        o_ref[:, LANES * r:LANES * (r + 1)] = jnp.concatenate([mixed[r], mixed[NSA_REP + r]], axis=0).T


def _nsa(qaT, gT, kc, vcT, sk_aug, svT, wk, wvT, B, S):
    tq = TQ_NSA
    assert CHAIN_LANES == tq, "the kernel treats one softmax chain as one head"
    nq = S // tq
    cols = NSA_HEADS * tq
    qcol =lambda h: pl.BlockSpec((h, tq), lambda b, i: (0, b * nq + i))
    seq = lambda w: pl.BlockSpec((S, w), lambda b, i: (b, 0))
    seqT = pl.BlockSpec((LANES, S), lambda b, i: (0, b))
    cspec = pl.BlockSpec((1, LANES, LANES), lambda b, i: (b, 0, 0))
    return pl.pallas_call(
        _nsa_kernel,
        grid=(B, nq),
        in_specs=[qcol(WIDTH_A), qcol(gT.shape[0]), cspec, cspec, seq(2 * LANES), seqT, seq(LANES), seqT],
        out_specs=pl.BlockSpec((tq, WIDTH_A), lambda b, i: (b * nq + i, 0)),
        out_shape=jax.ShapeDtypeStruct((B * S, WIDTH_A), F32),
        scratch_shapes=[pltpu.VMEM((2 * LANES, cols), BF16), pltpu.VMEM((HEAD_DIM, cols), F32),
                        pltpu.VMEM((cols // CHAIN_LANES, TK_SLC, CHAIN_LANES), F32)],
        compiler_params=_params("parallel", "arbitrary"),
        name="nsa_attention",
    )(qaT, gT, kc, vcT, sk_aug, svT, wk, wvT)


def _mla_kernel(qT_ref, k_ref, vT_ref, o_ref, acc_scr, s_scr):
    tq = TQ_MLA
    t0 = pl.program_id(1) * tq
    tq_l = t0 + lax.broadcasted_iota(jnp.int32, (1, tq), 1)
    krow = lax.broadcasted_iota(jnp.int32, (tq, tq), 0)
    acc_scr[...] = jnp.zeros((MLA_HEADS, MLA_V, tq), F32)

    def scores(k0, hd):
        k = k_ref[pl.ds(k0, tq), LANES * hd:LANES * (hd + 1)]
        return _dot(k, qT_ref[LANES * hd:LANES * (hd + 1), :])

    for hd in range(MXU_LOOKAHEAD):
        s_scr[hd] = scores(0, hd)

    def tile(k0, ms, ls, masked):
        new_m, new_l, accs = [], [], []
        ahead = [s_scr[hd] for hd in range(MXU_LOOKAHEAD)]
        for hd in range(MLA_HEADS):
            sc = ahead.pop(0)
            nxt = hd + MXU_LOOKAHEAD
            if nxt < MLA_HEADS:
                ahead.append(scores(k0, nxt))
            elif not masked:
                s_scr[nxt - MLA_HEADS] = scores(pl.multiple_of(k0 + tq, tq), nxt - MLA_HEADS)
            if masked:
                sc = jnp.where(k0 + krow <= tq_l, sc, NEG)
            m_new = jnp.maximum(ms[hd], jnp.max(sc, axis=0, keepdims=True))
            alpha = jnp.exp2(ms[hd] - m_new)
            pe = jnp.exp2(sc - m_new)
            new_l.append(alpha * ls[hd] + jnp.sum(pe, axis=0, keepdims=True))
            vt = vT_ref[MLA_V * hd:MLA_V * (hd + 1), pl.ds(k0, tq)]
            accs.append(alpha * acc_scr[hd] + _dot(vt, pe.astype(BF16)))
            new_m.append(m_new)
        acc_scr[...] = jnp.stack(accs)
        return tuple(new_m), tuple(new_l)

    def step(kt, carry):
        return tile(pl.multiple_of(kt * tq, tq), carry[0], carry[1], False)

    n_full = pl.program_id(1)
    init = (tuple(jnp.full((1, tq), NEG, F32) for _ in range(MLA_HEADS)),
            tuple(jnp.zeros((1, tq), F32) for _ in range(MLA_HEADS)))
    ms, ls = lax.fori_loop(0, n_full, step, init)
    ms, ls = tile(pl.multiple_of(n_full * tq, tq), ms, ls, True)
    for pr in range(MLA_HEADS // 2):
        t = jnp.concatenate([acc_scr[2 * pr] / ls[2 * pr], acc_scr[2 * pr + 1] / ls[2 * pr + 1]], axis=0)
        o_ref[:, LANES * pr:LANES * (pr + 1)] = t.T


def _mla(qmT, km, vmT, B, S):
    tq = TQ_MLA
    nq = S // tq
    return pl.pallas_call(
        _mla_kernel,
        grid=(B, nq),
        in_specs=[pl.BlockSpec((MLA_HEADS * LANES, tq), lambda b, i: (0, b * nq + i)),
                  pl.BlockSpec((S, MLA_HEADS * LANES), lambda b, i: (b, 0)),
                  pl.BlockSpec((WIDTH_B, S), lambda b, i: (0, b))],
        out_specs=pl.BlockSpec((tq, WIDTH_B), lambda b, i: (b * nq + i, 0)),
        out_shape=jax.ShapeDtypeStruct((B * S, WIDTH_B), F32),
        scratch_shapes=[pltpu.VMEM((MLA_HEADS, MLA_V, tq), F32), pltpu.VMEM((MXU_LOOKAHEAD, tq, tq), F32)],
        compiler_params=_params("parallel", "arbitrary"),
        name="mla_attention",
    )(qmT, km, vmT)


def _log_sigmoid(z):
    return jnp.minimum(z, 0.0) - jnp.log1p(jnp.exp(-jnp.abs(z)))


def _mlstm_kernel(cx_ref, v_ref, o_ref, g_ref, cw_ref, cb_ref, wq_ref, wk_ref, br_ref, gmh_ref, skip_ref, y_ref,
                  xc_scr, q_scr, k_scr, w_scr, inter_scr, floor_scr, mfull_scr, ut_scr, eo_scr, el_scr, cprev_scr):
    S = cx_ref.shape[0]
    L = MLSTM_TILE
    NC = S // L
    d = MLSTM_DIM
    pairs = MLSTM_HEADS // 2
    group = MLSTM_GROUP

    x = cx_ref[...]
    rowi = lax.broadcasted_iota(jnp.int32, (S, WIDTH_C), 0)
    conv = x * cw_ref[MLSTM_CONV - 1:MLSTM_CONV, :]
    for back in range(1, MLSTM_CONV):
        shifted = jnp.where(rowi >= back, pltpu.roll(x, back, 0), 0.0)
        conv = conv + shifted * cw_ref[MLSTM_CONV - 1 - back:MLSTM_CONV - back, :]
    xc = _silu(conv + cb_ref[...])
    xc_scr[...] = xc
    xcb = xc.astype(BF16)
    q_scr[...] = _dot(xcb, wq_ref[...]).astype(BF16)
    k_scr[...] = _dot(xcb, wk_ref[...]).astype(BF16)

    nh = MLSTM_HEADS
    lane_in_chunk = lax.broadcasted_iota(jnp.int32, (nh, S), 1) & (L - 1)

    def scan_lanes(val, op, fill):
        sh = 1
        while sh < L:
            val = op(val, jnp.where(lane_in_chunk >= sh, pltpu.roll(val, sh, 1), fill))
            sh *= 2
        return val

    gt = g_ref[...] + br_ref[...]
    ig = gt[0:nh]
    b = scan_lanes(_log_sigmoid(gt[nh:]), jnp.add, 0.0)
    u = ig - b
    cmu = scan_lanes(u, jnp.maximum, -jnp.inf)
    ut_scr[0:nh, :] = u

    m = jnp.zeros((nh, 1), F32)
    w_loc, inter, floor, m_tok = [], [], [], []
    for c in range(NC):
        blk = slice(L * c, L * (c + 1))
        b_last, u_max = b[:, L * (c + 1) - 1:L * (c + 1)], cmu[:, L * (c + 1) - 1:L * (c + 1)]
        m_top = jnp.maximum(m, u_max)
        eo_scr[c] = jnp.broadcast_to(jnp.exp(m - m_top), (nh, LANES))
        el_scr[c] = jnp.broadcast_to(jnp.exp(u_max - m_top), (nh, LANES))
        mt = jnp.maximum(cmu[:, blk], m)
        m_tok.append(mt)
        inter.append(jnp.exp(m - mt))
        w_loc.append(jnp.exp(u[:, blk] - u_max))
        floor.append(jnp.exp(-(b[:, blk] + mt)))
        m = b_last + m_top

    hrow = lax.broadcasted_iota(jnp.int32, (4 * nh, 2 * LANES), 0)
    hcol = lax.broadcasted_iota(jnp.int32, (4 * nh, 2 * LANES), 1)
    to_heads = jnp.where((hrow < 3 * nh) & (hcol // d == hrow % nh), 1.0, 0.0).astype(BF16)
    frow = lax.broadcasted_iota(jnp.int32, (4 * nh, 4 * LANES), 0)
    fcol = lax.broadcasted_iota(jnp.int32, (4 * nh, 4 * LANES), 1)
    to_full = jnp.where((frow < 3 * nh) & (fcol // LANES == frow % nh), 1.0, 0.0).astype(BF16)

    def spread(chunks, onehot):
        val = jnp.concatenate(chunks, axis=1)
        hi, mid, lo = _split3(val)
        pieces = jnp.concatenate([hi.astype(F32), mid.astype(F32), lo.astype(F32), jnp.zeros_like(val)], axis=0)
        return _dot_tn(pieces.astype(BF16), onehot)

    w_scr[...] = spread(w_loc, to_heads)
    inter_scr[...] = spread(inter, to_heads)
    floor_scr[...] = spread(floor, to_heads)
    mfull_scr[...] = spread(m_tok, to_full)

    arow = lax.broadcasted_iota(jnp.int32, (LANES, 2 * LANES), 0)
    acol = lax.broadcasted_iota(jnp.int32, (LANES, 2 * LANES), 1)
    blk2 = (arow // d) == ((acol & (LANES - 1)) // d)
    ones_v = jnp.ones((L, LANES), BF16)

    def head_rows(ref, c, pr):
        top = jnp.broadcast_to(ref[c, 2 * pr:2 * pr + 1, :], (d, LANES))
        bot = jnp.broadcast_to(ref[c, 2 * pr + 1:2 * pr + 2, :], (d, LANES))
        half = jnp.concatenate([top, bot], axis=0)
        return jnp.concatenate([half, half], axis=1)

    def state_group(g, carry):
        local = []
        for cc in range(group):
            r0 = pl.multiple_of((g * group + cc) * L, L)
            for pr in range(pairs):
                ps = slice(LANES * pr, LANES * (pr + 1))
                kw = (k_scr[pl.ds(r0, L), ps].astype(F32) * w_scr[pl.ds(r0, L), ps]).astype(BF16)
                vo = jnp.concatenate([v_ref[pl.ds(r0, L), ps], ones_v], axis=1)
                local.append(jnp.where(blk2, _dot_tn(kw, vo), 0.0))
        state = list(carry)
        for cc in range(group):
            c = g * group + cc
            for pr in range(pairs):
                cprev_scr[c, pr] = state[pr].astype(BF16)
                state[pr] = head_rows(eo_scr, c, pr) * state[pr] + head_rows(el_scr, c, pr) * local[cc * pairs + pr]
        return tuple(state)

    lax.fori_loop(0, NC // group, state_group, tuple(jnp.zeros((LANES, 2 * LANES), F32) for _ in range(pairs)))

    li = lax.broadcasted_iota(jnp.int32, (L, L), 0)
    si = lax.broadcasted_iota(jnp.int32, (L, L), 1)
    causal = si <= li
    lane = lax.broadcasted_iota(jnp.int32, (L, LANES), 1)
    h0_lane = lane < d
    avg = jnp.where((li // d) == (si // d), 1.0 / d, 0.0).astype(BF16)

    def dot2(val, rhs):
        hi = val.astype(BF16)
        return _dot(hi, rhs) + _dot((val - hi.astype(F32)).astype(BF16), rhs)

    def out_group(g, carry):
        units = [(cc, pr) for cc in range(group) for pr in range(pairs)]
        chunk_of = {u_: g * group + u_[0] for u_ in units}
        rows = {u_: pl.multiple_of(chunk_of[u_] * L, L) for u_ in units}
        qc, sc, pv = {}, {}, {}
        for u_ in units:
            c, pr = chunk_of[u_], u_[1]
            ps = slice(LANES * pr, LANES * (pr + 1))
            qp = q_scr[pl.ds(rows[u_], L), ps]
            kp = k_scr[pl.ds(rows[u_], L), ps]
            qc[u_] = _dot(qp, cprev_scr[c, pr])
            sc[u_] = [_dot_nt(jnp.where(h0_lane if hh == 0 else lane >= d, qp, jnp.zeros_like(qp)), kp)
                      for hh in range(2)]
        for u_ in units:
            pr = u_[1]
            ps = slice(LANES * pr, LANES * (pr + 1))
            vo = jnp.concatenate([v_ref[pl.ds(rows[u_], L), ps], ones_v], axis=1)
            pv[u_] = []
            for hh in range(2):
                hd = 2 * pr + hh
                u_row = ut_scr[hd:hd + 1, pl.ds(rows[u_], L)]
                decay = jnp.where(causal, jnp.exp(u_row - mfull_scr[pl.ds(rows[u_], L), LANES * hd:LANES * (hd + 1)]), 0.0)
                pv[u_].append(_dot((sc[u_][hh] * decay).astype(BF16), vo))
        hg, cen = {}, {}
        for u_ in units:
            pr = u_[1]
            ps = slice(LANES * pr, LANES * (pr + 1))
            it = inter_scr[pl.ds(rows[u_], L), ps]
            num = it * qc[u_][:, :LANES] + jnp.where(h0_lane, pv[u_][0][:, :LANES], pv[u_][1][:, :LANES])
            den = it * qc[u_][:, LANES:] + jnp.where(h0_lane, pv[u_][0][:, LANES:], pv[u_][1][:, LANES:])
            hcell = num / jnp.maximum(jnp.abs(den), floor_scr[pl.ds(rows[u_], L), ps])
            hg[u_] = _sigmoid(o_ref[pl.ds(rows[u_], L), ps]) * hcell
        for u_ in units:
            cen[u_] = hg[u_] - dot2(hg[u_], avg)
        for u_ in units:
            pr = u_[1]
            ps = slice(LANES * pr, LANES * (pr + 1))
            var = dot2(cen[u_] * cen[u_], avg)
            y_ref[pl.ds(rows[u_], L), ps] = (cen[u_] * lax.rsqrt(var + NORM_EPS) * gmh_ref[:, ps]
                                             + skip_ref[:, ps] * xc_scr[pl.ds(rows[u_], L), ps])
        return carry

    lax.fori_loop(0, NC // group, out_group, 0)


def _mlstm(cx, mv, mo, gif_t, conv_w, conv_b, w_q_m, w_k_m, b_igate, b_fgate, g_mh, skip_m, B, S):
    def blockdiag(w, scale):
        eye = jnp.eye(MLSTM_HEADS, dtype=F32)[:, None, :, None]
        return (w[:, :, None, :] * (eye * scale)).reshape(WIDTH_C, WIDTH_C).astype(BF16)

    nc = S // MLSTM_TILE
    wq = blockdiag(w_q_m, MLSTM_DIM ** -0.5)
    wk = blockdiag(w_k_m, 1.0)
    bias8 = jnp.concatenate([b_igate, b_fgate])
    br = jnp.broadcast_to(bias8[:, None], (2 * MLSTM_HEADS, S))
    seq = lambda w: pl.BlockSpec((S, w), lambda b: (b, 0))
    full = lambda a: pl.BlockSpec(a.shape, lambda b: (0,) * a.ndim)
    row = lambda a: a.reshape(1, -1)
    args = [conv_w, row(conv_b), wq, wk, br, row(g_mh), row(skip_m)]
    tok = lambda w, dt: pltpu.VMEM((S, w), dt)
    per_chunk = pltpu.VMEM((nc, MLSTM_HEADS, LANES), F32)
    return pl.pallas_call(
        _mlstm_kernel,
        grid=(B,),
        in_specs=[seq(WIDTH_C), seq(WIDTH_C), seq(WIDTH_C),
                  pl.BlockSpec((2 * MLSTM_HEADS, S), lambda b: (0, b))] + [full(a) for a in args],
        out_specs=seq(WIDTH_C),
        out_shape=jax.ShapeDtypeStruct((B * S, WIDTH_C), F32),
        scratch_shapes=[tok(WIDTH_C, F32), tok(WIDTH_C, BF16), tok(WIDTH_C, BF16),
                        tok(WIDTH_C, F32), tok(WIDTH_C, F32), tok(WIDTH_C, F32), tok(4 * LANES, F32),
                        pltpu.VMEM((2 * MLSTM_HEADS, S), F32), per_chunk, per_chunk,
                        pltpu.VMEM((nc, MLSTM_HEADS // 2, LANES, 2 * LANES), BF16)],
        compiler_params=_params("parallel"),
        name="mlstm_mixer",
    )(cx, mv, mo, gif_t, *args)


def _mix_ffn_kernel(ya_ref, yb_ref, yc_ref, x_ref, gate1_ref, ga_ref, gb_ref, wo_ref,
                    sh_ref, sc_ref, gate_ref, g2_ref, wgu_ref, wd32_ref, gf_ref, o_ref, wd_ref, *, final_norm):
    @pl.when(pl.program_id(0) == 0)
    def _():
        wd_ref[...] = wd32_ref[...].astype(BF16)

    a = _rms(ya_ref[...], ga_ref[...]).astype(BF16)
    b = _rms(yb_ref[...], gb_ref[...]).astype(BF16)
    c = yc_ref[...].astype(BF16)
    mixed = (_dot(a, wo_ref[0:WIDTH_A, :]) + _dot(b, wo_ref[WIDTH_A:WIDTH_A + WIDTH_B, :])
             + _dot(c, wo_ref[WIDTH_A + WIDTH_B:, :]))
    x = x_ref[...] + gate1_ref[0] * mixed
    h = (_rms(x, g2_ref[...]) * (1.0 + sc_ref[0]) + sh_ref[0]).astype(BF16)
    acc = jnp.zeros(x.shape, F32)
    for j in range(FFN_HIDDEN // FFN_CHUNK):
        gate = _dot(h, wgu_ref[:, FFN_CHUNK * j:FFN_CHUNK * (j + 1)])
        up = _dot(h, wgu_ref[:, FFN_HIDDEN + FFN_CHUNK * j:FFN_HIDDEN + FFN_CHUNK * (j + 1)])
        act = (_silu(gate) * up).astype(BF16)
        acc = acc + _dot(act, wd_ref[FFN_CHUNK * j:FFN_CHUNK * (j + 1), :])
    y = x + gate_ref[0] * acc
    if final_norm:
        y = _rms(y, gf_ref[...])
    o_ref[...] = y


def _mix_ffn(ya, yb, yc, x2, mod3, mod_first, ga, gb, w_out, g2, wgu, wd, gf, S, final_norm):
    T = x2.shape[0]
    tm = TM_PROJ
    per_b = S // tm
    row = lambda w: pl.BlockSpec((tm, w), lambda i: (i, 0))
    full = lambda a: pl.BlockSpec(a.shape, lambda i: (0,) * a.ndim)
    modspec = lambda k: _mod_spec(per_b, mod_first, k)
    return pl.pallas_call(
        functools.partial(_mix_ffn_kernel, final_norm=final_norm),
        grid=(T // tm,),
        in_specs=[row(WIDTH_A), row(WIDTH_B), row(WIDTH_C), row(D_MODEL), modspec(2), full(ga), full(gb), full(w_out),
                  modspec(3), modspec(4), modspec(5), full(g2), full(wgu), full(wd), full(gf)],
        out_specs=row(D_MODEL),
        out_shape=jax.ShapeDtypeStruct((T, D_MODEL), F32),
        scratch_shapes=[pltpu.VMEM(wd.shape, BF16)],
        compiler_params=_params("arbitrary"),
        name="mix_ffn_final" if final_norm else "mix_ffn",
    )(ya, yb, yc, x2, mod3, ga, gb, w_out, mod3, mod3, mod3, g2, wgu, wd, gf)


def _head_tile_perm():
    idx = []
    for r in range(NSA_REP):
        idx += list(range(HEAD_DIM * r, HEAD_DIM * (r + 1)))
        idx += list(range(HEAD_DIM * (NSA_REP + r), HEAD_DIM * (NSA_REP + r + 1)))
    return np.asarray(idx, np.int32)


def _in_cols():
    std = np.full((N_STD,), -1, np.int64)
    tr = np.full((N_T,), -1, np.int64)
    off = 0
    tr[TSEG_Q:TSEG_Q + WIDTH_A] = _head_tile_perm()
    off += WIDTH_A
    std[SEG_CKV:SEG_CKV + 2 * LANES] = off + np.arange(2 * LANES)
    off += 2 * LANES
    std[SEG_SK:SEG_SK + LANES] = off + np.arange(LANES)
    off += LANES
    tr[TSEG_SV:TSEG_SV + LANES] = off + np.arange(LANES)
    off += LANES
    std[SEG_WK:SEG_WK + LANES] = off + np.arange(LANES)
    off += LANES
    tr[TSEG_WV:TSEG_WV + LANES] = off + np.arange(LANES)
    off += LANES
    tr[TSEG_G:TSEG_G + N_GATES] = off + np.arange(N_GATES)
    off += N_GATES
    std[SEG_QL:SEG_QL + MLA_Q_LORA] = off + np.arange(MLA_Q_LORA)
    off += MLA_Q_LORA
    std[SEG_KVL:SEG_KVL + MLA_KV_LORA] = off + np.arange(MLA_KV_LORA)
    off += MLA_KV_LORA
    std[SEG_KR + MLA_NOPE:SEG_KR + MLA_NOPE + MLA_ROPE] = off + np.arange(MLA_ROPE)
    off += MLA_ROPE
    for seg in (SEG_CX, SEG_MV, SEG_MO):
        std[seg:seg + WIDTH_C] = off + np.arange(WIDTH_C)
        off += WIDTH_C
    tr[TSEG_IF:TSEG_IF + 2 * MLSTM_HEADS] = off + np.arange(2 * MLSTM_HEADS)
    return std, tr


def _gather_cols(w, cols):
    pieces, start = [], 0
    for i in range(1, len(cols) + 1):
        run_ends = i == len(cols) or (cols[i] != cols[i - 1] + 1 if cols[i - 1] >= 0 else cols[i] >= 0)
        if run_ends:
            first = int(cols[start])
            pieces.append(w[:, first:first + i - start] if first >= 0 else jnp.zeros((w.shape[0], i - start), w.dtype))
            start = i
    return jnp.concatenate(pieces, axis=1)


def _layer_weights(l, w_in, w_uq, w_ukv, w_out, w_gu, w_down, g_out_a):
    std, tr = _in_cols()
    w_std = _gather_cols(w_in[l], std).astype(BF16)
    w_t = _gather_cols(w_in[l], tr).T.astype(BF16)
    cq = np.full((MLA_HEADS * LANES,), -1, np.int64)
    ck = np.full((MLA_HEADS * LANES,), -1, np.int64)
    cv = np.zeros((WIDTH_B,), np.int64)
    dq = MLA_NOPE + MLA_ROPE
    dkv = MLA_NOPE + MLA_V
    for hd in range(MLA_HEADS):
        cq[LANES * hd:LANES * hd + dq] = dq * hd + np.arange(dq)
        ck[LANES * hd:LANES * hd + MLA_NOPE] = dkv * hd + np.arange(MLA_NOPE)
        cv[MLA_V * hd:MLA_V * (hd + 1)] = dkv * hd + MLA_NOPE + np.arange(MLA_V)
    wuqT = _gather_cols(w_uq[l], cq).T.astype(BF16)
    wkm = _gather_cols(w_ukv[l], ck).astype(BF16)
    wvmT = _gather_cols(w_ukv[l], cv).T.astype(BF16)
    perm = _head_tile_perm()
    head_rows = [w_out[l][int(o):int(o) + HEAD_DIM] for o in perm[::HEAD_DIM]]
    wo = jnp.concatenate(head_rows + [w_out[l][WIDTH_A:]], axis=0).astype(BF16)
    ga = _gather_cols(g_out_a[l].reshape(1, WIDTH_A), perm)
    return w_std, w_t, wuqT, wkm, wvmT, wo, ga, w_gu[l].astype(BF16), w_down[l]


def kernel(x, c, positions, g_norm1, g_norm2, w_ada, b_ada, w_in, cmp_pos, w_cmp_k, w_cmp_v, g_out_a, g_q_lora, w_uq, g_kv_lora, w_ukv, g_out_b, conv_w, conv_b, w_q_m, w_k_m, b_igate, b_fgate, g_mh, skip_m, w_out, w_gu, w_down, g_final):
    B, S, D = x.shape
    T = B * S
    x2 = x.reshape(T, D)
    tabs, tabs_t = _rope_tables(positions)
    mod3 = _ada(c, w_ada, b_ada).reshape(DEPTH * B * N_MOD, 1, D)
    row = lambda v: v.reshape(1, -1)
    for l in range(DEPTH):
        w_std, w_t, wuqT, wkm, wvmT, wo, ga, wgu, wd = _layer_weights(l, w_in, w_uq, w_ukv, w_out, w_gu, w_down, g_out_a)
        (qaT, gT, ck, cv, sk_aug, wk, svT, wvT, qmT, km, vmT, cx, mv, mo, gif_t) = _inproj(
            x2, mod3, l * B * N_MOD, row(g_norm1[l]), w_std, w_t, tabs, tabs_t, row(g_q_lora[l]), wuqT,
            row(g_kv_lora[l]), wkm, wvmT, S)
        kc, vcT = _compress(ck, cv, cmp_pos[l], w_cmp_k[l], w_cmp_v[l], B, S)
        ya = _nsa(qaT, gT, kc, vcT, sk_aug, svT, wk, wvT, B, S)
        yb = _mla(qmT, km, vmT, B, S)
        yc = _mlstm(cx, mv, mo, gif_t, conv_w[l], conv_b[l], w_q_m[l], w_k_m[l], b_igate[l], b_fgate[l],
                    g_mh[l], skip_m[l], B, S)
        x2 = _mix_ffn(ya, yb, yc, x2, mod3, l * B * N_MOD, ga, row(g_out_b[l]), wo, row(g_norm2[l]),
                      wgu, wd, row(g_final), S, final_norm=(l == DEPTH - 1))
    return x2.reshape(B, S, D)
```

```python
import functools

import numpy as np
import jax
import jax.numpy as jnp
from jax import lax
from jax.experimental import pallas as pl
from jax.experimental.pallas import tpu as pltpu

F32 = jnp.float32
BF16 = jnp.bfloat16

D_MODEL = 1024
DEPTH = 2
HEAD_DIM = 64
ROPE_THETA = 500000.0
NSA_ROT_HALF = HEAD_DIM // 8
NORM_EPS = 1e-6

NSA_HEADS = 6
NSA_KV_HEADS = 2
NSA_REP = NSA_HEADS // NSA_KV_HEADS
CMP_LEN = 32
CMP_STRIDE = 16
SLC_LEN = 64
SLC_TOPK = 16
WINDOW = 512

MLA_HEADS = 6
MLA_Q_LORA = 256
MLA_KV_LORA = 128
MLA_NOPE = 64
MLA_ROPE = 32
MLA_V = 64

MLSTM_HEADS = 4
MLSTM_DIM = 64
MLSTM_CONV = 4

WIDTH_A = NSA_HEADS * HEAD_DIM
WIDTH_B = MLA_HEADS * MLA_V
WIDTH_C = MLSTM_HEADS * MLSTM_DIM
FFN_HIDDEN = 2816
N_GATES = 3 * NSA_HEADS
GATE_ROWS = 24

LANES = 128
NEG = -1e30
LOG2E = 1.4426950408889634
VMEM_LIMIT = 56 * 1024 * 1024

TM_PROJ = 512
TQ_NSA = 256
TK_SLC = 256
CHAIN_LANES = 256
TQ_MLA = 256
FFN_CHUNK = 256
MLSTM_TILE = 128
MLSTM_GROUP = 8
MXU_LOOKAHEAD = 4

SEG_CKV, SEG_SK, SEG_WK, SEG_QL, SEG_KVL, SEG_KR, SEG_CX, SEG_MV, SEG_MO = (
    0, 256, 384, 512, 768, 896, 1024, 1280, 1536)
N_STD = 1792
TSEG_Q, TSEG_G, TSEG_SV, TSEG_WV, TSEG_IF = 0, 384, 416, 544, 672
TSEG_G_ROWS = 32
TSEG_IF_ROWS = 16
N_T = 688


def _params(*sem):
    return pltpu.CompilerParams(dimension_semantics=sem, vmem_limit_bytes=VMEM_LIMIT)


def _dot(a, b):
    return jnp.dot(a, b, preferred_element_type=F32)


def _dot_nt(a, b):
    return lax.dot_general(a, b, (((1,), (1,)), ((), ())), preferred_element_type=F32)


def _dot_tn(a, b):
    return lax.dot_general(a, b, (((0,), (0,)), ((), ())), preferred_element_type=F32)


def _split3(x):
    hi = x.astype(BF16)
    r1 = x - hi.astype(F32)
    mid = r1.astype(BF16)
    lo = (r1 - mid.astype(F32)).astype(BF16)
    return hi, mid, lo


def _rms(x, g):
    return x * lax.rsqrt(jnp.mean(x * x, axis=-1, keepdims=True) + NORM_EPS) * g


def _sigmoid(x):
    return 1.0 / (1.0 + jnp.exp(-x))


def _silu(x):
    return x * _sigmoid(x)


def _rope(x, cos, sin, half, x1_lane):
    xr = jnp.where(x1_lane, -pltpu.roll(x, LANES - half, 1), pltpu.roll(x, half, 1))
    return x * cos + xr * sin


def _ada_kernel(c_ref, w_ref, b_ref, o_ref):
    c = c_ref[...]
    ca = _silu(c).astype(BF16)
    o_ref[0] = _dot(ca, w_ref[0].astype(BF16)) + b_ref[0]


def _ada(c, w_ada, b_ada):
    L, D, N = w_ada.shape
    B = c.shape[0]
    tn = 1536
    return pl.pallas_call(
        _ada_kernel,
        grid=(L, N // tn),
        in_specs=[pl.BlockSpec((B, D), lambda l, j: (0, 0)),
                  pl.BlockSpec((1, D, tn), lambda l, j: (l, 0, j)),
                  pl.BlockSpec((1, 1, tn), lambda l, j: (l, 0, j))],
        out_specs=pl.BlockSpec((1, B, tn), lambda l, j: (l, 0, j)),
        out_shape=jax.ShapeDtypeStruct((L, B, N), F32),
        compiler_params=_params("parallel", "parallel"),
        name="ada_mod",
    )(c, w_ada, b_ada.reshape(L, 1, N))


ROPE_FREQS = NSA_ROT_HALF + MLA_ROPE // 2


def _rope_kernel(pos_ref, inv_ref, spread_ref, one_ref, posr_ref, invc_ref,
                 cn_ref, sn_ref, cm_ref, sm_ref, cnt_ref, snt_ref, cmt_ref, smt_ref):
    ang = pos_ref[...] * inv_ref[...]
    lane = lax.broadcasted_iota(jnp.int32, (1, LANES), 1)

    def spread(val):
        hi, mid, lo = _split3(val)
        packed = jnp.where(lane < ROPE_FREQS, hi, jnp.where(lane < 2 * ROPE_FREQS, mid, lo))
        return _dot(packed, spread_ref[...])

    c = spread(jnp.cos(ang)) + one_ref[...]
    s = spread(jnp.sin(ang))
    cn_ref[...] = c[:, :LANES]
    cm_ref[...] = c[:, LANES:]
    sn_ref[...] = s[:, :LANES]
    sm_ref[...] = s[:, LANES:]
    tm = posr_ref.shape[1]
    ang_t = jnp.concatenate([invc_ref[...]] * (tm // LANES), axis=1) * jnp.broadcast_to(posr_ref[...], (4 * NSA_ROT_HALF, tm))
    c_t, s_t = jnp.cos(ang_t), jnp.sin(ang_t)
    cnt_ref[...] = c_t[:NSA_ROT_HALF]
    snt_ref[...] = s_t[:NSA_ROT_HALF]
    cmt_ref[...] = c_t[NSA_ROT_HALF:ROPE_FREQS]
    smt_ref[...] = s_t[NSA_ROT_HALF:ROPE_FREQS]


def _rope_tables(positions):
    T = positions.size
    inv_n = jnp.power(ROPE_THETA, -jnp.arange(0, 2 * NSA_ROT_HALF, 2, dtype=F32) / (2 * NSA_ROT_HALF))
    inv_m = jnp.power(ROPE_THETA, -jnp.arange(0, MLA_ROPE, 2, dtype=F32) / MLA_ROPE)
    copies = 3
    inv = jnp.concatenate([inv_n, inv_m] * copies + [jnp.zeros((LANES - copies * ROPE_FREQS,), F32)]).reshape(1, LANES)
    onehot = np.zeros((LANES, 2 * LANES), np.float32)
    rotated = np.zeros((1, 2 * LANES), np.float32)
    for k in range(copies):
        for f in range(NSA_ROT_HALF):
            for lane in (f, f + NSA_ROT_HALF, HEAD_DIM + f, HEAD_DIM + f + NSA_ROT_HALF):
                onehot[ROPE_FREQS * k + f, lane] = 1.0
                rotated[0, lane] = 1.0
        for f in range(MLA_ROPE // 2):
            for lane in (MLA_NOPE + f, MLA_NOPE + MLA_ROPE // 2 + f):
                onehot[ROPE_FREQS * k + NSA_ROT_HALF + f, LANES + lane] = 1.0
                rotated[0, LANES + lane] = 1.0
    posf = positions.reshape(T).astype(F32)
    posb = jnp.broadcast_to(posf[:, None], (T, LANES))
    rows_t = 4 * NSA_ROT_HALF
    inv_col = jnp.concatenate([inv_n, inv_m, jnp.zeros((rows_t - ROPE_FREQS,), F32)])
    inv_col = jnp.broadcast_to(inv_col[:, None], (rows_t, LANES))
    tm = 2048
    spec = pl.BlockSpec((tm, LANES), lambda i: (i, 0))
    col = lambda h: pl.BlockSpec((h, tm), lambda i: (0, i))
    full = lambda shape: pl.BlockSpec(shape, lambda i: (0, 0))
    sds = jax.ShapeDtypeStruct
    outs = pl.pallas_call(
        _rope_kernel,
        grid=(T // tm,),
        in_specs=[spec, full((1, LANES)), full((LANES, 2 * LANES)), full((1, 2 * LANES)), col(1), full((rows_t, LANES))],
        out_specs=[spec] * 4 + [col(NSA_ROT_HALF)] * 2 + [col(MLA_ROPE // 2)] * 2,
        out_shape=[sds((T, LANES), F32)] * 4 + [sds((NSA_ROT_HALF, T), F32)] * 2 + [sds((MLA_ROPE // 2, T), F32)] * 2,
        compiler_params=_params("parallel"),
        name="rope_tables",
    )(posb, inv, jnp.asarray(onehot, BF16), jnp.asarray(1.0 - rotated), posf.reshape(1, T), inv_col)
    return outs[:4], outs[4:]


def _rope_rows(t, offset, half, cos, sin):
    x1, x2 = t[offset:offset + half], t[offset + half:offset + 2 * half]
    return x1 * cos - x2 * sin, x2 * cos + x1 * sin


def _inproj_kernel(x_ref, sh_ref, sc_ref, g1_ref, ws_ref, wt_ref, cn_ref, sn_ref, cm_ref, sm_ref,
                   cnT_ref, snT_ref, cmT_ref, smT_ref, gq_ref, wuqT_ref, gkv_ref, wkm_ref, wvmT_ref,
                   qaT_ref, gT_ref, ck_ref, cv_ref, ska_ref, wk_ref, svT_ref, wvT_ref,
                   qmT_ref, km_ref, vmT_ref, cx_ref, mv_ref, mo_ref, gif_ref, *, per_b):
    tm = x_ref.shape[0]
    x = x_ref[...]
    h = _rms(x, g1_ref[...]) * (1.0 + sc_ref[0]) + sh_ref[0]
    hb = h.astype(BF16)

    def seg(start, width):
        return _dot(hb, ws_ref[:, start:start + width])

    lane = lax.broadcasted_iota(jnp.int32, (1, LANES), 1)
    x1_n = (lane % HEAD_DIM) < NSA_ROT_HALF
    x1_m = lane < MLA_NOPE + MLA_ROPE // 2
    cn, sn, cm, sm = cn_ref[...], sn_ref[...], cm_ref[...], sm_ref[...]
    rope_n = lambda t: _rope(t, cn, sn, NSA_ROT_HALF, x1_n)
    rope_m = lambda t: _rope(t, cm, sm, MLA_ROPE // 2, x1_m)

    qn = _rms(seg(SEG_QL, MLA_Q_LORA), gq_ref[...]).astype(BF16)
    kvl_kr = seg(SEG_KVL, 2 * LANES)
    kvn = _rms(kvl_kr[:, :LANES], gkv_ref[...]).astype(BF16)
    kr = rope_m(kvl_kr[:, LANES:])

    out_t = _dot_nt(wt_ref[...], hb)
    seg_t = lambda start, height: out_t[start:start + height]

    qt = seg_t(TSEG_Q, WIDTH_A)
    cnt, snt = cnT_ref[...], snT_ref[...]
    parts = []
    for hd in range(NSA_HEADS):
        o = HEAD_DIM * hd
        parts += list(_rope_rows(qt, o, NSA_ROT_HALF, cnt, snt)) + [qt[o + 2 * NSA_ROT_HALF:o + HEAD_DIM]]
    qaT_ref[...] = (jnp.concatenate(parts, axis=0) * (HEAD_DIM ** -0.5 * LOG2E)).astype(BF16)
    gT_ref[...] = _sigmoid(seg_t(TSEG_G, TSEG_G_ROWS))[:GATE_ROWS]
    svT_ref[...] = seg_t(TSEG_SV, LANES).astype(BF16)
    wvT_ref[...] = seg_t(TSEG_WV, LANES).astype(BF16)
    gif_ref[...] = seg_t(TSEG_IF, TSEG_IF_ROWS)[:2 * MLSTM_HEADS]

    ckv = seg(SEG_CKV, 2 * LANES)
    ck_ref[...] = rope_n(ckv[:, :LANES])
    cv_ref[...] = ckv[:, LANES:]
    ska_ref[:, :LANES] = rope_n(seg(SEG_SK, LANES)).astype(BF16)
    srow = (pl.program_id(0) % per_b) * tm + lax.broadcasted_iota(jnp.int32, (tm, LANES), 0)
    lane2 = lax.broadcasted_iota(jnp.int32, (tm, LANES), 1)
    code = (lane2 < 4 * SLC_TOPK) & ((lane2 & (2 * SLC_TOPK - 1)) == srow // SLC_LEN)
    ska_ref[:, LANES:] = jnp.where(code, 1.0, 0.0).astype(BF16)
    wk_ref[...] = rope_n(seg(SEG_WK, LANES)).astype(BF16)

    qmt = _dot_nt(wuqT_ref[...], qn)
    kk = _dot(kvn, wkm_ref[...])
    vmt = _dot_nt(wvmT_ref[...], kvn)
    cx_ref[...] = seg(SEG_CX, WIDTH_C)
    mv_ref[...] = seg(SEG_MV, WIDTH_C).astype(BF16)
    mo_ref[...] = seg(SEG_MO, WIDTH_C)
    cmt, smt = cmT_ref[...], smT_ref[...]
    parts = []
    for hd in range(MLA_HEADS):
        o = LANES * hd
        parts += [qmt[o:o + MLA_NOPE]] + list(_rope_rows(qmt, o + MLA_NOPE, MLA_ROPE // 2, cmt, smt))
        parts += [qmt[o + MLA_NOPE + MLA_ROPE:o + LANES]]
    qmT_ref[...] = (jnp.concatenate(parts, axis=0) * ((MLA_NOPE + MLA_ROPE) ** -0.5 * LOG2E)).astype(BF16)
    for hd in range(MLA_HEADS):
        km_ref[:, LANES * hd:LANES * (hd + 1)] = (kk[:, LANES * hd:LANES * (hd + 1)] + kr).astype(BF16)
    vmT_ref[...] = vmt.astype(BF16)


N_MOD = 6


def _mod_spec(per_b, first, k):
    return pl.BlockSpec((1, 1, D_MODEL), lambda i: (first + (i // per_b) * N_MOD + k, 0, 0))


def _inproj(x2, mod3, mod_first, g1, w_std, w_t, tabs, tabs_t, gq, wuqT, gkv, wkm, wvmT, S):
    T = x2.shape[0]
    tm = TM_PROJ
    per_b = S // tm
    row = lambda w: pl.BlockSpec((tm, w), lambda i: (i, 0))
    col = lambda h: pl.BlockSpec((h, tm), lambda i: (0, i))
    full = lambda a: pl.BlockSpec(a.shape, lambda i: (0,) * a.ndim)
    outs = [(WIDTH_A, BF16, True), (GATE_ROWS, F32, True), (LANES, F32, False), (LANES, F32, False), (2 * LANES, BF16, False),
            (LANES, BF16, False), (LANES, BF16, True), (LANES, BF16, True),
            (MLA_HEADS * LANES, BF16, True), (MLA_HEADS * LANES, BF16, False), (WIDTH_B, BF16, True),
            (WIDTH_C, F32, False), (WIDTH_C, BF16, False), (WIDTH_C, F32, False), (2 * MLSTM_HEADS, F32, True)]
    return pl.pallas_call(
        functools.partial(_inproj_kernel, per_b=per_b),
        grid=(T // tm,),
        in_specs=[row(D_MODEL), _mod_spec(per_b, mod_first, 0), _mod_spec(per_b, mod_first, 1),
                  full(g1), full(w_std), full(w_t)] + [row(LANES)] * 4
                 + [col(t.shape[0]) for t in tabs_t] + [full(gq), full(wuqT), full(gkv), full(wkm), full(wvmT)],
        out_specs=[col(w) if tr else row(w) for w, _, tr in outs],
        out_shape=[jax.ShapeDtypeStruct((w, T) if tr else (T, w), dt) for w, dt, tr in outs],
        compiler_params=_params("parallel"),
        name="in_proj",
    )(x2, mod3, mod3, g1, w_std, w_t, *tabs, *tabs_t, gq, wuqT, gkv, wkm, wvmT)


def _compress_kernel(xk_ref, xv_ref, wk_ref, wv_ref, pos_ref, wkf_ref, wvf_ref, kc_ref, vcT_ref):
    ng = xk_ref.shape[0] // CMP_STRIDE
    row = lax.broadcasted_iota(jnp.int32, (ng, LANES), 0)
    pos = pos_ref[...].astype(BF16)
    acc_k = jnp.zeros((ng, 2 * LANES), F32)
    acc_v = jnp.zeros((ng, 2 * LANES), F32)
    for t in range(CMP_STRIDE):
        tok = pl.ds(t, ng, stride=CMP_STRIDE)
        acc_k = acc_k + _dot(xk_ref[tok, :].astype(BF16), wk_ref[t])
        acc_v = acc_v + _dot(xv_ref[tok, :].astype(BF16), wv_ref[t])

    def finish(acc, wf_ref):
        const = _dot(pos, wf_ref[...].astype(BF16))
        both = acc[:, :LANES] + pltpu.roll(acc[:, LANES:], ng - 1, 0) + jnp.concatenate([const, const], axis=1)
        return jnp.where(row < ng - 1, both, 0.0)

    kc_ref[0] = finish(acc_k, wkf_ref).astype(BF16)
    vcT_ref[0] = finish(acc_v, wvf_ref).T.astype(BF16)


def _compress(ck, cv, cmp_pos, w_cmp_k, w_cmp_v, B, S):
    ng = S // CMP_STRIDE

    def per_token(w):
        a = w[:CMP_STRIDE * HEAD_DIM].reshape(CMP_STRIDE, HEAD_DIM, HEAD_DIM)
        b = w[CMP_STRIDE * HEAD_DIM:].reshape(CMP_STRIDE, HEAD_DIM, HEAD_DIM)
        z = jnp.zeros_like(a)
        top = jnp.concatenate([a, z, b, z], axis=2)
        bot = jnp.concatenate([z, a, z, b], axis=2)
        return jnp.concatenate([top, bot], axis=1).astype(BF16)

    full = lambda a: pl.BlockSpec(a.shape, lambda b: (0,) * a.ndim)
    wk3, wv3 = per_token(w_cmp_k), per_token(w_cmp_v)
    posf = cmp_pos.reshape(1, CMP_LEN * HEAD_DIM)
    ospec = pl.BlockSpec((1, ng, LANES), lambda b: (b, 0, 0))
    return pl.pallas_call(
        _compress_kernel,
        grid=(B,),
        in_specs=[pl.BlockSpec((S, LANES), lambda b: (b, 0)), pl.BlockSpec((S, LANES), lambda b: (b, 0)),
                  full(wk3), full(wv3), full(posf), full(w_cmp_k), full(w_cmp_v)],
        out_specs=[ospec, ospec],
        out_shape=[jax.ShapeDtypeStruct((B, ng, LANES), BF16)] * 2,
        compiler_params=_params("parallel"),
        name="nsa_compress",
    )(ck, cv, wk3, wv3, posf, w_cmp_k, w_cmp_v)


def _nsa_kernel(qT_ref, gT_ref, kc_ref, vcT_ref, sk_ref, svT_ref, wk_ref, wvT_ref, o_ref, qaug_scr, acc_scr, s_scr):
    tq = TQ_NSA
    cols = NSA_HEADS * tq
    pair = CHAIN_LANES
    t0 = pl.program_id(1) * tq
    n_slc = SLC_TOPK * 2

    frow = lax.broadcasted_iota(jnp.int32, (LANES, tq), 0)
    g0_row = frow < HEAD_DIM
    tiles = [qT_ref[LANES * r:LANES * (r + 1), :] for r in range(NSA_REP)]
    zero = jnp.zeros_like(tiles[0])
    q6 = jnp.concatenate([jnp.where(g0_row, t, zero) for t in tiles]
                         + [jnp.where(g0_row, zero, t) for t in tiles], axis=1)
    qaug_scr[0:LANES, :] = q6
    tq_l = t0 + (lax.broadcasted_iota(jnp.int32, (1, cols), 1) & (tq - 1))

    pairs = [slice(pair * pp, pair * (pp + 1)) for pp in range(cols // pair)]
    vrows = lambda pp: slice(HEAD_DIM * (pp // NSA_REP), HEAD_DIM * (pp // NSA_REP + 1))
    s = _dot(kc_ref[0], q6)
    span = WINDOW + tq
    start = pl.multiple_of(jnp.maximum(t0 - WINDOW, 0), tq)
    kw = wk_ref[pl.ds(start, span), :]
    win_scores = [_dot(kw, q6[:, sl]) for sl in pairs]

    tq_1 = tq_l[:, :tq]
    nrow = lax.broadcasted_iota(jnp.int32, (LANES, tq), 0)
    cmp_bias = jnp.where(nrow * CMP_STRIDE + (CMP_LEN - 1) <= tq_1, 0.0, NEG)
    s = s + jnp.concatenate([cmp_bias] * NSA_HEADS, axis=1)
    e = jnp.exp2(s - jnp.max(s, axis=0, keepdims=True))
    seen = jnp.where(tq_l >= CMP_LEN - 1, 1.0, 0.0)
    p = e * (seen / jnp.sum(e, axis=0, keepdims=True))
    p_b = p.astype(BF16)
    o_cmp = [_dot(vcT_ref[0, vrows(pp), :], p_b[:, sl]) for pp, sl in enumerate(pairs)]

    jr = lax.broadcasted_iota(jnp.int32, (n_slc, LANES), 0)
    nc = lax.broadcasted_iota(jnp.int32, (n_slc, LANES), 1)
    ovl = ((nc * CMP_STRIDE < jr * SLC_LEN + SLC_LEN) & (nc * CMP_STRIDE + CMP_LEN > jr * SLC_LEN)
           & (nc < LANES - 1))
    ovl = jnp.where(ovl, 1.0, 0.0).astype(BF16)
    jq = lax.broadcasted_iota(jnp.int32, (n_slc, tq), 0)
    tl = t0 + lax.broadcasted_iota(jnp.int32, (n_slc, tq), 1)
    cur = tl // SLC_LEN
    forced = (jq == 0) | (jq == cur) | (jq == cur - 1)
    future = jq * SLC_LEN > tl
    bias_t = []
    for g in range(NSA_KV_HEADS):
        pg = p[:, (3 * g) * tq:(3 * g + 1) * tq] + p[:, (3 * g + 1) * tq:(3 * g + 2) * tq] + p[:, (3 * g + 2) * tq:(3 * g + 3) * tq]
        imp = sum(_dot(ovl, part) for part in _split3(pg))
        imp = jnp.where(forced, jnp.inf, imp)
        imp = jnp.where(future, -jnp.inf, imp)
        rows8 = [imp[8 * r:8 * (r + 1)] for r in range(n_slc // 8)]
        sub = lax.broadcasted_iota(jnp.int32, (8, tq), 0)
        ranks = [jnp.zeros((8, tq), F32) for _ in rows8]
        for jp in range(n_slc):
            rv = imp[jp:jp + 1, :]
            for r, blk in enumerate(rows8):
                if 8 * r > jp:
                    ahead = rv >= blk
                elif 8 * r + 7 < jp:
                    ahead = rv > blk
                else:
                    ahead = (rv > blk) | ((rv == blk) & (sub > jp - 8 * r))
                ranks[r] = ranks[r] + jnp.where(ahead, 1.0, 0.0)
        rank = jnp.concatenate(ranks, axis=0)
        bias_t.append(jnp.where(rank < float(SLC_TOPK), 0.0, NEG).astype(BF16))
    zb = jnp.zeros((n_slc, NSA_REP * tq), BF16)
    qaug_scr[LANES:LANES + n_slc, :] = jnp.concatenate([bias_t[0]] * NSA_REP + [zb], axis=1)
    qaug_scr[LANES + n_slc:LANES + 2 * n_slc, :] = jnp.concatenate([zb] + [bias_t[1]] * NSA_REP, axis=1)
    qaug_scr[LANES + 2 * n_slc:, :] = jnp.zeros((LANES - 2 * n_slc, cols), BF16)

    vwt = wvT_ref[:, pl.ds(start, span)]
    wrow = start + lax.broadcasted_iota(jnp.int32, (span, pair), 0)
    o_win = []
    for pp, (sl, sc) in enumerate(zip(pairs, win_scores)):
        in_window = lax.bitcast_convert_type(tq_l[:, sl] - wrow, jnp.uint32) < jnp.uint32(WINDOW)
        sc = jnp.where(in_window, sc, NEG)
        ew = jnp.exp2(sc - jnp.max(sc, axis=0, keepdims=True))
        o_win.append(_dot(vwt[vrows(pp)], ew.astype(BF16)) / jnp.sum(ew, axis=0, keepdims=True))

    acc_scr[...] = jnp.zeros((HEAD_DIM, cols), F32)
    krow = lax.broadcasted_iota(jnp.int32, (TK_SLC, pair), 0)

    def slc_scores(k0, sl):
        return _dot(sk_ref[pl.ds(k0, TK_SLC), :], qaug_scr[:, sl])

    nch = len(pairs)
    la = s_scr.shape[0]
    for pp in range(la):
        s_scr[pp] = slc_scores(0, pairs[pp])

    def slc_tile(k0, m, l, masked):
        vt = svT_ref[:, pl.ds(k0, TK_SLC)]
        ms, ls, accs = [], [], []
        ahead = [s_scr[pp] for pp in range(la)]
        for pp, sl in enumerate(pairs):
            sc = ahead.pop(0)
            nxt = pp + la
            if nxt < nch:
                ahead.append(slc_scores(k0, pairs[nxt]))
            elif not masked:
                s_scr[nxt - nch] = slc_scores(pl.multiple_of(k0 + TK_SLC, TK_SLC), pairs[nxt - nch])
            if masked:
                sc = jnp.where(k0 + krow <= tq_l[:, sl], sc, NEG)
            m_new = jnp.maximum(m[:, sl], jnp.max(sc, axis=0, keepdims=True))
            alpha = jnp.exp2(m[:, sl] - m_new)
            pe = jnp.exp2(sc - m_new)
            ls.append(alpha * l[:, sl] + jnp.sum(pe, axis=0, keepdims=True))
            accs.append(alpha * acc_scr[:, sl] + _dot(vt[vrows(pp)], pe.astype(BF16)))
            ms.append(m_new)
        acc_scr[...] = jnp.concatenate(accs, axis=1)
        return jnp.concatenate(ms, axis=1), jnp.concatenate(ls, axis=1)

    def slc_step(kt, carry):
        return slc_tile(pl.multiple_of(kt * TK_SLC, TK_SLC), carry[0], carry[1], False)

    n_full = t0 // TK_SLC
    m, l = lax.fori_loop(0, n_full, slc_step, (jnp.full((1, cols), NEG, F32), jnp.zeros((1, cols), F32)))
    m, l = slc_tile(pl.multiple_of(n_full * TK_SLC, TK_SLC), m, l, True)
    o_slc = acc_scr[...] / l

    gt = gT_ref[...]
    mixed = []
    for hd, sl in enumerate(pairs):
        gate = lambda j: gt[3 * hd + j:3 * hd + j + 1, :]
        mixed.append(gate(0) * o_cmp[hd] + gate(1) * o_slc[:, sl] + gate(2) * o_win[hd])
    for r in range(NSA_REP):
        o_ref[:, LANES * r:LANES * (r + 1)] = jnp.concatenate([mixed[r], mixed[NSA_REP + r]], axis=0).T


def _nsa(qaT, gT, kc, vcT, sk_aug, svT, wk, wvT, B, S):
    tq = TQ_NSA
    assert CHAIN_LANES == tq, "the kernel treats one softmax chain as one head"
    nq = S // tq
    cols = NSA_HEADS * tq
    qcol =lambda h: pl.BlockSpec((h, tq), lambda b, i: (0, b * nq + i))
    seq = lambda w: pl.BlockSpec((S, w), lambda b, i: (b, 0))
    seqT = pl.BlockSpec((LANES, S), lambda b, i: (0, b))
    cspec = pl.BlockSpec((1, LANES, LANES), lambda b, i: (b, 0, 0))
    return pl.pallas_call(
        _nsa_kernel,
        grid=(B, nq),
        in_specs=[qcol(WIDTH_A), qcol(gT.shape[0]), cspec, cspec, seq(2 * LANES), seqT, seq(LANES), seqT],
        out_specs=pl.BlockSpec((tq, WIDTH_A), lambda b, i: (b * nq + i, 0)),
        out_shape=jax.ShapeDtypeStruct((B * S, WIDTH_A), F32),
        scratch_shapes=[pltpu.VMEM((2 * LANES, cols), BF16), pltpu.VMEM((HEAD_DIM, cols), F32),
                        pltpu.VMEM((cols // CHAIN_LANES, TK_SLC, CHAIN_LANES), F32)],
        compiler_params=_params("parallel", "arbitrary"),
        name="nsa_attention",
    )(qaT, gT, kc, vcT, sk_aug, svT, wk, wvT)


def _mla_kernel(qT_ref, k_ref, vT_ref, o_ref, acc_scr, s_scr):
    tq = TQ_MLA
    t0 = pl.program_id(1) * tq
    tq_l = t0 + lax.broadcasted_iota(jnp.int32, (1, tq), 1)
    krow = lax.broadcasted_iota(jnp.int32, (tq, tq), 0)
    acc_scr[...] = jnp.zeros((MLA_HEADS, MLA_V, tq), F32)

    def scores(k0, hd):
        k = k_ref[pl.ds(k0, tq), LANES * hd:LANES * (hd + 1)]
        return _dot(k, qT_ref[LANES * hd:LANES * (hd + 1), :])

    for hd in range(MXU_LOOKAHEAD):
        s_scr[hd] = scores(0, hd)

    def tile(k0, ms, ls, masked):
        new_m, new_l, accs = [], [], []
        ahead = [s_scr[hd] for hd in range(MXU_LOOKAHEAD)]
        for hd in range(MLA_HEADS):
            sc = ahead.pop(0)
            nxt = hd + MXU_LOOKAHEAD
            if nxt < MLA_HEADS:
                ahead.append(scores(k0, nxt))
            elif not masked:
                s_scr[nxt - MLA_HEADS] = scores(pl.multiple_of(k0 + tq, tq), nxt - MLA_HEADS)
            if masked:
                sc = jnp.where(k0 + krow <= tq_l, sc, NEG)
            m_new = jnp.maximum(ms[hd], jnp.max(sc, axis=0, keepdims=True))
            alpha = jnp.exp2(ms[hd] - m_new)
            pe = jnp.exp2(sc - m_new)
            new_l.append(alpha * ls[hd] + jnp.sum(pe, axis=0, keepdims=True))
            vt = vT_ref[MLA_V * hd:MLA_V * (hd + 1), pl.ds(k0, tq)]
            accs.append(alpha * acc_scr[hd] + _dot(vt, pe.astype(BF16)))
            new_m.append(m_new)
        acc_scr[...] = jnp.stack(accs)
        return tuple(new_m), tuple(new_l)

    def step(kt, carry):
        return tile(pl.multiple_of(kt * tq, tq), carry[0], carry[1], False)

    n_full = pl.program_id(1)
    init = (tuple(jnp.full((1, tq), NEG, F32) for _ in range(MLA_HEADS)),
            tuple(jnp.zeros((1, tq), F32) for _ in range(MLA_HEADS)))
    ms, ls = lax.fori_loop(0, n_full, step, init)
    ms, ls = tile(pl.multiple_of(n_full * tq, tq), ms, ls, True)
    for pr in range(MLA_HEADS // 2):
        t = jnp.concatenate([acc_scr[2 * pr] / ls[2 * pr], acc_scr[2 * pr + 1] / ls[2 * pr + 1]], axis=0)
        o_ref[:, LANES * pr:LANES * (pr + 1)] = t.T


def _mla(qmT, km, vmT, B, S):
    tq = TQ_MLA
    nq = S // tq
    return pl.pallas_call(
        _mla_kernel,
        grid=(B, nq),
        in_specs=[pl.BlockSpec((MLA_HEADS * LANES, tq), lambda b, i: (0, b * nq + i)),
                  pl.BlockSpec((S, MLA_HEADS * LANES), lambda b, i: (b, 0)),
                  pl.BlockSpec((WIDTH_B, S), lambda b, i: (0, b))],
        out_specs=pl.BlockSpec((tq, WIDTH_B), lambda b, i: (b * nq + i, 0)),
        out_shape=jax.ShapeDtypeStruct((B * S, WIDTH_B), F32),
        scratch_shapes=[pltpu.VMEM((MLA_HEADS, MLA_V, tq), F32), pltpu.VMEM((MXU_LOOKAHEAD, tq, tq), F32)],
        compiler_params=_params("parallel", "arbitrary"),
        name="mla_attention",
    )(qmT, km, vmT)


def _log_sigmoid(z):
    return jnp.minimum(z, 0.0) - jnp.log1p(jnp.exp(-jnp.abs(z)))


def _mlstm_kernel(cx_ref, v_ref, o_ref, g_ref, cw_ref, cb_ref, wq_ref, wk_ref, br_ref, gmh_ref, skip_ref, y_ref,
                  xc_scr, q_scr, k_scr, w_scr, inter_scr, floor_scr, mfull_scr, ut_scr, eo_scr, el_scr, cprev_scr):
    S = cx_ref.shape[0]
    L = MLSTM_TILE
    NC = S // L
    d = MLSTM_DIM
    pairs = MLSTM_HEADS // 2
    group = MLSTM_GROUP

    x = cx_ref[...]
    rowi = lax.broadcasted_iota(jnp.int32, (S, WIDTH_C), 0)
    conv = x * cw_ref[MLSTM_CONV - 1:MLSTM_CONV, :]
    for back in range(1, MLSTM_CONV):
        shifted = jnp.where(rowi >= back, pltpu.roll(x, back, 0), 0.0)
        conv = conv + shifted * cw_ref[MLSTM_CONV - 1 - back:MLSTM_CONV - back, :]
    xc = _silu(conv + cb_ref[...])
    xc_scr[...] = xc
    xcb = xc.astype(BF16)
    q_scr[...] = _dot(xcb, wq_ref[...]).astype(BF16)
    k_scr[...] = _dot(xcb, wk_ref[...]).astype(BF16)

    nh = MLSTM_HEADS
    lane_in_chunk = lax.broadcasted_iota(jnp.int32, (nh, S), 1) & (L - 1)

    def scan_lanes(val, op, fill):
        sh = 1
        while sh < L:
            val = op(val, jnp.where(lane_in_chunk >= sh, pltpu.roll(val, sh, 1), fill))
            sh *= 2
        return val

    gt = g_ref[...] + br_ref[...]
    ig = gt[0:nh]
    b = scan_lanes(_log_sigmoid(gt[nh:]), jnp.add, 0.0)
    u = ig - b
    cmu = scan_lanes(u, jnp.maximum, -jnp.inf)
    ut_scr[0:nh, :] = u

    m = jnp.zeros((nh, 1), F32)
    w_loc, inter, floor, m_tok = [], [], [], []
    for c in range(NC):
        blk = slice(L * c, L * (c + 1))
        b_last, u_max = b[:, L * (c + 1) - 1:L * (c + 1)], cmu[:, L * (c + 1) - 1:L * (c + 1)]
        m_top = jnp.maximum(m, u_max)
        eo_scr[c] = jnp.broadcast_to(jnp.exp(m - m_top), (nh, LANES))
        el_scr[c] = jnp.broadcast_to(jnp.exp(u_max - m_top), (nh, LANES))
        mt = jnp.maximum(cmu[:, blk], m)
        m_tok.append(mt)
        inter.append(jnp.exp(m - mt))
        w_loc.append(jnp.exp(u[:, blk] - u_max))
        floor.append(jnp.exp(-(b[:, blk] + mt)))
        m = b_last + m_top

    hrow = lax.broadcasted_iota(jnp.int32, (4 * nh, 2 * LANES), 0)
    hcol = lax.broadcasted_iota(jnp.int32, (4 * nh, 2 * LANES), 1)
    to_heads = jnp.where((hrow < 3 * nh) & (hcol // d == hrow % nh), 1.0, 0.0).astype(BF16)
    frow = lax.broadcasted_iota(jnp.int32, (4 * nh, 4 * LANES), 0)
    fcol = lax.broadcasted_iota(jnp.int32, (4 * nh, 4 * LANES), 1)
    to_full = jnp.where((frow < 3 * nh) & (fcol // LANES == frow % nh), 1.0, 0.0).astype(BF16)

    def spread(chunks, onehot):
        val = jnp.concatenate(chunks, axis=1)
        hi, mid, lo = _split3(val)
        pieces = jnp.concatenate([hi.astype(F32), mid.astype(F32), lo.astype(F32), jnp.zeros_like(val)], axis=0)
        return _dot_tn(pieces.astype(BF16), onehot)

    w_scr[...] = spread(w_loc, to_heads)
    inter_scr[...] = spread(inter, to_heads)
    floor_scr[...] = spread(floor, to_heads)
    mfull_scr[...] = spread(m_tok, to_full)

    arow = lax.broadcasted_iota(jnp.int32, (LANES, 2 * LANES), 0)
    acol = lax.broadcasted_iota(jnp.int32, (LANES, 2 * LANES), 1)
    blk2 = (arow // d) == ((acol & (LANES - 1)) // d)
    ones_v = jnp.ones((L, LANES), BF16)

    def head_rows(ref, c, pr):
        top = jnp.broadcast_to(ref[c, 2 * pr:2 * pr + 1, :], (d, LANES))
        bot = jnp.broadcast_to(ref[c, 2 * pr + 1:2 * pr + 2, :], (d, LANES))
        half = jnp.concatenate([top, bot], axis=0)
        return jnp.concatenate([half, half], axis=1)

    def state_group(g, carry):
        local = []
        for cc in range(group):
            r0 = pl.multiple_of((g * group + cc) * L, L)
            for pr in range(pairs):
                ps = slice(LANES * pr, LANES * (pr + 1))
                kw = (k_scr[pl.ds(r0, L), ps].astype(F32) * w_scr[pl.ds(r0, L), ps]).astype(BF16)
                vo = jnp.concatenate([v_ref[pl.ds(r0, L), ps], ones_v], axis=1)
                local.append(jnp.where(blk2, _dot_tn(kw, vo), 0.0))
        state = list(carry)
        for cc in range(group):
            c = g * group + cc
            for pr in range(pairs):
                cprev_scr[c, pr] = state[pr].astype(BF16)
                state[pr] = head_rows(eo_scr, c, pr) * state[pr] + head_rows(el_scr, c, pr) * local[cc * pairs + pr]
        return tuple(state)

    lax.fori_loop(0, NC // group, state_group, tuple(jnp.zeros((LANES, 2 * LANES), F32) for _ in range(pairs)))

    li = lax.broadcasted_iota(jnp.int32, (L, L), 0)
    si = lax.broadcasted_iota(jnp.int32, (L, L), 1)
    causal = si <= li
    lane = lax.broadcasted_iota(jnp.int32, (L, LANES), 1)
    h0_lane = lane < d
    avg = jnp.where((li // d) == (si // d), 1.0 / d, 0.0).astype(BF16)

    def dot2(val, rhs):
        hi = val.astype(BF16)
        return _dot(hi, rhs) + _dot((val - hi.astype(F32)).astype(BF16), rhs)

    def out_group(g, carry):
        units = [(cc, pr) for cc in range(group) for pr in range(pairs)]
        chunk_of = {u_: g * group + u_[0] for u_ in units}
        rows = {u_: pl.multiple_of(chunk_of[u_] * L, L) for u_ in units}
        qc, sc, pv = {}, {}, {}
        for u_ in units:
            c, pr = chunk_of[u_], u_[1]
            ps = slice(LANES * pr, LANES * (pr + 1))
            qp = q_scr[pl.ds(rows[u_], L), ps]
            kp = k_scr[pl.ds(rows[u_], L), ps]
            qc[u_] = _dot(qp, cprev_scr[c, pr])
            sc[u_] = [_dot_nt(jnp.where(h0_lane if hh == 0 else lane >= d, qp, jnp.zeros_like(qp)), kp)
                      for hh in range(2)]
        for u_ in units:
            pr = u_[1]
            ps = slice(LANES * pr, LANES * (pr + 1))
            vo = jnp.concatenate([v_ref[pl.ds(rows[u_], L), ps], ones_v], axis=1)
            pv[u_] = []
            for hh in range(2):
                hd = 2 * pr + hh
                u_row = ut_scr[hd:hd + 1, pl.ds(rows[u_], L)]
                decay = jnp.where(causal, jnp.exp(u_row - mfull_scr[pl.ds(rows[u_], L), LANES * hd:LANES * (hd + 1)]), 0.0)
                pv[u_].append(_dot((sc[u_][hh] * decay).astype(BF16), vo))
        hg, cen = {}, {}
        for u_ in units:
            pr = u_[1]
            ps = slice(LANES * pr, LANES * (pr + 1))
            it = inter_scr[pl.ds(rows[u_], L), ps]
            num = it * qc[u_][:, :LANES] + jnp.where(h0_lane, pv[u_][0][:, :LANES], pv[u_][1][:, :LANES])
            den = it * qc[u_][:, LANES:] + jnp.where(h0_lane, pv[u_][0][:, LANES:], pv[u_][1][:, LANES:])
            hcell = num / jnp.maximum(jnp.abs(den), floor_scr[pl.ds(rows[u_], L), ps])
            hg[u_] = _sigmoid(o_ref[pl.ds(rows[u_], L), ps]) * hcell
        for u_ in units:
            cen[u_] = hg[u_] - dot2(hg[u_], avg)
        for u_ in units:
            pr = u_[1]
            ps = slice(LANES * pr, LANES * (pr + 1))
            var = dot2(cen[u_] * cen[u_], avg)
            y_ref[pl.ds(rows[u_], L), ps] = (cen[u_] * lax.rsqrt(var + NORM_EPS) * gmh_ref[:, ps]
                                             + skip_ref[:, ps] * xc_scr[pl.ds(rows[u_], L), ps])
        return carry

    lax.fori_loop(0, NC // group, out_group, 0)


def _mlstm(cx, mv, mo, gif_t, conv_w, conv_b, w_q_m, w_k_m, b_igate, b_fgate, g_mh, skip_m, B, S):
    def blockdiag(w, scale):
        eye = jnp.eye(MLSTM_HEADS, dtype=F32)[:, None, :, None]
        return (w[:, :, None, :] * (eye * scale)).reshape(WIDTH_C, WIDTH_C).astype(BF16)

    nc = S // MLSTM_TILE
    wq = blockdiag(w_q_m, MLSTM_DIM ** -0.5)
    wk = blockdiag(w_k_m, 1.0)
    bias8 = jnp.concatenate([b_igate, b_fgate])
    br = jnp.broadcast_to(bias8[:, None], (2 * MLSTM_HEADS, S))
    seq = lambda w: pl.BlockSpec((S, w), lambda b: (b, 0))
    full = lambda a: pl.BlockSpec(a.shape, lambda b: (0,) * a.ndim)
    row = lambda a: a.reshape(1, -1)
    args = [conv_w, row(conv_b), wq, wk, br, row(g_mh), row(skip_m)]
    tok = lambda w, dt: pltpu.VMEM((S, w), dt)
    per_chunk = pltpu.VMEM((nc, MLSTM_HEADS, LANES), F32)
    return pl.pallas_call(
        _mlstm_kernel,
        grid=(B,),
        in_specs=[seq(WIDTH_C), seq(WIDTH_C), seq(WIDTH_C),
                  pl.BlockSpec((2 * MLSTM_HEADS, S), lambda b: (0, b))] + [full(a) for a in args],
        out_specs=seq(WIDTH_C),
        out_shape=jax.ShapeDtypeStruct((B * S, WIDTH_C), F32),
        scratch_shapes=[tok(WIDTH_C, F32), tok(WIDTH_C, BF16), tok(WIDTH_C, BF16),
                        tok(WIDTH_C, F32), tok(WIDTH_C, F32), tok(WIDTH_C, F32), tok(4 * LANES, F32),
                        pltpu.VMEM((2 * MLSTM_HEADS, S), F32), per_chunk, per_chunk,
                        pltpu.VMEM((nc, MLSTM_HEADS // 2, LANES, 2 * LANES), BF16)],
        compiler_params=_params("parallel"),
        name="mlstm_mixer",
    )(cx, mv, mo, gif_t, *args)


def _mix_ffn_kernel(ya_ref, yb_ref, yc_ref, x_ref, gate1_ref, ga_ref, gb_ref, wo_ref,
                    sh_ref, sc_ref, gate_ref, g2_ref, wgu_ref, wd32_ref, gf_ref, o_ref, wd_ref, *, final_norm):
    @pl.when(pl.program_id(0) == 0)
    def _():
        wd_ref[...] = wd32_ref[0].astype(BF16)

    a = _rms(ya_ref[...], ga_ref[...]).astype(BF16)
    b = _rms(yb_ref[...], gb_ref[...]).astype(BF16)
    c = yc_ref[...].astype(BF16)
    mixed = (_dot(a, wo_ref[0:WIDTH_A, :]) + _dot(b, wo_ref[WIDTH_A:WIDTH_A + WIDTH_B, :])
             + _dot(c, wo_ref[WIDTH_A + WIDTH_B:, :]))
    x = x_ref[...] + gate1_ref[0] * mixed
    h = (_rms(x, g2_ref[...]) * (1.0 + sc_ref[0]) + sh_ref[0]).astype(BF16)
    acc = jnp.zeros(x.shape, F32)
    for j in range(FFN_HIDDEN // FFN_CHUNK):
        gate = _dot(h, wgu_ref[:, FFN_CHUNK * j:FFN_CHUNK * (j + 1)])
        up = _dot(h, wgu_ref[:, FFN_HIDDEN + FFN_CHUNK * j:FFN_HIDDEN + FFN_CHUNK * (j + 1)])
        act = (_silu(gate) * up).astype(BF16)
        acc = acc + _dot(act, wd_ref[FFN_CHUNK * j:FFN_CHUNK * (j + 1), :])
    y = x + gate_ref[0] * acc
    if final_norm:
        y = _rms(y, gf_ref[...])
    o_ref[...] = y


def _mix_ffn(ya, yb, yc, x2, mod3, mod_first, ga, gb, w_out, g2, wgu, w_down, layer, gf, S, final_norm):
    T = x2.shape[0]
    tm = TM_PROJ
    per_b = S // tm
    row = lambda w: pl.BlockSpec((tm, w), lambda i: (i, 0))
    full = lambda a: pl.BlockSpec(a.shape, lambda i: (0,) * a.ndim)
    modspec = lambda k: _mod_spec(per_b, mod_first, k)
    return pl.pallas_call(
        functools.partial(_mix_ffn_kernel, final_norm=final_norm),
        grid=(T // tm,),
        in_specs=[row(WIDTH_A), row(WIDTH_B), row(WIDTH_C), row(D_MODEL), modspec(2), full(ga), full(gb), full(w_out),
                  modspec(3), modspec(4), modspec(5), full(g2), full(wgu),
                  pl.BlockSpec((1,) + w_down.shape[1:], lambda i: (layer, 0, 0)), full(gf)],
        out_specs=row(D_MODEL),
        out_shape=jax.ShapeDtypeStruct((T, D_MODEL), F32),
        scratch_shapes=[pltpu.VMEM(w_down.shape[1:], BF16)],
        compiler_params=_params("arbitrary"),
        name="mix_ffn_final" if final_norm else "mix_ffn",
    )(ya, yb, yc, x2, mod3, ga, gb, w_out, mod3, mod3, mod3, g2, wgu, w_down, gf)


def _head_tile_perm():
    idx = []
    for r in range(NSA_REP):
        idx += list(range(HEAD_DIM * r, HEAD_DIM * (r + 1)))
        idx += list(range(HEAD_DIM * (NSA_REP + r), HEAD_DIM * (NSA_REP + r + 1)))
    return np.asarray(idx, np.int32)


def _in_cols():
    std = np.full((N_STD,), -1, np.int64)
    tr = np.full((N_T,), -1, np.int64)
    off = 0
    tr[TSEG_Q:TSEG_Q + WIDTH_A] = _head_tile_perm()
    off += WIDTH_A
    std[SEG_CKV:SEG_CKV + 2 * LANES] = off + np.arange(2 * LANES)
    off += 2 * LANES
    std[SEG_SK:SEG_SK + LANES] = off + np.arange(LANES)
    off += LANES
    tr[TSEG_SV:TSEG_SV + LANES] = off + np.arange(LANES)
    off += LANES
    std[SEG_WK:SEG_WK + LANES] = off + np.arange(LANES)
    off += LANES
    tr[TSEG_WV:TSEG_WV + LANES] = off + np.arange(LANES)
    off += LANES
    tr[TSEG_G:TSEG_G + N_GATES] = off + np.arange(N_GATES)
    off += N_GATES
    std[SEG_QL:SEG_QL + MLA_Q_LORA] = off + np.arange(MLA_Q_LORA)
    off += MLA_Q_LORA
    std[SEG_KVL:SEG_KVL + MLA_KV_LORA] = off + np.arange(MLA_KV_LORA)
    off += MLA_KV_LORA
    std[SEG_KR + MLA_NOPE:SEG_KR + MLA_NOPE + MLA_ROPE] = off + np.arange(MLA_ROPE)
    off += MLA_ROPE
    for seg in (SEG_CX, SEG_MV, SEG_MO):
        std[seg:seg + WIDTH_C] = off + np.arange(WIDTH_C)
        off += WIDTH_C
    tr[TSEG_IF:TSEG_IF + 2 * MLSTM_HEADS] = off + np.arange(2 * MLSTM_HEADS)
    return std, tr


def _gather_cols(w, cols):
    pieces, start = [], 0
    for i in range(1, len(cols) + 1):
        run_ends = i == len(cols) or (cols[i] != cols[i - 1] + 1 if cols[i - 1] >= 0 else cols[i] >= 0)
        if run_ends:
            first = int(cols[start])
            pieces.append(w[:, first:first + i - start] if first >= 0 else jnp.zeros((w.shape[0], i - start), w.dtype))
            start = i
    return jnp.concatenate(pieces, axis=1)


def _layer_weights(l, w_in, w_uq, w_ukv, w_out, w_gu, g_out_a):
    std, tr = _in_cols()
    w_std = _gather_cols(w_in[l], std).astype(BF16)
    w_t = _gather_cols(w_in[l], tr).T.astype(BF16)
    cq = np.full((MLA_HEADS * LANES,), -1, np.int64)
    ck = np.full((MLA_HEADS * LANES,), -1, np.int64)
    cv = np.zeros((WIDTH_B,), np.int64)
    dq = MLA_NOPE + MLA_ROPE
    dkv = MLA_NOPE + MLA_V
    for hd in range(MLA_HEADS):
        cq[LANES * hd:LANES * hd + dq] = dq * hd + np.arange(dq)
        ck[LANES * hd:LANES * hd + MLA_NOPE] = dkv * hd + np.arange(MLA_NOPE)
        cv[MLA_V * hd:MLA_V * (hd + 1)] = dkv * hd + MLA_NOPE + np.arange(MLA_V)
    wuqT = _gather_cols(w_uq[l], cq).T.astype(BF16)
    wkm = _gather_cols(w_ukv[l], ck).astype(BF16)
    wvmT = _gather_cols(w_ukv[l], cv).T.astype(BF16)
    perm = _head_tile_perm()
    head_rows = [w_out[l][int(o):int(o) + HEAD_DIM] for o in perm[::HEAD_DIM]]
    wo = jnp.concatenate(head_rows + [w_out[l][WIDTH_A:]], axis=0).astype(BF16)
    ga = _gather_cols(g_out_a[l].reshape(1, WIDTH_A), perm)
    return w_std, w_t, wuqT, wkm, wvmT, wo, ga, w_gu[l].astype(BF16)


def kernel(x, c, positions, g_norm1, g_norm2, w_ada, b_ada, w_in, cmp_pos, w_cmp_k, w_cmp_v, g_out_a, g_q_lora, w_uq, g_kv_lora, w_ukv, g_out_b, conv_w, conv_b, w_q_m, w_k_m, b_igate, b_fgate, g_mh, skip_m, w_out, w_gu, w_down, g_final):
    B, S, D = x.shape
    T = B * S
    x2 = x.reshape(T, D)
    tabs, tabs_t = _rope_tables(positions)
    mod3 = _ada(c, w_ada, b_ada).reshape(DEPTH * B * N_MOD, 1, D)
    row = lambda v: v.reshape(1, -1)
    for l in range(DEPTH):
        w_std, w_t, wuqT, wkm, wvmT, wo, ga, wgu = _layer_weights(l, w_in, w_uq, w_ukv, w_out, w_gu, g_out_a)
        (qaT, gT, ck, cv, sk_aug, wk, svT, wvT, qmT, km, vmT, cx, mv, mo, gif_t) = _inproj(
            x2, mod3, l * B * N_MOD, row(g_norm1[l]), w_std, w_t, tabs, tabs_t, row(g_q_lora[l]), wuqT,
            row(g_kv_lora[l]), wkm, wvmT, S)
        kc, vcT = _compress(ck, cv, cmp_pos[l], w_cmp_k[l], w_cmp_v[l], B, S)
        ya = _nsa(qaT, gT, kc, vcT, sk_aug, svT, wk, wvT, B, S)
        yb = _mla(qmT, km, vmT, B, S)
        yc = _mlstm(cx, mv, mo, gif_t, conv_w[l], conv_b[l], w_q_m[l], w_k_m[l], b_igate[l], b_fgate[l],
                    g_mh[l], skip_m[l], B, S)
        x2 = _mix_ffn(ya, yb, yc, x2, mod3, l * B * N_MOD, ga, row(g_out_b[l]), wo, row(g_norm2[l]),
                      wgu, w_down, l, row(g_final), S, final_norm=(l == DEPTH - 1))
    return x2.reshape(B, S, D)
```

```python
import functools

import numpy as np
import jax
import jax.numpy as jnp
from jax import lax
from jax.experimental import pallas as pl
from jax.experimental.pallas import tpu as pltpu

F32 = jnp.float32
BF16 = jnp.bfloat16

D_MODEL = 1024
DEPTH = 2
HEAD_DIM = 64
ROPE_THETA = 500000.0
NSA_ROT_HALF = HEAD_DIM // 8
NORM_EPS = 1e-6

NSA_HEADS = 6
NSA_KV_HEADS = 2
NSA_REP = NSA_HEADS // NSA_KV_HEADS
CMP_LEN = 32
CMP_STRIDE = 16
SLC_LEN = 64
SLC_TOPK = 16
WINDOW = 512

MLA_HEADS = 6
MLA_Q_LORA = 256
MLA_KV_LORA = 128
MLA_NOPE = 64
MLA_ROPE = 32
MLA_V = 64

MLSTM_HEADS = 4
MLSTM_DIM = 64
MLSTM_CONV = 4

WIDTH_A = NSA_HEADS * HEAD_DIM
WIDTH_B = MLA_HEADS * MLA_V
WIDTH_C = MLSTM_HEADS * MLSTM_DIM
FFN_HIDDEN = 2816
N_GATES = 3 * NSA_HEADS
GATE_ROWS = 24

LANES = 128
NEG = -1e30
LOG2E = 1.4426950408889634
VMEM_LIMIT = 56 * 1024 * 1024

TM_PROJ = 512
TQ_NSA = 256
TK_SLC = 256
CHAIN_LANES = 256
TQ_MLA = 256
FFN_CHUNK = 256
MLSTM_TILE = 128
MLSTM_GROUP = 8
MXU_LOOKAHEAD = 4

SEG_CKV, SEG_SK, SEG_WK, SEG_QL, SEG_KVL, SEG_KR, SEG_CX, SEG_MV, SEG_MO = (
    0, 256, 384, 512, 768, 896, 1024, 1280, 1536)
N_STD = 1792
TSEG_Q, TSEG_G, TSEG_SV, TSEG_WV, TSEG_IF = 0, 384, 416, 544, 672
TSEG_G_ROWS = 32
TSEG_IF_ROWS = 16
N_T = 688


def _params(*sem):
    return pltpu.CompilerParams(dimension_semantics=sem, vmem_limit_bytes=VMEM_LIMIT)


def _dot(a, b):
    return jnp.dot(a, b, preferred_element_type=F32)


def _dot_nt(a, b):
    return lax.dot_general(a, b, (((1,), (1,)), ((), ())), preferred_element_type=F32)


def _dot_tn(a, b):
    return lax.dot_general(a, b, (((0,), (0,)), ((), ())), preferred_element_type=F32)


def _split3(x):
    hi = x.astype(BF16)
    r1 = x - hi.astype(F32)
    mid = r1.astype(BF16)
    lo = (r1 - mid.astype(F32)).astype(BF16)
    return hi, mid, lo


def _rms(x, g):
    return x * lax.rsqrt(jnp.mean(x * x, axis=-1, keepdims=True) + NORM_EPS) * g


def _sigmoid(x):
    return 1.0 / (1.0 + jnp.exp(-x))


def _silu(x):
    return x * _sigmoid(x)


def _rope(x, cos, sin, half, x1_lane):
    xr = jnp.where(x1_lane, -pltpu.roll(x, LANES - half, 1), pltpu.roll(x, half, 1))
    return x * cos + xr * sin


def _ada_kernel(c_ref, w_ref, b_ref, o_ref):
    c = c_ref[...]
    ca = _silu(c).astype(BF16)
    o_ref[0] = _dot(ca, w_ref[0].astype(BF16)) + b_ref[0]


def _ada(c, w_ada, b_ada):
    L, D, N = w_ada.shape
    B = c.shape[0]
    tn = 1536
    return pl.pallas_call(
        _ada_kernel,
        grid=(L, N // tn),
        in_specs=[pl.BlockSpec((B, D), lambda l, j: (0, 0)),
                  pl.BlockSpec((1, D, tn), lambda l, j: (l, 0, j)),
                  pl.BlockSpec((1, 1, tn), lambda l, j: (l, 0, j))],
        out_specs=pl.BlockSpec((1, B, tn), lambda l, j: (l, 0, j)),
        out_shape=jax.ShapeDtypeStruct((L, B, N), F32),
        compiler_params=_params("parallel", "parallel"),
        name="ada_mod",
    )(c, w_ada, b_ada.reshape(L, 1, N))


ROPE_FREQS = NSA_ROT_HALF + MLA_ROPE // 2


def _rope_kernel(posr_ref, invc_ref, spread_ref, one_ref, cn_ref, sn_ref, cm_ref, sm_ref, cnt_ref, snt_ref, cmt_ref, smt_ref):
    tm = posr_ref.shape[1]
    rows_t = invc_ref.shape[0]
    ang_t = jnp.concatenate([invc_ref[...]] * (tm // LANES), axis=1) * jnp.broadcast_to(posr_ref[...], (rows_t, tm))
    c_t, s_t = jnp.cos(ang_t), jnp.sin(ang_t)
    cnt_ref[...] = c_t[:NSA_ROT_HALF]
    snt_ref[...] = s_t[:NSA_ROT_HALF]
    cmt_ref[...] = c_t[NSA_ROT_HALF:ROPE_FREQS]
    smt_ref[...] = s_t[NSA_ROT_HALF:ROPE_FREQS]

    def spread(val):
        hi, mid, lo = _split3(val)
        pieces = jnp.concatenate([hi.astype(F32), mid.astype(F32), lo.astype(F32)], axis=0).astype(BF16)
        return _dot_tn(pieces, spread_ref[...])

    c = spread(c_t) + one_ref[...]
    s = spread(s_t)
    cn_ref[...] = c[:, :LANES]
    cm_ref[...] = c[:, LANES:]
    sn_ref[...] = s[:, :LANES]
    sm_ref[...] = s[:, LANES:]


def _rope_tables(positions):
    T = positions.size
    inv_n = jnp.power(ROPE_THETA, -jnp.arange(0, 2 * NSA_ROT_HALF, 2, dtype=F32) / (2 * NSA_ROT_HALF))
    inv_m = jnp.power(ROPE_THETA, -jnp.arange(0, MLA_ROPE, 2, dtype=F32) / MLA_ROPE)
    rows_t = 4 * NSA_ROT_HALF
    onehot = np.zeros((3 * rows_t, 2 * LANES), np.float32)
    rotated = np.zeros((1, 2 * LANES), np.float32)
    for k in range(3):
        for f in range(NSA_ROT_HALF):
            for lane in (f, f + NSA_ROT_HALF, HEAD_DIM + f, HEAD_DIM + f + NSA_ROT_HALF):
                onehot[rows_t * k + f, lane] = 1.0
                rotated[0, lane] = 1.0
        for f in range(MLA_ROPE // 2):
            for lane in (MLA_NOPE + f, MLA_NOPE + MLA_ROPE // 2 + f):
                onehot[rows_t * k + NSA_ROT_HALF + f, LANES + lane] = 1.0
                rotated[0, LANES + lane] = 1.0
    inv_col = jnp.concatenate([inv_n, inv_m, jnp.zeros((rows_t - ROPE_FREQS,), F32)])
    inv_col = jnp.broadcast_to(inv_col[:, None], (rows_t, LANES))
    tm = 2048
    spec = pl.BlockSpec((tm, LANES), lambda i: (i, 0))
    col = lambda h: pl.BlockSpec((h, tm), lambda i: (0, i))
    full = lambda shape: pl.BlockSpec(shape, lambda i: (0, 0))
    sds = jax.ShapeDtypeStruct
    outs = pl.pallas_call(
        _rope_kernel,
        grid=(T // tm,),
        in_specs=[col(1), full((rows_t, LANES)), full((3 * rows_t, 2 * LANES)), full((1, 2 * LANES))],
        out_specs=[spec] * 4 + [col(NSA_ROT_HALF)] * 2 + [col(MLA_ROPE // 2)] * 2,
        out_shape=[sds((T, LANES), F32)] * 4 + [sds((NSA_ROT_HALF, T), F32)] * 2 + [sds((MLA_ROPE // 2, T), F32)] * 2,
        compiler_params=_params("parallel"),
        name="rope_tables",
    )(positions.reshape(1, T).astype(F32), inv_col, jnp.asarray(onehot, BF16), jnp.asarray(1.0 - rotated))
    return outs[:4], outs[4:]


def _rope_rows(t, offset, half, cos, sin):
    x1, x2 = t[offset:offset + half], t[offset + half:offset + 2 * half]
    return x1 * cos - x2 * sin, x2 * cos + x1 * sin


def _inproj_kernel(x_ref, sh_ref, sc_ref, g1_ref, ws_ref, wt_ref, cn_ref, sn_ref, cm_ref, sm_ref,
                   cnT_ref, snT_ref, cmT_ref, smT_ref, gq_ref, wuqT_ref, gkv_ref, wkm_ref, wvmT_ref,
                   qaT_ref, gT_ref, ck_ref, cv_ref, ska_ref, wk_ref, svT_ref, wvT_ref,
                   qmT_ref, km_ref, vmT_ref, cx_ref, mv_ref, mo_ref, gif_ref, *, per_b):
    tm = x_ref.shape[0]
    x = x_ref[...]
    h = _rms(x, g1_ref[...]) * (1.0 + sc_ref[0]) + sh_ref[0]
    hb = h.astype(BF16)

    def seg(start, width):
        return _dot(hb, ws_ref[:, start:start + width])

    lane = lax.broadcasted_iota(jnp.int32, (1, LANES), 1)
    x1_n = (lane % HEAD_DIM) < NSA_ROT_HALF
    x1_m = lane < MLA_NOPE + MLA_ROPE // 2
    cn, sn, cm, sm = cn_ref[...], sn_ref[...], cm_ref[...], sm_ref[...]
    rope_n = lambda t: _rope(t, cn, sn, NSA_ROT_HALF, x1_n)
    rope_m = lambda t: _rope(t, cm, sm, MLA_ROPE // 2, x1_m)

    qn = _rms(seg(SEG_QL, MLA_Q_LORA), gq_ref[...]).astype(BF16)
    kvl_kr = seg(SEG_KVL, 2 * LANES)
    kvn = _rms(kvl_kr[:, :LANES], gkv_ref[...]).astype(BF16)
    kr = rope_m(kvl_kr[:, LANES:])

    out_t = _dot_nt(wt_ref[...], hb)
    seg_t = lambda start, height: out_t[start:start + height]

    qt = seg_t(TSEG_Q, WIDTH_A)
    cnt, snt = cnT_ref[...], snT_ref[...]
    parts = []
    for hd in range(NSA_HEADS):
        o = HEAD_DIM * hd
        parts += list(_rope_rows(qt, o, NSA_ROT_HALF, cnt, snt)) + [qt[o + 2 * NSA_ROT_HALF:o + HEAD_DIM]]
    qaT_ref[...] = (jnp.concatenate(parts, axis=0) * (HEAD_DIM ** -0.5 * LOG2E)).astype(BF16)
    gT_ref[...] = _sigmoid(seg_t(TSEG_G, TSEG_G_ROWS))[:GATE_ROWS]
    svT_ref[...] = seg_t(TSEG_SV, LANES).astype(BF16)
    wvT_ref[...] = seg_t(TSEG_WV, LANES).astype(BF16)
    gif_ref[...] = seg_t(TSEG_IF, TSEG_IF_ROWS)[:2 * MLSTM_HEADS]

    ckv = seg(SEG_CKV, 2 * LANES)
    ck_ref[...] = rope_n(ckv[:, :LANES])
    cv_ref[...] = ckv[:, LANES:]
    ska_ref[:, :LANES] = rope_n(seg(SEG_SK, LANES)).astype(BF16)
    srow = (pl.program_id(0) % per_b) * tm + lax.broadcasted_iota(jnp.int32, (tm, LANES), 0)
    lane2 = lax.broadcasted_iota(jnp.int32, (tm, LANES), 1)
    code = (lane2 < 4 * SLC_TOPK) & ((lane2 & (2 * SLC_TOPK - 1)) == srow // SLC_LEN)
    ska_ref[:, LANES:] = jnp.where(code, 1.0, 0.0).astype(BF16)
    wk_ref[...] = rope_n(seg(SEG_WK, LANES)).astype(BF16)

    qmt = _dot_nt(wuqT_ref[...], qn)
    kk = _dot(kvn, wkm_ref[...])
    vmt = _dot_nt(wvmT_ref[...], kvn)
    cx_ref[...] = seg(SEG_CX, WIDTH_C)
    mv_ref[...] = seg(SEG_MV, WIDTH_C).astype(BF16)
    mo_ref[...] = seg(SEG_MO, WIDTH_C)
    cmt, smt = cmT_ref[...], smT_ref[...]
    parts = []
    for hd in range(MLA_HEADS):
        o = LANES * hd
        parts += [qmt[o:o + MLA_NOPE]] + list(_rope_rows(qmt, o + MLA_NOPE, MLA_ROPE // 2, cmt, smt))
        parts += [qmt[o + MLA_NOPE + MLA_ROPE:o + LANES]]
    qmT_ref[...] = (jnp.concatenate(parts, axis=0) * ((MLA_NOPE + MLA_ROPE) ** -0.5 * LOG2E)).astype(BF16)
    for hd in range(MLA_HEADS):
        km_ref[:, LANES * hd:LANES * (hd + 1)] = (kk[:, LANES * hd:LANES * (hd + 1)] + kr).astype(BF16)
    vmT_ref[...] = vmt.astype(BF16)


N_MOD = 6


def _mod_spec(per_b, first, k):
    return pl.BlockSpec((1, 1, D_MODEL), lambda i: (first + (i // per_b) * N_MOD + k, 0, 0))


def _inproj(x2, mod3, mod_first, g1, w_std, w_t, tabs, tabs_t, gq, wuqT, gkv, wkm, wvmT, S):
    T = x2.shape[0]
    tm = TM_PROJ
    per_b = S // tm
    row = lambda w: pl.BlockSpec((tm, w), lambda i: (i, 0))
    col = lambda h: pl.BlockSpec((h, tm), lambda i: (0, i))
    full = lambda a: pl.BlockSpec(a.shape, lambda i: (0,) * a.ndim)
    outs = [(WIDTH_A, BF16, True), (GATE_ROWS, F32, True), (LANES, F32, False), (LANES, F32, False), (2 * LANES, BF16, False),
            (LANES, BF16, False), (LANES, BF16, True), (LANES, BF16, True),
            (MLA_HEADS * LANES, BF16, True), (MLA_HEADS * LANES, BF16, False), (WIDTH_B, BF16, True),
            (WIDTH_C, F32, False), (WIDTH_C, BF16, False), (WIDTH_C, F32, False), (2 * MLSTM_HEADS, F32, True)]
    return pl.pallas_call(
        functools.partial(_inproj_kernel, per_b=per_b),
        grid=(T // tm,),
        in_specs=[row(D_MODEL), _mod_spec(per_b, mod_first, 0), _mod_spec(per_b, mod_first, 1),
                  full(g1), full(w_std), full(w_t)] + [row(LANES)] * 4
                 + [col(t.shape[0]) for t in tabs_t] + [full(gq), full(wuqT), full(gkv), full(wkm), full(wvmT)],
        out_specs=[col(w) if tr else row(w) for w, _, tr in outs],
        out_shape=[jax.ShapeDtypeStruct((w, T) if tr else (T, w), dt) for w, dt, tr in outs],
        compiler_params=_params("parallel"),
        name="in_proj",
    )(x2, mod3, mod3, g1, w_std, w_t, *tabs, *tabs_t, gq, wuqT, gkv, wkm, wvmT)


def _compress_kernel(xk_ref, xv_ref, wk_ref, wv_ref, pos_ref, wkf_ref, wvf_ref, kc_ref, vcT_ref):
    ng = xk_ref.shape[0] // CMP_STRIDE
    row = lax.broadcasted_iota(jnp.int32, (ng, LANES), 0)
    pos = pos_ref[...].astype(BF16)
    acc_k = jnp.zeros((ng, 2 * LANES), F32)
    acc_v = jnp.zeros((ng, 2 * LANES), F32)
    for t in range(CMP_STRIDE):
        tok = pl.ds(t, ng, stride=CMP_STRIDE)
        acc_k = acc_k + _dot(xk_ref[tok, :].astype(BF16), wk_ref[t])
        acc_v = acc_v + _dot(xv_ref[tok, :].astype(BF16), wv_ref[t])

    def finish(acc, wf_ref):
        const = _dot(pos, wf_ref[...].astype(BF16))
        both = acc[:, :LANES] + pltpu.roll(acc[:, LANES:], ng - 1, 0) + jnp.concatenate([const, const], axis=1)
        return jnp.where(row < ng - 1, both, 0.0)

    kc_ref[0] = finish(acc_k, wkf_ref).astype(BF16)
    vcT_ref[0] = finish(acc_v, wvf_ref).T.astype(BF16)


def _compress(ck, cv, cmp_pos, w_cmp_k, w_cmp_v, B, S):
    ng = S // CMP_STRIDE

    def per_token(w):
        a = w[:CMP_STRIDE * HEAD_DIM].reshape(CMP_STRIDE, HEAD_DIM, HEAD_DIM)
        b = w[CMP_STRIDE * HEAD_DIM:].reshape(CMP_STRIDE, HEAD_DIM, HEAD_DIM)
        z = jnp.zeros_like(a)
        top = jnp.concatenate([a, z, b, z], axis=2)
        bot = jnp.concatenate([z, a, z, b], axis=2)
        return jnp.concatenate([top, bot], axis=1).astype(BF16)

    full = lambda a: pl.BlockSpec(a.shape, lambda b: (0,) * a.ndim)
    wk3, wv3 = per_token(w_cmp_k), per_token(w_cmp_v)
    posf = cmp_pos.reshape(1, CMP_LEN * HEAD_DIM)
    ospec = pl.BlockSpec((1, ng, LANES), lambda b: (b, 0, 0))
    return pl.pallas_call(
        _compress_kernel,
        grid=(B,),
        in_specs=[pl.BlockSpec((S, LANES), lambda b: (b, 0)), pl.BlockSpec((S, LANES), lambda b: (b, 0)),
                  full(wk3), full(wv3), full(posf), full(w_cmp_k), full(w_cmp_v)],
        out_specs=[ospec, ospec],
        out_shape=[jax.ShapeDtypeStruct((B, ng, LANES), BF16)] * 2,
        compiler_params=_params("parallel"),
        name="nsa_compress",
    )(ck, cv, wk3, wv3, posf, w_cmp_k, w_cmp_v)


def _nsa_kernel(qT_ref, gT_ref, kc_ref, vcT_ref, sk_ref, svT_ref, wk_ref, wvT_ref, o_ref, qaug_scr, acc_scr, s_scr):
    tq = TQ_NSA
    cols = NSA_HEADS * tq
    pair = CHAIN_LANES
    t0 = pl.program_id(1) * tq
    n_slc = SLC_TOPK * 2

    frow = lax.broadcasted_iota(jnp.int32, (LANES, tq), 0)
    g0_row = frow < HEAD_DIM
    tiles = [qT_ref[LANES * r:LANES * (r + 1), :] for r in range(NSA_REP)]
    zero = jnp.zeros_like(tiles[0])
    q6 = jnp.concatenate([jnp.where(g0_row, t, zero) for t in tiles]
                         + [jnp.where(g0_row, zero, t) for t in tiles], axis=1)
    qaug_scr[0:LANES, :] = q6
    tq_l = t0 + (lax.broadcasted_iota(jnp.int32, (1, cols), 1) & (tq - 1))

    pairs = [slice(pair * pp, pair * (pp + 1)) for pp in range(cols // pair)]
    vrows = lambda pp: slice(HEAD_DIM * (pp // NSA_REP), HEAD_DIM * (pp // NSA_REP + 1))
    s = _dot(kc_ref[0], q6)
    span = WINDOW + tq
    start = pl.multiple_of(jnp.maximum(t0 - WINDOW, 0), tq)
    kw = wk_ref[pl.ds(start, span), :]
    win_scores = [_dot(kw, q6[:, sl]) for sl in pairs]

    tq_1 = tq_l[:, :tq]
    nrow = lax.broadcasted_iota(jnp.int32, (LANES, tq), 0)
    cmp_bias = jnp.where(nrow * CMP_STRIDE + (CMP_LEN - 1) <= tq_1, 0.0, NEG)
    s = s + jnp.concatenate([cmp_bias] * NSA_HEADS, axis=1)
    e = jnp.exp2(s - jnp.max(s, axis=0, keepdims=True))
    seen = jnp.where(tq_l >= CMP_LEN - 1, 1.0, 0.0)
    p = e * (seen / jnp.sum(e, axis=0, keepdims=True))
    p_b = p.astype(BF16)
    o_cmp = [_dot(vcT_ref[0, vrows(pp), :], p_b[:, sl]) for pp, sl in enumerate(pairs)]

    jr = lax.broadcasted_iota(jnp.int32, (n_slc, LANES), 0)
    nc = lax.broadcasted_iota(jnp.int32, (n_slc, LANES), 1)
    ovl = ((nc * CMP_STRIDE < jr * SLC_LEN + SLC_LEN) & (nc * CMP_STRIDE + CMP_LEN > jr * SLC_LEN)
           & (nc < LANES - 1))
    ovl = jnp.where(ovl, 1.0, 0.0).astype(BF16)
    jq = lax.broadcasted_iota(jnp.int32, (n_slc, tq), 0)
    tl = t0 + lax.broadcasted_iota(jnp.int32, (n_slc, tq), 1)
    cur = tl // SLC_LEN
    forced = (jq == 0) | (jq == cur) | (jq == cur - 1)
    future = jq * SLC_LEN > tl
    bias_t = []
    for g in range(NSA_KV_HEADS):
        pg = p[:, (3 * g) * tq:(3 * g + 1) * tq] + p[:, (3 * g + 1) * tq:(3 * g + 2) * tq] + p[:, (3 * g + 2) * tq:(3 * g + 3) * tq]
        imp = sum(_dot(ovl, part) for part in _split3(pg))
        imp = jnp.where(forced, jnp.inf, imp)
        imp = jnp.where(future, -jnp.inf, imp)
        rows8 = [imp[8 * r:8 * (r + 1)] for r in range(n_slc // 8)]
        sub = lax.broadcasted_iota(jnp.int32, (8, tq), 0)
        ranks = [jnp.zeros((8, tq), F32) for _ in rows8]
        for jp in range(n_slc):
            rv = imp[jp:jp + 1, :]
            for r, blk in enumerate(rows8):
                if 8 * r > jp:
                    ahead = rv >= blk
                elif 8 * r + 7 < jp:
                    ahead = rv > blk
                else:
                    ahead = (rv > blk) | ((rv == blk) & (sub > jp - 8 * r))
                ranks[r] = ranks[r] + jnp.where(ahead, 1.0, 0.0)
        rank = jnp.concatenate(ranks, axis=0)
        bias_t.append(jnp.where(rank < float(SLC_TOPK), 0.0, NEG).astype(BF16))
    zb = jnp.zeros((n_slc, NSA_REP * tq), BF16)
    qaug_scr[LANES:LANES + n_slc, :] = jnp.concatenate([bias_t[0]] * NSA_REP + [zb], axis=1)
    qaug_scr[LANES + n_slc:LANES + 2 * n_slc, :] = jnp.concatenate([zb] + [bias_t[1]] * NSA_REP, axis=1)
    qaug_scr[LANES + 2 * n_slc:, :] = jnp.zeros((LANES - 2 * n_slc, cols), BF16)

    vwt = wvT_ref[:, pl.ds(start, span)]
    wrow = start + lax.broadcasted_iota(jnp.int32, (span, pair), 0)
    o_win = []
    for pp, (sl, sc) in enumerate(zip(pairs, win_scores)):
        in_window = lax.bitcast_convert_type(tq_l[:, sl] - wrow, jnp.uint32) < jnp.uint32(WINDOW)
        sc = jnp.where(in_window, sc, NEG)
        ew = jnp.exp2(sc - jnp.max(sc, axis=0, keepdims=True))
        o_win.append(_dot(vwt[vrows(pp)], ew.astype(BF16)) / jnp.sum(ew, axis=0, keepdims=True))

    acc_scr[...] = jnp.zeros((HEAD_DIM, cols), F32)
    krow = lax.broadcasted_iota(jnp.int32, (TK_SLC, pair), 0)

    def slc_scores(k0, sl):
        return _dot(sk_ref[pl.ds(k0, TK_SLC), :], qaug_scr[:, sl])

    nch = len(pairs)
    la = s_scr.shape[0]
    for pp in range(la):
        s_scr[pp] = slc_scores(0, pairs[pp])

    def slc_tile(k0, m, l, masked):
        vt = svT_ref[:, pl.ds(k0, TK_SLC)]
        ms, ls, accs = [], [], []
        ahead = [s_scr[pp] for pp in range(la)]
        for pp, sl in enumerate(pairs):
            sc = ahead.pop(0)
            nxt = pp + la
            if nxt < nch:
                ahead.append(slc_scores(k0, pairs[nxt]))
            elif not masked:
                s_scr[nxt - nch] = slc_scores(pl.multiple_of(k0 + TK_SLC, TK_SLC), pairs[nxt - nch])
            if masked:
                sc = jnp.where(k0 + krow <= tq_l[:, sl], sc, NEG)
            m_new = jnp.maximum(m[:, sl], jnp.max(sc, axis=0, keepdims=True))
            alpha = jnp.exp2(m[:, sl] - m_new)
            pe = jnp.exp2(sc - m_new)
            ls.append(alpha * l[:, sl] + jnp.sum(pe, axis=0, keepdims=True))
            accs.append(alpha * acc_scr[:, sl] + _dot(vt[vrows(pp)], pe.astype(BF16)))
            ms.append(m_new)
        acc_scr[...] = jnp.concatenate(accs, axis=1)
        return jnp.concatenate(ms, axis=1), jnp.concatenate(ls, axis=1)

    def slc_step(kt, carry):
        return slc_tile(pl.multiple_of(kt * TK_SLC, TK_SLC), carry[0], carry[1], False)

    n_full = t0 // TK_SLC
    m, l = lax.fori_loop(0, n_full, slc_step, (jnp.full((1, cols), NEG, F32), jnp.zeros((1, cols), F32)))
    m, l = slc_tile(pl.multiple_of(n_full * TK_SLC, TK_SLC), m, l, True)
    o_slc = acc_scr[...] / l

    gt = gT_ref[...]
    mixed = []
    for hd, sl in enumerate(pairs):
        gate = lambda j: gt[3 * hd + j:3 * hd + j + 1, :]
        mixed.append(gate(0) * o_cmp[hd] + gate(1) * o_slc[:, sl] + gate(2) * o_win[hd])
    for r in range(NSA_REP):
        o_ref[:, LANES * r:LANES * (r + 1)] = jnp.concatenate([mixed[r], mixed[NSA_REP + r]], axis=0).T


def _nsa(qaT, gT, kc, vcT, sk_aug, svT, wk, wvT, B, S):
    tq = TQ_NSA
    assert CHAIN_LANES == tq, "the kernel treats one softmax chain as one head"
    nq = S // tq
    cols = NSA_HEADS * tq
    qcol =lambda h: pl.BlockSpec((h, tq), lambda b, i: (0, b * nq + i))
    seq = lambda w: pl.BlockSpec((S, w), lambda b, i: (b, 0))
    seqT = pl.BlockSpec((LANES, S), lambda b, i: (0, b))
    cspec = pl.BlockSpec((1, LANES, LANES), lambda b, i: (b, 0, 0))
    return pl.pallas_call(
        _nsa_kernel,
        grid=(B, nq),
        in_specs=[qcol(WIDTH_A), qcol(gT.shape[0]), cspec, cspec, seq(2 * LANES), seqT, seq(LANES), seqT],
        out_specs=pl.BlockSpec((tq, WIDTH_A), lambda b, i: (b * nq + i, 0)),
        out_shape=jax.ShapeDtypeStruct((B * S, WIDTH_A), F32),
        scratch_shapes=[pltpu.VMEM((2 * LANES, cols), BF16), pltpu.VMEM((HEAD_DIM, cols), F32),
                        pltpu.VMEM((cols // CHAIN_LANES, TK_SLC, CHAIN_LANES), F32)],
        compiler_params=_params("parallel", "arbitrary"),
        name="nsa_attention",
    )(qaT, gT, kc, vcT, sk_aug, svT, wk, wvT)


def _mla_kernel(qT_ref, k_ref, vT_ref, o_ref, acc_scr, s_scr):
    tq = TQ_MLA
    t0 = pl.program_id(1) * tq
    tq_l = t0 + lax.broadcasted_iota(jnp.int32, (1, tq), 1)
    krow = lax.broadcasted_iota(jnp.int32, (tq, tq), 0)
    acc_scr[...] = jnp.zeros((MLA_HEADS, MLA_V, tq), F32)

    def scores(k0, hd):
        k = k_ref[pl.ds(k0, tq), LANES * hd:LANES * (hd + 1)]
        return _dot(k, qT_ref[LANES * hd:LANES * (hd + 1), :])

    for hd in range(MXU_LOOKAHEAD):
        s_scr[hd] = scores(0, hd)

    def tile(k0, ms, ls, masked):
        new_m, new_l, accs = [], [], []
        ahead = [s_scr[hd] for hd in range(MXU_LOOKAHEAD)]
        for hd in range(MLA_HEADS):
            sc = ahead.pop(0)
            nxt = hd + MXU_LOOKAHEAD
            if nxt < MLA_HEADS:
                ahead.append(scores(k0, nxt))
            elif not masked:
                s_scr[nxt - MLA_HEADS] = scores(pl.multiple_of(k0 + tq, tq), nxt - MLA_HEADS)
            if masked:
                sc = jnp.where(k0 + krow <= tq_l, sc, NEG)
            m_new = jnp.maximum(ms[hd], jnp.max(sc, axis=0, keepdims=True))
            alpha = jnp.exp2(ms[hd] - m_new)
            pe = jnp.exp2(sc - m_new)
            new_l.append(alpha * ls[hd] + jnp.sum(pe, axis=0, keepdims=True))
            vt = vT_ref[MLA_V * hd:MLA_V * (hd + 1), pl.ds(k0, tq)]
            accs.append(alpha * acc_scr[hd] + _dot(vt, pe.astype(BF16)))
            new_m.append(m_new)
        acc_scr[...] = jnp.stack(accs)
        return tuple(new_m), tuple(new_l)

    def step(kt, carry):
        return tile(pl.multiple_of(kt * tq, tq), carry[0], carry[1], False)

    n_full = pl.program_id(1)
    init = (tuple(jnp.full((1, tq), NEG, F32) for _ in range(MLA_HEADS)),
            tuple(jnp.zeros((1, tq), F32) for _ in range(MLA_HEADS)))
    ms, ls = lax.fori_loop(0, n_full, step, init)
    ms, ls = tile(pl.multiple_of(n_full * tq, tq), ms, ls, True)
    for pr in range(MLA_HEADS // 2):
        t = jnp.concatenate([acc_scr[2 * pr] / ls[2 * pr], acc_scr[2 * pr + 1] / ls[2 * pr + 1]], axis=0)
        o_ref[:, LANES * pr:LANES * (pr + 1)] = t.T


def _mla(qmT, km, vmT, B, S):
    tq = TQ_MLA
    nq = S // tq
    return pl.pallas_call(
        _mla_kernel,
        grid=(B, nq),
        in_specs=[pl.BlockSpec((MLA_HEADS * LANES, tq), lambda b, i: (0, b * nq + i)),
                  pl.BlockSpec((S, MLA_HEADS * LANES), lambda b, i: (b, 0)),
                  pl.BlockSpec((WIDTH_B, S), lambda b, i: (0, b))],
        out_specs=pl.BlockSpec((tq, WIDTH_B), lambda b, i: (b * nq + i, 0)),
        out_shape=jax.ShapeDtypeStruct((B * S, WIDTH_B), F32),
        scratch_shapes=[pltpu.VMEM((MLA_HEADS, MLA_V, tq), F32), pltpu.VMEM((MXU_LOOKAHEAD, tq, tq), F32)],
        compiler_params=_params("parallel", "arbitrary"),
        name="mla_attention",
    )(qmT, km, vmT)


def _log_sigmoid(z):
    return jnp.minimum(z, 0.0) - jnp.log1p(jnp.exp(-jnp.abs(z)))


def _mlstm_kernel(cx_ref, v_ref, o_ref, g_ref, cw_ref, cb_ref, wq_ref, wk_ref, br_ref, gmh_ref, skip_ref, y_ref,
                  xc_scr, q_scr, k_scr, w_scr, inter_scr, floor_scr, mfull_scr, ut_scr, eo_scr, el_scr, cprev_scr):
    S = cx_ref.shape[0]
    L = MLSTM_TILE
    NC = S // L
    d = MLSTM_DIM
    pairs = MLSTM_HEADS // 2
    group = MLSTM_GROUP

    x = cx_ref[...]
    rowi = lax.broadcasted_iota(jnp.int32, (S, WIDTH_C), 0)
    conv = x * cw_ref[MLSTM_CONV - 1:MLSTM_CONV, :]
    for back in range(1, MLSTM_CONV):
        shifted = jnp.where(rowi >= back, pltpu.roll(x, back, 0), 0.0)
        conv = conv + shifted * cw_ref[MLSTM_CONV - 1 - back:MLSTM_CONV - back, :]
    xc = _silu(conv + cb_ref[...])
    xc_scr[...] = xc
    xcb = xc.astype(BF16)
    q_scr[...] = _dot(xcb, wq_ref[...]).astype(BF16)
    k_scr[...] = _dot(xcb, wk_ref[...]).astype(BF16)

    nh = MLSTM_HEADS
    lane_in_chunk = lax.broadcasted_iota(jnp.int32, (nh, S), 1) & (L - 1)

    def scan_lanes(val, op, fill):
        sh = 1
        while sh < L:
            val = op(val, jnp.where(lane_in_chunk >= sh, pltpu.roll(val, sh, 1), fill))
            sh *= 2
        return val

    gt = g_ref[...] + br_ref[...]
    ig = gt[0:nh]
    b = scan_lanes(_log_sigmoid(gt[nh:]), jnp.add, 0.0)
    u = ig - b
    cmu = scan_lanes(u, jnp.maximum, -jnp.inf)
    ut_scr[0:nh, :] = u

    m = jnp.zeros((nh, 1), F32)
    w_loc, inter, floor, m_tok = [], [], [], []
    for c in range(NC):
        blk = slice(L * c, L * (c + 1))
        b_last, u_max = b[:, L * (c + 1) - 1:L * (c + 1)], cmu[:, L * (c + 1) - 1:L * (c + 1)]
        m_top = jnp.maximum(m, u_max)
        eo_scr[c] = jnp.broadcast_to(jnp.exp(m - m_top), (nh, LANES))
        el_scr[c] = jnp.broadcast_to(jnp.exp(u_max - m_top), (nh, LANES))
        mt = jnp.maximum(cmu[:, blk], m)
        m_tok.append(mt)
        inter.append(jnp.exp(m - mt))
        w_loc.append(jnp.exp(u[:, blk] - u_max))
        floor.append(jnp.exp(-(b[:, blk] + mt)))
        m = b_last + m_top

    hrow = lax.broadcasted_iota(jnp.int32, (4 * nh, 2 * LANES), 0)
    hcol = lax.broadcasted_iota(jnp.int32, (4 * nh, 2 * LANES), 1)
    to_heads = jnp.where((hrow < 3 * nh) & (hcol // d == hrow % nh), 1.0, 0.0).astype(BF16)
    frow = lax.broadcasted_iota(jnp.int32, (4 * nh, 4 * LANES), 0)
    fcol = lax.broadcasted_iota(jnp.int32, (4 * nh, 4 * LANES), 1)
    to_full = jnp.where((frow < 3 * nh) & (fcol // LANES == frow % nh), 1.0, 0.0).astype(BF16)

    def spread(chunks, onehot):
        val = jnp.concatenate(chunks, axis=1)
        hi, mid, lo = _split3(val)
        pieces = jnp.concatenate([hi.astype(F32), mid.astype(F32), lo.astype(F32), jnp.zeros_like(val)], axis=0)
        return _dot_tn(pieces.astype(BF16), onehot)

    w_scr[...] = spread(w_loc, to_heads)
    inter_scr[...] = spread(inter, to_heads)
    floor_scr[...] = spread(floor, to_heads)
    mfull_scr[...] = spread(m_tok, to_full)

    arow = lax.broadcasted_iota(jnp.int32, (LANES, 2 * LANES), 0)
    acol = lax.broadcasted_iota(jnp.int32, (LANES, 2 * LANES), 1)
    blk2 = (arow // d) == ((acol & (LANES - 1)) // d)
    ones_v = jnp.ones((L, LANES), BF16)

    def head_rows(ref, c, pr):
        top = jnp.broadcast_to(ref[c, 2 * pr:2 * pr + 1, :], (d, LANES))
        bot = jnp.broadcast_to(ref[c, 2 * pr + 1:2 * pr + 2, :], (d, LANES))
        half = jnp.concatenate([top, bot], axis=0)
        return jnp.concatenate([half, half], axis=1)

    def state_group(g, carry):
        local = []
        for cc in range(group):
            r0 = pl.multiple_of((g * group + cc) * L, L)
            for pr in range(pairs):
                ps = slice(LANES * pr, LANES * (pr + 1))
                kw = (k_scr[pl.ds(r0, L), ps].astype(F32) * w_scr[pl.ds(r0, L), ps]).astype(BF16)
                vo = jnp.concatenate([v_ref[pl.ds(r0, L), ps], ones_v], axis=1)
                local.append(jnp.where(blk2, _dot_tn(kw, vo), 0.0))
        state = list(carry)
        for cc in range(group):
            c = g * group + cc
            for pr in range(pairs):
                cprev_scr[c, pr] = state[pr].astype(BF16)
                state[pr] = head_rows(eo_scr, c, pr) * state[pr] + head_rows(el_scr, c, pr) * local[cc * pairs + pr]
        return tuple(state)

    lax.fori_loop(0, NC // group, state_group, tuple(jnp.zeros((LANES, 2 * LANES), F32) for _ in range(pairs)))

    li = lax.broadcasted_iota(jnp.int32, (L, L), 0)
    si = lax.broadcasted_iota(jnp.int32, (L, L), 1)
    causal = si <= li
    lane = lax.broadcasted_iota(jnp.int32, (L, LANES), 1)
    h0_lane = lane < d
    avg = jnp.where((li // d) == (si // d), 1.0 / d, 0.0).astype(BF16)

    def dot2(val, rhs):
        hi = val.astype(BF16)
        return _dot(hi, rhs) + _dot((val - hi.astype(F32)).astype(BF16), rhs)

    def out_group(g, carry):
        units = [(cc, pr) for cc in range(group) for pr in range(pairs)]
        chunk_of = {u_: g * group + u_[0] for u_ in units}
        rows = {u_: pl.multiple_of(chunk_of[u_] * L, L) for u_ in units}
        qc, sc, pv = {}, {}, {}
        for u_ in units:
            c, pr = chunk_of[u_], u_[1]
            ps = slice(LANES * pr, LANES * (pr + 1))
            qp = q_scr[pl.ds(rows[u_], L), ps]
            kp = k_scr[pl.ds(rows[u_], L), ps]
            qc[u_] = _dot(qp, cprev_scr[c, pr])
            sc[u_] = [_dot_nt(jnp.where(h0_lane if hh == 0 else lane >= d, qp, jnp.zeros_like(qp)), kp)
                      for hh in range(2)]
        for u_ in units:
            pr = u_[1]
            ps = slice(LANES * pr, LANES * (pr + 1))
            vo = jnp.concatenate([v_ref[pl.ds(rows[u_], L), ps], ones_v], axis=1)
            pv[u_] = []
            for hh in range(2):
                hd = 2 * pr + hh
                u_row = ut_scr[hd:hd + 1, pl.ds(rows[u_], L)]
                decay = jnp.where(causal, jnp.exp(u_row - mfull_scr[pl.ds(rows[u_], L), LANES * hd:LANES * (hd + 1)]), 0.0)
                pv[u_].append(_dot((sc[u_][hh] * decay).astype(BF16), vo))
        hg, cen = {}, {}
        for u_ in units:
            pr = u_[1]
            ps = slice(LANES * pr, LANES * (pr + 1))
            it = inter_scr[pl.ds(rows[u_], L), ps]
            num = it * qc[u_][:, :LANES] + jnp.where(h0_lane, pv[u_][0][:, :LANES], pv[u_][1][:, :LANES])
            den = it * qc[u_][:, LANES:] + jnp.where(h0_lane, pv[u_][0][:, LANES:], pv[u_][1][:, LANES:])
            hcell = num / jnp.maximum(jnp.abs(den), floor_scr[pl.ds(rows[u_], L), ps])
            hg[u_] = _sigmoid(o_ref[pl.ds(rows[u_], L), ps]) * hcell
        for u_ in units:
            cen[u_] = hg[u_] - dot2(hg[u_], avg)
        for u_ in units:
            pr = u_[1]
            ps = slice(LANES * pr, LANES * (pr + 1))
            var = dot2(cen[u_] * cen[u_], avg)
            y_ref[pl.ds(rows[u_], L), ps] = (cen[u_] * lax.rsqrt(var + NORM_EPS) * gmh_ref[:, ps]
                                             + skip_ref[:, ps] * xc_scr[pl.ds(rows[u_], L), ps])
        return carry

    lax.fori_loop(0, NC // group, out_group, 0)


def _mlstm(cx, mv, mo, gif_t, conv_w, conv_b, w_q_m, w_k_m, b_igate, b_fgate, g_mh, skip_m, B, S):
    def blockdiag(w, scale):
        eye = jnp.eye(MLSTM_HEADS, dtype=F32)[:, None, :, None]
        return (w[:, :, None, :] * (eye * scale)).reshape(WIDTH_C, WIDTH_C).astype(BF16)

    nc = S // MLSTM_TILE
    wq = blockdiag(w_q_m, MLSTM_DIM ** -0.5)
    wk = blockdiag(w_k_m, 1.0)
    bias8 = jnp.concatenate([b_igate, b_fgate])
    br = jnp.broadcast_to(bias8[:, None], (2 * MLSTM_HEADS, S))
    seq = lambda w: pl.BlockSpec((S, w), lambda b: (b, 0))
    full = lambda a: pl.BlockSpec(a.shape, lambda b: (0,) * a.ndim)
    row = lambda a: a.reshape(1, -1)
    args = [conv_w, row(conv_b), wq, wk, br, row(g_mh), row(skip_m)]
    tok = lambda w, dt: pltpu.VMEM((S, w), dt)
    per_chunk = pltpu.VMEM((nc, MLSTM_HEADS, LANES), F32)
    return pl.pallas_call(
        _mlstm_kernel,
        grid=(B,),
        in_specs=[seq(WIDTH_C), seq(WIDTH_C), seq(WIDTH_C),
                  pl.BlockSpec((2 * MLSTM_HEADS, S), lambda b: (0, b))] + [full(a) for a in args],
        out_specs=seq(WIDTH_C),
        out_shape=jax.ShapeDtypeStruct((B * S, WIDTH_C), F32),
        scratch_shapes=[tok(WIDTH_C, F32), tok(WIDTH_C, BF16), tok(WIDTH_C, BF16),
                        tok(WIDTH_C, F32), tok(WIDTH_C, F32), tok(WIDTH_C, F32), tok(4 * LANES, F32),
                        pltpu.VMEM((2 * MLSTM_HEADS, S), F32), per_chunk, per_chunk,
                        pltpu.VMEM((nc, MLSTM_HEADS // 2, LANES, 2 * LANES), BF16)],
        compiler_params=_params("parallel"),
        name="mlstm_mixer",
    )(cx, mv, mo, gif_t, *args)


def _mix_ffn_kernel(ya_ref, yb_ref, yc_ref, x_ref, gate1_ref, ga_ref, gb_ref, wo_ref,
                    sh_ref, sc_ref, gate_ref, g2_ref, wgu_ref, wd32_ref, gf_ref, o_ref, wd_ref, *, final_norm):
    @pl.when(pl.program_id(0) == 0)
    def _():
        wd_ref[...] = wd32_ref[0].astype(BF16)

    a = _rms(ya_ref[...], ga_ref[...]).astype(BF16)
    b = _rms(yb_ref[...], gb_ref[...]).astype(BF16)
    c = yc_ref[...].astype(BF16)
    mixed = (_dot(a, wo_ref[0:WIDTH_A, :]) + _dot(b, wo_ref[WIDTH_A:WIDTH_A + WIDTH_B, :])
             + _dot(c, wo_ref[WIDTH_A + WIDTH_B:, :]))
    x = x_ref[...] + gate1_ref[0] * mixed
    h = (_rms(x, g2_ref[...]) * (1.0 + sc_ref[0]) + sh_ref[0]).astype(BF16)
    acc = jnp.zeros(x.shape, F32)
    for j in range(FFN_HIDDEN // FFN_CHUNK):
        gate = _dot(h, wgu_ref[0, :, FFN_CHUNK * j:FFN_CHUNK * (j + 1)])
        up = _dot(h, wgu_ref[0, :, FFN_HIDDEN + FFN_CHUNK * j:FFN_HIDDEN + FFN_CHUNK * (j + 1)])
        act = (_silu(gate) * up).astype(BF16)
        acc = acc + _dot(act, wd_ref[FFN_CHUNK * j:FFN_CHUNK * (j + 1), :])
    y = x + gate_ref[0] * acc
    if final_norm:
        y = _rms(y, gf_ref[...])
    o_ref[...] = y


def _mix_ffn(ya, yb, yc, x2, mod3, mod_first, ga, gb, w_out, g2, wgu, w_down, layer, gf, S, final_norm):
    T = x2.shape[0]
    tm = TM_PROJ
    per_b = S // tm
    row = lambda w: pl.BlockSpec((tm, w), lambda i: (i, 0))
    full = lambda a: pl.BlockSpec(a.shape, lambda i: (0,) * a.ndim)
    modspec = lambda k: _mod_spec(per_b, mod_first, k)
    return pl.pallas_call(
        functools.partial(_mix_ffn_kernel, final_norm=final_norm),
        grid=(T // tm,),
        in_specs=[row(WIDTH_A), row(WIDTH_B), row(WIDTH_C), row(D_MODEL), modspec(2), full(ga), full(gb), full(w_out),
                  modspec(3), modspec(4), modspec(5), full(g2),
                  pl.BlockSpec((1,) + wgu.shape[1:], lambda i: (layer, 0, 0)),
                  pl.BlockSpec((1,) + w_down.shape[1:], lambda i: (layer, 0, 0)), full(gf)],
        out_specs=row(D_MODEL),
        out_shape=jax.ShapeDtypeStruct((T, D_MODEL), F32),
        scratch_shapes=[pltpu.VMEM(w_down.shape[1:], BF16)],
        compiler_params=_params("arbitrary"),
        name="mix_ffn_final" if final_norm else "mix_ffn",
    )(ya, yb, yc, x2, mod3, ga, gb, w_out, mod3, mod3, mod3, g2, wgu, w_down, gf)


def _head_tile_perm():
    idx = []
    for r in range(NSA_REP):
        idx += list(range(HEAD_DIM * r, HEAD_DIM * (r + 1)))
        idx += list(range(HEAD_DIM * (NSA_REP + r), HEAD_DIM * (NSA_REP + r + 1)))
    return np.asarray(idx, np.int32)


def _in_cols():
    std = np.full((N_STD,), -1, np.int64)
    tr = np.full((N_T,), -1, np.int64)
    off = 0
    tr[TSEG_Q:TSEG_Q + WIDTH_A] = _head_tile_perm()
    off += WIDTH_A
    std[SEG_CKV:SEG_CKV + 2 * LANES] = off + np.arange(2 * LANES)
    off += 2 * LANES
    std[SEG_SK:SEG_SK + LANES] = off + np.arange(LANES)
    off += LANES
    tr[TSEG_SV:TSEG_SV + LANES] = off + np.arange(LANES)
    off += LANES
    std[SEG_WK:SEG_WK + LANES] = off + np.arange(LANES)
    off += LANES
    tr[TSEG_WV:TSEG_WV + LANES] = off + np.arange(LANES)
    off += LANES
    tr[TSEG_G:TSEG_G + N_GATES] = off + np.arange(N_GATES)
    off += N_GATES
    std[SEG_QL:SEG_QL + MLA_Q_LORA] = off + np.arange(MLA_Q_LORA)
    off += MLA_Q_LORA
    std[SEG_KVL:SEG_KVL + MLA_KV_LORA] = off + np.arange(MLA_KV_LORA)
    off += MLA_KV_LORA
    std[SEG_KR + MLA_NOPE:SEG_KR + MLA_NOPE + MLA_ROPE] = off + np.arange(MLA_ROPE)
    off += MLA_ROPE
    for seg in (SEG_CX, SEG_MV, SEG_MO):
        std[seg:seg + WIDTH_C] = off + np.arange(WIDTH_C)
        off += WIDTH_C
    tr[TSEG_IF:TSEG_IF + 2 * MLSTM_HEADS] = off + np.arange(2 * MLSTM_HEADS)
    return std, tr


def _gather_cols(w, cols):
    pieces, start = [], 0
    for i in range(1, len(cols) + 1):
        run_ends = i == len(cols) or (cols[i] != cols[i - 1] + 1 if cols[i - 1] >= 0 else cols[i] >= 0)
        if run_ends:
            first = int(cols[start])
            pieces.append(w[:, first:first + i - start] if first >= 0 else jnp.zeros((w.shape[0], i - start), w.dtype))
            start = i
    return jnp.concatenate(pieces, axis=1)


def _layer_weights(l, w_in, w_uq, w_ukv, w_out, g_out_a):
    std, tr = _in_cols()
    w_std = _gather_cols(w_in[l], std).astype(BF16)
    w_t = _gather_cols(w_in[l], tr).T.astype(BF16)
    cq = np.full((MLA_HEADS * LANES,), -1, np.int64)
    ck = np.full((MLA_HEADS * LANES,), -1, np.int64)
    cv = np.zeros((WIDTH_B,), np.int64)
    dq = MLA_NOPE + MLA_ROPE
    dkv = MLA_NOPE + MLA_V
    for hd in range(MLA_HEADS):
        cq[LANES * hd:LANES * hd + dq] = dq * hd + np.arange(dq)
        ck[LANES * hd:LANES * hd + MLA_NOPE] = dkv * hd + np.arange(MLA_NOPE)
        cv[MLA_V * hd:MLA_V * (hd + 1)] = dkv * hd + MLA_NOPE + np.arange(MLA_V)
    wuqT = _gather_cols(w_uq[l], cq).T.astype(BF16)
    wkm = _gather_cols(w_ukv[l], ck).astype(BF16)
    wvmT = _gather_cols(w_ukv[l], cv).T.astype(BF16)
    perm = _head_tile_perm()
    head_rows = [w_out[l][int(o):int(o) + HEAD_DIM] for o in perm[::HEAD_DIM]]
    wo = jnp.concatenate(head_rows + [w_out[l][WIDTH_A:]], axis=0).astype(BF16)
    ga = _gather_cols(g_out_a[l].reshape(1, WIDTH_A), perm)
    return w_std, w_t, wuqT, wkm, wvmT, wo, ga


def kernel(x, c, positions, g_norm1, g_norm2, w_ada, b_ada, w_in, cmp_pos, w_cmp_k, w_cmp_v, g_out_a, g_q_lora, w_uq, g_kv_lora, w_ukv, g_out_b, conv_w, conv_b, w_q_m, w_k_m, b_igate, b_fgate, g_mh, skip_m, w_out, w_gu, w_down, g_final):
    B, S, D = x.shape
    T = B * S
    x2 = x.reshape(T, D)
    tabs, tabs_t = _rope_tables(positions)
    mod3 = _ada(c, w_ada, b_ada).reshape(DEPTH * B * N_MOD, 1, D)
    row = lambda v: v.reshape(1, -1)
    wgu = w_gu.astype(BF16)
    for l in range(DEPTH):
        w_std, w_t, wuqT, wkm, wvmT, wo, ga = _layer_weights(l, w_in, w_uq, w_ukv, w_out, g_out_a)
        (qaT, gT, ck, cv, sk_aug, wk, svT, wvT, qmT, km, vmT, cx, mv, mo, gif_t) = _inproj(
            x2, mod3, l * B * N_MOD, row(g_norm1[l]), w_std, w_t, tabs, tabs_t, row(g_q_lora[l]), wuqT,
            row(g_kv_lora[l]), wkm, wvmT, S)
        kc, vcT = _compress(ck, cv, cmp_pos[l], w_cmp_k[l], w_cmp_v[l], B, S)
        ya = _nsa(qaT, gT, kc, vcT, sk_aug, svT, wk, wvT, B, S)
        yb = _mla(qmT, km, vmT, B, S)
        yc = _mlstm(cx, mv, mo, gif_t, conv_w[l], conv_b[l], w_q_m[l], w_k_m[l], b_igate[l], b_fgate[l],
                    g_mh[l], skip_m[l], B, S)
        x2 = _mix_ffn(ya, yb, yc, x2, mod3, l * B * N_MOD, ga, row(g_out_b[l]), wo, row(g_norm2[l]),
                      wgu, w_down, l, row(g_final), S, final_norm=(l == DEPTH - 1))
    return x2.reshape(B, S, D)
```

```python
import functools

import numpy as np
import jax
import jax.numpy as jnp
from jax import lax
from jax.experimental import pallas as pl
from jax.experimental.pallas import tpu as pltpu

F32 = jnp.float32
BF16 = jnp.bfloat16

D_MODEL = 1024
DEPTH = 2
HEAD_DIM = 64
ROPE_THETA = 500000.0
NSA_ROT_HALF = HEAD_DIM // 8
NORM_EPS = 1e-6

NSA_HEADS = 6
NSA_KV_HEADS = 2
NSA_REP = NSA_HEADS // NSA_KV_HEADS
CMP_LEN = 32
CMP_STRIDE = 16
SLC_LEN = 64
SLC_TOPK = 16
WINDOW = 512

MLA_HEADS = 6
MLA_Q_LORA = 256
MLA_KV_LORA = 128
MLA_NOPE = 64
MLA_ROPE = 32
MLA_V = 64

MLSTM_HEADS = 4
MLSTM_DIM = 64
MLSTM_CONV = 4

WIDTH_A = NSA_HEADS * HEAD_DIM
WIDTH_B = MLA_HEADS * MLA_V
WIDTH_C = MLSTM_HEADS * MLSTM_DIM
FFN_HIDDEN = 2816
N_GATES = 3 * NSA_HEADS
GATE_ROWS = 24

LANES = 128
NEG = -1e30
LOG2E = 1.4426950408889634
VMEM_LIMIT = 56 * 1024 * 1024

TM_PROJ = 512
TQ_NSA = 256
TK_SLC = 256
CHAIN_LANES = 256
TQ_MLA = 256
FFN_CHUNK = 256
MLSTM_TILE = 128
MLSTM_GROUP = 16
MXU_LOOKAHEAD = 4

SEG_CKV, SEG_SK, SEG_WK, SEG_QL, SEG_KVL, SEG_KR, SEG_CX, SEG_MV, SEG_MO = (
    0, 256, 384, 512, 768, 896, 1024, 1280, 1536)
N_STD = 1792
TSEG_Q, TSEG_G, TSEG_SV, TSEG_WV, TSEG_IF = 0, 384, 416, 544, 672
TSEG_G_ROWS = 32
TSEG_IF_ROWS = 16
N_T = 688


def _params(*sem):
    return pltpu.CompilerParams(dimension_semantics=sem, vmem_limit_bytes=VMEM_LIMIT)


def _dot(a, b):
    return jnp.dot(a, b, preferred_element_type=F32)


def _dot_nt(a, b):
    return lax.dot_general(a, b, (((1,), (1,)), ((), ())), preferred_element_type=F32)


def _dot_tn(a, b):
    return lax.dot_general(a, b, (((0,), (0,)), ((), ())), preferred_element_type=F32)


def _split3(x):
    hi = x.astype(BF16)
    r1 = x - hi.astype(F32)
    mid = r1.astype(BF16)
    lo = (r1 - mid.astype(F32)).astype(BF16)
    return hi, mid, lo


def _rms(x, g):
    return x * lax.rsqrt(jnp.mean(x * x, axis=-1, keepdims=True) + NORM_EPS) * g


def _sigmoid(x):
    return 1.0 / (1.0 + jnp.exp(-x))


def _silu(x):
    return x * _sigmoid(x)


def _rope(x, cos, sin, half, x1_lane):
    xr = jnp.where(x1_lane, -pltpu.roll(x, LANES - half, 1), pltpu.roll(x, half, 1))
    return x * cos + xr * sin


def _ada_kernel(c_ref, w_ref, b_ref, o_ref):
    c = c_ref[...]
    ca = _silu(c).astype(BF16)
    o_ref[0] = _dot(ca, w_ref[0].astype(BF16)) + b_ref[0]


def _ada(c, w_ada, b_ada):
    L, D, N = w_ada.shape
    B = c.shape[0]
    tn = 1536
    return pl.pallas_call(
        _ada_kernel,
        grid=(L, N // tn),
        in_specs=[pl.BlockSpec((B, D), lambda l, j: (0, 0)),
                  pl.BlockSpec((1, D, tn), lambda l, j: (l, 0, j)),
                  pl.BlockSpec((1, 1, tn), lambda l, j: (l, 0, j))],
        out_specs=pl.BlockSpec((1, B, tn), lambda l, j: (l, 0, j)),
        out_shape=jax.ShapeDtypeStruct((L, B, N), F32),
        compiler_params=_params("parallel", "parallel"),
        name="ada_mod",
    )(c, w_ada, b_ada.reshape(L, 1, N))


ROPE_FREQS = NSA_ROT_HALF + MLA_ROPE // 2


def _rope_kernel(posr_ref, invc_ref, spread_ref, one_ref, cn_ref, sn_ref, cm_ref, sm_ref, cnt_ref, snt_ref, cmt_ref, smt_ref):
    tm = posr_ref.shape[1]
    rows_t = invc_ref.shape[0]
    ang_t = jnp.concatenate([invc_ref[...]] * (tm // LANES), axis=1) * jnp.broadcast_to(posr_ref[...], (rows_t, tm))
    c_t, s_t = jnp.cos(ang_t), jnp.sin(ang_t)
    cnt_ref[...] = c_t[:NSA_ROT_HALF]
    snt_ref[...] = s_t[:NSA_ROT_HALF]
    cmt_ref[...] = c_t[NSA_ROT_HALF:ROPE_FREQS]
    smt_ref[...] = s_t[NSA_ROT_HALF:ROPE_FREQS]

    def spread(val):
        hi, mid, lo = _split3(val)
        pieces = jnp.concatenate([hi.astype(F32), mid.astype(F32), lo.astype(F32)], axis=0).astype(BF16)
        return _dot_tn(pieces, spread_ref[...])

    c = spread(c_t) + one_ref[...]
    s = spread(s_t)
    cn_ref[...] = c[:, :LANES]
    cm_ref[...] = c[:, LANES:]
    sn_ref[...] = s[:, :LANES]
    sm_ref[...] = s[:, LANES:]


def _rope_tables(positions):
    T = positions.size
    inv_n = jnp.power(ROPE_THETA, -jnp.arange(0, 2 * NSA_ROT_HALF, 2, dtype=F32) / (2 * NSA_ROT_HALF))
    inv_m = jnp.power(ROPE_THETA, -jnp.arange(0, MLA_ROPE, 2, dtype=F32) / MLA_ROPE)
    rows_t = 4 * NSA_ROT_HALF
    onehot = np.zeros((3 * rows_t, 2 * LANES), np.float32)
    rotated = np.zeros((1, 2 * LANES), np.float32)
    for k in range(3):
        for f in range(NSA_ROT_HALF):
            for lane in (f, f + NSA_ROT_HALF, HEAD_DIM + f, HEAD_DIM + f + NSA_ROT_HALF):
                onehot[rows_t * k + f, lane] = 1.0
                rotated[0, lane] = 1.0
        for f in range(MLA_ROPE // 2):
            for lane in (MLA_NOPE + f, MLA_NOPE + MLA_ROPE // 2 + f):
                onehot[rows_t * k + NSA_ROT_HALF + f, LANES + lane] = 1.0
                rotated[0, LANES + lane] = 1.0
    inv_col = jnp.concatenate([inv_n, inv_m, jnp.zeros((rows_t - ROPE_FREQS,), F32)])
    inv_col = jnp.broadcast_to(inv_col[:, None], (rows_t, LANES))
    tm = 2048
    spec = pl.BlockSpec((tm, LANES), lambda i: (i, 0))
    col = lambda h: pl.BlockSpec((h, tm), lambda i: (0, i))
    full = lambda shape: pl.BlockSpec(shape, lambda i: (0, 0))
    sds = jax.ShapeDtypeStruct
    outs = pl.pallas_call(
        _rope_kernel,
        grid=(T // tm,),
        in_specs=[col(1), full((rows_t, LANES)), full((3 * rows_t, 2 * LANES)), full((1, 2 * LANES))],
        out_specs=[spec] * 4 + [col(NSA_ROT_HALF)] * 2 + [col(MLA_ROPE // 2)] * 2,
        out_shape=[sds((T, LANES), F32)] * 4 + [sds((NSA_ROT_HALF, T), F32)] * 2 + [sds((MLA_ROPE // 2, T), F32)] * 2,
        compiler_params=_params("parallel"),
        name="rope_tables",
    )(positions.reshape(1, T).astype(F32), inv_col, jnp.asarray(onehot, BF16), jnp.asarray(1.0 - rotated))
    return outs[:4], outs[4:]


def _rope_rows(t, offset, half, cos, sin):
    x1, x2 = t[offset:offset + half], t[offset + half:offset + 2 * half]
    return x1 * cos - x2 * sin, x2 * cos + x1 * sin


def _inproj_kernel(x_ref, sh_ref, sc_ref, g1_ref, ws_ref, wt_ref, cn_ref, sn_ref, cm_ref, sm_ref,
                   cnT_ref, snT_ref, cmT_ref, smT_ref, gq_ref, wuqT_ref, gkv_ref, wkm_ref, wvmT_ref,
                   qaT_ref, gT_ref, ck_ref, cv_ref, ska_ref, wk_ref, svT_ref, wvT_ref,
                   qmT_ref, km_ref, vmT_ref, cx_ref, mv_ref, mo_ref, gif_ref, *, per_b):
    tm = x_ref.shape[0]
    x = x_ref[...]
    h = _rms(x, g1_ref[...]) * (1.0 + sc_ref[0]) + sh_ref[0]
    hb = h.astype(BF16)

    def seg(start, width):
        return _dot(hb, ws_ref[:, start:start + width])

    lane = lax.broadcasted_iota(jnp.int32, (1, LANES), 1)
    x1_n = (lane % HEAD_DIM) < NSA_ROT_HALF
    x1_m = lane < MLA_NOPE + MLA_ROPE // 2
    cn, sn, cm, sm = cn_ref[...], sn_ref[...], cm_ref[...], sm_ref[...]
    rope_n = lambda t: _rope(t, cn, sn, NSA_ROT_HALF, x1_n)
    rope_m = lambda t: _rope(t, cm, sm, MLA_ROPE // 2, x1_m)

    qn = _rms(seg(SEG_QL, MLA_Q_LORA), gq_ref[...]).astype(BF16)
    kvl_kr = seg(SEG_KVL, 2 * LANES)
    kvn = _rms(kvl_kr[:, :LANES], gkv_ref[...]).astype(BF16)
    kr = rope_m(kvl_kr[:, LANES:])

    out_t = _dot_nt(wt_ref[...], hb)
    seg_t = lambda start, height: out_t[start:start + height]

    qt = seg_t(TSEG_Q, WIDTH_A)
    cnt, snt = cnT_ref[...], snT_ref[...]
    parts = []
    for hd in range(NSA_HEADS):
        o = HEAD_DIM * hd
        parts += list(_rope_rows(qt, o, NSA_ROT_HALF, cnt, snt)) + [qt[o + 2 * NSA_ROT_HALF:o + HEAD_DIM]]
    qaT_ref[...] = (jnp.concatenate(parts, axis=0) * (HEAD_DIM ** -0.5 * LOG2E)).astype(BF16)
    gT_ref[...] = _sigmoid(seg_t(TSEG_G, TSEG_G_ROWS))[:GATE_ROWS]
    svT_ref[...] = seg_t(TSEG_SV, LANES).astype(BF16)
    wvT_ref[...] = seg_t(TSEG_WV, LANES).astype(BF16)
    gif_ref[...] = seg_t(TSEG_IF, TSEG_IF_ROWS)[:2 * MLSTM_HEADS]

    ckv = seg(SEG_CKV, 2 * LANES)
    ck_ref[...] = rope_n(ckv[:, :LANES])
    cv_ref[...] = ckv[:, LANES:]
    ska_ref[:, :LANES] = rope_n(seg(SEG_SK, LANES)).astype(BF16)
    srow = (pl.program_id(0) % per_b) * tm + lax.broadcasted_iota(jnp.int32, (tm, LANES), 0)
    lane2 = lax.broadcasted_iota(jnp.int32, (tm, LANES), 1)
    code = (lane2 < 4 * SLC_TOPK) & ((lane2 & (2 * SLC_TOPK - 1)) == srow // SLC_LEN)
    ska_ref[:, LANES:] = jnp.where(code, 1.0, 0.0).astype(BF16)
    wk_ref[...] = rope_n(seg(SEG_WK, LANES)).astype(BF16)

    qmt = _dot_nt(wuqT_ref[...], qn)
    kk = _dot(kvn, wkm_ref[...])
    vmt = _dot_nt(wvmT_ref[...], kvn)
    cx_ref[...] = seg(SEG_CX, WIDTH_C)
    mv_ref[...] = seg(SEG_MV, WIDTH_C).astype(BF16)
    mo_ref[...] = seg(SEG_MO, WIDTH_C)
    cmt, smt = cmT_ref[...], smT_ref[...]
    parts = []
    for hd in range(MLA_HEADS):
        o = LANES * hd
        parts += [qmt[o:o + MLA_NOPE]] + list(_rope_rows(qmt, o + MLA_NOPE, MLA_ROPE // 2, cmt, smt))
        parts += [qmt[o + MLA_NOPE + MLA_ROPE:o + LANES]]
    qmT_ref[...] = (jnp.concatenate(parts, axis=0) * ((MLA_NOPE + MLA_ROPE) ** -0.5 * LOG2E)).astype(BF16)
    for hd in range(MLA_HEADS):
        km_ref[:, LANES * hd:LANES * (hd + 1)] = (kk[:, LANES * hd:LANES * (hd + 1)] + kr).astype(BF16)
    vmT_ref[...] = vmt.astype(BF16)


N_MOD = 6


def _mod_spec(per_b, first, k):
    return pl.BlockSpec((1, 1, D_MODEL), lambda i: (first + (i // per_b) * N_MOD + k, 0, 0))


def _inproj(x2, mod3, mod_first, g1, w_std, w_t, tabs, tabs_t, gq, wuqT, gkv, wkm, wvmT, S):
    T = x2.shape[0]
    tm = TM_PROJ
    per_b = S // tm
    row = lambda w: pl.BlockSpec((tm, w), lambda i: (i, 0))
    col = lambda h: pl.BlockSpec((h, tm), lambda i: (0, i))
    full = lambda a: pl.BlockSpec(a.shape, lambda i: (0,) * a.ndim)
    outs = [(WIDTH_A, BF16, True), (GATE_ROWS, F32, True), (LANES, F32, False), (LANES, F32, False), (2 * LANES, BF16, False),
            (LANES, BF16, False), (LANES, BF16, True), (LANES, BF16, True),
            (MLA_HEADS * LANES, BF16, True), (MLA_HEADS * LANES, BF16, False), (WIDTH_B, BF16, True),
            (WIDTH_C, F32, False), (WIDTH_C, BF16, False), (WIDTH_C, F32, False), (2 * MLSTM_HEADS, F32, True)]
    return pl.pallas_call(
        functools.partial(_inproj_kernel, per_b=per_b),
        grid=(T // tm,),
        in_specs=[row(D_MODEL), _mod_spec(per_b, mod_first, 0), _mod_spec(per_b, mod_first, 1),
                  full(g1), full(w_std), full(w_t)] + [row(LANES)] * 4
                 + [col(t.shape[0]) for t in tabs_t] + [full(gq), full(wuqT), full(gkv), full(wkm), full(wvmT)],
        out_specs=[col(w) if tr else row(w) for w, _, tr in outs],
        out_shape=[jax.ShapeDtypeStruct((w, T) if tr else (T, w), dt) for w, dt, tr in outs],
        compiler_params=_params("parallel"),
        name="in_proj",
    )(x2, mod3, mod3, g1, w_std, w_t, *tabs, *tabs_t, gq, wuqT, gkv, wkm, wvmT)


def _compress_kernel(xk_ref, xv_ref, wk_ref, wv_ref, pos_ref, wkf_ref, wvf_ref, kc_ref, vcT_ref):
    ng = xk_ref.shape[0] // CMP_STRIDE
    row = lax.broadcasted_iota(jnp.int32, (ng, LANES), 0)
    pos = pos_ref[...].astype(BF16)
    acc_k = jnp.zeros((ng, 2 * LANES), F32)
    acc_v = jnp.zeros((ng, 2 * LANES), F32)
    for t in range(CMP_STRIDE):
        tok = pl.ds(t, ng, stride=CMP_STRIDE)
        acc_k = acc_k + _dot(xk_ref[tok, :].astype(BF16), wk_ref[t])
        acc_v = acc_v + _dot(xv_ref[tok, :].astype(BF16), wv_ref[t])

    def finish(acc, wf_ref):
        const = _dot(pos, wf_ref[...].astype(BF16))
        both = acc[:, :LANES] + pltpu.roll(acc[:, LANES:], ng - 1, 0) + jnp.concatenate([const, const], axis=1)
        return jnp.where(row < ng - 1, both, 0.0)

    kc_ref[0] = finish(acc_k, wkf_ref).astype(BF16)
    vcT_ref[0] = finish(acc_v, wvf_ref).T.astype(BF16)


def _compress(ck, cv, cmp_pos, w_cmp_k, w_cmp_v, B, S):
    ng = S // CMP_STRIDE

    def per_token(w):
        a = w[:CMP_STRIDE * HEAD_DIM].reshape(CMP_STRIDE, HEAD_DIM, HEAD_DIM)
        b = w[CMP_STRIDE * HEAD_DIM:].reshape(CMP_STRIDE, HEAD_DIM, HEAD_DIM)
        z = jnp.zeros_like(a)
        top = jnp.concatenate([a, z, b, z], axis=2)
        bot = jnp.concatenate([z, a, z, b], axis=2)
        return jnp.concatenate([top, bot], axis=1).astype(BF16)

    full = lambda a: pl.BlockSpec(a.shape, lambda b: (0,) * a.ndim)
    wk3, wv3 = per_token(w_cmp_k), per_token(w_cmp_v)
    posf = cmp_pos.reshape(1, CMP_LEN * HEAD_DIM)
    ospec = pl.BlockSpec((1, ng, LANES), lambda b: (b, 0, 0))
    return pl.pallas_call(
        _compress_kernel,
        grid=(B,),
        in_specs=[pl.BlockSpec((S, LANES), lambda b: (b, 0)), pl.BlockSpec((S, LANES), lambda b: (b, 0)),
                  full(wk3), full(wv3), full(posf), full(w_cmp_k), full(w_cmp_v)],
        out_specs=[ospec, ospec],
        out_shape=[jax.ShapeDtypeStruct((B, ng, LANES), BF16)] * 2,
        compiler_params=_params("parallel"),
        name="nsa_compress",
    )(ck, cv, wk3, wv3, posf, w_cmp_k, w_cmp_v)


def _nsa_kernel(qT_ref, gT_ref, kc_ref, vcT_ref, sk_ref, svT_ref, wk_ref, wvT_ref, o_ref, qaug_scr, acc_scr, s_scr):
    tq = TQ_NSA
    cols = NSA_HEADS * tq
    pair = CHAIN_LANES
    t0 = pl.program_id(1) * tq
    n_slc = SLC_TOPK * 2

    frow = lax.broadcasted_iota(jnp.int32, (LANES, tq), 0)
    g0_row = frow < HEAD_DIM
    tiles = [qT_ref[LANES * r:LANES * (r + 1), :] for r in range(NSA_REP)]
    zero = jnp.zeros_like(tiles[0])
    q6 = jnp.concatenate([jnp.where(g0_row, t, zero) for t in tiles]
                         + [jnp.where(g0_row, zero, t) for t in tiles], axis=1)
    qaug_scr[0:LANES, :] = q6
    tq_l = t0 + (lax.broadcasted_iota(jnp.int32, (1, cols), 1) & (tq - 1))

    pairs = [slice(pair * pp, pair * (pp + 1)) for pp in range(cols // pair)]
    vrows = lambda pp: slice(HEAD_DIM * (pp // NSA_REP), HEAD_DIM * (pp // NSA_REP + 1))
    s = _dot(kc_ref[0], q6)
    span = WINDOW + tq
    start = pl.multiple_of(jnp.maximum(t0 - WINDOW, 0), tq)
    kw = wk_ref[pl.ds(start, span), :]
    win_scores = [_dot(kw, q6[:, sl]) for sl in pairs]

    tq_1 = tq_l[:, :tq]
    nrow = lax.broadcasted_iota(jnp.int32, (LANES, tq), 0)
    cmp_bias = jnp.where(nrow * CMP_STRIDE + (CMP_LEN - 1) <= tq_1, 0.0, NEG)
    s = s + jnp.concatenate([cmp_bias] * NSA_HEADS, axis=1)
    e = jnp.exp2(s - jnp.max(s, axis=0, keepdims=True))
    seen = jnp.where(tq_l >= CMP_LEN - 1, 1.0, 0.0)
    p = e * (seen / jnp.sum(e, axis=0, keepdims=True))
    p_b = p.astype(BF16)
    o_cmp = [_dot(vcT_ref[0, vrows(pp), :], p_b[:, sl]) for pp, sl in enumerate(pairs)]

    jr = lax.broadcasted_iota(jnp.int32, (n_slc, LANES), 0)
    nc = lax.broadcasted_iota(jnp.int32, (n_slc, LANES), 1)
    ovl = ((nc * CMP_STRIDE < jr * SLC_LEN + SLC_LEN) & (nc * CMP_STRIDE + CMP_LEN > jr * SLC_LEN)
           & (nc < LANES - 1))
    ovl = jnp.where(ovl, 1.0, 0.0).astype(BF16)
    jq = lax.broadcasted_iota(jnp.int32, (n_slc, tq), 0)
    tl = t0 + lax.broadcasted_iota(jnp.int32, (n_slc, tq), 1)
    cur = tl // SLC_LEN
    forced = (jq == 0) | (jq == cur) | (jq == cur - 1)
    future = jq * SLC_LEN > tl
    bias_t = []
    for g in range(NSA_KV_HEADS):
        pg = p[:, (3 * g) * tq:(3 * g + 1) * tq] + p[:, (3 * g + 1) * tq:(3 * g + 2) * tq] + p[:, (3 * g + 2) * tq:(3 * g + 3) * tq]
        imp = sum(_dot(ovl, part) for part in _split3(pg))
        imp = jnp.where(forced, jnp.inf, imp)
        imp = jnp.where(future, -jnp.inf, imp)
        rows8 = [imp[8 * r:8 * (r + 1)] for r in range(n_slc // 8)]
        sub = lax.broadcasted_iota(jnp.int32, (8, tq), 0)
        ranks = [jnp.zeros((8, tq), F32) for _ in rows8]
        for jp in range(n_slc):
            rv = imp[jp:jp + 1, :]
            for r, blk in enumerate(rows8):
                if 8 * r > jp:
                    ahead = rv >= blk
                elif 8 * r + 7 < jp:
                    ahead = rv > blk
                else:
                    ahead = (rv > blk) | ((rv == blk) & (sub > jp - 8 * r))
                ranks[r] = ranks[r] + jnp.where(ahead, 1.0, 0.0)
        rank = jnp.concatenate(ranks, axis=0)
        bias_t.append(jnp.where(rank < float(SLC_TOPK), 0.0, NEG).astype(BF16))
    zb = jnp.zeros((n_slc, NSA_REP * tq), BF16)
    qaug_scr[LANES:LANES + n_slc, :] = jnp.concatenate([bias_t[0]] * NSA_REP + [zb], axis=1)
    qaug_scr[LANES + n_slc:LANES + 2 * n_slc, :] = jnp.concatenate([zb] + [bias_t[1]] * NSA_REP, axis=1)
    qaug_scr[LANES + 2 * n_slc:, :] = jnp.zeros((LANES - 2 * n_slc, cols), BF16)

    vwt = wvT_ref[:, pl.ds(start, span)]
    wrow = start + lax.broadcasted_iota(jnp.int32, (span, pair), 0)
    o_win = []
    for pp, (sl, sc) in enumerate(zip(pairs, win_scores)):
        in_window = lax.bitcast_convert_type(tq_l[:, sl] - wrow, jnp.uint32) < jnp.uint32(WINDOW)
        sc = jnp.where(in_window, sc, NEG)
        ew = jnp.exp2(sc - jnp.max(sc, axis=0, keepdims=True))
        o_win.append(_dot(vwt[vrows(pp)], ew.astype(BF16)) / jnp.sum(ew, axis=0, keepdims=True))

    acc_scr[...] = jnp.zeros((HEAD_DIM, cols), F32)
    krow = lax.broadcasted_iota(jnp.int32, (TK_SLC, pair), 0)

    def slc_scores(k0, sl):
        return _dot(sk_ref[pl.ds(k0, TK_SLC), :], qaug_scr[:, sl])

    nch = len(pairs)
    la = s_scr.shape[0]
    for pp in range(la):
        s_scr[pp] = slc_scores(0, pairs[pp])

    def slc_tile(k0, m, l, masked):
        vt = svT_ref[:, pl.ds(k0, TK_SLC)]
        ms, ls, accs = [], [], []
        ahead = [s_scr[pp] for pp in range(la)]
        for pp, sl in enumerate(pairs):
            sc = ahead.pop(0)
            nxt = pp + la
            if nxt < nch:
                ahead.append(slc_scores(k0, pairs[nxt]))
            elif not masked:
                s_scr[nxt - nch] = slc_scores(pl.multiple_of(k0 + TK_SLC, TK_SLC), pairs[nxt - nch])
            if masked:
                sc = jnp.where(k0 + krow <= tq_l[:, sl], sc, NEG)
            m_new = jnp.maximum(m[:, sl], jnp.max(sc, axis=0, keepdims=True))
            alpha = jnp.exp2(m[:, sl] - m_new)
            pe = jnp.exp2(sc - m_new)
            ls.append(alpha * l[:, sl] + jnp.sum(pe, axis=0, keepdims=True))
            accs.append(alpha * acc_scr[:, sl] + _dot(vt[vrows(pp)], pe.astype(BF16)))
            ms.append(m_new)
        acc_scr[...] = jnp.concatenate(accs, axis=1)
        return jnp.concatenate(ms, axis=1), jnp.concatenate(ls, axis=1)

    def slc_step(kt, carry):
        return slc_tile(pl.multiple_of(kt * TK_SLC, TK_SLC), carry[0], carry[1], False)

    n_full = t0 // TK_SLC
    m, l = lax.fori_loop(0, n_full, slc_step, (jnp.full((1, cols), NEG, F32), jnp.zeros((1, cols), F32)))
    m, l = slc_tile(pl.multiple_of(n_full * TK_SLC, TK_SLC), m, l, True)
    o_slc = acc_scr[...] / l

    gt = gT_ref[...]
    mixed = []
    for hd, sl in enumerate(pairs):
        gate = lambda j: gt[3 * hd + j:3 * hd + j + 1, :]
        mixed.append(gate(0) * o_cmp[hd] + gate(1) * o_slc[:, sl] + gate(2) * o_win[hd])
    for r in range(NSA_REP):
        o_ref[:, LANES * r:LANES * (r + 1)] = jnp.concatenate([mixed[r], mixed[NSA_REP + r]], axis=0).T


def _nsa(qaT, gT, kc, vcT, sk_aug, svT, wk, wvT, B, S):
    tq = TQ_NSA
    assert CHAIN_LANES == tq, "the kernel treats one softmax chain as one head"
    nq = S // tq
    cols = NSA_HEADS * tq
    qcol =lambda h: pl.BlockSpec((h, tq), lambda b, i: (0, b * nq + i))
    seq = lambda w: pl.BlockSpec((S, w), lambda b, i: (b, 0))
    seqT = pl.BlockSpec((LANES, S), lambda b, i: (0, b))
    cspec = pl.BlockSpec((1, LANES, LANES), lambda b, i: (b, 0, 0))
    return pl.pallas_call(
        _nsa_kernel,
        grid=(B, nq),
        in_specs=[qcol(WIDTH_A), qcol(gT.shape[0]), cspec, cspec, seq(2 * LANES), seqT, seq(LANES), seqT],
        out_specs=pl.BlockSpec((tq, WIDTH_A), lambda b, i: (b * nq + i, 0)),
        out_shape=jax.ShapeDtypeStruct((B * S, WIDTH_A), F32),
        scratch_shapes=[pltpu.VMEM((2 * LANES, cols), BF16), pltpu.VMEM((HEAD_DIM, cols), F32),
                        pltpu.VMEM((cols // CHAIN_LANES, TK_SLC, CHAIN_LANES), F32)],
        compiler_params=_params("parallel", "arbitrary"),
        name="nsa_attention",
    )(qaT, gT, kc, vcT, sk_aug, svT, wk, wvT)


def _mla_kernel(qT_ref, k_ref, vT_ref, o_ref, acc_scr, s_scr):
    tq = TQ_MLA
    t0 = pl.program_id(1) * tq
    tq_l = t0 + lax.broadcasted_iota(jnp.int32, (1, tq), 1)
    krow = lax.broadcasted_iota(jnp.int32, (tq, tq), 0)
    acc_scr[...] = jnp.zeros((MLA_HEADS, MLA_V, tq), F32)

    def scores(k0, hd):
        k = k_ref[pl.ds(k0, tq), LANES * hd:LANES * (hd + 1)]
        return _dot(k, qT_ref[LANES * hd:LANES * (hd + 1), :])

    for hd in range(MXU_LOOKAHEAD):
        s_scr[hd] = scores(0, hd)

    def tile(k0, ms, ls, masked):
        new_m, new_l, accs = [], [], []
        ahead = [s_scr[hd] for hd in range(MXU_LOOKAHEAD)]
        for hd in range(MLA_HEADS):
            sc = ahead.pop(0)
            nxt = hd + MXU_LOOKAHEAD
            if nxt < MLA_HEADS:
                ahead.append(scores(k0, nxt))
            elif not masked:
                s_scr[nxt - MLA_HEADS] = scores(pl.multiple_of(k0 + tq, tq), nxt - MLA_HEADS)
            if masked:
                sc = jnp.where(k0 + krow <= tq_l, sc, NEG)
            m_new = jnp.maximum(ms[hd], jnp.max(sc, axis=0, keepdims=True))
            alpha = jnp.exp2(ms[hd] - m_new)
            pe = jnp.exp2(sc - m_new)
            new_l.append(alpha * ls[hd] + jnp.sum(pe, axis=0, keepdims=True))
            vt = vT_ref[MLA_V * hd:MLA_V * (hd + 1), pl.ds(k0, tq)]
            accs.append(alpha * acc_scr[hd] + _dot(vt, pe.astype(BF16)))
            new_m.append(m_new)
        acc_scr[...] = jnp.stack(accs)
        return tuple(new_m), tuple(new_l)

    def step(kt, carry):
        return tile(pl.multiple_of(kt * tq, tq), carry[0], carry[1], False)

    n_full = pl.program_id(1)
    init = (tuple(jnp.full((1, tq), NEG, F32) for _ in range(MLA_HEADS)),
            tuple(jnp.zeros((1, tq), F32) for _ in range(MLA_HEADS)))
    ms, ls = lax.fori_loop(0, n_full, step, init)
    ms, ls = tile(pl.multiple_of(n_full * tq, tq), ms, ls, True)
    for pr in range(MLA_HEADS // 2):
        t = jnp.concatenate([acc_scr[2 * pr] / ls[2 * pr], acc_scr[2 * pr + 1] / ls[2 * pr + 1]], axis=0)
        o_ref[:, LANES * pr:LANES * (pr + 1)] = t.T


def _mla(qmT, km, vmT, B, S):
    tq = TQ_MLA
    nq = S // tq
    return pl.pallas_call(
        _mla_kernel,
        grid=(B, nq),
        in_specs=[pl.BlockSpec((MLA_HEADS * LANES, tq), lambda b, i: (0, b * nq + i)),
                  pl.BlockSpec((S, MLA_HEADS * LANES), lambda b, i: (b, 0)),
                  pl.BlockSpec((WIDTH_B, S), lambda b, i: (0, b))],
        out_specs=pl.BlockSpec((tq, WIDTH_B), lambda b, i: (b * nq + i, 0)),
        out_shape=jax.ShapeDtypeStruct((B * S, WIDTH_B), F32),
        scratch_shapes=[pltpu.VMEM((MLA_HEADS, MLA_V, tq), F32), pltpu.VMEM((MXU_LOOKAHEAD, tq, tq), F32)],
        compiler_params=_params("parallel", "arbitrary"),
        name="mla_attention",
    )(qmT, km, vmT)


def _log_sigmoid(z):
    return jnp.minimum(z, 0.0) - jnp.log1p(jnp.exp(-jnp.abs(z)))


def _mlstm_kernel(cx_ref, v_ref, o_ref, g_ref, cw_ref, cb_ref, wq_ref, wk_ref, br_ref, gmh_ref, skip_ref, y_ref,
                  xc_scr, q_scr, k_scr, w_scr, inter_scr, floor_scr, mfull_scr, ut_scr, eo_scr, el_scr, cprev_scr):
    S = cx_ref.shape[0]
    L = MLSTM_TILE
    NC = S // L
    d = MLSTM_DIM
    pairs = MLSTM_HEADS // 2
    group = MLSTM_GROUP

    x = cx_ref[...]
    rowi = lax.broadcasted_iota(jnp.int32, (S, WIDTH_C), 0)
    conv = x * cw_ref[MLSTM_CONV - 1:MLSTM_CONV, :]
    for back in range(1, MLSTM_CONV):
        shifted = jnp.where(rowi >= back, pltpu.roll(x, back, 0), 0.0)
        conv = conv + shifted * cw_ref[MLSTM_CONV - 1 - back:MLSTM_CONV - back, :]
    xc = _silu(conv + cb_ref[...])
    xc_scr[...] = xc
    xcb = xc.astype(BF16)
    q_scr[...] = _dot(xcb, wq_ref[...]).astype(BF16)
    k_scr[...] = _dot(xcb, wk_ref[...]).astype(BF16)

    nh = MLSTM_HEADS
    lane_in_chunk = lax.broadcasted_iota(jnp.int32, (nh, S), 1) & (L - 1)

    def scan_lanes(val, op, fill):
        sh = 1
        while sh < L:
            val = op(val, jnp.where(lane_in_chunk >= sh, pltpu.roll(val, sh, 1), fill))
            sh *= 2
        return val

    gt = g_ref[...] + br_ref[...]
    ig = gt[0:nh]
    b = scan_lanes(_log_sigmoid(gt[nh:]), jnp.add, 0.0)
    u = ig - b
    cmu = scan_lanes(u, jnp.maximum, -jnp.inf)
    ut_scr[0:nh, :] = u

    m = jnp.zeros((nh, 1), F32)
    w_loc, inter, floor, m_tok = [], [], [], []
    for c in range(NC):
        blk = slice(L * c, L * (c + 1))
        b_last, u_max = b[:, L * (c + 1) - 1:L * (c + 1)], cmu[:, L * (c + 1) - 1:L * (c + 1)]
        m_top = jnp.maximum(m, u_max)
        eo_scr[c] = jnp.broadcast_to(jnp.exp(m - m_top), (nh, LANES))
        el_scr[c] = jnp.broadcast_to(jnp.exp(u_max - m_top), (nh, LANES))
        mt = jnp.maximum(cmu[:, blk], m)
        m_tok.append(mt)
        inter.append(jnp.exp(m - mt))
        w_loc.append(jnp.exp(u[:, blk] - u_max))
        floor.append(jnp.exp(-(b[:, blk] + mt)))
        m = b_last + m_top

    hrow = lax.broadcasted_iota(jnp.int32, (4 * nh, 2 * LANES), 0)
    hcol = lax.broadcasted_iota(jnp.int32, (4 * nh, 2 * LANES), 1)
    to_heads = jnp.where((hrow < 3 * nh) & (hcol // d == hrow % nh), 1.0, 0.0).astype(BF16)
    frow = lax.broadcasted_iota(jnp.int32, (4 * nh, 4 * LANES), 0)
    fcol = lax.broadcasted_iota(jnp.int32, (4 * nh, 4 * LANES), 1)
    to_full = jnp.where((frow < 3 * nh) & (fcol // LANES == frow % nh), 1.0, 0.0).astype(BF16)

    def spread(chunks, onehot):
        val = jnp.concatenate(chunks, axis=1)
        hi, mid, lo = _split3(val)
        pieces = jnp.concatenate([hi.astype(F32), mid.astype(F32), lo.astype(F32), jnp.zeros_like(val)], axis=0)
        return _dot_tn(pieces.astype(BF16), onehot)

    w_scr[...] = spread(w_loc, to_heads)
    inter_scr[...] = spread(inter, to_heads)
    floor_scr[...] = spread(floor, to_heads)
    mfull_scr[...] = spread(m_tok, to_full)

    arow = lax.broadcasted_iota(jnp.int32, (LANES, 2 * LANES), 0)
    acol = lax.broadcasted_iota(jnp.int32, (LANES, 2 * LANES), 1)
    blk2 = (arow // d) == ((acol & (LANES - 1)) // d)
    ones_v = jnp.ones((L, LANES), BF16)

    def head_rows(ref, c, pr):
        top = jnp.broadcast_to(ref[c, 2 * pr:2 * pr + 1, :], (d, LANES))
        bot = jnp.broadcast_to(ref[c, 2 * pr + 1:2 * pr + 2, :], (d, LANES))
        half = jnp.concatenate([top, bot], axis=0)
        return jnp.concatenate([half, half], axis=1)

    def state_group(g, carry):
        local = []
        for cc in range(group):
            r0 = pl.multiple_of((g * group + cc) * L, L)
            for pr in range(pairs):
                ps = slice(LANES * pr, LANES * (pr + 1))
                kw = (k_scr[pl.ds(r0, L), ps].astype(F32) * w_scr[pl.ds(r0, L), ps]).astype(BF16)
                vo = jnp.concatenate([v_ref[pl.ds(r0, L), ps], ones_v], axis=1)
                local.append(jnp.where(blk2, _dot_tn(kw, vo), 0.0))
        state = list(carry)
        for cc in range(group):
            c = g * group + cc
            for pr in range(pairs):
                cprev_scr[c, pr] = state[pr].astype(BF16)
                state[pr] = head_rows(eo_scr, c, pr) * state[pr] + head_rows(el_scr, c, pr) * local[cc * pairs + pr]
        return tuple(state)

    lax.fori_loop(0, NC // group, state_group, tuple(jnp.zeros((LANES, 2 * LANES), F32) for _ in range(pairs)))

    li = lax.broadcasted_iota(jnp.int32, (L, L), 0)
    si = lax.broadcasted_iota(jnp.int32, (L, L), 1)
    causal = si <= li
    lane = lax.broadcasted_iota(jnp.int32, (L, LANES), 1)
    h0_lane = lane < d
    avg = jnp.where((li // d) == (si // d), 1.0 / d, 0.0).astype(BF16)

    def dot2(val, rhs):
        hi = val.astype(BF16)
        return _dot(hi, rhs) + _dot((val - hi.astype(F32)).astype(BF16), rhs)

    def out_group(g, carry):
        units = [(cc, pr) for cc in range(group) for pr in range(pairs)]
        chunk_of = {u_: g * group + u_[0] for u_ in units}
        rows = {u_: pl.multiple_of(chunk_of[u_] * L, L) for u_ in units}
        qc, sc, pv = {}, {}, {}
        for u_ in units:
            c, pr = chunk_of[u_], u_[1]
            ps = slice(LANES * pr, LANES * (pr + 1))
            qp = q_scr[pl.ds(rows[u_], L), ps]
            kp = k_scr[pl.ds(rows[u_], L), ps]
            qc[u_] = _dot(qp, cprev_scr[c, pr])
            sc[u_] = [_dot_nt(jnp.where(h0_lane if hh == 0 else lane >= d, qp, jnp.zeros_like(qp)), kp)
                      for hh in range(2)]
        for u_ in units:
            pr = u_[1]
            ps = slice(LANES * pr, LANES * (pr + 1))
            vo = jnp.concatenate([v_ref[pl.ds(rows[u_], L), ps], ones_v], axis=1)
            pv[u_] = []
            for hh in range(2):
                hd = 2 * pr + hh
                u_row = ut_scr[hd:hd + 1, pl.ds(rows[u_], L)]
                decay = jnp.where(causal, jnp.exp(u_row - mfull_scr[pl.ds(rows[u_], L), LANES * hd:LANES * (hd + 1)]), 0.0)
                pv[u_].append(_dot((sc[u_][hh] * decay).astype(BF16), vo))
        hg, cen = {}, {}
        for u_ in units:
            pr = u_[1]
            ps = slice(LANES * pr, LANES * (pr + 1))
            it = inter_scr[pl.ds(rows[u_], L), ps]
            num = it * qc[u_][:, :LANES] + jnp.where(h0_lane, pv[u_][0][:, :LANES], pv[u_][1][:, :LANES])
            den = it * qc[u_][:, LANES:] + jnp.where(h0_lane, pv[u_][0][:, LANES:], pv[u_][1][:, LANES:])
            hcell = num / jnp.maximum(jnp.abs(den), floor_scr[pl.ds(rows[u_], L), ps])
            hg[u_] = _sigmoid(o_ref[pl.ds(rows[u_], L), ps]) * hcell
        for u_ in units:
            cen[u_] = hg[u_] - dot2(hg[u_], avg)
        for u_ in units:
            pr = u_[1]
            ps = slice(LANES * pr, LANES * (pr + 1))
            var = dot2(cen[u_] * cen[u_], avg)
            y_ref[pl.ds(rows[u_], L), ps] = (cen[u_] * lax.rsqrt(var + NORM_EPS) * gmh_ref[:, ps]
                                             + skip_ref[:, ps] * xc_scr[pl.ds(rows[u_], L), ps])
        return carry

    lax.fori_loop(0, NC // group, out_group, 0)


def _mlstm(cx, mv, mo, gif_t, conv_w, conv_b, w_q_m, w_k_m, b_igate, b_fgate, g_mh, skip_m, B, S):
    def blockdiag(w, scale):
        eye = jnp.eye(MLSTM_HEADS, dtype=F32)[:, None, :, None]
        return (w[:, :, None, :] * (eye * scale)).reshape(WIDTH_C, WIDTH_C).astype(BF16)

    nc = S // MLSTM_TILE
    wq = blockdiag(w_q_m, MLSTM_DIM ** -0.5)
    wk = blockdiag(w_k_m, 1.0)
    bias8 = jnp.concatenate([b_igate, b_fgate])
    br = jnp.broadcast_to(bias8[:, None], (2 * MLSTM_HEADS, S))
    seq = lambda w: pl.BlockSpec((S, w), lambda b: (b, 0))
    full = lambda a: pl.BlockSpec(a.shape, lambda b: (0,) * a.ndim)
    row = lambda a: a.reshape(1, -1)
    args = [conv_w, row(conv_b), wq, wk, br, row(g_mh), row(skip_m)]
    tok = lambda w, dt: pltpu.VMEM((S, w), dt)
    per_chunk = pltpu.VMEM((nc, MLSTM_HEADS, LANES), F32)
    return pl.pallas_call(
        _mlstm_kernel,
        grid=(B,),
        in_specs=[seq(WIDTH_C), seq(WIDTH_C), seq(WIDTH_C),
                  pl.BlockSpec((2 * MLSTM_HEADS, S), lambda b: (0, b))] + [full(a) for a in args],
        out_specs=seq(WIDTH_C),
        out_shape=jax.ShapeDtypeStruct((B * S, WIDTH_C), F32),
        scratch_shapes=[tok(WIDTH_C, F32), tok(WIDTH_C, BF16), tok(WIDTH_C, BF16),
                        tok(WIDTH_C, F32), tok(WIDTH_C, F32), tok(WIDTH_C, F32), tok(4 * LANES, F32),
                        pltpu.VMEM((2 * MLSTM_HEADS, S), F32), per_chunk, per_chunk,
                        pltpu.VMEM((nc, MLSTM_HEADS // 2, LANES, 2 * LANES), BF16)],
        compiler_params=_params("parallel"),
        name="mlstm_mixer",
    )(cx, mv, mo, gif_t, *args)


def _mix_ffn_kernel(ya_ref, yb_ref, yc_ref, x_ref, gate1_ref, ga_ref, gb_ref, wo_ref,
                    sh_ref, sc_ref, gate_ref, g2_ref, wgu_ref, wd32_ref, gf_ref, o_ref, wd_ref, *, final_norm):
    @pl.when(pl.program_id(0) == 0)
    def _():
        wd_ref[...] = wd32_ref[0].astype(BF16)

    a = _rms(ya_ref[...], ga_ref[...]).astype(BF16)
    b = _rms(yb_ref[...], gb_ref[...]).astype(BF16)
    c = yc_ref[...].astype(BF16)
    mixed = (_dot(a, wo_ref[0:WIDTH_A, :]) + _dot(b, wo_ref[WIDTH_A:WIDTH_A + WIDTH_B, :])
             + _dot(c, wo_ref[WIDTH_A + WIDTH_B:, :]))
    x = x_ref[...] + gate1_ref[0] * mixed
    h = (_rms(x, g2_ref[...]) * (1.0 + sc_ref[0]) + sh_ref[0]).astype(BF16)
    acc = jnp.zeros(x.shape, F32)
    for j in range(FFN_HIDDEN // FFN_CHUNK):
        gate = _dot(h, wgu_ref[0, :, FFN_CHUNK * j:FFN_CHUNK * (j + 1)])
        up = _dot(h, wgu_ref[0, :, FFN_HIDDEN + FFN_CHUNK * j:FFN_HIDDEN + FFN_CHUNK * (j + 1)])
        act = (_silu(gate) * up).astype(BF16)
        acc = acc + _dot(act, wd_ref[FFN_CHUNK * j:FFN_CHUNK * (j + 1), :])
    y = x + gate_ref[0] * acc
    if final_norm:
        y = _rms(y, gf_ref[...])
    o_ref[...] = y


def _mix_ffn(ya, yb, yc, x2, mod3, mod_first, ga, gb, w_out, g2, wgu, w_down, layer, gf, S, final_norm):
    T = x2.shape[0]
    tm = TM_PROJ
    per_b = S // tm
    row = lambda w: pl.BlockSpec((tm, w), lambda i: (i, 0))
    full = lambda a: pl.BlockSpec(a.shape, lambda i: (0,) * a.ndim)
    modspec = lambda k: _mod_spec(per_b, mod_first, k)
    return pl.pallas_call(
        functools.partial(_mix_ffn_kernel, final_norm=final_norm),
        grid=(T // tm,),
        in_specs=[row(WIDTH_A), row(WIDTH_B), row(WIDTH_C), row(D_MODEL), modspec(2), full(ga), full(gb), full(w_out),
                  modspec(3), modspec(4), modspec(5), full(g2),
                  pl.BlockSpec((1,) + wgu.shape[1:], lambda i: (layer, 0, 0)),
                  pl.BlockSpec((1,) + w_down.shape[1:], lambda i: (layer, 0, 0)), full(gf)],
        out_specs=row(D_MODEL),
        out_shape=jax.ShapeDtypeStruct((T, D_MODEL), F32),
        scratch_shapes=[pltpu.VMEM(w_down.shape[1:], BF16)],
        compiler_params=_params("arbitrary"),
        name="mix_ffn_final" if final_norm else "mix_ffn",
    )(ya, yb, yc, x2, mod3, ga, gb, w_out, mod3, mod3, mod3, g2, wgu, w_down, gf)


def _head_tile_perm():
    idx = []
    for r in range(NSA_REP):
        idx += list(range(HEAD_DIM * r, HEAD_DIM * (r + 1)))
        idx += list(range(HEAD_DIM * (NSA_REP + r), HEAD_DIM * (NSA_REP + r + 1)))
    return np.asarray(idx, np.int32)


def _in_cols():
    std = np.full((N_STD,), -1, np.int64)
    tr = np.full((N_T,), -1, np.int64)
    off = 0
    tr[TSEG_Q:TSEG_Q + WIDTH_A] = _head_tile_perm()
    off += WIDTH_A
    std[SEG_CKV:SEG_CKV + 2 * LANES] = off + np.arange(2 * LANES)
    off += 2 * LANES
    std[SEG_SK:SEG_SK + LANES] = off + np.arange(LANES)
    off += LANES
    tr[TSEG_SV:TSEG_SV + LANES] = off + np.arange(LANES)
    off += LANES
    std[SEG_WK:SEG_WK + LANES] = off + np.arange(LANES)
    off += LANES
    tr[TSEG_WV:TSEG_WV + LANES] = off + np.arange(LANES)
    off += LANES
    tr[TSEG_G:TSEG_G + N_GATES] = off + np.arange(N_GATES)
    off += N_GATES
    std[SEG_QL:SEG_QL + MLA_Q_LORA] = off + np.arange(MLA_Q_LORA)
    off += MLA_Q_LORA
    std[SEG_KVL:SEG_KVL + MLA_KV_LORA] = off + np.arange(MLA_KV_LORA)
    off += MLA_KV_LORA
    std[SEG_KR + MLA_NOPE:SEG_KR + MLA_NOPE + MLA_ROPE] = off + np.arange(MLA_ROPE)
    off += MLA_ROPE
    for seg in (SEG_CX, SEG_MV, SEG_MO):
        std[seg:seg + WIDTH_C] = off + np.arange(WIDTH_C)
        off += WIDTH_C
    tr[TSEG_IF:TSEG_IF + 2 * MLSTM_HEADS] = off + np.arange(2 * MLSTM_HEADS)
    return std, tr


def _gather_cols(w, cols):
    pieces, start = [], 0
    for i in range(1, len(cols) + 1):
        run_ends = i == len(cols) or (cols[i] != cols[i - 1] + 1 if cols[i - 1] >= 0 else cols[i] >= 0)
        if run_ends:
            first = int(cols[start])
            pieces.append(w[:, first:first + i - start] if first >= 0 else jnp.zeros((w.shape[0], i - start), w.dtype))
            start = i
    return jnp.concatenate(pieces, axis=1)


def _layer_weights(l, w_in, w_uq, w_ukv, w_out, g_out_a):
    std, tr = _in_cols()
    w_std = _gather_cols(w_in[l], std).astype(BF16)
    w_t = _gather_cols(w_in[l], tr).T.astype(BF16)
    cq = np.full((MLA_HEADS * LANES,), -1, np.int64)
    ck = np.full((MLA_HEADS * LANES,), -1, np.int64)
    cv = np.zeros((WIDTH_B,), np.int64)
    dq = MLA_NOPE + MLA_ROPE
    dkv = MLA_NOPE + MLA_V
    for hd in range(MLA_HEADS):
        cq[LANES * hd:LANES * hd + dq] = dq * hd + np.arange(dq)
        ck[LANES * hd:LANES * hd + MLA_NOPE] = dkv * hd + np.arange(MLA_NOPE)
        cv[MLA_V * hd:MLA_V * (hd + 1)] = dkv * hd + MLA_NOPE + np.arange(MLA_V)
    wuqT = _gather_cols(w_uq[l], cq).T.astype(BF16)
    wkm = _gather_cols(w_ukv[l], ck).astype(BF16)
    wvmT = _gather_cols(w_ukv[l], cv).T.astype(BF16)
    perm = _head_tile_perm()
    head_rows = [w_out[l][int(o):int(o) + HEAD_DIM] for o in perm[::HEAD_DIM]]
    wo = jnp.concatenate(head_rows + [w_out[l][WIDTH_A:]], axis=0).astype(BF16)
    ga = _gather_cols(g_out_a[l].reshape(1, WIDTH_A), perm)
    return w_std, w_t, wuqT, wkm, wvmT, wo, ga


def kernel(x, c, positions, g_norm1, g_norm2, w_ada, b_ada, w_in, cmp_pos, w_cmp_k, w_cmp_v, g_out_a, g_q_lora, w_uq, g_kv_lora, w_ukv, g_out_b, conv_w, conv_b, w_q_m, w_k_m, b_igate, b_fgate, g_mh, skip_m, w_out, w_gu, w_down, g_final):
    B, S, D = x.shape
    T = B * S
    x2 = x.reshape(T, D)
    tabs, tabs_t = _rope_tables(positions)
    mod3 = _ada(c, w_ada, b_ada).reshape(DEPTH * B * N_MOD, 1, D)
    row = lambda v: v.reshape(1, -1)
    wgu = w_gu.astype(BF16)
    for l in range(DEPTH):
        w_std, w_t, wuqT, wkm, wvmT, wo, ga = _layer_weights(l, w_in, w_uq, w_ukv, w_out, g_out_a)
        (qaT, gT, ck, cv, sk_aug, wk, svT, wvT, qmT, km, vmT, cx, mv, mo, gif_t) = _inproj(
            x2, mod3, l * B * N_MOD, row(g_norm1[l]), w_std, w_t, tabs, tabs_t, row(g_q_lora[l]), wuqT,
            row(g_kv_lora[l]), wkm, wvmT, S)
        kc, vcT = _compress(ck, cv, cmp_pos[l], w_cmp_k[l], w_cmp_v[l], B, S)
        ya = _nsa(qaT, gT, kc, vcT, sk_aug, svT, wk, wvT, B, S)
        yb = _mla(qmT, km, vmT, B, S)
        yc = _mlstm(cx, mv, mo, gif_t, conv_w[l], conv_b[l], w_q_m[l], w_k_m[l], b_igate[l], b_fgate[l],
                    g_mh[l], skip_m[l], B, S)
        x2 = _mix_ffn(ya, yb, yc, x2, mod3, l * B * N_MOD, ga, row(g_out_b[l]), wo, row(g_norm2[l]),
                      wgu, w_down, l, row(g_final), S, final_norm=(l == DEPTH - 1))
    return x2.reshape(B, S, D)
```

```python
import functools

import numpy as np
import jax
import jax.numpy as jnp
from jax import lax
from jax.experimental import pallas as pl
from jax.experimental.pallas import tpu as pltpu

F32 = jnp.float32
BF16 = jnp.bfloat16

D_MODEL = 1024
DEPTH = 2
HEAD_DIM = 64
ROPE_THETA = 500000.0
NSA_ROT_HALF = HEAD_DIM // 8
NORM_EPS = 1e-6

NSA_HEADS = 6
NSA_KV_HEADS = 2
NSA_REP = NSA_HEADS // NSA_KV_HEADS
CMP_LEN = 32
CMP_STRIDE = 16
SLC_LEN = 64
SLC_TOPK = 16
WINDOW = 512

MLA_HEADS = 6
MLA_Q_LORA = 256
MLA_KV_LORA = 128
MLA_NOPE = 64
MLA_ROPE = 32
MLA_V = 64

MLSTM_HEADS = 4
MLSTM_DIM = 64
MLSTM_CONV = 4

WIDTH_A = NSA_HEADS * HEAD_DIM
WIDTH_B = MLA_HEADS * MLA_V
WIDTH_C = MLSTM_HEADS * MLSTM_DIM
FFN_HIDDEN = 2816
N_GATES = 3 * NSA_HEADS
GATE_ROWS = 24

LANES = 128
NEG = -1e30
LOG2E = 1.4426950408889634
VMEM_LIMIT = 56 * 1024 * 1024

TM_PROJ = 512
TQ_NSA = 256
TK_SLC = 256
CHAIN_LANES = 256
TQ_MLA = 256
FFN_CHUNK = 256
MLSTM_TILE = 128
MLSTM_GROUP = 16
MXU_LOOKAHEAD = 4

SEG_CKV, SEG_SK, SEG_WK, SEG_QL, SEG_KVL, SEG_KR, SEG_CX, SEG_MV, SEG_MO = (
    0, 256, 384, 512, 768, 896, 1024, 1280, 1536)
N_STD = 1792
TSEG_Q, TSEG_G, TSEG_SV, TSEG_WV, TSEG_IF = 0, 384, 416, 544, 672
TSEG_G_ROWS = 32
TSEG_IF_ROWS = 16
N_T = 688


def _params(*sem):
    return pltpu.CompilerParams(dimension_semantics=sem, vmem_limit_bytes=VMEM_LIMIT)


def _dot(a, b):
    return jnp.dot(a, b, preferred_element_type=F32)


def _dot_nt(a, b):
    return lax.dot_general(a, b, (((1,), (1,)), ((), ())), preferred_element_type=F32)


def _dot_tn(a, b):
    return lax.dot_general(a, b, (((0,), (0,)), ((), ())), preferred_element_type=F32)


def _split3(x):
    hi = x.astype(BF16)
    r1 = x - hi.astype(F32)
    mid = r1.astype(BF16)
    lo = (r1 - mid.astype(F32)).astype(BF16)
    return hi, mid, lo


def _rms(x, g):
    return x * lax.rsqrt(jnp.mean(x * x, axis=-1, keepdims=True) + NORM_EPS) * g


def _sigmoid(x):
    return 1.0 / (1.0 + jnp.exp(-x))


def _silu(x):
    return x * _sigmoid(x)


def _rope(x, cos, sin, half, x1_lane):
    xr = jnp.where(x1_lane, -pltpu.roll(x, LANES - half, 1), pltpu.roll(x, half, 1))
    return x * cos + xr * sin


def _ada_kernel(c_ref, w_ref, b_ref, o_ref):
    c = c_ref[...]
    ca = _silu(c).astype(BF16)
    o_ref[0] = _dot(ca, w_ref[0].astype(BF16)) + b_ref[0]


def _ada(c, w_ada, b_ada):
    L, D, N = w_ada.shape
    B = c.shape[0]
    tn = 1536
    return pl.pallas_call(
        _ada_kernel,
        grid=(L, N // tn),
        in_specs=[pl.BlockSpec((B, D), lambda l, j: (0, 0)),
                  pl.BlockSpec((1, D, tn), lambda l, j: (l, 0, j)),
                  pl.BlockSpec((1, 1, tn), lambda l, j: (l, 0, j))],
        out_specs=pl.BlockSpec((1, B, tn), lambda l, j: (l, 0, j)),
        out_shape=jax.ShapeDtypeStruct((L, B, N), F32),
        compiler_params=_params("parallel", "parallel"),
        name="ada_mod",
    )(c, w_ada, b_ada.reshape(L, 1, N))


ROPE_FREQS = NSA_ROT_HALF + MLA_ROPE // 2


def _rope_kernel(posr_ref, invc_ref, spread_ref, one_ref, cn_ref, sn_ref, cm_ref, sm_ref, cnt_ref, snt_ref, cmt_ref, smt_ref):
    tm = posr_ref.shape[1]
    rows_t = invc_ref.shape[0]
    ang_t = jnp.concatenate([invc_ref[...]] * (tm // LANES), axis=1) * jnp.broadcast_to(posr_ref[...], (rows_t, tm))
    c_t, s_t = jnp.cos(ang_t), jnp.sin(ang_t)
    cnt_ref[...] = c_t[:NSA_ROT_HALF]
    snt_ref[...] = s_t[:NSA_ROT_HALF]
    cmt_ref[...] = c_t[NSA_ROT_HALF:ROPE_FREQS]
    smt_ref[...] = s_t[NSA_ROT_HALF:ROPE_FREQS]

    def spread(val):
        hi, mid, lo = _split3(val)
        pieces = jnp.concatenate([hi.astype(F32), mid.astype(F32), lo.astype(F32)], axis=0).astype(BF16)
        return _dot_tn(pieces, spread_ref[...])

    c = spread(c_t) + one_ref[...]
    s = spread(s_t)
    cn_ref[...] = c[:, :LANES]
    cm_ref[...] = c[:, LANES:]
    sn_ref[...] = s[:, :LANES]
    sm_ref[...] = s[:, LANES:]


def _rope_tables(positions):
    T = positions.size
    inv_n = jnp.power(ROPE_THETA, -jnp.arange(0, 2 * NSA_ROT_HALF, 2, dtype=F32) / (2 * NSA_ROT_HALF))
    inv_m = jnp.power(ROPE_THETA, -jnp.arange(0, MLA_ROPE, 2, dtype=F32) / MLA_ROPE)
    rows_t = 4 * NSA_ROT_HALF
    onehot = np.zeros((3 * rows_t, 2 * LANES), np.float32)
    rotated = np.zeros((1, 2 * LANES), np.float32)
    for k in range(3):
        for f in range(NSA_ROT_HALF):
            for lane in (f, f + NSA_ROT_HALF, HEAD_DIM + f, HEAD_DIM + f + NSA_ROT_HALF):
                onehot[rows_t * k + f, lane] = 1.0
                rotated[0, lane] = 1.0
        for f in range(MLA_ROPE // 2):
            for lane in (MLA_NOPE + f, MLA_NOPE + MLA_ROPE // 2 + f):
                onehot[rows_t * k + NSA_ROT_HALF + f, LANES + lane] = 1.0
                rotated[0, LANES + lane] = 1.0
    inv_col = jnp.concatenate([inv_n, inv_m, jnp.zeros((rows_t - ROPE_FREQS,), F32)])
    inv_col = jnp.broadcast_to(inv_col[:, None], (rows_t, LANES))
    tm = 2048
    spec = pl.BlockSpec((tm, LANES), lambda i: (i, 0))
    col = lambda h: pl.BlockSpec((h, tm), lambda i: (0, i))
    full = lambda shape: pl.BlockSpec(shape, lambda i: (0, 0))
    sds = jax.ShapeDtypeStruct
    outs = pl.pallas_call(
        _rope_kernel,
        grid=(T // tm,),
        in_specs=[col(1), full((rows_t, LANES)), full((3 * rows_t, 2 * LANES)), full((1, 2 * LANES))],
        out_specs=[spec] * 4 + [col(NSA_ROT_HALF)] * 2 + [col(MLA_ROPE // 2)] * 2,
        out_shape=[sds((T, LANES), F32)] * 4 + [sds((NSA_ROT_HALF, T), F32)] * 2 + [sds((MLA_ROPE // 2, T), F32)] * 2,
        compiler_params=_params("parallel"),
        name="rope_tables",
    )(positions.reshape(1, T).astype(F32), inv_col, jnp.asarray(onehot, BF16), jnp.asarray(1.0 - rotated))
    return outs[:4], outs[4:]


def _rope_rows(t, offset, half, cos, sin):
    x1, x2 = t[offset:offset + half], t[offset + half:offset + 2 * half]
    return x1 * cos - x2 * sin, x2 * cos + x1 * sin


def _inproj_kernel(x_ref, sh_ref, sc_ref, g1_ref, ws_ref, wt_ref, cn_ref, sn_ref, cm_ref, sm_ref,
                   cnT_ref, snT_ref, cmT_ref, smT_ref, gq_ref, wuqT_ref, gkv_ref, wkm_ref, wvmT_ref,
                   qaT_ref, gT_ref, ck_ref, cv_ref, ska_ref, wk_ref, svT_ref, wvT_ref,
                   qmT_ref, km_ref, vmT_ref, cx_ref, mv_ref, mo_ref, gif_ref, *, per_b):
    tm = x_ref.shape[0]
    x = x_ref[...]
    h = _rms(x, g1_ref[...]) * (1.0 + sc_ref[0]) + sh_ref[0]
    hb = h.astype(BF16)

    def seg(start, width):
        return _dot(hb, ws_ref[:, start:start + width])

    lane = lax.broadcasted_iota(jnp.int32, (1, LANES), 1)
    x1_n = (lane % HEAD_DIM) < NSA_ROT_HALF
    x1_m = lane < MLA_NOPE + MLA_ROPE // 2
    cn, sn, cm, sm = cn_ref[...], sn_ref[...], cm_ref[...], sm_ref[...]
    rope_n = lambda t: _rope(t, cn, sn, NSA_ROT_HALF, x1_n)
    rope_m = lambda t: _rope(t, cm, sm, MLA_ROPE // 2, x1_m)

    qn = _rms(seg(SEG_QL, MLA_Q_LORA), gq_ref[...]).astype(BF16)
    kvl_kr = seg(SEG_KVL, 2 * LANES)
    kvn = _rms(kvl_kr[:, :LANES], gkv_ref[...]).astype(BF16)
    kr = rope_m(kvl_kr[:, LANES:])

    out_t = _dot_nt(wt_ref[...], hb)
    seg_t = lambda start, height: out_t[start:start + height]

    qt = seg_t(TSEG_Q, WIDTH_A)
    cnt, snt = cnT_ref[...], snT_ref[...]
    parts = []
    for hd in range(NSA_HEADS):
        o = HEAD_DIM * hd
        parts += list(_rope_rows(qt, o, NSA_ROT_HALF, cnt, snt)) + [qt[o + 2 * NSA_ROT_HALF:o + HEAD_DIM]]
    qaT_ref[...] = (jnp.concatenate(parts, axis=0) * (HEAD_DIM ** -0.5 * LOG2E)).astype(BF16)
    gT_ref[...] = _sigmoid(seg_t(TSEG_G, TSEG_G_ROWS))[:GATE_ROWS]
    svT_ref[...] = seg_t(TSEG_SV, LANES).astype(BF16)
    wvT_ref[...] = seg_t(TSEG_WV, LANES).astype(BF16)
    gif_ref[...] = seg_t(TSEG_IF, TSEG_IF_ROWS)[:2 * MLSTM_HEADS]

    ckv = seg(SEG_CKV, 2 * LANES)
    ck_ref[...] = rope_n(ckv[:, :LANES])
    cv_ref[...] = ckv[:, LANES:]
    ska_ref[:, :LANES] = rope_n(seg(SEG_SK, LANES)).astype(BF16)
    srow = (pl.program_id(0) % per_b) * tm + lax.broadcasted_iota(jnp.int32, (tm, LANES), 0)
    lane2 = lax.broadcasted_iota(jnp.int32, (tm, LANES), 1)
    code = (lane2 < 4 * SLC_TOPK) & ((lane2 & (2 * SLC_TOPK - 1)) == srow // SLC_LEN)
    ska_ref[:, LANES:] = jnp.where(code, 1.0, 0.0).astype(BF16)
    wk_ref[...] = rope_n(seg(SEG_WK, LANES)).astype(BF16)

    qmt = _dot_nt(wuqT_ref[...], qn)
    kk = _dot(kvn, wkm_ref[...])
    vmt = _dot_nt(wvmT_ref[...], kvn)
    cx_ref[...] = seg(SEG_CX, WIDTH_C)
    mv_ref[...] = seg(SEG_MV, WIDTH_C).astype(BF16)
    mo_ref[...] = seg(SEG_MO, WIDTH_C)
    cmt, smt = cmT_ref[...], smT_ref[...]
    parts = []
    for hd in range(MLA_HEADS):
        o = LANES * hd
        parts += [qmt[o:o + MLA_NOPE]] + list(_rope_rows(qmt, o + MLA_NOPE, MLA_ROPE // 2, cmt, smt))
        parts += [qmt[o + MLA_NOPE + MLA_ROPE:o + LANES]]
    qmT_ref[...] = (jnp.concatenate(parts, axis=0) * ((MLA_NOPE + MLA_ROPE) ** -0.5 * LOG2E)).astype(BF16)
    for hd in range(MLA_HEADS):
        km_ref[:, LANES * hd:LANES * (hd + 1)] = (kk[:, LANES * hd:LANES * (hd + 1)] + kr).astype(BF16)
    vmT_ref[...] = vmt.astype(BF16)


N_MOD = 6


def _mod_spec(per_b, first, k):
    return pl.BlockSpec((1, 1, D_MODEL), lambda i: (first + (i // per_b) * N_MOD + k, 0, 0))


def _inproj(x2, mod3, mod_first, g1, w_std, w_t, tabs, tabs_t, gq, wuqT, gkv, wkm, wvmT, S):
    T = x2.shape[0]
    tm = TM_PROJ
    per_b = S // tm
    row = lambda w: pl.BlockSpec((tm, w), lambda i: (i, 0))
    col = lambda h: pl.BlockSpec((h, tm), lambda i: (0, i))
    full = lambda a: pl.BlockSpec(a.shape, lambda i: (0,) * a.ndim)
    outs = [(WIDTH_A, BF16, True), (GATE_ROWS, F32, True), (LANES, F32, False), (LANES, F32, False), (2 * LANES, BF16, False),
            (LANES, BF16, False), (LANES, BF16, True), (LANES, BF16, True),
            (MLA_HEADS * LANES, BF16, True), (MLA_HEADS * LANES, BF16, False), (WIDTH_B, BF16, True),
            (WIDTH_C, F32, False), (WIDTH_C, BF16, False), (WIDTH_C, F32, False), (2 * MLSTM_HEADS, F32, True)]
    return pl.pallas_call(
        functools.partial(_inproj_kernel, per_b=per_b),
        grid=(T // tm,),
        in_specs=[row(D_MODEL), _mod_spec(per_b, mod_first, 0), _mod_spec(per_b, mod_first, 1),
                  full(g1), full(w_std), full(w_t)] + [row(LANES)] * 4
                 + [col(t.shape[0]) for t in tabs_t] + [full(gq), full(wuqT), full(gkv), full(wkm), full(wvmT)],
        out_specs=[col(w) if tr else row(w) for w, _, tr in outs],
        out_shape=[jax.ShapeDtypeStruct((w, T) if tr else (T, w), dt) for w, dt, tr in outs],
        compiler_params=_params("parallel"),
        name="in_proj",
    )(x2, mod3, mod3, g1, w_std, w_t, *tabs, *tabs_t, gq, wuqT, gkv, wkm, wvmT)


def _compress_kernel(xk_ref, xv_ref, wk_ref, wv_ref, pos_ref, wkf_ref, wvf_ref, kc_ref, vcT_ref):
    ng = xk_ref.shape[0] // CMP_STRIDE
    row = lax.broadcasted_iota(jnp.int32, (ng, LANES), 0)
    pos = pos_ref[...].astype(BF16)
    acc_k = jnp.zeros((ng, 2 * LANES), F32)
    acc_v = jnp.zeros((ng, 2 * LANES), F32)
    for t in range(CMP_STRIDE):
        tok = pl.ds(t, ng, stride=CMP_STRIDE)
        acc_k = acc_k + _dot(xk_ref[tok, :].astype(BF16), wk_ref[t])
        acc_v = acc_v + _dot(xv_ref[tok, :].astype(BF16), wv_ref[t])

    def finish(acc, wf_ref):
        const = _dot(pos, wf_ref[...].astype(BF16))
        both = acc[:, :LANES] + pltpu.roll(acc[:, LANES:], ng - 1, 0) + jnp.concatenate([const, const], axis=1)
        return jnp.where(row < ng - 1, both, 0.0)

    kc_ref[0] = finish(acc_k, wkf_ref).astype(BF16)
    vcT_ref[0] = finish(acc_v, wvf_ref).T.astype(BF16)


def _compress(ck, cv, cmp_pos, w_cmp_k, w_cmp_v, B, S):
    ng = S // CMP_STRIDE

    def per_token(w):
        a = w[:CMP_STRIDE * HEAD_DIM].reshape(CMP_STRIDE, HEAD_DIM, HEAD_DIM)
        b = w[CMP_STRIDE * HEAD_DIM:].reshape(CMP_STRIDE, HEAD_DIM, HEAD_DIM)
        z = jnp.zeros_like(a)
        top = jnp.concatenate([a, z, b, z], axis=2)
        bot = jnp.concatenate([z, a, z, b], axis=2)
        return jnp.concatenate([top, bot], axis=1).astype(BF16)

    full = lambda a: pl.BlockSpec(a.shape, lambda b: (0,) * a.ndim)
    wk3, wv3 = per_token(w_cmp_k), per_token(w_cmp_v)
    posf = cmp_pos.reshape(1, CMP_LEN * HEAD_DIM)
    ospec = pl.BlockSpec((1, ng, LANES), lambda b: (b, 0, 0))
    return pl.pallas_call(
        _compress_kernel,
        grid=(B,),
        in_specs=[pl.BlockSpec((S, LANES), lambda b: (b, 0)), pl.BlockSpec((S, LANES), lambda b: (b, 0)),
                  full(wk3), full(wv3), full(posf), full(w_cmp_k), full(w_cmp_v)],
        out_specs=[ospec, ospec],
        out_shape=[jax.ShapeDtypeStruct((B, ng, LANES), BF16)] * 2,
        compiler_params=_params("parallel"),
        name="nsa_compress",
    )(ck, cv, wk3, wv3, posf, w_cmp_k, w_cmp_v)


def _nsa_kernel(qT_ref, gT_ref, kc_ref, vcT_ref, sk_ref, svT_ref, wk_ref, wvT_ref, o_ref, qaug_scr, acc_scr, s_scr):
    tq = TQ_NSA
    cols = NSA_HEADS * tq
    pair = CHAIN_LANES
    t0 = pl.program_id(1) * tq
    n_slc = SLC_TOPK * 2

    frow = lax.broadcasted_iota(jnp.int32, (LANES, tq), 0)
    g0_row = frow < HEAD_DIM
    tiles = [qT_ref[LANES * r:LANES * (r + 1), :] for r in range(NSA_REP)]
    zero = jnp.zeros_like(tiles[0])
    q6 = jnp.concatenate([jnp.where(g0_row, t, zero) for t in tiles]
                         + [jnp.where(g0_row, zero, t) for t in tiles], axis=1)
    qaug_scr[0:LANES, :] = q6
    tq_l = t0 + (lax.broadcasted_iota(jnp.int32, (1, cols), 1) & (tq - 1))

    pairs = [slice(pair * pp, pair * (pp + 1)) for pp in range(cols // pair)]
    vrows = lambda pp: slice(HEAD_DIM * (pp // NSA_REP), HEAD_DIM * (pp // NSA_REP + 1))
    s = _dot(kc_ref[0], q6)
    span = WINDOW + tq
    start = pl.multiple_of(jnp.maximum(t0 - WINDOW, 0), tq)
    kw = wk_ref[pl.ds(start, span), :]
    win_scores = [_dot(kw, q6[:, sl]) for sl in pairs]

    tq_1 = tq_l[:, :tq]
    nrow = lax.broadcasted_iota(jnp.int32, (LANES, tq), 0)
    cmp_bias = jnp.where(nrow * CMP_STRIDE + (CMP_LEN - 1) <= tq_1, 0.0, NEG)
    s = s + jnp.concatenate([cmp_bias] * NSA_HEADS, axis=1)
    e = jnp.exp2(s - jnp.max(s, axis=0, keepdims=True))
    seen = jnp.where(tq_l >= CMP_LEN - 1, 1.0, 0.0)
    p = e * (seen / jnp.sum(e, axis=0, keepdims=True))
    p_b = p.astype(BF16)
    o_cmp = [_dot(vcT_ref[0, vrows(pp), :], p_b[:, sl]) for pp, sl in enumerate(pairs)]

    jr = lax.broadcasted_iota(jnp.int32, (n_slc, LANES), 0)
    nc = lax.broadcasted_iota(jnp.int32, (n_slc, LANES), 1)
    ovl = ((nc * CMP_STRIDE < jr * SLC_LEN + SLC_LEN) & (nc * CMP_STRIDE + CMP_LEN > jr * SLC_LEN)
           & (nc < LANES - 1))
    ovl = jnp.where(ovl, 1.0, 0.0).astype(BF16)
    jq = lax.broadcasted_iota(jnp.int32, (n_slc, tq), 0)
    tl = t0 + lax.broadcasted_iota(jnp.int32, (n_slc, tq), 1)
    cur = tl // SLC_LEN
    forced = (jq == 0) | (jq == cur) | (jq == cur - 1)
    future = jq * SLC_LEN > tl
    bias_t = []
    for g in range(NSA_KV_HEADS):
        pg = p[:, (3 * g) * tq:(3 * g + 1) * tq] + p[:, (3 * g + 1) * tq:(3 * g + 2) * tq] + p[:, (3 * g + 2) * tq:(3 * g + 3) * tq]
        imp = sum(_dot(ovl, part) for part in _split3(pg))
        imp = jnp.where(forced, jnp.inf, imp)
        imp = jnp.where(future, -jnp.inf, imp)
        rows8 = [imp[8 * r:8 * (r + 1)] for r in range(n_slc // 8)]
        sub = lax.broadcasted_iota(jnp.int32, (8, tq), 0)
        ranks = [jnp.zeros((8, tq), F32) for _ in rows8]
        for jp in range(n_slc):
            rv = imp[jp:jp + 1, :]
            for r, blk in enumerate(rows8):
                if 8 * r > jp:
                    ahead = rv >= blk
                elif 8 * r + 7 < jp:
                    ahead = rv > blk
                else:
                    ahead = (rv > blk) | ((rv == blk) & (sub > jp - 8 * r))
                ranks[r] = ranks[r] + jnp.where(ahead, 1.0, 0.0)
        rank = jnp.concatenate(ranks, axis=0)
        bias_t.append(jnp.where(rank < float(SLC_TOPK), 0.0, NEG).astype(BF16))
    zb = jnp.zeros((n_slc, NSA_REP * tq), BF16)
    qaug_scr[LANES:LANES + n_slc, :] = jnp.concatenate([bias_t[0]] * NSA_REP + [zb], axis=1)
    qaug_scr[LANES + n_slc:LANES + 2 * n_slc, :] = jnp.concatenate([zb] + [bias_t[1]] * NSA_REP, axis=1)
    qaug_scr[LANES + 2 * n_slc:, :] = jnp.zeros((LANES - 2 * n_slc, cols), BF16)

    vwt = wvT_ref[:, pl.ds(start, span)]
    wrow = start + lax.broadcasted_iota(jnp.int32, (span, pair), 0)
    o_win = []
    for pp, (sl, sc) in enumerate(zip(pairs, win_scores)):
        in_window = lax.bitcast_convert_type(tq_l[:, sl] - wrow, jnp.uint32) < jnp.uint32(WINDOW)
        sc = jnp.where(in_window, sc, NEG)
        ew = jnp.exp2(sc - jnp.max(sc, axis=0, keepdims=True))
        o_win.append(_dot(vwt[vrows(pp)], ew.astype(BF16)) / jnp.sum(ew, axis=0, keepdims=True))

    acc_scr[...] = jnp.zeros((HEAD_DIM, cols), F32)
    krow = lax.broadcasted_iota(jnp.int32, (TK_SLC, pair), 0)

    def slc_scores(k0, sl):
        return _dot(sk_ref[pl.ds(k0, TK_SLC), :], qaug_scr[:, sl])

    nch = len(pairs)
    la = s_scr.shape[0]
    for pp in range(la):
        s_scr[pp] = slc_scores(0, pairs[pp])

    def slc_tile(k0, m, l, masked):
        vt = svT_ref[:, pl.ds(k0, TK_SLC)]
        ms, ls, accs = [], [], []
        ahead = [s_scr[pp] for pp in range(la)]
        for pp, sl in enumerate(pairs):
            sc = ahead.pop(0)
            nxt = pp + la
            if nxt < nch:
                ahead.append(slc_scores(k0, pairs[nxt]))
            elif not masked:
                s_scr[nxt - nch] = slc_scores(pl.multiple_of(k0 + TK_SLC, TK_SLC), pairs[nxt - nch])
            if masked:
                sc = jnp.where(k0 + krow <= tq_l[:, sl], sc, NEG)
            m_new = jnp.maximum(m[:, sl], jnp.max(sc, axis=0, keepdims=True))
            alpha = jnp.exp2(m[:, sl] - m_new)
            pe = jnp.exp2(sc - m_new)
            ls.append(alpha * l[:, sl] + jnp.sum(pe, axis=0, keepdims=True))
            accs.append(alpha * acc_scr[:, sl] + _dot(vt[vrows(pp)], pe.astype(BF16)))
            ms.append(m_new)
        acc_scr[...] = jnp.concatenate(accs, axis=1)
        return jnp.concatenate(ms, axis=1), jnp.concatenate(ls, axis=1)

    def slc_step(kt, carry):
        return slc_tile(pl.multiple_of(kt * TK_SLC, TK_SLC), carry[0], carry[1], False)

    n_full = t0 // TK_SLC
    m, l = lax.fori_loop(0, n_full, slc_step, (jnp.full((1, cols), NEG, F32), jnp.zeros((1, cols), F32)))
    m, l = slc_tile(pl.multiple_of(n_full * TK_SLC, TK_SLC), m, l, True)
    o_slc = acc_scr[...] / l

    gt = gT_ref[...]
    mixed = []
    for hd, sl in enumerate(pairs):
        gate = lambda j: gt[3 * hd + j:3 * hd + j + 1, :]
        mixed.append(gate(0) * o_cmp[hd] + gate(1) * o_slc[:, sl] + gate(2) * o_win[hd])
    for t in range(NSA_HEADS // 2):
        o_ref[:, LANES * t:LANES * (t + 1)] = jnp.concatenate([mixed[2 * t], mixed[2 * t + 1]], axis=0).T


def _nsa(qaT, gT, kc, vcT, sk_aug, svT, wk, wvT, B, S):
    tq = TQ_NSA
    assert CHAIN_LANES == tq, "the kernel treats one softmax chain as one head"
    nq = S // tq
    cols = NSA_HEADS * tq
    qcol =lambda h: pl.BlockSpec((h, tq), lambda b, i: (0, b * nq + i))
    seq = lambda w: pl.BlockSpec((S, w), lambda b, i: (b, 0))
    seqT = pl.BlockSpec((LANES, S), lambda b, i: (0, b))
    cspec = pl.BlockSpec((1, LANES, LANES), lambda b, i: (b, 0, 0))
    return pl.pallas_call(
        _nsa_kernel,
        grid=(B, nq),
        in_specs=[qcol(WIDTH_A), qcol(gT.shape[0]), cspec, cspec, seq(2 * LANES), seqT, seq(LANES), seqT],
        out_specs=pl.BlockSpec((tq, WIDTH_A), lambda b, i: (b * nq + i, 0)),
        out_shape=jax.ShapeDtypeStruct((B * S, WIDTH_A), F32),
        scratch_shapes=[pltpu.VMEM((2 * LANES, cols), BF16), pltpu.VMEM((HEAD_DIM, cols), F32),
                        pltpu.VMEM((cols // CHAIN_LANES, TK_SLC, CHAIN_LANES), F32)],
        compiler_params=_params("parallel", "arbitrary"),
        name="nsa_attention",
    )(qaT, gT, kc, vcT, sk_aug, svT, wk, wvT)


def _mla_kernel(qT_ref, k_ref, vT_ref, o_ref, acc_scr, s_scr):
    tq = TQ_MLA
    t0 = pl.program_id(1) * tq
    tq_l = t0 + lax.broadcasted_iota(jnp.int32, (1, tq), 1)
    krow = lax.broadcasted_iota(jnp.int32, (tq, tq), 0)
    acc_scr[...] = jnp.zeros((MLA_HEADS, MLA_V, tq), F32)

    def scores(k0, hd):
        k = k_ref[pl.ds(k0, tq), LANES * hd:LANES * (hd + 1)]
        return _dot(k, qT_ref[LANES * hd:LANES * (hd + 1), :])

    for hd in range(MXU_LOOKAHEAD):
        s_scr[hd] = scores(0, hd)

    def tile(k0, ms, ls, masked):
        new_m, new_l, accs = [], [], []
        ahead = [s_scr[hd] for hd in range(MXU_LOOKAHEAD)]
        for hd in range(MLA_HEADS):
            sc = ahead.pop(0)
            nxt = hd + MXU_LOOKAHEAD
            if nxt < MLA_HEADS:
                ahead.append(scores(k0, nxt))
            elif not masked:
                s_scr[nxt - MLA_HEADS] = scores(pl.multiple_of(k0 + tq, tq), nxt - MLA_HEADS)
            if masked:
                sc = jnp.where(k0 + krow <= tq_l, sc, NEG)
            m_new = jnp.maximum(ms[hd], jnp.max(sc, axis=0, keepdims=True))
            alpha = jnp.exp2(ms[hd] - m_new)
            pe = jnp.exp2(sc - m_new)
            new_l.append(alpha * ls[hd] + jnp.sum(pe, axis=0, keepdims=True))
            vt = vT_ref[MLA_V * hd:MLA_V * (hd + 1), pl.ds(k0, tq)]
            accs.append(alpha * acc_scr[hd] + _dot(vt, pe.astype(BF16)))
            new_m.append(m_new)
        acc_scr[...] = jnp.stack(accs)
        return tuple(new_m), tuple(new_l)

    def step(kt, carry):
        return tile(pl.multiple_of(kt * tq, tq), carry[0], carry[1], False)

    n_full = pl.program_id(1)
    init = (tuple(jnp.full((1, tq), NEG, F32) for _ in range(MLA_HEADS)),
            tuple(jnp.zeros((1, tq), F32) for _ in range(MLA_HEADS)))
    ms, ls = lax.fori_loop(0, n_full, step, init)
    ms, ls = tile(pl.multiple_of(n_full * tq, tq), ms, ls, True)
    for pr in range(MLA_HEADS // 2):
        t = jnp.concatenate([acc_scr[2 * pr] / ls[2 * pr], acc_scr[2 * pr + 1] / ls[2 * pr + 1]], axis=0)
        o_ref[:, LANES * pr:LANES * (pr + 1)] = t.T


def _mla(qmT, km, vmT, B, S):
    tq = TQ_MLA
    nq = S // tq
    return pl.pallas_call(
        _mla_kernel,
        grid=(B, nq),
        in_specs=[pl.BlockSpec((MLA_HEADS * LANES, tq), lambda b, i: (0, b * nq + i)),
                  pl.BlockSpec((S, MLA_HEADS * LANES), lambda b, i: (b, 0)),
                  pl.BlockSpec((WIDTH_B, S), lambda b, i: (0, b))],
        out_specs=pl.BlockSpec((tq, WIDTH_B), lambda b, i: (b * nq + i, 0)),
        out_shape=jax.ShapeDtypeStruct((B * S, WIDTH_B), F32),
        scratch_shapes=[pltpu.VMEM((MLA_HEADS, MLA_V, tq), F32), pltpu.VMEM((MXU_LOOKAHEAD, tq, tq), F32)],
        compiler_params=_params("parallel", "arbitrary"),
        name="mla_attention",
    )(qmT, km, vmT)


def _log_sigmoid(z):
    return jnp.minimum(z, 0.0) - jnp.log1p(jnp.exp(-jnp.abs(z)))


def _mlstm_kernel(cx_ref, v_ref, o_ref, g_ref, cw_ref, cb_ref, wq_ref, wk_ref, br_ref, gmh_ref, skip_ref, y_ref,
                  xc_scr, q_scr, k_scr, w_scr, inter_scr, floor_scr, mfull_scr, ut_scr, eo_scr, el_scr, cprev_scr):
    S = cx_ref.shape[0]
    L = MLSTM_TILE
    NC = S // L
    d = MLSTM_DIM
    pairs = MLSTM_HEADS // 2
    group = MLSTM_GROUP

    x = cx_ref[...]
    rowi = lax.broadcasted_iota(jnp.int32, (S, WIDTH_C), 0)
    conv = x * cw_ref[MLSTM_CONV - 1:MLSTM_CONV, :]
    for back in range(1, MLSTM_CONV):
        shifted = jnp.where(rowi >= back, pltpu.roll(x, back, 0), 0.0)
        conv = conv + shifted * cw_ref[MLSTM_CONV - 1 - back:MLSTM_CONV - back, :]
    xc = _silu(conv + cb_ref[...])
    xc_scr[...] = xc
    xcb = xc.astype(BF16)
    q_scr[...] = _dot(xcb, wq_ref[...]).astype(BF16)
    k_scr[...] = _dot(xcb, wk_ref[...]).astype(BF16)

    nh = MLSTM_HEADS
    lane_in_chunk = lax.broadcasted_iota(jnp.int32, (nh, S), 1) & (L - 1)

    def scan_lanes(val, op, fill):
        sh = 1
        while sh < L:
            val = op(val, jnp.where(lane_in_chunk >= sh, pltpu.roll(val, sh, 1), fill))
            sh *= 2
        return val

    gt = g_ref[...] + br_ref[...]
    ig = gt[0:nh]
    b = scan_lanes(_log_sigmoid(gt[nh:]), jnp.add, 0.0)
    u = ig - b
    cmu = scan_lanes(u, jnp.maximum, -jnp.inf)
    ut_scr[0:nh, :] = u

    m = jnp.zeros((nh, 1), F32)
    w_loc, inter, floor, m_tok = [], [], [], []
    for c in range(NC):
        blk = slice(L * c, L * (c + 1))
        b_last, u_max = b[:, L * (c + 1) - 1:L * (c + 1)], cmu[:, L * (c + 1) - 1:L * (c + 1)]
        m_top = jnp.maximum(m, u_max)
        eo_scr[c] = jnp.broadcast_to(jnp.exp(m - m_top), (nh, LANES))
        el_scr[c] = jnp.broadcast_to(jnp.exp(u_max - m_top), (nh, LANES))
        mt = jnp.maximum(cmu[:, blk], m)
        m_tok.append(mt)
        inter.append(jnp.exp(m - mt))
        w_loc.append(jnp.exp(u[:, blk] - u_max))
        floor.append(jnp.exp(-(b[:, blk] + mt)))
        m = b_last + m_top

    hrow = lax.broadcasted_iota(jnp.int32, (4 * nh, 2 * LANES), 0)
    hcol = lax.broadcasted_iota(jnp.int32, (4 * nh, 2 * LANES), 1)
    to_heads = jnp.where((hrow < 3 * nh) & (hcol // d == hrow % nh), 1.0, 0.0).astype(BF16)
    frow = lax.broadcasted_iota(jnp.int32, (4 * nh, 4 * LANES), 0)
    fcol = lax.broadcasted_iota(jnp.int32, (4 * nh, 4 * LANES), 1)
    to_full = jnp.where((frow < 3 * nh) & (fcol // LANES == frow % nh), 1.0, 0.0).astype(BF16)

    def spread(chunks, onehot):
        val = jnp.concatenate(chunks, axis=1)
        hi, mid, lo = _split3(val)
        pieces = jnp.concatenate([hi.astype(F32), mid.astype(F32), lo.astype(F32), jnp.zeros_like(val)], axis=0)
        return _dot_tn(pieces.astype(BF16), onehot)

    w_scr[...] = spread(w_loc, to_heads)
    inter_scr[...] = spread(inter, to_heads)
    floor_scr[...] = spread(floor, to_heads)
    mfull_scr[...] = spread(m_tok, to_full)

    arow = lax.broadcasted_iota(jnp.int32, (LANES, 2 * LANES), 0)
    acol = lax.broadcasted_iota(jnp.int32, (LANES, 2 * LANES), 1)
    blk2 = (arow // d) == ((acol & (LANES - 1)) // d)
    ones_v = jnp.ones((L, LANES), BF16)

    def head_rows(ref, c, pr):
        top = jnp.broadcast_to(ref[c, 2 * pr:2 * pr + 1, :], (d, LANES))
        bot = jnp.broadcast_to(ref[c, 2 * pr + 1:2 * pr + 2, :], (d, LANES))
        half = jnp.concatenate([top, bot], axis=0)
        return jnp.concatenate([half, half], axis=1)

    def state_group(g, carry):
        local = []
        for cc in range(group):
            r0 = pl.multiple_of((g * group + cc) * L, L)
            for pr in range(pairs):
                ps = slice(LANES * pr, LANES * (pr + 1))
                kw = (k_scr[pl.ds(r0, L), ps].astype(F32) * w_scr[pl.ds(r0, L), ps]).astype(BF16)
                vo = jnp.concatenate([v_ref[pl.ds(r0, L), ps], ones_v], axis=1)
                local.append(jnp.where(blk2, _dot_tn(kw, vo), 0.0))
        state = list(carry)
        for cc in range(group):
            c = g * group + cc
            for pr in range(pairs):
                cprev_scr[c, pr] = state[pr].astype(BF16)
                state[pr] = head_rows(eo_scr, c, pr) * state[pr] + head_rows(el_scr, c, pr) * local[cc * pairs + pr]
        return tuple(state)

    lax.fori_loop(0, NC // group, state_group, tuple(jnp.zeros((LANES, 2 * LANES), F32) for _ in range(pairs)))

    li = lax.broadcasted_iota(jnp.int32, (L, L), 0)
    si = lax.broadcasted_iota(jnp.int32, (L, L), 1)
    causal = si <= li
    lane = lax.broadcasted_iota(jnp.int32, (L, LANES), 1)
    h0_lane = lane < d
    avg = jnp.where((li // d) == (si // d), 1.0 / d, 0.0).astype(BF16)

    def dot2(val, rhs):
        hi = val.astype(BF16)
        return _dot(hi, rhs) + _dot((val - hi.astype(F32)).astype(BF16), rhs)

    def out_group(g, carry):
        units = [(cc, pr) for cc in range(group) for pr in range(pairs)]
        chunk_of = {u_: g * group + u_[0] for u_ in units}
        rows = {u_: pl.multiple_of(chunk_of[u_] * L, L) for u_ in units}
        qc, sc, pv = {}, {}, {}
        for u_ in units:
            c, pr = chunk_of[u_], u_[1]
            ps = slice(LANES * pr, LANES * (pr + 1))
            qp = q_scr[pl.ds(rows[u_], L), ps]
            kp = k_scr[pl.ds(rows[u_], L), ps]
            qc[u_] = _dot(qp, cprev_scr[c, pr])
            sc[u_] = [_dot_nt(jnp.where(h0_lane if hh == 0 else lane >= d, qp, jnp.zeros_like(qp)), kp)
                      for hh in range(2)]
        for u_ in units:
            pr = u_[1]
            ps = slice(LANES * pr, LANES * (pr + 1))
            vo = jnp.concatenate([v_ref[pl.ds(rows[u_], L), ps], ones_v], axis=1)
            pv[u_] = []
            for hh in range(2):
                hd = 2 * pr + hh
                u_row = ut_scr[hd:hd + 1, pl.ds(rows[u_], L)]
                decay = jnp.where(causal, jnp.exp(u_row - mfull_scr[pl.ds(rows[u_], L), LANES * hd:LANES * (hd + 1)]), 0.0)
                pv[u_].append(_dot((sc[u_][hh] * decay).astype(BF16), vo))
        hg, cen = {}, {}
        for u_ in units:
            pr = u_[1]
            ps = slice(LANES * pr, LANES * (pr + 1))
            it = inter_scr[pl.ds(rows[u_], L), ps]
            num = it * qc[u_][:, :LANES] + jnp.where(h0_lane, pv[u_][0][:, :LANES], pv[u_][1][:, :LANES])
            den = it * qc[u_][:, LANES:] + jnp.where(h0_lane, pv[u_][0][:, LANES:], pv[u_][1][:, LANES:])
            hcell = num / jnp.maximum(jnp.abs(den), floor_scr[pl.ds(rows[u_], L), ps])
            hg[u_] = _sigmoid(o_ref[pl.ds(rows[u_], L), ps]) * hcell
        for u_ in units:
            cen[u_] = hg[u_] - dot2(hg[u_], avg)
        for u_ in units:
            pr = u_[1]
            ps = slice(LANES * pr, LANES * (pr + 1))
            var = dot2(cen[u_] * cen[u_], avg)
            y_ref[pl.ds(rows[u_], L), ps] = (cen[u_] * lax.rsqrt(var + NORM_EPS) * gmh_ref[:, ps]
                                             + skip_ref[:, ps] * xc_scr[pl.ds(rows[u_], L), ps])
        return carry

    lax.fori_loop(0, NC // group, out_group, 0)


def _mlstm(cx, mv, mo, gif_t, conv_w, conv_b, w_q_m, w_k_m, b_igate, b_fgate, g_mh, skip_m, B, S):
    def blockdiag(w, scale):
        eye = jnp.eye(MLSTM_HEADS, dtype=F32)[:, None, :, None]
        return (w[:, :, None, :] * (eye * scale)).reshape(WIDTH_C, WIDTH_C).astype(BF16)

    nc = S // MLSTM_TILE
    wq = blockdiag(w_q_m, MLSTM_DIM ** -0.5)
    wk = blockdiag(w_k_m, 1.0)
    bias8 = jnp.concatenate([b_igate, b_fgate])
    br = jnp.broadcast_to(bias8[:, None], (2 * MLSTM_HEADS, S))
    seq = lambda w: pl.BlockSpec((S, w), lambda b: (b, 0))
    full = lambda a: pl.BlockSpec(a.shape, lambda b: (0,) * a.ndim)
    row = lambda a: a.reshape(1, -1)
    args = [conv_w, row(conv_b), wq, wk, br, row(g_mh), row(skip_m)]
    tok = lambda w, dt: pltpu.VMEM((S, w), dt)
    per_chunk = pltpu.VMEM((nc, MLSTM_HEADS, LANES), F32)
    return pl.pallas_call(
        _mlstm_kernel,
        grid=(B,),
        in_specs=[seq(WIDTH_C), seq(WIDTH_C), seq(WIDTH_C),
                  pl.BlockSpec((2 * MLSTM_HEADS, S), lambda b: (0, b))] + [full(a) for a in args],
        out_specs=seq(WIDTH_C),
        out_shape=jax.ShapeDtypeStruct((B * S, WIDTH_C), F32),
        scratch_shapes=[tok(WIDTH_C, F32), tok(WIDTH_C, BF16), tok(WIDTH_C, BF16),
                        tok(WIDTH_C, F32), tok(WIDTH_C, F32), tok(WIDTH_C, F32), tok(4 * LANES, F32),
                        pltpu.VMEM((2 * MLSTM_HEADS, S), F32), per_chunk, per_chunk,
                        pltpu.VMEM((nc, MLSTM_HEADS // 2, LANES, 2 * LANES), BF16)],
        compiler_params=_params("parallel"),
        name="mlstm_mixer",
    )(cx, mv, mo, gif_t, *args)


def _mix_ffn_kernel(ya_ref, yb_ref, yc_ref, x_ref, gate1_ref, ga_ref, gb_ref, wo32_ref,
                    sh_ref, sc_ref, gate_ref, g2_ref, wgu_ref, wd32_ref, gf_ref, o_ref, wo_ref, wd_ref, *, final_norm):
    @pl.when(pl.program_id(0) == 0)
    def _():
        wo_ref[...] = wo32_ref[0].astype(BF16)
        wd_ref[...] = wd32_ref[0].astype(BF16)

    a = _rms(ya_ref[...], ga_ref[...]).astype(BF16)
    b = _rms(yb_ref[...], gb_ref[...]).astype(BF16)
    c = yc_ref[...].astype(BF16)
    mixed = (_dot(a, wo_ref[0:WIDTH_A, :]) + _dot(b, wo_ref[WIDTH_A:WIDTH_A + WIDTH_B, :])
             + _dot(c, wo_ref[WIDTH_A + WIDTH_B:, :]))
    x = x_ref[...] + gate1_ref[0] * mixed
    h = (_rms(x, g2_ref[...]) * (1.0 + sc_ref[0]) + sh_ref[0]).astype(BF16)
    acc = jnp.zeros(x.shape, F32)
    for j in range(FFN_HIDDEN // FFN_CHUNK):
        gate = _dot(h, wgu_ref[0, :, FFN_CHUNK * j:FFN_CHUNK * (j + 1)])
        up = _dot(h, wgu_ref[0, :, FFN_HIDDEN + FFN_CHUNK * j:FFN_HIDDEN + FFN_CHUNK * (j + 1)])
        act = (_silu(gate) * up).astype(BF16)
        acc = acc + _dot(act, wd_ref[FFN_CHUNK * j:FFN_CHUNK * (j + 1), :])
    y = x + gate_ref[0] * acc
    if final_norm:
        y = _rms(y, gf_ref[...])
    o_ref[...] = y


def _mix_ffn(ya, yb, yc, x2, mod3, mod_first, ga, gb, w_out, g2, wgu, w_down, layer, gf, S, final_norm):
    T = x2.shape[0]
    tm = TM_PROJ
    per_b = S // tm
    row = lambda w: pl.BlockSpec((tm, w), lambda i: (i, 0))
    full = lambda a: pl.BlockSpec(a.shape, lambda i: (0,) * a.ndim)
    slab = lambda a: pl.BlockSpec((1,) + a.shape[1:], lambda i: (layer, 0, 0))
    modspec = lambda k: _mod_spec(per_b, mod_first, k)
    return pl.pallas_call(
        functools.partial(_mix_ffn_kernel, final_norm=final_norm),
        grid=(T // tm,),
        in_specs=[row(WIDTH_A), row(WIDTH_B), row(WIDTH_C), row(D_MODEL), modspec(2), full(ga), full(gb), slab(w_out),
                  modspec(3), modspec(4), modspec(5), full(g2), slab(wgu), slab(w_down), full(gf)],
        out_specs=row(D_MODEL),
        out_shape=jax.ShapeDtypeStruct((T, D_MODEL), F32),
        scratch_shapes=[pltpu.VMEM(w_out.shape[1:], BF16), pltpu.VMEM(w_down.shape[1:], BF16)],
        compiler_params=_params("arbitrary"),
        name="mix_ffn_final" if final_norm else "mix_ffn",
    )(ya, yb, yc, x2, mod3, ga, gb, w_out, mod3, mod3, mod3, g2, wgu, w_down, gf)


def _head_tile_perm():
    idx = []
    for r in range(NSA_REP):
        idx += list(range(HEAD_DIM * r, HEAD_DIM * (r + 1)))
        idx += list(range(HEAD_DIM * (NSA_REP + r), HEAD_DIM * (NSA_REP + r + 1)))
    return np.asarray(idx, np.int32)


def _in_cols():
    std = np.full((N_STD,), -1, np.int64)
    tr = np.full((N_T,), -1, np.int64)
    off = 0
    tr[TSEG_Q:TSEG_Q + WIDTH_A] = _head_tile_perm()
    off += WIDTH_A
    std[SEG_CKV:SEG_CKV + 2 * LANES] = off + np.arange(2 * LANES)
    off += 2 * LANES
    std[SEG_SK:SEG_SK + LANES] = off + np.arange(LANES)
    off += LANES
    tr[TSEG_SV:TSEG_SV + LANES] = off + np.arange(LANES)
    off += LANES
    std[SEG_WK:SEG_WK + LANES] = off + np.arange(LANES)
    off += LANES
    tr[TSEG_WV:TSEG_WV + LANES] = off + np.arange(LANES)
    off += LANES
    tr[TSEG_G:TSEG_G + N_GATES] = off + np.arange(N_GATES)
    off += N_GATES
    std[SEG_QL:SEG_QL + MLA_Q_LORA] = off + np.arange(MLA_Q_LORA)
    off += MLA_Q_LORA
    std[SEG_KVL:SEG_KVL + MLA_KV_LORA] = off + np.arange(MLA_KV_LORA)
    off += MLA_KV_LORA
    std[SEG_KR + MLA_NOPE:SEG_KR + MLA_NOPE + MLA_ROPE] = off + np.arange(MLA_ROPE)
    off += MLA_ROPE
    for seg in (SEG_CX, SEG_MV, SEG_MO):
        std[seg:seg + WIDTH_C] = off + np.arange(WIDTH_C)
        off += WIDTH_C
    tr[TSEG_IF:TSEG_IF + 2 * MLSTM_HEADS] = off + np.arange(2 * MLSTM_HEADS)
    return std, tr


def _gather_cols(w, cols):
    pieces, start = [], 0
    for i in range(1, len(cols) + 1):
        run_ends = i == len(cols) or (cols[i] != cols[i - 1] + 1 if cols[i - 1] >= 0 else cols[i] >= 0)
        if run_ends:
            first = int(cols[start])
            pieces.append(w[:, first:first + i - start] if first >= 0 else jnp.zeros((w.shape[0], i - start), w.dtype))
            start = i
    return jnp.concatenate(pieces, axis=1)


def _layer_weights(l, w_in, w_uq, w_ukv):
    std, tr = _in_cols()
    w_std = _gather_cols(w_in[l], std).astype(BF16)
    w_t = _gather_cols(w_in[l], tr).T.astype(BF16)
    cq = np.full((MLA_HEADS * LANES,), -1, np.int64)
    ck = np.full((MLA_HEADS * LANES,), -1, np.int64)
    cv = np.zeros((WIDTH_B,), np.int64)
    dq = MLA_NOPE + MLA_ROPE
    dkv = MLA_NOPE + MLA_V
    for hd in range(MLA_HEADS):
        cq[LANES * hd:LANES * hd + dq] = dq * hd + np.arange(dq)
        ck[LANES * hd:LANES * hd + MLA_NOPE] = dkv * hd + np.arange(MLA_NOPE)
        cv[MLA_V * hd:MLA_V * (hd + 1)] = dkv * hd + MLA_NOPE + np.arange(MLA_V)
    wuqT = _gather_cols(w_uq[l], cq).T.astype(BF16)
    wkm = _gather_cols(w_ukv[l], ck).astype(BF16)
    wvmT = _gather_cols(w_ukv[l], cv).T.astype(BF16)
    return w_std, w_t, wuqT, wkm, wvmT


def kernel(x, c, positions, g_norm1, g_norm2, w_ada, b_ada, w_in, cmp_pos, w_cmp_k, w_cmp_v, g_out_a, g_q_lora, w_uq, g_kv_lora, w_ukv, g_out_b, conv_w, conv_b, w_q_m, w_k_m, b_igate, b_fgate, g_mh, skip_m, w_out, w_gu, w_down, g_final):
    B, S, D = x.shape
    T = B * S
    x2 = x.reshape(T, D)
    tabs, tabs_t = _rope_tables(positions)
    mod3 = _ada(c, w_ada, b_ada).reshape(DEPTH * B * N_MOD, 1, D)
    row = lambda v: v.reshape(1, -1)
    wgu = w_gu.astype(BF16)
    for l in range(DEPTH):
        w_std, w_t, wuqT, wkm, wvmT = _layer_weights(l, w_in, w_uq, w_ukv)
        (qaT, gT, ck, cv, sk_aug, wk, svT, wvT, qmT, km, vmT, cx, mv, mo, gif_t) = _inproj(
            x2, mod3, l * B * N_MOD, row(g_norm1[l]), w_std, w_t, tabs, tabs_t, row(g_q_lora[l]), wuqT,
            row(g_kv_lora[l]), wkm, wvmT, S)
        kc, vcT = _compress(ck, cv, cmp_pos[l], w_cmp_k[l], w_cmp_v[l], B, S)
        ya = _nsa(qaT, gT, kc, vcT, sk_aug, svT, wk, wvT, B, S)
        yb = _mla(qmT, km, vmT, B, S)
        yc = _mlstm(cx, mv, mo, gif_t, conv_w[l], conv_b[l], w_q_m[l], w_k_m[l], b_igate[l], b_fgate[l],
                    g_mh[l], skip_m[l], B, S)
        x2 = _mix_ffn(ya, yb, yc, x2, mod3, l * B * N_MOD, row(g_out_a[l]), row(g_out_b[l]), w_out, row(g_norm2[l]),
                      wgu, w_down, l, row(g_final), S, final_norm=(l == DEPTH - 1))
    return x2.reshape(B, S, D)
```

```python
import functools

import numpy as np
import jax
import jax.numpy as jnp
from jax import lax
from jax.experimental import pallas as pl
from jax.experimental.pallas import tpu as pltpu

F32 = jnp.float32
BF16 = jnp.bfloat16

D_MODEL = 1024
DEPTH = 2
HEAD_DIM = 64
ROPE_THETA = 500000.0
NSA_ROT_HALF = HEAD_DIM // 8
NORM_EPS = 1e-6

NSA_HEADS = 6
NSA_KV_HEADS = 2
NSA_REP = NSA_HEADS // NSA_KV_HEADS
CMP_LEN = 32
CMP_STRIDE = 16
SLC_LEN = 64
SLC_TOPK = 16
WINDOW = 512

MLA_HEADS = 6
MLA_Q_LORA = 256
MLA_KV_LORA = 128
MLA_NOPE = 64
MLA_ROPE = 32
MLA_V = 64

MLSTM_HEADS = 4
MLSTM_DIM = 64
MLSTM_CONV = 4

WIDTH_A = NSA_HEADS * HEAD_DIM
WIDTH_B = MLA_HEADS * MLA_V
WIDTH_C = MLSTM_HEADS * MLSTM_DIM
FFN_HIDDEN = 2816
N_GATES = 3 * NSA_HEADS
GATE_ROWS = 24

LANES = 128
NEG = -1e30
LOG2E = 1.4426950408889634
VMEM_LIMIT = 56 * 1024 * 1024

TM_PROJ = 512
TQ_NSA = 256
TK_SLC = 256
CHAIN_LANES = 256
TQ_MLA = 256
FFN_CHUNK = 256
MLSTM_TILE = 128
MLSTM_GROUP = 16
MXU_LOOKAHEAD = 3
NSA_LOOKAHEAD = 4

SEG_CKV, SEG_SK, SEG_WK, SEG_QL, SEG_KVL, SEG_KR, SEG_CX, SEG_MV, SEG_MO = (
    0, 256, 384, 512, 768, 896, 1024, 1280, 1536)
N_STD = 1792
TSEG_Q, TSEG_G, TSEG_SV, TSEG_WV, TSEG_IF = 0, 384, 416, 544, 672
TSEG_G_ROWS = 32
TSEG_IF_ROWS = 16
N_T = 688


def _params(*sem):
    return pltpu.CompilerParams(dimension_semantics=sem, vmem_limit_bytes=VMEM_LIMIT)


def _dot(a, b):
    return jnp.dot(a, b, preferred_element_type=F32)


def _dot_nt(a, b):
    return lax.dot_general(a, b, (((1,), (1,)), ((), ())), preferred_element_type=F32)


def _dot_tn(a, b):
    return lax.dot_general(a, b, (((0,), (0,)), ((), ())), preferred_element_type=F32)


def _split3(x):
    hi = x.astype(BF16)
    r1 = x - hi.astype(F32)
    mid = r1.astype(BF16)
    lo = (r1 - mid.astype(F32)).astype(BF16)
    return hi, mid, lo


def _rms(x, g):
    return x * lax.rsqrt(jnp.mean(x * x, axis=-1, keepdims=True) + NORM_EPS) * g


def _sigmoid(x):
    return 1.0 / (1.0 + jnp.exp(-x))


def _silu(x):
    return x * _sigmoid(x)


def _rope(x, cos, sin, half, x1_lane):
    xr = jnp.where(x1_lane, -pltpu.roll(x, LANES - half, 1), pltpu.roll(x, half, 1))
    return x * cos + xr * sin


def _ada_kernel(c_ref, w_ref, b_ref, o_ref):
    c = c_ref[...]
    ca = _silu(c).astype(BF16)
    o_ref[0] = _dot(ca, w_ref[0].astype(BF16)) + b_ref[0]


def _ada(c, w_ada, b_ada):
    L, D, N = w_ada.shape
    B = c.shape[0]
    tn = 1536
    return pl.pallas_call(
        _ada_kernel,
        grid=(L, N // tn),
        in_specs=[pl.BlockSpec((B, D), lambda l, j: (0, 0)),
                  pl.BlockSpec((1, D, tn), lambda l, j: (l, 0, j)),
                  pl.BlockSpec((1, 1, tn), lambda l, j: (l, 0, j))],
        out_specs=pl.BlockSpec((1, B, tn), lambda l, j: (l, 0, j)),
        out_shape=jax.ShapeDtypeStruct((L, B, N), F32),
        compiler_params=_params("parallel", "parallel"),
        name="ada_mod",
    )(c, w_ada, b_ada.reshape(L, 1, N))


ROPE_FREQS = NSA_ROT_HALF + MLA_ROPE // 2


def _rope_kernel(posr_ref, invc_ref, spread_ref, one_ref, cn_ref, sn_ref, cm_ref, sm_ref, cnt_ref, snt_ref, cmt_ref, smt_ref):
    tm = posr_ref.shape[1]
    rows_t = invc_ref.shape[0]
    ang_t = jnp.concatenate([invc_ref[...]] * (tm // LANES), axis=1) * jnp.broadcast_to(posr_ref[...], (rows_t, tm))
    c_t, s_t = jnp.cos(ang_t), jnp.sin(ang_t)
    cnt_ref[...] = c_t[:NSA_ROT_HALF]
    snt_ref[...] = s_t[:NSA_ROT_HALF]
    cmt_ref[...] = c_t[NSA_ROT_HALF:ROPE_FREQS]
    smt_ref[...] = s_t[NSA_ROT_HALF:ROPE_FREQS]

    def spread(val):
        hi, mid, lo = _split3(val)
        pieces = jnp.concatenate([hi.astype(F32), mid.astype(F32), lo.astype(F32)], axis=0).astype(BF16)
        return _dot_tn(pieces, spread_ref[...])

    c = spread(c_t) + one_ref[...]
    s = spread(s_t)
    cn_ref[...] = c[:, :LANES]
    cm_ref[...] = c[:, LANES:]
    sn_ref[...] = s[:, :LANES]
    sm_ref[...] = s[:, LANES:]


def _rope_tables(positions):
    T = positions.size
    inv_n = jnp.power(ROPE_THETA, -jnp.arange(0, 2 * NSA_ROT_HALF, 2, dtype=F32) / (2 * NSA_ROT_HALF))
    inv_m = jnp.power(ROPE_THETA, -jnp.arange(0, MLA_ROPE, 2, dtype=F32) / MLA_ROPE)
    rows_t = 4 * NSA_ROT_HALF
    onehot = np.zeros((3 * rows_t, 2 * LANES), np.float32)
    rotated = np.zeros((1, 2 * LANES), np.float32)
    for k in range(3):
        for f in range(NSA_ROT_HALF):
            for lane in (f, f + NSA_ROT_HALF, HEAD_DIM + f, HEAD_DIM + f + NSA_ROT_HALF):
                onehot[rows_t * k + f, lane] = 1.0
                rotated[0, lane] = 1.0
        for f in range(MLA_ROPE // 2):
            for lane in (MLA_NOPE + f, MLA_NOPE + MLA_ROPE // 2 + f):
                onehot[rows_t * k + NSA_ROT_HALF + f, LANES + lane] = 1.0
                rotated[0, LANES + lane] = 1.0
    inv_col = jnp.concatenate([inv_n, inv_m, jnp.zeros((rows_t - ROPE_FREQS,), F32)])
    inv_col = jnp.broadcast_to(inv_col[:, None], (rows_t, LANES))
    tm = 2048
    spec = pl.BlockSpec((tm, LANES), lambda i: (i, 0))
    col = lambda h: pl.BlockSpec((h, tm), lambda i: (0, i))
    full = lambda shape: pl.BlockSpec(shape, lambda i: (0, 0))
    sds = jax.ShapeDtypeStruct
    outs = pl.pallas_call(
        _rope_kernel,
        grid=(T // tm,),
        in_specs=[col(1), full((rows_t, LANES)), full((3 * rows_t, 2 * LANES)), full((1, 2 * LANES))],
        out_specs=[spec] * 4 + [col(NSA_ROT_HALF)] * 2 + [col(MLA_ROPE // 2)] * 2,
        out_shape=[sds((T, LANES), F32)] * 4 + [sds((NSA_ROT_HALF, T), F32)] * 2 + [sds((MLA_ROPE // 2, T), F32)] * 2,
        compiler_params=_params("parallel"),
        name="rope_tables",
    )(positions.reshape(1, T).astype(F32), inv_col, jnp.asarray(onehot, BF16), jnp.asarray(1.0 - rotated))
    return outs[:4], outs[4:]


def _rope_rows(t, offset, half, cos, sin):
    x1, x2 = t[offset:offset + half], t[offset + half:offset + 2 * half]
    return x1 * cos - x2 * sin, x2 * cos + x1 * sin


def _inproj_kernel(x_ref, sh_ref, sc_ref, g1_ref, ws_ref, wt_ref, cn_ref, sn_ref, cm_ref, sm_ref,
                   cnT_ref, snT_ref, cmT_ref, smT_ref, gq_ref, wuqT_ref, gkv_ref, wkm_ref, wvmT_ref,
                   qaT_ref, gT_ref, ck_ref, cv_ref, ska_ref, wk_ref, svT_ref, wvT_ref,
                   qmT_ref, km_ref, vmT_ref, cx_ref, mv_ref, mo_ref, gif_ref, *, per_b):
    tm = x_ref.shape[0]
    x = x_ref[...]
    h = _rms(x, g1_ref[...]) * (1.0 + sc_ref[0]) + sh_ref[0]
    hb = h.astype(BF16)

    def seg(start, width):
        return _dot(hb, ws_ref[:, start:start + width])

    lane = lax.broadcasted_iota(jnp.int32, (1, LANES), 1)
    x1_n = (lane % HEAD_DIM) < NSA_ROT_HALF
    x1_m = lane < MLA_NOPE + MLA_ROPE // 2
    cn, sn, cm, sm = cn_ref[...], sn_ref[...], cm_ref[...], sm_ref[...]
    rope_n = lambda t: _rope(t, cn, sn, NSA_ROT_HALF, x1_n)
    rope_m = lambda t: _rope(t, cm, sm, MLA_ROPE // 2, x1_m)

    qn = _rms(seg(SEG_QL, MLA_Q_LORA), gq_ref[...]).astype(BF16)
    kvl_kr = seg(SEG_KVL, 2 * LANES)
    kvn = _rms(kvl_kr[:, :LANES], gkv_ref[...]).astype(BF16)
    kr = rope_m(kvl_kr[:, LANES:])

    out_t = _dot_nt(wt_ref[...], hb)
    seg_t = lambda start, height: out_t[start:start + height]

    qt = seg_t(TSEG_Q, WIDTH_A)
    cnt, snt = cnT_ref[...], snT_ref[...]
    parts = []
    for hd in range(NSA_HEADS):
        o = HEAD_DIM * hd
        parts += list(_rope_rows(qt, o, NSA_ROT_HALF, cnt, snt)) + [qt[o + 2 * NSA_ROT_HALF:o + HEAD_DIM]]
    qaT_ref[...] = (jnp.concatenate(parts, axis=0) * (HEAD_DIM ** -0.5 * LOG2E)).astype(BF16)
    gT_ref[...] = _sigmoid(seg_t(TSEG_G, TSEG_G_ROWS))[:GATE_ROWS]
    svT_ref[...] = seg_t(TSEG_SV, LANES).astype(BF16)
    wvT_ref[...] = seg_t(TSEG_WV, LANES).astype(BF16)
    gif_ref[...] = seg_t(TSEG_IF, TSEG_IF_ROWS)[:2 * MLSTM_HEADS]

    ckv = seg(SEG_CKV, 2 * LANES)
    ck_ref[...] = rope_n(ckv[:, :LANES])
    cv_ref[...] = ckv[:, LANES:]
    ska_ref[:, :LANES] = rope_n(seg(SEG_SK, LANES)).astype(BF16)
    srow = (pl.program_id(0) % per_b) * tm + lax.broadcasted_iota(jnp.int32, (tm, LANES), 0)
    lane2 = lax.broadcasted_iota(jnp.int32, (tm, LANES), 1)
    code = (lane2 < 4 * SLC_TOPK) & ((lane2 & (2 * SLC_TOPK - 1)) == srow // SLC_LEN)
    ska_ref[:, LANES:] = jnp.where(code, 1.0, 0.0).astype(BF16)
    wk_ref[...] = rope_n(seg(SEG_WK, LANES)).astype(BF16)

    qmt = _dot_nt(wuqT_ref[...], qn)
    kk = _dot(kvn, wkm_ref[...])
    vmt = _dot_nt(wvmT_ref[...], kvn)
    cx_ref[...] = seg(SEG_CX, WIDTH_C)
    mv_ref[...] = seg(SEG_MV, WIDTH_C).astype(BF16)
    mo_ref[...] = seg(SEG_MO, WIDTH_C)
    cmt, smt = cmT_ref[...], smT_ref[...]
    parts = []
    for hd in range(MLA_HEADS):
        o = LANES * hd
        parts += [qmt[o:o + MLA_NOPE]] + list(_rope_rows(qmt, o + MLA_NOPE, MLA_ROPE // 2, cmt, smt))
        parts += [qmt[o + MLA_NOPE + MLA_ROPE:o + LANES]]
    qmT_ref[...] = (jnp.concatenate(parts, axis=0) * ((MLA_NOPE + MLA_ROPE) ** -0.5 * LOG2E)).astype(BF16)
    for hd in range(MLA_HEADS):
        km_ref[:, LANES * hd:LANES * (hd + 1)] = (kk[:, LANES * hd:LANES * (hd + 1)] + kr).astype(BF16)
    vmT_ref[...] = vmt.astype(BF16)


N_MOD = 6


def _mod_spec(per_b, first, k):
    return pl.BlockSpec((1, 1, D_MODEL), lambda i: (first + (i // per_b) * N_MOD + k, 0, 0))


def _inproj(x2, mod3, mod_first, g1, w_std, w_t, tabs, tabs_t, gq, wuqT, gkv, wkm, wvmT, S):
    T = x2.shape[0]
    tm = TM_PROJ
    per_b = S // tm
    row = lambda w: pl.BlockSpec((tm, w), lambda i: (i, 0))
    col = lambda h: pl.BlockSpec((h, tm), lambda i: (0, i))
    full = lambda a: pl.BlockSpec(a.shape, lambda i: (0,) * a.ndim)
    outs = [(WIDTH_A, BF16, True), (GATE_ROWS, F32, True), (LANES, F32, False), (LANES, F32, False), (2 * LANES, BF16, False),
            (LANES, BF16, False), (LANES, BF16, True), (LANES, BF16, True),
            (MLA_HEADS * LANES, BF16, True), (MLA_HEADS * LANES, BF16, False), (WIDTH_B, BF16, True),
            (WIDTH_C, F32, False), (WIDTH_C, BF16, False), (WIDTH_C, F32, False), (2 * MLSTM_HEADS, F32, True)]
    return pl.pallas_call(
        functools.partial(_inproj_kernel, per_b=per_b),
        grid=(T // tm,),
        in_specs=[row(D_MODEL), _mod_spec(per_b, mod_first, 0), _mod_spec(per_b, mod_first, 1),
                  full(g1), full(w_std), full(w_t)] + [row(LANES)] * 4
                 + [col(t.shape[0]) for t in tabs_t] + [full(gq), full(wuqT), full(gkv), full(wkm), full(wvmT)],
        out_specs=[col(w) if tr else row(w) for w, _, tr in outs],
        out_shape=[jax.ShapeDtypeStruct((w, T) if tr else (T, w), dt) for w, dt, tr in outs],
        compiler_params=_params("parallel"),
        name="in_proj",
    )(x2, mod3, mod3, g1, w_std, w_t, *tabs, *tabs_t, gq, wuqT, gkv, wkm, wvmT)


def _compress_kernel(xk_ref, xv_ref, wk_ref, wv_ref, pos_ref, wkf_ref, wvf_ref, kc_ref, vcT_ref):
    ng = xk_ref.shape[0] // CMP_STRIDE
    row = lax.broadcasted_iota(jnp.int32, (ng, LANES), 0)
    pos = pos_ref[...].astype(BF16)
    acc_k = jnp.zeros((ng, 2 * LANES), F32)
    acc_v = jnp.zeros((ng, 2 * LANES), F32)
    for t in range(CMP_STRIDE):
        tok = pl.ds(t, ng, stride=CMP_STRIDE)
        acc_k = acc_k + _dot(xk_ref[tok, :].astype(BF16), wk_ref[t])
        acc_v = acc_v + _dot(xv_ref[tok, :].astype(BF16), wv_ref[t])

    def finish(acc, wf_ref):
        const = _dot(pos, wf_ref[...].astype(BF16))
        both = acc[:, :LANES] + pltpu.roll(acc[:, LANES:], ng - 1, 0) + jnp.concatenate([const, const], axis=1)
        return jnp.where(row < ng - 1, both, 0.0)

    kc_ref[0] = finish(acc_k, wkf_ref).astype(BF16)
    vcT_ref[0] = finish(acc_v, wvf_ref).T.astype(BF16)


def _compress(ck, cv, cmp_pos, w_cmp_k, w_cmp_v, B, S):
    ng = S // CMP_STRIDE

    def per_token(w):
        a = w[:CMP_STRIDE * HEAD_DIM].reshape(CMP_STRIDE, HEAD_DIM, HEAD_DIM)
        b = w[CMP_STRIDE * HEAD_DIM:].reshape(CMP_STRIDE, HEAD_DIM, HEAD_DIM)
        z = jnp.zeros_like(a)
        top = jnp.concatenate([a, z, b, z], axis=2)
        bot = jnp.concatenate([z, a, z, b], axis=2)
        return jnp.concatenate([top, bot], axis=1).astype(BF16)

    full = lambda a: pl.BlockSpec(a.shape, lambda b: (0,) * a.ndim)
    wk3, wv3 = per_token(w_cmp_k), per_token(w_cmp_v)
    posf = cmp_pos.reshape(1, CMP_LEN * HEAD_DIM)
    ospec = pl.BlockSpec((1, ng, LANES), lambda b: (b, 0, 0))
    return pl.pallas_call(
        _compress_kernel,
        grid=(B,),
        in_specs=[pl.BlockSpec((S, LANES), lambda b: (b, 0)), pl.BlockSpec((S, LANES), lambda b: (b, 0)),
                  full(wk3), full(wv3), full(posf), full(w_cmp_k), full(w_cmp_v)],
        out_specs=[ospec, ospec],
        out_shape=[jax.ShapeDtypeStruct((B, ng, LANES), BF16)] * 2,
        compiler_params=_params("parallel"),
        name="nsa_compress",
    )(ck, cv, wk3, wv3, posf, w_cmp_k, w_cmp_v)


def _nsa_kernel(qT_ref, gT_ref, kc_ref, vcT_ref, sk_ref, svT_ref, wk_ref, wvT_ref, o_ref, qaug_scr, acc_scr, s_scr):
    tq = TQ_NSA
    cols = NSA_HEADS * tq
    pair = CHAIN_LANES
    t0 = pl.program_id(1) * tq
    n_slc = SLC_TOPK * 2

    frow = lax.broadcasted_iota(jnp.int32, (LANES, tq), 0)
    g0_row = frow < HEAD_DIM
    tiles = [qT_ref[LANES * r:LANES * (r + 1), :] for r in range(NSA_REP)]
    zero = jnp.zeros_like(tiles[0])
    q6 = jnp.concatenate([jnp.where(g0_row, t, zero) for t in tiles]
                         + [jnp.where(g0_row, zero, t) for t in tiles], axis=1)
    qaug_scr[0:LANES, :] = q6
    tq_l = t0 + (lax.broadcasted_iota(jnp.int32, (1, cols), 1) & (tq - 1))

    pairs = [slice(pair * pp, pair * (pp + 1)) for pp in range(cols // pair)]
    vrows = lambda pp: slice(HEAD_DIM * (pp // NSA_REP), HEAD_DIM * (pp // NSA_REP + 1))
    s = _dot(kc_ref[0], q6)
    span = WINDOW + tq
    start = pl.multiple_of(jnp.maximum(t0 - WINDOW, 0), tq)
    kw = wk_ref[pl.ds(start, span), :]
    win_scores = [_dot(kw, q6[:, sl]) for sl in pairs]

    tq_1 = tq_l[:, :tq]
    nrow = lax.broadcasted_iota(jnp.int32, (LANES, tq), 0)
    cmp_bias = jnp.where(nrow * CMP_STRIDE + (CMP_LEN - 1) <= tq_1, 0.0, NEG)
    s = s + jnp.concatenate([cmp_bias] * NSA_HEADS, axis=1)
    e = jnp.exp2(s - jnp.max(s, axis=0, keepdims=True))
    seen = jnp.where(tq_l >= CMP_LEN - 1, 1.0, 0.0)
    p = e * (seen / jnp.sum(e, axis=0, keepdims=True))
    p_b = p.astype(BF16)
    o_cmp = [_dot(vcT_ref[0, vrows(pp), :], p_b[:, sl]) for pp, sl in enumerate(pairs)]

    jr = lax.broadcasted_iota(jnp.int32, (n_slc, LANES), 0)
    nc = lax.broadcasted_iota(jnp.int32, (n_slc, LANES), 1)
    ovl = ((nc * CMP_STRIDE < jr * SLC_LEN + SLC_LEN) & (nc * CMP_STRIDE + CMP_LEN > jr * SLC_LEN)
           & (nc < LANES - 1))
    ovl = jnp.where(ovl, 1.0, 0.0).astype(BF16)
    jq = lax.broadcasted_iota(jnp.int32, (n_slc, tq), 0)
    tl = t0 + lax.broadcasted_iota(jnp.int32, (n_slc, tq), 1)
    cur = tl // SLC_LEN
    forced = (jq == 0) | (jq == cur) | (jq == cur - 1)
    future = jq * SLC_LEN > tl
    bias_t = []
    for g in range(NSA_KV_HEADS):
        pg = p[:, (3 * g) * tq:(3 * g + 1) * tq] + p[:, (3 * g + 1) * tq:(3 * g + 2) * tq] + p[:, (3 * g + 2) * tq:(3 * g + 3) * tq]
        imp = sum(_dot(ovl, part) for part in _split3(pg))
        imp = jnp.where(forced, jnp.inf, imp)
        imp = jnp.where(future, -jnp.inf, imp)
        rows8 = [imp[8 * r:8 * (r + 1)] for r in range(n_slc // 8)]
        sub = lax.broadcasted_iota(jnp.int32, (8, tq), 0)
        ranks = [jnp.zeros((8, tq), F32) for _ in rows8]
        for jp in range(n_slc):
            rv = imp[jp:jp + 1, :]
            for r, blk in enumerate(rows8):
                if 8 * r > jp:
                    ahead = rv >= blk
                elif 8 * r + 7 < jp:
                    ahead = rv > blk
                else:
                    ahead = (rv > blk) | ((rv == blk) & (sub > jp - 8 * r))
                ranks[r] = ranks[r] + jnp.where(ahead, 1.0, 0.0)
        rank = jnp.concatenate(ranks, axis=0)
        bias_t.append(jnp.where(rank < float(SLC_TOPK), 0.0, NEG).astype(BF16))
    zb = jnp.zeros((n_slc, NSA_REP * tq), BF16)
    qaug_scr[LANES:LANES + n_slc, :] = jnp.concatenate([bias_t[0]] * NSA_REP + [zb], axis=1)
    qaug_scr[LANES + n_slc:LANES + 2 * n_slc, :] = jnp.concatenate([zb] + [bias_t[1]] * NSA_REP, axis=1)
    qaug_scr[LANES + 2 * n_slc:, :] = jnp.zeros((LANES - 2 * n_slc, cols), BF16)

    vwt = wvT_ref[:, pl.ds(start, span)]
    wrow = start + lax.broadcasted_iota(jnp.int32, (span, pair), 0)
    o_win = []
    for pp, (sl, sc) in enumerate(zip(pairs, win_scores)):
        in_window = lax.bitcast_convert_type(tq_l[:, sl] - wrow, jnp.uint32) < jnp.uint32(WINDOW)
        sc = jnp.where(in_window, sc, NEG)
        ew = jnp.exp2(sc - jnp.max(sc, axis=0, keepdims=True))
        o_win.append(_dot(vwt[vrows(pp)], ew.astype(BF16)) / jnp.sum(ew, axis=0, keepdims=True))

    acc_scr[...] = jnp.zeros((HEAD_DIM, cols), F32)
    krow = lax.broadcasted_iota(jnp.int32, (TK_SLC, pair), 0)

    def slc_scores(k0, sl):
        return _dot(sk_ref[pl.ds(k0, TK_SLC), :], qaug_scr[:, sl])

    nch = len(pairs)
    la = s_scr.shape[0]
    for pp in range(la):
        s_scr[pp] = slc_scores(0, pairs[pp])

    def slc_tile(k0, m, l, masked):
        vt = svT_ref[:, pl.ds(k0, TK_SLC)]
        ms, ls, accs = [], [], []
        ahead = [s_scr[pp] for pp in range(la)]
        for pp, sl in enumerate(pairs):
            sc = ahead.pop(0)
            nxt = pp + la
            if nxt < nch:
                ahead.append(slc_scores(k0, pairs[nxt]))
            elif not masked:
                s_scr[nxt - nch] = slc_scores(pl.multiple_of(k0 + TK_SLC, TK_SLC), pairs[nxt - nch])
            if masked:
                sc = jnp.where(k0 + krow <= tq_l[:, sl], sc, NEG)
            m_new = jnp.maximum(m[:, sl], jnp.max(sc, axis=0, keepdims=True))
            alpha = jnp.exp2(m[:, sl] - m_new)
            pe = jnp.exp2(sc - m_new)
            ls.append(alpha * l[:, sl] + jnp.sum(pe, axis=0, keepdims=True))
            accs.append(alpha * acc_scr[:, sl] + _dot(vt[vrows(pp)], pe.astype(BF16)))
            ms.append(m_new)
        acc_scr[...] = jnp.concatenate(accs, axis=1)
        return jnp.concatenate(ms, axis=1), jnp.concatenate(ls, axis=1)

    def slc_step(kt, carry):
        return slc_tile(pl.multiple_of(kt * TK_SLC, TK_SLC), carry[0], carry[1], False)

    n_full = t0 // TK_SLC
    m, l = lax.fori_loop(0, n_full, slc_step, (jnp.full((1, cols), NEG, F32), jnp.zeros((1, cols), F32)))
    m, l = slc_tile(pl.multiple_of(n_full * TK_SLC, TK_SLC), m, l, True)
    o_slc = acc_scr[...] / l

    gt = gT_ref[...]
    mixed = []
    for hd, sl in enumerate(pairs):
        gate = lambda j: gt[3 * hd + j:3 * hd + j + 1, :]
        mixed.append(gate(0) * o_cmp[hd] + gate(1) * o_slc[:, sl] + gate(2) * o_win[hd])
    for t in range(NSA_HEADS // 2):
        o_ref[:, LANES * t:LANES * (t + 1)] = jnp.concatenate([mixed[2 * t], mixed[2 * t + 1]], axis=0).T


def _nsa(qaT, gT, kc, vcT, sk_aug, svT, wk, wvT, B, S):
    tq = TQ_NSA
    assert CHAIN_LANES == tq, "the kernel treats one softmax chain as one head"
    nq = S // tq
    cols = NSA_HEADS * tq
    qcol =lambda h: pl.BlockSpec((h, tq), lambda b, i: (0, b * nq + i))
    seq = lambda w: pl.BlockSpec((S, w), lambda b, i: (b, 0))
    seqT = pl.BlockSpec((LANES, S), lambda b, i: (0, b))
    cspec = pl.BlockSpec((1, LANES, LANES), lambda b, i: (b, 0, 0))
    return pl.pallas_call(
        _nsa_kernel,
        grid=(B, nq),
        in_specs=[qcol(WIDTH_A), qcol(gT.shape[0]), cspec, cspec, seq(2 * LANES), seqT, seq(LANES), seqT],
        out_specs=pl.BlockSpec((tq, WIDTH_A), lambda b, i: (b * nq + i, 0)),
        out_shape=jax.ShapeDtypeStruct((B * S, WIDTH_A), F32),
        scratch_shapes=[pltpu.VMEM((2 * LANES, cols), BF16), pltpu.VMEM((HEAD_DIM, cols), F32),
                        pltpu.VMEM((NSA_LOOKAHEAD, TK_SLC, CHAIN_LANES), F32)],
        compiler_params=_params("parallel", "arbitrary"),
        name="nsa_attention",
    )(qaT, gT, kc, vcT, sk_aug, svT, wk, wvT)


def _mla_kernel(qT_ref, k_ref, vT_ref, o_ref, acc_scr, s_scr):
    tq = TQ_MLA
    t0 = pl.program_id(1) * tq
    tq_l = t0 + lax.broadcasted_iota(jnp.int32, (1, tq), 1)
    krow = lax.broadcasted_iota(jnp.int32, (tq, tq), 0)
    acc_scr[...] = jnp.zeros((MLA_HEADS, MLA_V, tq), F32)

    def scores(k0, hd):
        k = k_ref[pl.ds(k0, tq), LANES * hd:LANES * (hd + 1)]
        return _dot(k, qT_ref[LANES * hd:LANES * (hd + 1), :])

    for hd in range(MXU_LOOKAHEAD):
        s_scr[hd] = scores(0, hd)

    def tile(k0, ms, ls, masked):
        new_m, new_l, accs = [], [], []
        ahead = [s_scr[hd] for hd in range(MXU_LOOKAHEAD)]
        for hd in range(MLA_HEADS):
            sc = ahead.pop(0)
            nxt = hd + MXU_LOOKAHEAD
            if nxt < MLA_HEADS:
                ahead.append(scores(k0, nxt))
            elif not masked:
                s_scr[nxt - MLA_HEADS] = scores(pl.multiple_of(k0 + tq, tq), nxt - MLA_HEADS)
            if masked:
                sc = jnp.where(k0 + krow <= tq_l, sc, NEG)
            m_new = jnp.maximum(ms[hd], jnp.max(sc, axis=0, keepdims=True))
            alpha = jnp.exp2(ms[hd] - m_new)
            pe = jnp.exp2(sc - m_new)
            new_l.append(alpha * ls[hd] + jnp.sum(pe, axis=0, keepdims=True))
            vt = vT_ref[MLA_V * hd:MLA_V * (hd + 1), pl.ds(k0, tq)]
            accs.append(alpha * acc_scr[hd] + _dot(vt, pe.astype(BF16)))
            new_m.append(m_new)
        acc_scr[...] = jnp.stack(accs)
        return tuple(new_m), tuple(new_l)

    def step(kt, carry):
        return tile(pl.multiple_of(kt * tq, tq), carry[0], carry[1], False)

    n_full = pl.program_id(1)
    init = (tuple(jnp.full((1, tq), NEG, F32) for _ in range(MLA_HEADS)),
            tuple(jnp.zeros((1, tq), F32) for _ in range(MLA_HEADS)))
    ms, ls = lax.fori_loop(0, n_full, step, init)
    ms, ls = tile(pl.multiple_of(n_full * tq, tq), ms, ls, True)
    for pr in range(MLA_HEADS // 2):
        t = jnp.concatenate([acc_scr[2 * pr] / ls[2 * pr], acc_scr[2 * pr + 1] / ls[2 * pr + 1]], axis=0)
        o_ref[:, LANES * pr:LANES * (pr + 1)] = t.T


def _mla(qmT, km, vmT, B, S):
    tq = TQ_MLA
    nq = S // tq
    return pl.pallas_call(
        _mla_kernel,
        grid=(B, nq),
        in_specs=[pl.BlockSpec((MLA_HEADS * LANES, tq), lambda b, i: (0, b * nq + i)),
                  pl.BlockSpec((S, MLA_HEADS * LANES), lambda b, i: (b, 0)),
                  pl.BlockSpec((WIDTH_B, S), lambda b, i: (0, b))],
        out_specs=pl.BlockSpec((tq, WIDTH_B), lambda b, i: (b * nq + i, 0)),
        out_shape=jax.ShapeDtypeStruct((B * S, WIDTH_B), F32),
        scratch_shapes=[pltpu.VMEM((MLA_HEADS, MLA_V, tq), F32), pltpu.VMEM((MXU_LOOKAHEAD, tq, tq), F32)],
        compiler_params=_params("parallel", "arbitrary"),
        name="mla_attention",
    )(qmT, km, vmT)


def _log_sigmoid(z):
    return jnp.minimum(z, 0.0) - jnp.log1p(jnp.exp(-jnp.abs(z)))


def _mlstm_kernel(cx_ref, v_ref, o_ref, g_ref, cw_ref, cb_ref, wq_ref, wk_ref, br_ref, gmh_ref, skip_ref, y_ref,
                  xc_scr, q_scr, k_scr, w_scr, inter_scr, floor_scr, mfull_scr, ut_scr, eo_scr, el_scr, cprev_scr):
    S = cx_ref.shape[0]
    L = MLSTM_TILE
    NC = S // L
    d = MLSTM_DIM
    pairs = MLSTM_HEADS // 2
    group = MLSTM_GROUP

    x = cx_ref[...]
    rowi = lax.broadcasted_iota(jnp.int32, (S, WIDTH_C), 0)
    conv = x * cw_ref[MLSTM_CONV - 1:MLSTM_CONV, :]
    for back in range(1, MLSTM_CONV):
        shifted = jnp.where(rowi >= back, pltpu.roll(x, back, 0), 0.0)
        conv = conv + shifted * cw_ref[MLSTM_CONV - 1 - back:MLSTM_CONV - back, :]
    xc = _silu(conv + cb_ref[...])
    xc_scr[...] = xc
    xcb = xc.astype(BF16)
    q_scr[...] = _dot(xcb, wq_ref[...]).astype(BF16)
    k_scr[...] = _dot(xcb, wk_ref[...]).astype(BF16)

    nh = MLSTM_HEADS
    lane_in_chunk = lax.broadcasted_iota(jnp.int32, (nh, S), 1) & (L - 1)

    def scan_lanes(val, op, fill):
        sh = 1
        while sh < L:
            val = op(val, jnp.where(lane_in_chunk >= sh, pltpu.roll(val, sh, 1), fill))
            sh *= 2
        return val

    gt = g_ref[...] + br_ref[...]
    ig = gt[0:nh]
    b = scan_lanes(_log_sigmoid(gt[nh:]), jnp.add, 0.0)
    u = ig - b
    cmu = scan_lanes(u, jnp.maximum, -jnp.inf)
    ut_scr[0:nh, :] = u

    m = jnp.zeros((nh, 1), F32)
    w_loc, inter, floor, m_tok = [], [], [], []
    for c in range(NC):
        blk = slice(L * c, L * (c + 1))
        b_last, u_max = b[:, L * (c + 1) - 1:L * (c + 1)], cmu[:, L * (c + 1) - 1:L * (c + 1)]
        m_top = jnp.maximum(m, u_max)
        eo_scr[c] = jnp.broadcast_to(jnp.exp(m - m_top), (nh, LANES))
        el_scr[c] = jnp.broadcast_to(jnp.exp(u_max - m_top), (nh, LANES))
        mt = jnp.maximum(cmu[:, blk], m)
        m_tok.append(mt)
        inter.append(jnp.exp(m - mt))
        w_loc.append(jnp.exp(u[:, blk] - u_max))
        floor.append(jnp.exp(-(b[:, blk] + mt)))
        m = b_last + m_top

    hrow = lax.broadcasted_iota(jnp.int32, (4 * nh, 2 * LANES), 0)
    hcol = lax.broadcasted_iota(jnp.int32, (4 * nh, 2 * LANES), 1)
    to_heads = jnp.where((hrow < 3 * nh) & (hcol // d == hrow % nh), 1.0, 0.0).astype(BF16)
    frow = lax.broadcasted_iota(jnp.int32, (4 * nh, 4 * LANES), 0)
    fcol = lax.broadcasted_iota(jnp.int32, (4 * nh, 4 * LANES), 1)
    to_full = jnp.where((frow < 3 * nh) & (fcol // LANES == frow % nh), 1.0, 0.0).astype(BF16)

    def spread(chunks, onehot):
        val = jnp.concatenate(chunks, axis=1)
        hi, mid, lo = _split3(val)
        pieces = jnp.concatenate([hi.astype(F32), mid.astype(F32), lo.astype(F32), jnp.zeros_like(val)], axis=0)
        return _dot_tn(pieces.astype(BF16), onehot)

    w_scr[...] = spread(w_loc, to_heads)
    inter_scr[...] = spread(inter, to_heads)
    floor_scr[...] = spread(floor, to_heads)
    mfull_scr[...] = spread(m_tok, to_full)

    arow = lax.broadcasted_iota(jnp.int32, (LANES, 2 * LANES), 0)
    acol = lax.broadcasted_iota(jnp.int32, (LANES, 2 * LANES), 1)
    blk2 = (arow // d) == ((acol & (LANES - 1)) // d)
    ones_v = jnp.ones((L, LANES), BF16)

    def head_rows(ref, c, pr):
        top = jnp.broadcast_to(ref[c, 2 * pr:2 * pr + 1, :], (d, LANES))
        bot = jnp.broadcast_to(ref[c, 2 * pr + 1:2 * pr + 2, :], (d, LANES))
        half = jnp.concatenate([top, bot], axis=0)
        return jnp.concatenate([half, half], axis=1)

    def state_group(g, carry):
        local = []
        for cc in range(group):
            r0 = pl.multiple_of((g * group + cc) * L, L)
            for pr in range(pairs):
                ps = slice(LANES * pr, LANES * (pr + 1))
                kw = (k_scr[pl.ds(r0, L), ps].astype(F32) * w_scr[pl.ds(r0, L), ps]).astype(BF16)
                vo = jnp.concatenate([v_ref[pl.ds(r0, L), ps], ones_v], axis=1)
                local.append(jnp.where(blk2, _dot_tn(kw, vo), 0.0))
        state = list(carry)
        for cc in range(group):
            c = g * group + cc
            for pr in range(pairs):
                cprev_scr[c, pr] = state[pr].astype(BF16)
                state[pr] = head_rows(eo_scr, c, pr) * state[pr] + head_rows(el_scr, c, pr) * local[cc * pairs + pr]
        return tuple(state)

    lax.fori_loop(0, NC // group, state_group, tuple(jnp.zeros((LANES, 2 * LANES), F32) for _ in range(pairs)))

    li = lax.broadcasted_iota(jnp.int32, (L, L), 0)
    si = lax.broadcasted_iota(jnp.int32, (L, L), 1)
    causal = si <= li
    lane = lax.broadcasted_iota(jnp.int32, (L, LANES), 1)
    h0_lane = lane < d
    avg = jnp.where((li // d) == (si // d), 1.0 / d, 0.0).astype(BF16)

    def dot2(val, rhs):
        hi = val.astype(BF16)
        return _dot(hi, rhs) + _dot((val - hi.astype(F32)).astype(BF16), rhs)

    def out_group(g, carry):
        units = [(cc, pr) for cc in range(group) for pr in range(pairs)]
        chunk_of = {u_: g * group + u_[0] for u_ in units}
        rows = {u_: pl.multiple_of(chunk_of[u_] * L, L) for u_ in units}
        qc, sc, pv = {}, {}, {}
        for u_ in units:
            c, pr = chunk_of[u_], u_[1]
            ps = slice(LANES * pr, LANES * (pr + 1))
            qp = q_scr[pl.ds(rows[u_], L), ps]
            kp = k_scr[pl.ds(rows[u_], L), ps]
            qc[u_] = _dot(qp, cprev_scr[c, pr])
            sc[u_] = [_dot_nt(jnp.where(h0_lane if hh == 0 else lane >= d, qp, jnp.zeros_like(qp)), kp)
                      for hh in range(2)]
        for u_ in units:
            pr = u_[1]
            ps = slice(LANES * pr, LANES * (pr + 1))
            vo = jnp.concatenate([v_ref[pl.ds(rows[u_], L), ps], ones_v], axis=1)
            pv[u_] = []
            for hh in range(2):
                hd = 2 * pr + hh
                u_row = ut_scr[hd:hd + 1, pl.ds(rows[u_], L)]
                decay = jnp.where(causal, jnp.exp(u_row - mfull_scr[pl.ds(rows[u_], L), LANES * hd:LANES * (hd + 1)]), 0.0)
                pv[u_].append(_dot((sc[u_][hh] * decay).astype(BF16), vo))
        hg, cen = {}, {}
        for u_ in units:
            pr = u_[1]
            ps = slice(LANES * pr, LANES * (pr + 1))
            it = inter_scr[pl.ds(rows[u_], L), ps]
            num = it * qc[u_][:, :LANES] + jnp.where(h0_lane, pv[u_][0][:, :LANES], pv[u_][1][:, :LANES])
            den = it * qc[u_][:, LANES:] + jnp.where(h0_lane, pv[u_][0][:, LANES:], pv[u_][1][:, LANES:])
            hcell = num / jnp.maximum(jnp.abs(den), floor_scr[pl.ds(rows[u_], L), ps])
            hg[u_] = _sigmoid(o_ref[pl.ds(rows[u_], L), ps]) * hcell
        for u_ in units:
            cen[u_] = hg[u_] - dot2(hg[u_], avg)
        for u_ in units:
            pr = u_[1]
            ps = slice(LANES * pr, LANES * (pr + 1))
            var = dot2(cen[u_] * cen[u_], avg)
            y_ref[pl.ds(rows[u_], L), ps] = (cen[u_] * lax.rsqrt(var + NORM_EPS) * gmh_ref[:, ps]
                                             + skip_ref[:, ps] * xc_scr[pl.ds(rows[u_], L), ps])
        return carry

    lax.fori_loop(0, NC // group, out_group, 0)


def _mlstm(cx, mv, mo, gif_t, conv_w, conv_b, w_q_m, w_k_m, b_igate, b_fgate, g_mh, skip_m, B, S):
    def blockdiag(w, scale):
        eye = jnp.eye(MLSTM_HEADS, dtype=F32)[:, None, :, None]
        return (w[:, :, None, :] * (eye * scale)).reshape(WIDTH_C, WIDTH_C).astype(BF16)

    nc = S // MLSTM_TILE
    wq = blockdiag(w_q_m, MLSTM_DIM ** -0.5)
    wk = blockdiag(w_k_m, 1.0)
    bias8 = jnp.concatenate([b_igate, b_fgate])
    br = jnp.broadcast_to(bias8[:, None], (2 * MLSTM_HEADS, S))
    seq = lambda w: pl.BlockSpec((S, w), lambda b: (b, 0))
    full = lambda a: pl.BlockSpec(a.shape, lambda b: (0,) * a.ndim)
    row = lambda a: a.reshape(1, -1)
    args = [conv_w, row(conv_b), wq, wk, br, row(g_mh), row(skip_m)]
    tok = lambda w, dt: pltpu.VMEM((S, w), dt)
    per_chunk = pltpu.VMEM((nc, MLSTM_HEADS, LANES), F32)
    return pl.pallas_call(
        _mlstm_kernel,
        grid=(B,),
        in_specs=[seq(WIDTH_C), seq(WIDTH_C), seq(WIDTH_C),
                  pl.BlockSpec((2 * MLSTM_HEADS, S), lambda b: (0, b))] + [full(a) for a in args],
        out_specs=seq(WIDTH_C),
        out_shape=jax.ShapeDtypeStruct((B * S, WIDTH_C), F32),
        scratch_shapes=[tok(WIDTH_C, F32), tok(WIDTH_C, BF16), tok(WIDTH_C, BF16),
                        tok(WIDTH_C, F32), tok(WIDTH_C, F32), tok(WIDTH_C, F32), tok(4 * LANES, F32),
                        pltpu.VMEM((2 * MLSTM_HEADS, S), F32), per_chunk, per_chunk,
                        pltpu.VMEM((nc, MLSTM_HEADS // 2, LANES, 2 * LANES), BF16)],
        compiler_params=_params("parallel"),
        name="mlstm_mixer",
    )(cx, mv, mo, gif_t, *args)


def _mix_ffn_kernel(ya_ref, yb_ref, yc_ref, x_ref, gate1_ref, ga_ref, gb_ref, wo32_ref,
                    sh_ref, sc_ref, gate_ref, g2_ref, wgu_ref, wd32_ref, gf_ref, o_ref, wo_ref, wd_ref, *, final_norm):
    @pl.when(pl.program_id(0) == 0)
    def _():
        wo_ref[...] = wo32_ref[0].astype(BF16)
        wd_ref[...] = wd32_ref[0].astype(BF16)

    a = _rms(ya_ref[...], ga_ref[...]).astype(BF16)
    b = _rms(yb_ref[...], gb_ref[...]).astype(BF16)
    c = yc_ref[...].astype(BF16)
    mixed = (_dot(a, wo_ref[0:WIDTH_A, :]) + _dot(b, wo_ref[WIDTH_A:WIDTH_A + WIDTH_B, :])
             + _dot(c, wo_ref[WIDTH_A + WIDTH_B:, :]))
    x = x_ref[...] + gate1_ref[0] * mixed
    h = (_rms(x, g2_ref[...]) * (1.0 + sc_ref[0]) + sh_ref[0]).astype(BF16)
    acc = jnp.zeros(x.shape, F32)
    for j in range(FFN_HIDDEN // FFN_CHUNK):
        gate = _dot(h, wgu_ref[0, :, FFN_CHUNK * j:FFN_CHUNK * (j + 1)])
        up = _dot(h, wgu_ref[0, :, FFN_HIDDEN + FFN_CHUNK * j:FFN_HIDDEN + FFN_CHUNK * (j + 1)])
        act = (_silu(gate) * up).astype(BF16)
        acc = acc + _dot(act, wd_ref[FFN_CHUNK * j:FFN_CHUNK * (j + 1), :])
    y = x + gate_ref[0] * acc
    if final_norm:
        y = _rms(y, gf_ref[...])
    o_ref[...] = y


def _mix_ffn(ya, yb, yc, x2, mod3, mod_first, ga, gb, w_out, g2, wgu, w_down, layer, gf, S, final_norm):
    T = x2.shape[0]
    tm = TM_PROJ
    per_b = S // tm
    row = lambda w: pl.BlockSpec((tm, w), lambda i: (i, 0))
    full = lambda a: pl.BlockSpec(a.shape, lambda i: (0,) * a.ndim)
    slab = lambda a: pl.BlockSpec((1,) + a.shape[1:], lambda i: (layer, 0, 0))
    modspec = lambda k: _mod_spec(per_b, mod_first, k)
    return pl.pallas_call(
        functools.partial(_mix_ffn_kernel, final_norm=final_norm),
        grid=(T // tm,),
        in_specs=[row(WIDTH_A), row(WIDTH_B), row(WIDTH_C), row(D_MODEL), modspec(2), full(ga), full(gb), slab(w_out),
                  modspec(3), modspec(4), modspec(5), full(g2), slab(wgu), slab(w_down), full(gf)],
        out_specs=row(D_MODEL),
        out_shape=jax.ShapeDtypeStruct((T, D_MODEL), F32),
        scratch_shapes=[pltpu.VMEM(w_out.shape[1:], BF16), pltpu.VMEM(w_down.shape[1:], BF16)],
        compiler_params=_params("arbitrary"),
        name="mix_ffn_final" if final_norm else "mix_ffn",
    )(ya, yb, yc, x2, mod3, ga, gb, w_out, mod3, mod3, mod3, g2, wgu, w_down, gf)


def _head_tile_perm():
    idx = []
    for r in range(NSA_REP):
        idx += list(range(HEAD_DIM * r, HEAD_DIM * (r + 1)))
        idx += list(range(HEAD_DIM * (NSA_REP + r), HEAD_DIM * (NSA_REP + r + 1)))
    return np.asarray(idx, np.int32)


def _in_cols():
    std = np.full((N_STD,), -1, np.int64)
    tr = np.full((N_T,), -1, np.int64)
    off = 0
    tr[TSEG_Q:TSEG_Q + WIDTH_A] = _head_tile_perm()
    off += WIDTH_A
    std[SEG_CKV:SEG_CKV + 2 * LANES] = off + np.arange(2 * LANES)
    off += 2 * LANES
    std[SEG_SK:SEG_SK + LANES] = off + np.arange(LANES)
    off += LANES
    tr[TSEG_SV:TSEG_SV + LANES] = off + np.arange(LANES)
    off += LANES
    std[SEG_WK:SEG_WK + LANES] = off + np.arange(LANES)
    off += LANES
    tr[TSEG_WV:TSEG_WV + LANES] = off + np.arange(LANES)
    off += LANES
    tr[TSEG_G:TSEG_G + N_GATES] = off + np.arange(N_GATES)
    off += N_GATES
    std[SEG_QL:SEG_QL + MLA_Q_LORA] = off + np.arange(MLA_Q_LORA)
    off += MLA_Q_LORA
    std[SEG_KVL:SEG_KVL + MLA_KV_LORA] = off + np.arange(MLA_KV_LORA)
    off += MLA_KV_LORA
    std[SEG_KR + MLA_NOPE:SEG_KR + MLA_NOPE + MLA_ROPE] = off + np.arange(MLA_ROPE)
    off += MLA_ROPE
    for seg in (SEG_CX, SEG_MV, SEG_MO):
        std[seg:seg + WIDTH_C] = off + np.arange(WIDTH_C)
        off += WIDTH_C
    tr[TSEG_IF:TSEG_IF + 2 * MLSTM_HEADS] = off + np.arange(2 * MLSTM_HEADS)
    return std, tr


def _gather_cols(w, cols):
    pieces, start = [], 0
    for i in range(1, len(cols) + 1):
        run_ends = i == len(cols) or (cols[i] != cols[i - 1] + 1 if cols[i - 1] >= 0 else cols[i] >= 0)
        if run_ends:
            first = int(cols[start])
            pieces.append(w[:, first:first + i - start] if first >= 0 else jnp.zeros((w.shape[0], i - start), w.dtype))
            start = i
    return jnp.concatenate(pieces, axis=1)


def _layer_weights(l, w_in, w_uq, w_ukv):
    std, tr = _in_cols()
    w_std = _gather_cols(w_in[l], std).astype(BF16)
    w_t = _gather_cols(w_in[l], tr).T.astype(BF16)
    cq = np.full((MLA_HEADS * LANES,), -1, np.int64)
    ck = np.full((MLA_HEADS * LANES,), -1, np.int64)
    cv = np.zeros((WIDTH_B,), np.int64)
    dq = MLA_NOPE + MLA_ROPE
    dkv = MLA_NOPE + MLA_V
    for hd in range(MLA_HEADS):
        cq[LANES * hd:LANES * hd + dq] = dq * hd + np.arange(dq)
        ck[LANES * hd:LANES * hd + MLA_NOPE] = dkv * hd + np.arange(MLA_NOPE)
        cv[MLA_V * hd:MLA_V * (hd + 1)] = dkv * hd + MLA_NOPE + np.arange(MLA_V)
    wuqT = _gather_cols(w_uq[l], cq).T.astype(BF16)
    wkm = _gather_cols(w_ukv[l], ck).astype(BF16)
    wvmT = _gather_cols(w_ukv[l], cv).T.astype(BF16)
    return w_std, w_t, wuqT, wkm, wvmT


def kernel(x, c, positions, g_norm1, g_norm2, w_ada, b_ada, w_in, cmp_pos, w_cmp_k, w_cmp_v, g_out_a, g_q_lora, w_uq, g_kv_lora, w_ukv, g_out_b, conv_w, conv_b, w_q_m, w_k_m, b_igate, b_fgate, g_mh, skip_m, w_out, w_gu, w_down, g_final):
    B, S, D = x.shape
    T = B * S
    x2 = x.reshape(T, D)
    tabs, tabs_t = _rope_tables(positions)
    mod3 = _ada(c, w_ada, b_ada).reshape(DEPTH * B * N_MOD, 1, D)
    row = lambda v: v.reshape(1, -1)
    wgu = w_gu.astype(BF16)
    for l in range(DEPTH):
        w_std, w_t, wuqT, wkm, wvmT = _layer_weights(l, w_in, w_uq, w_ukv)
        (qaT, gT, ck, cv, sk_aug, wk, svT, wvT, qmT, km, vmT, cx, mv, mo, gif_t) = _inproj(
            x2, mod3, l * B * N_MOD, row(g_norm1[l]), w_std, w_t, tabs, tabs_t, row(g_q_lora[l]), wuqT,
            row(g_kv_lora[l]), wkm, wvmT, S)
        kc, vcT = _compress(ck, cv, cmp_pos[l], w_cmp_k[l], w_cmp_v[l], B, S)
        ya = _nsa(qaT, gT, kc, vcT, sk_aug, svT, wk, wvT, B, S)
        yb = _mla(qmT, km, vmT, B, S)
        yc = _mlstm(cx, mv, mo, gif_t, conv_w[l], conv_b[l], w_q_m[l], w_k_m[l], b_igate[l], b_fgate[l],
                    g_mh[l], skip_m[l], B, S)
        x2 = _mix_ffn(ya, yb, yc, x2, mod3, l * B * N_MOD, row(g_out_a[l]), row(g_out_b[l]), w_out, row(g_norm2[l]),
                      wgu, w_down, l, row(g_final), S, final_norm=(l == DEPTH - 1))
    return x2.reshape(B, S, D)
```

```python
import functools

import numpy as np
import jax
import jax.numpy as jnp
from jax import lax
from jax.experimental import pallas as pl
from jax.experimental.pallas import tpu as pltpu

F32 = jnp.float32
BF16 = jnp.bfloat16

D_MODEL = 1024
DEPTH = 2
HEAD_DIM = 64
ROPE_THETA = 500000.0
NSA_ROT_HALF = HEAD_DIM // 8
NORM_EPS = 1e-6

NSA_HEADS = 6
NSA_KV_HEADS = 2
NSA_REP = NSA_HEADS // NSA_KV_HEADS
CMP_LEN = 32
CMP_STRIDE = 16
SLC_LEN = 64
SLC_TOPK = 16
WINDOW = 512

MLA_HEADS = 6
MLA_Q_LORA = 256
MLA_KV_LORA = 128
MLA_NOPE = 64
MLA_ROPE = 32
MLA_V = 64

MLSTM_HEADS = 4
MLSTM_DIM = 64
MLSTM_CONV = 4

WIDTH_A = NSA_HEADS * HEAD_DIM
WIDTH_B = MLA_HEADS * MLA_V
WIDTH_C = MLSTM_HEADS * MLSTM_DIM
FFN_HIDDEN = 2816
N_GATES = 3 * NSA_HEADS
GATE_ROWS = 24

LANES = 128
NEG = -1e30
LOG2E = 1.4426950408889634
VMEM_LIMIT = 56 * 1024 * 1024

TM_PROJ = 512
TQ_NSA = 256
TK_SLC = 256
CHAIN_LANES = 256
TQ_MLA = 256
FFN_CHUNK = 256
MLSTM_TILE = 128
MLSTM_GROUP = 16
MXU_LOOKAHEAD = 4

SEG_CKV, SEG_SK, SEG_WK, SEG_QL, SEG_KVL, SEG_KR, SEG_CX, SEG_MV, SEG_MO = (
    0, 256, 384, 512, 768, 896, 1024, 1280, 1536)
N_STD = 1792
TSEG_Q, TSEG_G, TSEG_SV, TSEG_WV, TSEG_IF = 0, 384, 416, 544, 672
TSEG_G_ROWS = 32
TSEG_IF_ROWS = 16
N_T = 688


def _params(*sem):
    return pltpu.CompilerParams(dimension_semantics=sem, vmem_limit_bytes=VMEM_LIMIT)


def _dot(a, b):
    return jnp.dot(a, b, preferred_element_type=F32)


def _dot_nt(a, b):
    return lax.dot_general(a, b, (((1,), (1,)), ((), ())), preferred_element_type=F32)


def _dot_tn(a, b):
    return lax.dot_general(a, b, (((0,), (0,)), ((), ())), preferred_element_type=F32)


def _split3(x):
    hi = x.astype(BF16)
    r1 = x - hi.astype(F32)
    mid = r1.astype(BF16)
    lo = (r1 - mid.astype(F32)).astype(BF16)
    return hi, mid, lo


def _rms(x, g):
    return x * lax.rsqrt(jnp.mean(x * x, axis=-1, keepdims=True) + NORM_EPS) * g


def _sigmoid(x):
    return 1.0 / (1.0 + jnp.exp(-x))


def _silu(x):
    return x * _sigmoid(x)


def _rope(x, cos, sin, half, x1_lane):
    xr = jnp.where(x1_lane, -pltpu.roll(x, LANES - half, 1), pltpu.roll(x, half, 1))
    return x * cos + xr * sin


def _ada_kernel(c_ref, w_ref, b_ref, wgu_ref, o_ref, wgu_bf_ref):
    c = c_ref[...]
    ca = _silu(c).astype(BF16)
    o_ref[0] = _dot(ca, w_ref[0].astype(BF16)) + b_ref[0]
    wgu_bf_ref[...] = wgu_ref[...].astype(BF16)


def _ada(c, w_ada, b_ada, w_gu):
    L, D, N = w_ada.shape
    B = c.shape[0]
    steps = 4
    tn, tg = N // steps, w_gu.shape[2] // steps
    return pl.pallas_call(
        _ada_kernel,
        grid=(L, steps),
        in_specs=[pl.BlockSpec((B, D), lambda l, j: (0, 0)),
                  pl.BlockSpec((1, D, tn), lambda l, j: (l, 0, j)),
                  pl.BlockSpec((1, 1, tn), lambda l, j: (l, 0, j)),
                  pl.BlockSpec((1, D, tg), lambda l, j: (l, 0, j))],
        out_specs=[pl.BlockSpec((1, B, tn), lambda l, j: (l, 0, j)),
                   pl.BlockSpec((1, D, tg), lambda l, j: (l, 0, j))],
        out_shape=[jax.ShapeDtypeStruct((L, B, N), F32), jax.ShapeDtypeStruct(w_gu.shape, BF16)],
        compiler_params=_params("parallel", "parallel"),
        name="ada_mod",
    )(c, w_ada, b_ada.reshape(L, 1, N), w_gu)


ROPE_FREQS = NSA_ROT_HALF + MLA_ROPE // 2


def _rope_kernel(posr_ref, invc_ref, spread_ref, one_ref, cn_ref, sn_ref, cm_ref, sm_ref, cnt_ref, snt_ref, cmt_ref, smt_ref):
    tm = posr_ref.shape[1]
    rows_t = invc_ref.shape[0]
    ang_t = jnp.concatenate([invc_ref[...]] * (tm // LANES), axis=1) * jnp.broadcast_to(posr_ref[...], (rows_t, tm))
    c_t, s_t = jnp.cos(ang_t), jnp.sin(ang_t)
    cnt_ref[...] = c_t[:NSA_ROT_HALF]
    snt_ref[...] = s_t[:NSA_ROT_HALF]
    cmt_ref[...] = c_t[NSA_ROT_HALF:ROPE_FREQS]
    smt_ref[...] = s_t[NSA_ROT_HALF:ROPE_FREQS]

    def spread(val):
        hi, mid, lo = _split3(val)
        pieces = jnp.concatenate([hi.astype(F32), mid.astype(F32), lo.astype(F32)], axis=0).astype(BF16)
        return _dot_tn(pieces, spread_ref[...])

    c = spread(c_t) + one_ref[...]
    s = spread(s_t)
    cn_ref[...] = c[:, :LANES]
    cm_ref[...] = c[:, LANES:]
    sn_ref[...] = s[:, :LANES]
    sm_ref[...] = s[:, LANES:]


def _rope_tables(positions):
    T = positions.size
    inv_n = jnp.power(ROPE_THETA, -jnp.arange(0, 2 * NSA_ROT_HALF, 2, dtype=F32) / (2 * NSA_ROT_HALF))
    inv_m = jnp.power(ROPE_THETA, -jnp.arange(0, MLA_ROPE, 2, dtype=F32) / MLA_ROPE)
    rows_t = 4 * NSA_ROT_HALF
    onehot = np.zeros((3 * rows_t, 2 * LANES), np.float32)
    rotated = np.zeros((1, 2 * LANES), np.float32)
    for k in range(3):
        for f in range(NSA_ROT_HALF):
            for lane in (f, f + NSA_ROT_HALF, HEAD_DIM + f, HEAD_DIM + f + NSA_ROT_HALF):
                onehot[rows_t * k + f, lane] = 1.0
                rotated[0, lane] = 1.0
        for f in range(MLA_ROPE // 2):
            for lane in (MLA_NOPE + f, MLA_NOPE + MLA_ROPE // 2 + f):
                onehot[rows_t * k + NSA_ROT_HALF + f, LANES + lane] = 1.0
                rotated[0, LANES + lane] = 1.0
    inv_col = jnp.concatenate([inv_n, inv_m, jnp.zeros((rows_t - ROPE_FREQS,), F32)])
    inv_col = jnp.broadcast_to(inv_col[:, None], (rows_t, LANES))
    tm = 2048
    spec = pl.BlockSpec((tm, LANES), lambda i: (i, 0))
    col = lambda h: pl.BlockSpec((h, tm), lambda i: (0, i))
    full = lambda shape: pl.BlockSpec(shape, lambda i: (0, 0))
    sds = jax.ShapeDtypeStruct
    outs = pl.pallas_call(
        _rope_kernel,
        grid=(T // tm,),
        in_specs=[col(1), full((rows_t, LANES)), full((3 * rows_t, 2 * LANES)), full((1, 2 * LANES))],
        out_specs=[spec] * 4 + [col(NSA_ROT_HALF)] * 2 + [col(MLA_ROPE // 2)] * 2,
        out_shape=[sds((T, LANES), F32)] * 4 + [sds((NSA_ROT_HALF, T), F32)] * 2 + [sds((MLA_ROPE // 2, T), F32)] * 2,
        compiler_params=_params("parallel"),
        name="rope_tables",
    )(positions.reshape(1, T).astype(F32), inv_col, jnp.asarray(onehot, BF16), jnp.asarray(1.0 - rotated))
    return outs[:4], outs[4:]


def _rope_rows(t, offset, half, cos, sin):
    x1, x2 = t[offset:offset + half], t[offset + half:offset + 2 * half]
    return x1 * cos - x2 * sin, x2 * cos + x1 * sin


def _inproj_kernel(x_ref, sh_ref, sc_ref, g1_ref, ws_ref, wt_ref, cn_ref, sn_ref, cm_ref, sm_ref,
                   cnT_ref, snT_ref, cmT_ref, smT_ref, gq_ref, wuqT_ref, gkv_ref, wkm_ref, wvmT_ref,
                   qaT_ref, gT_ref, ck_ref, cv_ref, ska_ref, wk_ref, svT_ref, wvT_ref,
                   qmT_ref, km_ref, vmT_ref, cx_ref, mv_ref, mo_ref, gif_ref, *, per_b):
    tm = x_ref.shape[0]
    x = x_ref[...]
    h = _rms(x, g1_ref[...]) * (1.0 + sc_ref[0]) + sh_ref[0]
    hb = h.astype(BF16)

    def seg(start, width):
        return _dot(hb, ws_ref[:, start:start + width])

    lane = lax.broadcasted_iota(jnp.int32, (1, LANES), 1)
    x1_n = (lane % HEAD_DIM) < NSA_ROT_HALF
    x1_m = lane < MLA_NOPE + MLA_ROPE // 2
    cn, sn, cm, sm = cn_ref[...], sn_ref[...], cm_ref[...], sm_ref[...]
    rope_n = lambda t: _rope(t, cn, sn, NSA_ROT_HALF, x1_n)
    rope_m = lambda t: _rope(t, cm, sm, MLA_ROPE // 2, x1_m)

    qn = _rms(seg(SEG_QL, MLA_Q_LORA), gq_ref[...]).astype(BF16)
    kvl_kr = seg(SEG_KVL, 2 * LANES)
    kvn = _rms(kvl_kr[:, :LANES], gkv_ref[...]).astype(BF16)
    kr = rope_m(kvl_kr[:, LANES:])

    out_t = _dot_nt(wt_ref[...], hb)
    seg_t = lambda start, height: out_t[start:start + height]

    qt = seg_t(TSEG_Q, WIDTH_A)
    cnt, snt = cnT_ref[...], snT_ref[...]
    parts = []
    for hd in range(NSA_HEADS):
        o = HEAD_DIM * hd
        parts += list(_rope_rows(qt, o, NSA_ROT_HALF, cnt, snt)) + [qt[o + 2 * NSA_ROT_HALF:o + HEAD_DIM]]
    qaT_ref[...] = (jnp.concatenate(parts, axis=0) * (HEAD_DIM ** -0.5 * LOG2E)).astype(BF16)
    gT_ref[...] = _sigmoid(seg_t(TSEG_G, TSEG_G_ROWS))[:GATE_ROWS]
    svT_ref[...] = seg_t(TSEG_SV, LANES).astype(BF16)
    wvT_ref[...] = seg_t(TSEG_WV, LANES).astype(BF16)
    gif_ref[...] = seg_t(TSEG_IF, TSEG_IF_ROWS)[:2 * MLSTM_HEADS]

    ckv = seg(SEG_CKV, 2 * LANES)
    ck_ref[...] = rope_n(ckv[:, :LANES])
    cv_ref[...] = ckv[:, LANES:]
    ska_ref[:, :LANES] = rope_n(seg(SEG_SK, LANES)).astype(BF16)
    srow = (pl.program_id(0) % per_b) * tm + lax.broadcasted_iota(jnp.int32, (tm, LANES), 0)
    lane2 = lax.broadcasted_iota(jnp.int32, (tm, LANES), 1)
    code = (lane2 < 4 * SLC_TOPK) & ((lane2 & (2 * SLC_TOPK - 1)) == srow // SLC_LEN)
    ska_ref[:, LANES:] = jnp.where(code, 1.0, 0.0).astype(BF16)
    wk_ref[...] = rope_n(seg(SEG_WK, LANES)).astype(BF16)

    qmt = _dot_nt(wuqT_ref[...], qn)
    kk = _dot(kvn, wkm_ref[...])
    vmt = _dot_nt(wvmT_ref[...], kvn)
    cx_ref[...] = seg(SEG_CX, WIDTH_C)
    mv_ref[...] = seg(SEG_MV, WIDTH_C).astype(BF16)
    mo_ref[...] = seg(SEG_MO, WIDTH_C)
    cmt, smt = cmT_ref[...], smT_ref[...]
    parts = []
    for hd in range(MLA_HEADS):
        o = LANES * hd
        parts += [qmt[o:o + MLA_NOPE]] + list(_rope_rows(qmt, o + MLA_NOPE, MLA_ROPE // 2, cmt, smt))
        parts += [qmt[o + MLA_NOPE + MLA_ROPE:o + LANES]]
    qmT_ref[...] = (jnp.concatenate(parts, axis=0) * ((MLA_NOPE + MLA_ROPE) ** -0.5 * LOG2E)).astype(BF16)
    for hd in range(MLA_HEADS):
        km_ref[:, LANES * hd:LANES * (hd + 1)] = (kk[:, LANES * hd:LANES * (hd + 1)] + kr).astype(BF16)
    vmT_ref[...] = vmt.astype(BF16)


N_MOD = 6


def _mod_spec(per_b, first, k):
    return pl.BlockSpec((1, 1, D_MODEL), lambda i: (first + (i // per_b) * N_MOD + k, 0, 0))


def _inproj(x2, mod3, mod_first, g1, w_std, w_t, tabs, tabs_t, gq, wuqT, gkv, wkm, wvmT, S):
    T = x2.shape[0]
    tm = TM_PROJ
    per_b = S // tm
    row = lambda w: pl.BlockSpec((tm, w), lambda i: (i, 0))
    col = lambda h: pl.BlockSpec((h, tm), lambda i: (0, i))
    full = lambda a: pl.BlockSpec(a.shape, lambda i: (0,) * a.ndim)
    outs = [(WIDTH_A, BF16, True), (GATE_ROWS, F32, True), (LANES, F32, False), (LANES, F32, False), (2 * LANES, BF16, False),
            (LANES, BF16, False), (LANES, BF16, True), (LANES, BF16, True),
            (MLA_HEADS * LANES, BF16, True), (MLA_HEADS * LANES, BF16, False), (WIDTH_B, BF16, True),
            (WIDTH_C, F32, False), (WIDTH_C, BF16, False), (WIDTH_C, F32, False), (2 * MLSTM_HEADS, F32, True)]
    return pl.pallas_call(
        functools.partial(_inproj_kernel, per_b=per_b),
        grid=(T // tm,),
        in_specs=[row(D_MODEL), _mod_spec(per_b, mod_first, 0), _mod_spec(per_b, mod_first, 1),
                  full(g1), full(w_std), full(w_t)] + [row(LANES)] * 4
                 + [col(t.shape[0]) for t in tabs_t] + [full(gq), full(wuqT), full(gkv), full(wkm), full(wvmT)],
        out_specs=[col(w) if tr else row(w) for w, _, tr in outs],
        out_shape=[jax.ShapeDtypeStruct((w, T) if tr else (T, w), dt) for w, dt, tr in outs],
        compiler_params=_params("parallel"),
        name="in_proj",
    )(x2, mod3, mod3, g1, w_std, w_t, *tabs, *tabs_t, gq, wuqT, gkv, wkm, wvmT)


def _compress_kernel(xk_ref, xv_ref, wk_ref, wv_ref, pos_ref, wkf_ref, wvf_ref, kc_ref, vcT_ref):
    ng = xk_ref.shape[0] // CMP_STRIDE
    row = lax.broadcasted_iota(jnp.int32, (ng, LANES), 0)
    pos = pos_ref[...].astype(BF16)
    acc_k = jnp.zeros((ng, 2 * LANES), F32)
    acc_v = jnp.zeros((ng, 2 * LANES), F32)
    for t in range(CMP_STRIDE):
        tok = pl.ds(t, ng, stride=CMP_STRIDE)
        acc_k = acc_k + _dot(xk_ref[tok, :].astype(BF16), wk_ref[t])
        acc_v = acc_v + _dot(xv_ref[tok, :].astype(BF16), wv_ref[t])

    def finish(acc, wf_ref):
        const = _dot(pos, wf_ref[...].astype(BF16))
        both = acc[:, :LANES] + pltpu.roll(acc[:, LANES:], ng - 1, 0) + jnp.concatenate([const, const], axis=1)
        return jnp.where(row < ng - 1, both, 0.0)

    kc_ref[0] = finish(acc_k, wkf_ref).astype(BF16)
    vcT_ref[0] = finish(acc_v, wvf_ref).T.astype(BF16)


def _compress(ck, cv, cmp_pos, w_cmp_k, w_cmp_v, B, S):
    ng = S // CMP_STRIDE

    def per_token(w):
        a = w[:CMP_STRIDE * HEAD_DIM].reshape(CMP_STRIDE, HEAD_DIM, HEAD_DIM)
        b = w[CMP_STRIDE * HEAD_DIM:].reshape(CMP_STRIDE, HEAD_DIM, HEAD_DIM)
        z = jnp.zeros_like(a)
        top = jnp.concatenate([a, z, b, z], axis=2)
        bot = jnp.concatenate([z, a, z, b], axis=2)
        return jnp.concatenate([top, bot], axis=1).astype(BF16)

    full = lambda a: pl.BlockSpec(a.shape, lambda b: (0,) * a.ndim)
    wk3, wv3 = per_token(w_cmp_k), per_token(w_cmp_v)
    posf = cmp_pos.reshape(1, CMP_LEN * HEAD_DIM)
    ospec = pl.BlockSpec((1, ng, LANES), lambda b: (b, 0, 0))
    return pl.pallas_call(
        _compress_kernel,
        grid=(B,),
        in_specs=[pl.BlockSpec((S, LANES), lambda b: (b, 0)), pl.BlockSpec((S, LANES), lambda b: (b, 0)),
                  full(wk3), full(wv3), full(posf), full(w_cmp_k), full(w_cmp_v)],
        out_specs=[ospec, ospec],
        out_shape=[jax.ShapeDtypeStruct((B, ng, LANES), BF16)] * 2,
        compiler_params=_params("parallel"),
        name="nsa_compress",
    )(ck, cv, wk3, wv3, posf, w_cmp_k, w_cmp_v)


def _nsa_kernel(qT_ref, gT_ref, kc_ref, vcT_ref, sk_ref, svT_ref, wk_ref, wvT_ref, o_ref, qaug_scr, acc_scr, s_scr):
    tq = TQ_NSA
    cols = NSA_HEADS * tq
    pair = CHAIN_LANES
    t0 = pl.program_id(1) * tq
    n_slc = SLC_TOPK * 2

    frow = lax.broadcasted_iota(jnp.int32, (LANES, tq), 0)
    g0_row = frow < HEAD_DIM
    tiles = [qT_ref[LANES * r:LANES * (r + 1), :] for r in range(NSA_REP)]
    zero = jnp.zeros_like(tiles[0])
    q6 = jnp.concatenate([jnp.where(g0_row, t, zero) for t in tiles]
                         + [jnp.where(g0_row, zero, t) for t in tiles], axis=1)
    qaug_scr[0:LANES, :] = q6
    tq_l = t0 + (lax.broadcasted_iota(jnp.int32, (1, cols), 1) & (tq - 1))

    pairs = [slice(pair * pp, pair * (pp + 1)) for pp in range(cols // pair)]
    vrows = lambda pp: slice(HEAD_DIM * (pp // NSA_REP), HEAD_DIM * (pp // NSA_REP + 1))
    s = _dot(kc_ref[0], q6)
    span = WINDOW + tq
    start = pl.multiple_of(jnp.maximum(t0 - WINDOW, 0), tq)
    kw = wk_ref[pl.ds(start, span), :]
    win_scores = [_dot(kw, q6[:, sl]) for sl in pairs]

    tq_1 = tq_l[:, :tq]
    nrow = lax.broadcasted_iota(jnp.int32, (LANES, tq), 0)
    cmp_bias = jnp.where(nrow * CMP_STRIDE + (CMP_LEN - 1) <= tq_1, 0.0, NEG)
    s = s + jnp.concatenate([cmp_bias] * NSA_HEADS, axis=1)
    e = jnp.exp2(s - jnp.max(s, axis=0, keepdims=True))
    seen = jnp.where(tq_l >= CMP_LEN - 1, 1.0, 0.0)
    p = e * (seen / jnp.sum(e, axis=0, keepdims=True))
    p_b = p.astype(BF16)
    o_cmp = [_dot(vcT_ref[0, vrows(pp), :], p_b[:, sl]) for pp, sl in enumerate(pairs)]

    jr = lax.broadcasted_iota(jnp.int32, (n_slc, LANES), 0)
    nc = lax.broadcasted_iota(jnp.int32, (n_slc, LANES), 1)
    ovl = ((nc * CMP_STRIDE < jr * SLC_LEN + SLC_LEN) & (nc * CMP_STRIDE + CMP_LEN > jr * SLC_LEN)
           & (nc < LANES - 1))
    ovl = jnp.where(ovl, 1.0, 0.0).astype(BF16)
    jq = lax.broadcasted_iota(jnp.int32, (n_slc, tq), 0)
    tl = t0 + lax.broadcasted_iota(jnp.int32, (n_slc, tq), 1)
    cur = tl // SLC_LEN
    forced = (jq == 0) | (jq == cur) | (jq == cur - 1)
    future = jq * SLC_LEN > tl
    bias_t = []
    for g in range(NSA_KV_HEADS):
        pg = p[:, (3 * g) * tq:(3 * g + 1) * tq] + p[:, (3 * g + 1) * tq:(3 * g + 2) * tq] + p[:, (3 * g + 2) * tq:(3 * g + 3) * tq]
        imp = sum(_dot(ovl, part) for part in _split3(pg))
        imp = jnp.where(forced, jnp.inf, imp)
        imp = jnp.where(future, -jnp.inf, imp)
        rows8 = [imp[8 * r:8 * (r + 1)] for r in range(n_slc // 8)]
        sub = lax.broadcasted_iota(jnp.int32, (8, tq), 0)
        ranks = [jnp.zeros((8, tq), F32) for _ in rows8]
        for jp in range(n_slc):
            rv = imp[jp:jp + 1, :]
            for r, blk in enumerate(rows8):
                if 8 * r > jp:
                    ahead = rv >= blk
                elif 8 * r + 7 < jp:
                    ahead = rv > blk
                else:
                    ahead = (rv > blk) | ((rv == blk) & (sub > jp - 8 * r))
                ranks[r] = ranks[r] + jnp.where(ahead, 1.0, 0.0)
        rank = jnp.concatenate(ranks, axis=0)
        bias_t.append(jnp.where(rank < float(SLC_TOPK), 0.0, NEG).astype(BF16))
    zb = jnp.zeros((n_slc, NSA_REP * tq), BF16)
    qaug_scr[LANES:LANES + n_slc, :] = jnp.concatenate([bias_t[0]] * NSA_REP + [zb], axis=1)
    qaug_scr[LANES + n_slc:LANES + 2 * n_slc, :] = jnp.concatenate([zb] + [bias_t[1]] * NSA_REP, axis=1)
    qaug_scr[LANES + 2 * n_slc:, :] = jnp.zeros((LANES - 2 * n_slc, cols), BF16)

    vwt = wvT_ref[:, pl.ds(start, span)]
    wrow = start + lax.broadcasted_iota(jnp.int32, (span, pair), 0)
    o_win = []
    for pp, (sl, sc) in enumerate(zip(pairs, win_scores)):
        in_window = lax.bitcast_convert_type(tq_l[:, sl] - wrow, jnp.uint32) < jnp.uint32(WINDOW)
        sc = jnp.where(in_window, sc, NEG)
        ew = jnp.exp2(sc - jnp.max(sc, axis=0, keepdims=True))
        o_win.append(_dot(vwt[vrows(pp)], ew.astype(BF16)) / jnp.sum(ew, axis=0, keepdims=True))

    acc_scr[...] = jnp.zeros((HEAD_DIM, cols), F32)
    krow = lax.broadcasted_iota(jnp.int32, (TK_SLC, pair), 0)

    def slc_scores(k0, sl):
        return _dot(sk_ref[pl.ds(k0, TK_SLC), :], qaug_scr[:, sl])

    nch = len(pairs)
    la = s_scr.shape[0]
    for pp in range(la):
        s_scr[pp] = slc_scores(0, pairs[pp])

    def slc_tile(k0, m, l, masked):
        vt = svT_ref[:, pl.ds(k0, TK_SLC)]
        ms, ls, accs = [], [], []
        ahead = [s_scr[pp] for pp in range(la)]
        for pp, sl in enumerate(pairs):
            sc = ahead.pop(0)
            nxt = pp + la
            if nxt < nch:
                ahead.append(slc_scores(k0, pairs[nxt]))
            elif not masked:
                s_scr[nxt - nch] = slc_scores(pl.multiple_of(k0 + TK_SLC, TK_SLC), pairs[nxt - nch])
            if masked:
                sc = jnp.where(k0 + krow <= tq_l[:, sl], sc, NEG)
            m_new = jnp.maximum(m[:, sl], jnp.max(sc, axis=0, keepdims=True))
            alpha = jnp.exp2(m[:, sl] - m_new)
            pe = jnp.exp2(sc - m_new)
            ls.append(alpha * l[:, sl] + jnp.sum(pe, axis=0, keepdims=True))
            accs.append(alpha * acc_scr[:, sl] + _dot(vt[vrows(pp)], pe.astype(BF16)))
            ms.append(m_new)
        acc_scr[...] = jnp.concatenate(accs, axis=1)
        return jnp.concatenate(ms, axis=1), jnp.concatenate(ls, axis=1)

    def slc_step(kt, carry):
        return slc_tile(pl.multiple_of(kt * TK_SLC, TK_SLC), carry[0], carry[1], False)

    n_full = t0 // TK_SLC
    m, l = lax.fori_loop(0, n_full, slc_step, (jnp.full((1, cols), NEG, F32), jnp.zeros((1, cols), F32)))
    m, l = slc_tile(pl.multiple_of(n_full * TK_SLC, TK_SLC), m, l, True)
    o_slc = acc_scr[...] / l

    gt = gT_ref[...]
    mixed = []
    for hd, sl in enumerate(pairs):
        gate = lambda j: gt[3 * hd + j:3 * hd + j + 1, :]
        mixed.append(gate(0) * o_cmp[hd] + gate(1) * o_slc[:, sl] + gate(2) * o_win[hd])
    for t in range(NSA_HEADS // 2):
        o_ref[:, LANES * t:LANES * (t + 1)] = jnp.concatenate([mixed[2 * t], mixed[2 * t + 1]], axis=0).T


def _nsa(qaT, gT, kc, vcT, sk_aug, svT, wk, wvT, B, S):
    tq = TQ_NSA
    assert CHAIN_LANES == tq, "the kernel treats one softmax chain as one head"
    nq = S // tq
    cols = NSA_HEADS * tq
    qcol =lambda h: pl.BlockSpec((h, tq), lambda b, i: (0, b * nq + i))
    seq = lambda w: pl.BlockSpec((S, w), lambda b, i: (b, 0))
    seqT = pl.BlockSpec((LANES, S), lambda b, i: (0, b))
    cspec = pl.BlockSpec((1, LANES, LANES), lambda b, i: (b, 0, 0))
    return pl.pallas_call(
        _nsa_kernel,
        grid=(B, nq),
        in_specs=[qcol(WIDTH_A), qcol(gT.shape[0]), cspec, cspec, seq(2 * LANES), seqT, seq(LANES), seqT],
        out_specs=pl.BlockSpec((tq, WIDTH_A), lambda b, i: (b * nq + i, 0)),
        out_shape=jax.ShapeDtypeStruct((B * S, WIDTH_A), F32),
        scratch_shapes=[pltpu.VMEM((2 * LANES, cols), BF16), pltpu.VMEM((HEAD_DIM, cols), F32),
                        pltpu.VMEM((cols // CHAIN_LANES, TK_SLC, CHAIN_LANES), F32)],
        compiler_params=_params("parallel", "arbitrary"),
        name="nsa_attention",
    )(qaT, gT, kc, vcT, sk_aug, svT, wk, wvT)


def _mla_kernel(qT_ref, k_ref, vT_ref, o_ref, acc_scr, s_scr):
    tq = TQ_MLA
    t0 = pl.program_id(1) * tq
    tq_l = t0 + lax.broadcasted_iota(jnp.int32, (1, tq), 1)
    krow = lax.broadcasted_iota(jnp.int32, (tq, tq), 0)
    acc_scr[...] = jnp.zeros((MLA_HEADS, MLA_V, tq), F32)

    def scores(k0, hd):
        k = k_ref[pl.ds(k0, tq), LANES * hd:LANES * (hd + 1)]
        return _dot(k, qT_ref[LANES * hd:LANES * (hd + 1), :])

    for hd in range(MXU_LOOKAHEAD):
        s_scr[hd] = scores(0, hd)

    def tile(k0, ms, ls, masked):
        new_m, new_l, accs = [], [], []
        ahead = [s_scr[hd] for hd in range(MXU_LOOKAHEAD)]
        for hd in range(MLA_HEADS):
            sc = ahead.pop(0)
            nxt = hd + MXU_LOOKAHEAD
            if nxt < MLA_HEADS:
                ahead.append(scores(k0, nxt))
            elif not masked:
                s_scr[nxt - MLA_HEADS] = scores(pl.multiple_of(k0 + tq, tq), nxt - MLA_HEADS)
            if masked:
                sc = jnp.where(k0 + krow <= tq_l, sc, NEG)
            m_new = jnp.maximum(ms[hd], jnp.max(sc, axis=0, keepdims=True))
            alpha = jnp.exp2(ms[hd] - m_new)
            pe = jnp.exp2(sc - m_new)
            new_l.append(alpha * ls[hd] + jnp.sum(pe, axis=0, keepdims=True))
            vt = vT_ref[MLA_V * hd:MLA_V * (hd + 1), pl.ds(k0, tq)]
            accs.append(alpha * acc_scr[hd] + _dot(vt, pe.astype(BF16)))
            new_m.append(m_new)
        acc_scr[...] = jnp.stack(accs)
        return tuple(new_m), tuple(new_l)

    def step(kt, carry):
        return tile(pl.multiple_of(kt * tq, tq), carry[0], carry[1], False)

    n_full = pl.program_id(1)
    init = (tuple(jnp.full((1, tq), NEG, F32) for _ in range(MLA_HEADS)),
            tuple(jnp.zeros((1, tq), F32) for _ in range(MLA_HEADS)))
    ms, ls = lax.fori_loop(0, n_full, step, init)
    ms, ls = tile(pl.multiple_of(n_full * tq, tq), ms, ls, True)
    for pr in range(MLA_HEADS // 2):
        t = jnp.concatenate([acc_scr[2 * pr] / ls[2 * pr], acc_scr[2 * pr + 1] / ls[2 * pr + 1]], axis=0)
        o_ref[:, LANES * pr:LANES * (pr + 1)] = t.T


def _mla(qmT, km, vmT, B, S):
    tq = TQ_MLA
    nq = S // tq
    return pl.pallas_call(
        _mla_kernel,
        grid=(B, nq),
        in_specs=[pl.BlockSpec((MLA_HEADS * LANES, tq), lambda b, i: (0, b * nq + i)),
                  pl.BlockSpec((S, MLA_HEADS * LANES), lambda b, i: (b, 0)),
                  pl.BlockSpec((WIDTH_B, S), lambda b, i: (0, b))],
        out_specs=pl.BlockSpec((tq, WIDTH_B), lambda b, i: (b * nq + i, 0)),
        out_shape=jax.ShapeDtypeStruct((B * S, WIDTH_B), F32),
        scratch_shapes=[pltpu.VMEM((MLA_HEADS, MLA_V, tq), F32), pltpu.VMEM((MXU_LOOKAHEAD, tq, tq), F32)],
        compiler_params=_params("parallel", "arbitrary"),
        name="mla_attention",
    )(qmT, km, vmT)


def _log_sigmoid(z):
    return jnp.minimum(z, 0.0) - jnp.log1p(jnp.exp(-jnp.abs(z)))


def _mlstm_kernel(cx_ref, v_ref, o_ref, g_ref, cw_ref, cb_ref, wq_ref, wk_ref, br_ref, gmh_ref, skip_ref, y_ref,
                  xc_scr, q_scr, k_scr, w_scr, inter_scr, floor_scr, mfull_scr, ut_scr, eo_scr, el_scr, cprev_scr):
    S = cx_ref.shape[0]
    L = MLSTM_TILE
    NC = S // L
    d = MLSTM_DIM
    pairs = MLSTM_HEADS // 2
    group = MLSTM_GROUP

    x = cx_ref[...]
    rowi = lax.broadcasted_iota(jnp.int32, (S, WIDTH_C), 0)
    conv = x * cw_ref[MLSTM_CONV - 1:MLSTM_CONV, :]
    for back in range(1, MLSTM_CONV):
        shifted = jnp.where(rowi >= back, pltpu.roll(x, back, 0), 0.0)
        conv = conv + shifted * cw_ref[MLSTM_CONV - 1 - back:MLSTM_CONV - back, :]
    xc = _silu(conv + cb_ref[...])
    xc_scr[...] = xc
    xcb = xc.astype(BF16)
    q_scr[...] = _dot(xcb, wq_ref[...]).astype(BF16)
    k_scr[...] = _dot(xcb, wk_ref[...]).astype(BF16)

    nh = MLSTM_HEADS
    lane_in_chunk = lax.broadcasted_iota(jnp.int32, (nh, S), 1) & (L - 1)

    def scan_lanes(val, op, fill):
        sh = 1
        while sh < L:
            val = op(val, jnp.where(lane_in_chunk >= sh, pltpu.roll(val, sh, 1), fill))
            sh *= 2
        return val

    gt = g_ref[...] + br_ref[...]
    ig = gt[0:nh]
    b = scan_lanes(_log_sigmoid(gt[nh:]), jnp.add, 0.0)
    u = ig - b
    cmu = scan_lanes(u, jnp.maximum, -jnp.inf)
    ut_scr[0:nh, :] = u

    m = jnp.zeros((nh, 1), F32)
    w_loc, inter, floor, m_tok = [], [], [], []
    for c in range(NC):
        blk = slice(L * c, L * (c + 1))
        b_last, u_max = b[:, L * (c + 1) - 1:L * (c + 1)], cmu[:, L * (c + 1) - 1:L * (c + 1)]
        m_top = jnp.maximum(m, u_max)
        eo_scr[c] = jnp.broadcast_to(jnp.exp(m - m_top), (nh, LANES))
        el_scr[c] = jnp.broadcast_to(jnp.exp(u_max - m_top), (nh, LANES))
        mt = jnp.maximum(cmu[:, blk], m)
        m_tok.append(mt)
        inter.append(jnp.exp(m - mt))
        w_loc.append(jnp.exp(u[:, blk] - u_max))
        floor.append(jnp.exp(-(b[:, blk] + mt)))
        m = b_last + m_top

    hrow = lax.broadcasted_iota(jnp.int32, (4 * nh, 2 * LANES), 0)
    hcol = lax.broadcasted_iota(jnp.int32, (4 * nh, 2 * LANES), 1)
    to_heads = jnp.where((hrow < 3 * nh) & (hcol // d == hrow % nh), 1.0, 0.0).astype(BF16)
    frow = lax.broadcasted_iota(jnp.int32, (4 * nh, 4 * LANES), 0)
    fcol = lax.broadcasted_iota(jnp.int32, (4 * nh, 4 * LANES), 1)
    to_full = jnp.where((frow < 3 * nh) & (fcol // LANES == frow % nh), 1.0, 0.0).astype(BF16)

    def spread(chunks, onehot):
        val = jnp.concatenate(chunks, axis=1)
        hi, mid, lo = _split3(val)
        pieces = jnp.concatenate([hi.astype(F32), mid.astype(F32), lo.astype(F32), jnp.zeros_like(val)], axis=0)
        return _dot_tn(pieces.astype(BF16), onehot)

    w_scr[...] = spread(w_loc, to_heads)
    inter_scr[...] = spread(inter, to_heads)
    floor_scr[...] = spread(floor, to_heads)
    mfull_scr[...] = spread(m_tok, to_full)

    arow = lax.broadcasted_iota(jnp.int32, (LANES, 2 * LANES), 0)
    acol = lax.broadcasted_iota(jnp.int32, (LANES, 2 * LANES), 1)
    blk2 = (arow // d) == ((acol & (LANES - 1)) // d)
    ones_v = jnp.ones((L, LANES), BF16)

    def head_rows(ref, c, pr):
        top = jnp.broadcast_to(ref[c, 2 * pr:2 * pr + 1, :], (d, LANES))
        bot = jnp.broadcast_to(ref[c, 2 * pr + 1:2 * pr + 2, :], (d, LANES))
        half = jnp.concatenate([top, bot], axis=0)
        return jnp.concatenate([half, half], axis=1)

    def state_group(g, carry):
        local = []
        for cc in range(group):
            r0 = pl.multiple_of((g * group + cc) * L, L)
            for pr in range(pairs):
                ps = slice(LANES * pr, LANES * (pr + 1))
                kw = (k_scr[pl.ds(r0, L), ps].astype(F32) * w_scr[pl.ds(r0, L), ps]).astype(BF16)
                vo = jnp.concatenate([v_ref[pl.ds(r0, L), ps], ones_v], axis=1)
                local.append(jnp.where(blk2, _dot_tn(kw, vo), 0.0))
        state = list(carry)
        for cc in range(group):
            c = g * group + cc
            for pr in range(pairs):
                cprev_scr[c, pr] = state[pr].astype(BF16)
                state[pr] = head_rows(eo_scr, c, pr) * state[pr] + head_rows(el_scr, c, pr) * local[cc * pairs + pr]
        return tuple(state)

    lax.fori_loop(0, NC // group, state_group, tuple(jnp.zeros((LANES, 2 * LANES), F32) for _ in range(pairs)))

    li = lax.broadcasted_iota(jnp.int32, (L, L), 0)
    si = lax.broadcasted_iota(jnp.int32, (L, L), 1)
    causal = si <= li
    lane = lax.broadcasted_iota(jnp.int32, (L, LANES), 1)
    h0_lane = lane < d
    avg = jnp.where((li // d) == (si // d), 1.0 / d, 0.0).astype(BF16)

    def dot2(val, rhs):
        hi = val.astype(BF16)
        return _dot(hi, rhs) + _dot((val - hi.astype(F32)).astype(BF16), rhs)

    def out_group(g, carry):
        units = [(cc, pr) for cc in range(group) for pr in range(pairs)]
        chunk_of = {u_: g * group + u_[0] for u_ in units}
        rows = {u_: pl.multiple_of(chunk_of[u_] * L, L) for u_ in units}
        qc, sc, pv = {}, {}, {}
        for u_ in units:
            c, pr = chunk_of[u_], u_[1]
            ps = slice(LANES * pr, LANES * (pr + 1))
            qp = q_scr[pl.ds(rows[u_], L), ps]
            kp = k_scr[pl.ds(rows[u_], L), ps]
            qc[u_] = _dot(qp, cprev_scr[c, pr])
            sc[u_] = [_dot_nt(jnp.where(h0_lane if hh == 0 else lane >= d, qp, jnp.zeros_like(qp)), kp)
                      for hh in range(2)]
        for u_ in units:
            pr = u_[1]
            ps = slice(LANES * pr, LANES * (pr + 1))
            vo = jnp.concatenate([v_ref[pl.ds(rows[u_], L), ps], ones_v], axis=1)
            pv[u_] = []
            for hh in range(2):
                hd = 2 * pr + hh
                u_row = ut_scr[hd:hd + 1, pl.ds(rows[u_], L)]
                decay = jnp.where(causal, jnp.exp(u_row - mfull_scr[pl.ds(rows[u_], L), LANES * hd:LANES * (hd + 1)]), 0.0)
                pv[u_].append(_dot((sc[u_][hh] * decay).astype(BF16), vo))
        hg, cen = {}, {}
        for u_ in units:
            pr = u_[1]
            ps = slice(LANES * pr, LANES * (pr + 1))
            it = inter_scr[pl.ds(rows[u_], L), ps]
            num = it * qc[u_][:, :LANES] + jnp.where(h0_lane, pv[u_][0][:, :LANES], pv[u_][1][:, :LANES])
            den = it * qc[u_][:, LANES:] + jnp.where(h0_lane, pv[u_][0][:, LANES:], pv[u_][1][:, LANES:])
            hcell = num / jnp.maximum(jnp.abs(den), floor_scr[pl.ds(rows[u_], L), ps])
            hg[u_] = _sigmoid(o_ref[pl.ds(rows[u_], L), ps]) * hcell
        for u_ in units:
            cen[u_] = hg[u_] - dot2(hg[u_], avg)
        for u_ in units:
            pr = u_[1]
            ps = slice(LANES * pr, LANES * (pr + 1))
            var = dot2(cen[u_] * cen[u_], avg)
            y_ref[pl.ds(rows[u_], L), ps] = (cen[u_] * lax.rsqrt(var + NORM_EPS) * gmh_ref[:, ps]
                                             + skip_ref[:, ps] * xc_scr[pl.ds(rows[u_], L), ps])
        return carry

    lax.fori_loop(0, NC // group, out_group, 0)


def _mlstm(cx, mv, mo, gif_t, conv_w, conv_b, w_q_m, w_k_m, b_igate, b_fgate, g_mh, skip_m, B, S):
    def blockdiag(w, scale):
        eye = jnp.eye(MLSTM_HEADS, dtype=F32)[:, None, :, None]
        return (w[:, :, None, :] * (eye * scale)).reshape(WIDTH_C, WIDTH_C).astype(BF16)

    nc = S // MLSTM_TILE
    wq = blockdiag(w_q_m, MLSTM_DIM ** -0.5)
    wk = blockdiag(w_k_m, 1.0)
    bias8 = jnp.concatenate([b_igate, b_fgate])
    br = jnp.broadcast_to(bias8[:, None], (2 * MLSTM_HEADS, S))
    seq = lambda w: pl.BlockSpec((S, w), lambda b: (b, 0))
    full = lambda a: pl.BlockSpec(a.shape, lambda b: (0,) * a.ndim)
    row = lambda a: a.reshape(1, -1)
    args = [conv_w, row(conv_b), wq, wk, br, row(g_mh), row(skip_m)]
    tok = lambda w, dt: pltpu.VMEM((S, w), dt)
    per_chunk = pltpu.VMEM((nc, MLSTM_HEADS, LANES), F32)
    return pl.pallas_call(
        _mlstm_kernel,
        grid=(B,),
        in_specs=[seq(WIDTH_C), seq(WIDTH_C), seq(WIDTH_C),
                  pl.BlockSpec((2 * MLSTM_HEADS, S), lambda b: (0, b))] + [full(a) for a in args],
        out_specs=seq(WIDTH_C),
        out_shape=jax.ShapeDtypeStruct((B * S, WIDTH_C), F32),
        scratch_shapes=[tok(WIDTH_C, F32), tok(WIDTH_C, BF16), tok(WIDTH_C, BF16),
                        tok(WIDTH_C, F32), tok(WIDTH_C, F32), tok(WIDTH_C, F32), tok(4 * LANES, F32),
                        pltpu.VMEM((2 * MLSTM_HEADS, S), F32), per_chunk, per_chunk,
                        pltpu.VMEM((nc, MLSTM_HEADS // 2, LANES, 2 * LANES), BF16)],
        compiler_params=_params("parallel"),
        name="mlstm_mixer",
    )(cx, mv, mo, gif_t, *args)


def _mix_ffn_kernel(ya_ref, yb_ref, yc_ref, x_ref, gate1_ref, ga_ref, gb_ref, wo32_ref,
                    sh_ref, sc_ref, gate_ref, g2_ref, wgu_ref, wd32_ref, gf_ref, o_ref, wo_ref, wd_ref, *, final_norm):
    @pl.when(pl.program_id(0) == 0)
    def _():
        wo_ref[...] = wo32_ref[0].astype(BF16)
        wd_ref[...] = wd32_ref[0].astype(BF16)

    a = _rms(ya_ref[...], ga_ref[...]).astype(BF16)
    b = _rms(yb_ref[...], gb_ref[...]).astype(BF16)
    c = yc_ref[...].astype(BF16)
    mixed = (_dot(a, wo_ref[0:WIDTH_A, :]) + _dot(b, wo_ref[WIDTH_A:WIDTH_A + WIDTH_B, :])
             + _dot(c, wo_ref[WIDTH_A + WIDTH_B:, :]))
    x = x_ref[...] + gate1_ref[0] * mixed
    h = (_rms(x, g2_ref[...]) * (1.0 + sc_ref[0]) + sh_ref[0]).astype(BF16)
    acc = jnp.zeros(x.shape, F32)
    for j in range(FFN_HIDDEN // FFN_CHUNK):
        gate = _dot(h, wgu_ref[0, :, FFN_CHUNK * j:FFN_CHUNK * (j + 1)])
        up = _dot(h, wgu_ref[0, :, FFN_HIDDEN + FFN_CHUNK * j:FFN_HIDDEN + FFN_CHUNK * (j + 1)])
        act = (_silu(gate) * up).astype(BF16)
        acc = acc + _dot(act, wd_ref[FFN_CHUNK * j:FFN_CHUNK * (j + 1), :])
    y = x + gate_ref[0] * acc
    if final_norm:
        y = _rms(y, gf_ref[...])
    o_ref[...] = y


def _mix_ffn(ya, yb, yc, x2, mod3, mod_first, ga, gb, w_out, g2, wgu, w_down, layer, gf, S, final_norm):
    T = x2.shape[0]
    tm = TM_PROJ
    per_b = S // tm
    row = lambda w: pl.BlockSpec((tm, w), lambda i: (i, 0))
    full = lambda a: pl.BlockSpec(a.shape, lambda i: (0,) * a.ndim)
    slab = lambda a: pl.BlockSpec((1,) + a.shape[1:], lambda i: (layer, 0, 0))
    modspec = lambda k: _mod_spec(per_b, mod_first, k)
    return pl.pallas_call(
        functools.partial(_mix_ffn_kernel, final_norm=final_norm),
        grid=(T // tm,),
        in_specs=[row(WIDTH_A), row(WIDTH_B), row(WIDTH_C), row(D_MODEL), modspec(2), full(ga), full(gb), slab(w_out),
                  modspec(3), modspec(4), modspec(5), full(g2), slab(wgu), slab(w_down), full(gf)],
        out_specs=row(D_MODEL),
        out_shape=jax.ShapeDtypeStruct((T, D_MODEL), F32),
        scratch_shapes=[pltpu.VMEM(w_out.shape[1:], BF16), pltpu.VMEM(w_down.shape[1:], BF16)],
        compiler_params=_params("arbitrary"),
        name="mix_ffn_final" if final_norm else "mix_ffn",
    )(ya, yb, yc, x2, mod3, ga, gb, w_out, mod3, mod3, mod3, g2, wgu, w_down, gf)


def _head_tile_perm():
    idx = []
    for r in range(NSA_REP):
        idx += list(range(HEAD_DIM * r, HEAD_DIM * (r + 1)))
        idx += list(range(HEAD_DIM * (NSA_REP + r), HEAD_DIM * (NSA_REP + r + 1)))
    return np.asarray(idx, np.int32)


def _in_cols():
    std = np.full((N_STD,), -1, np.int64)
    tr = np.full((N_T,), -1, np.int64)
    off = 0
    tr[TSEG_Q:TSEG_Q + WIDTH_A] = _head_tile_perm()
    off += WIDTH_A
    std[SEG_CKV:SEG_CKV + 2 * LANES] = off + np.arange(2 * LANES)
    off += 2 * LANES
    std[SEG_SK:SEG_SK + LANES] = off + np.arange(LANES)
    off += LANES
    tr[TSEG_SV:TSEG_SV + LANES] = off + np.arange(LANES)
    off += LANES
    std[SEG_WK:SEG_WK + LANES] = off + np.arange(LANES)
    off += LANES
    tr[TSEG_WV:TSEG_WV + LANES] = off + np.arange(LANES)
    off += LANES
    tr[TSEG_G:TSEG_G + N_GATES] = off + np.arange(N_GATES)
    off += N_GATES
    std[SEG_QL:SEG_QL + MLA_Q_LORA] = off + np.arange(MLA_Q_LORA)
    off += MLA_Q_LORA
    std[SEG_KVL:SEG_KVL + MLA_KV_LORA] = off + np.arange(MLA_KV_LORA)
    off += MLA_KV_LORA
    std[SEG_KR + MLA_NOPE:SEG_KR + MLA_NOPE + MLA_ROPE] = off + np.arange(MLA_ROPE)
    off += MLA_ROPE
    for seg in (SEG_CX, SEG_MV, SEG_MO):
        std[seg:seg + WIDTH_C] = off + np.arange(WIDTH_C)
        off += WIDTH_C
    tr[TSEG_IF:TSEG_IF + 2 * MLSTM_HEADS] = off + np.arange(2 * MLSTM_HEADS)
    return std, tr


def _gather_cols(w, cols):
    pieces, start = [], 0
    for i in range(1, len(cols) + 1):
        run_ends = i == len(cols) or (cols[i] != cols[i - 1] + 1 if cols[i - 1] >= 0 else cols[i] >= 0)
        if run_ends:
            first = int(cols[start])
            pieces.append(w[:, first:first + i - start] if first >= 0 else jnp.zeros((w.shape[0], i - start), w.dtype))
            start = i
    return jnp.concatenate(pieces, axis=1)


def _layer_weights(l, w_in, w_uq, w_ukv):
    std, tr = _in_cols()
    w_std = _gather_cols(w_in[l], std).astype(BF16)
    w_t = _gather_cols(w_in[l], tr).T.astype(BF16)
    cq = np.full((MLA_HEADS * LANES,), -1, np.int64)
    ck = np.full((MLA_HEADS * LANES,), -1, np.int64)
    cv = np.zeros((WIDTH_B,), np.int64)
    dq = MLA_NOPE + MLA_ROPE
    dkv = MLA_NOPE + MLA_V
    for hd in range(MLA_HEADS):
        cq[LANES * hd:LANES * hd + dq] = dq * hd + np.arange(dq)
        ck[LANES * hd:LANES * hd + MLA_NOPE] = dkv * hd + np.arange(MLA_NOPE)
        cv[MLA_V * hd:MLA_V * (hd + 1)] = dkv * hd + MLA_NOPE + np.arange(MLA_V)
    wuqT = _gather_cols(w_uq[l], cq).T.astype(BF16)
    wkm = _gather_cols(w_ukv[l], ck).astype(BF16)
    wvmT = _gather_cols(w_ukv[l], cv).T.astype(BF16)
    return w_std, w_t, wuqT, wkm, wvmT


def kernel(x, c, positions, g_norm1, g_norm2, w_ada, b_ada, w_in, cmp_pos, w_cmp_k, w_cmp_v, g_out_a, g_q_lora, w_uq, g_kv_lora, w_ukv, g_out_b, conv_w, conv_b, w_q_m, w_k_m, b_igate, b_fgate, g_mh, skip_m, w_out, w_gu, w_down, g_final):
    B, S, D = x.shape
    T = B * S
    x2 = x.reshape(T, D)
    tabs, tabs_t = _rope_tables(positions)
    mod, wgu = _ada(c, w_ada, b_ada, w_gu)
    mod3 = mod.reshape(DEPTH * B * N_MOD, 1, D)
    row = lambda v: v.reshape(1, -1)
    for l in range(DEPTH):
        w_std, w_t, wuqT, wkm, wvmT = _layer_weights(l, w_in, w_uq, w_ukv)
        (qaT, gT, ck, cv, sk_aug, wk, svT, wvT, qmT, km, vmT, cx, mv, mo, gif_t) = _inproj(
            x2, mod3, l * B * N_MOD, row(g_norm1[l]), w_std, w_t, tabs, tabs_t, row(g_q_lora[l]), wuqT,
            row(g_kv_lora[l]), wkm, wvmT, S)
        kc, vcT = _compress(ck, cv, cmp_pos[l], w_cmp_k[l], w_cmp_v[l], B, S)
        ya = _nsa(qaT, gT, kc, vcT, sk_aug, svT, wk, wvT, B, S)
        yb = _mla(qmT, km, vmT, B, S)
        yc = _mlstm(cx, mv, mo, gif_t, conv_w[l], conv_b[l], w_q_m[l], w_k_m[l], b_igate[l], b_fgate[l],
                    g_mh[l], skip_m[l], B, S)
        x2 = _mix_ffn(ya, yb, yc, x2, mod3, l * B * N_MOD, row(g_out_a[l]), row(g_out_b[l]), w_out, row(g_norm2[l]),
                      wgu, w_down, l, row(g_final), S, final_norm=(l == DEPTH - 1))
    return x2.reshape(B, S, D)
```
